```python
import math
import jax, jax.numpy as jnp
from jax import lax
import numpy as np

D_MODEL = 2048
BATCH = 1
SEQ = 8192
DEPTH = 1

D_MIX = D_MODEL
GLA_HEADS = 4
GLA_DV = D_MIX // 2
GLA_DK = GLA_DV // 2
GLA_HEAD_K = GLA_DK // GLA_HEADS
GLA_HEAD_V = GLA_DV // GLA_HEADS
GLA_GATE_RANK = 16
GLA_GATE_TAU = 16.0
GLA_CHUNK = 64
S5_WIDTH = D_MIX - GLA_DV
S5_GROUP = 16
S5_GROUPS = S5_WIDTH // S5_GROUP
S5_STATE = 64
S5_DT_MIN = 1e-3
S5_DT_MAX = 1e-1
PROJ_SPLITS = (GLA_DK, 2 * GLA_DK, 2 * GLA_DK + GLA_DV, 2 * GLA_DK + 2 * GLA_DV,
               2 * GLA_DK + 2 * GLA_DV + GLA_GATE_RANK)
PROJ_WIDTH = 2 * GLA_DK + 2 * GLA_DV + GLA_GATE_RANK + S5_WIDTH
N_EXPERTS = 32
TOP_K = 4
D_FF_EXPERT = D_MODEL
SWIGLU_ALPHA = 1.702
SWIGLU_LIMIT = 7.0
MOE_BLOCK = 128
RMS_EPS = 1e-6

kernel_name = "hymba_style_gla_s5_moe_block"


def rms_norm(x, g, eps=RMS_EPS):
    xf = x.astype(jnp.float32)
    y = xf * lax.rsqrt(jnp.mean(xf * xf, axis=-1, keepdims=True) + eps)
    return (y * g.astype(jnp.float32)).astype(x.dtype)


def gla_chunked(q, k, v, g):
    bsz, t, h, dk = q.shape
    dv = v.shape[-1]
    n = t // GLA_CHUNK

    def to_chunks(a):
        return a.astype(jnp.float32).reshape(bsz, n, GLA_CHUNK, h, a.shape[-1]).transpose(1, 0, 3, 2, 4)

    causal = jnp.tril(jnp.ones((GLA_CHUNK, GLA_CHUNK), dtype=bool))[:, :, None]

    def step(state, inp):
        qc, kc, vc, gc = inp
        b = jnp.cumsum(gc, axis=2)
        o_inter = jnp.einsum('bhti,bhij->bhtj', qc * jnp.exp(b), state)
        diff = b[:, :, :, None, :] - b[:, :, None, :, :]
        decay = jnp.exp(jnp.where(causal, diff, -jnp.inf))
        scores = jnp.einsum('bhti,bhsi,bhtsi->bhts', qc, kc, decay)
        o = o_inter + jnp.einsum('bhts,bhsj->bhtj', scores, vc)
        b_last = b[:, :, -1:, :]
        state = (jnp.exp(b_last[:, :, 0, :])[..., None] * state
                 + jnp.einsum('bhsi,bhsj->bhij', kc * jnp.exp(b_last - b), vc))
        return state, o

    state0 = jnp.zeros((bsz, h, dk, dv), jnp.float32)
    _, o = lax.scan(step, state0, (to_chunks(q), to_chunks(k), to_chunks(v), to_chunks(g)))
    return o.transpose(1, 0, 3, 2, 4).reshape(bsz, t, h, dv)


def _ssm_combine(e1, e2):
    a1, b1 = e1
    a2, b2 = e2
    return a1 * a2, a2 * b1 + b2


def s5_scan(u, a_re, a_im, b_re, b_im, c_re, c_im, d_skip, log_step):
    bsz, t, _ = u.shape
    f32 = jnp.float32
    lam = lax.complex(a_re.astype(f32), a_im.astype(f32))
    delta = jnp.exp(log_step.astype(f32))[:, None]
    a_bar = jnp.exp(lam * delta)
    b_bar = ((a_bar - 1.0) / lam)[:, :, None] * lax.complex(b_re.astype(f32), b_im.astype(f32))
    c = lax.complex(c_re.astype(f32), c_im.astype(f32))
    ug = u.astype(f32).reshape(bsz, t, S5_GROUPS, S5_GROUP)
    bu = jnp.einsum('gpn,btgn->btgp', b_bar, ug.astype(jnp.complex64))
    a_seq = jnp.broadcast_to(a_bar, bu.shape)
    _, states = lax.associative_scan(_ssm_combine, (a_seq, bu), axis=1)
    y = jnp.real(jnp.einsum('gnp,btgp->btgn', c, states)) + d_skip.astype(f32) * ug
    return y.reshape(bsz, t, S5_WIDTH)


def hybrid_mixer(hn, w_in, gla_gate_w2, gla_gate_b, gla_norm_g, s5_a_re, s5_a_im, s5_b_re, s5_b_im,
                 s5_c_re, s5_c_im, s5_d, s5_log_step, s5_glu_w, s5_glu_b, s5_norm_g, w_out):
    bsz, t, _ = hn.shape
    f32 = jnp.float32
    proj = hn @ w_in
    q, k, v, r, g_lr, u = jnp.split(proj, PROJ_SPLITS, axis=-1)
    q = q.reshape(bsz, t, GLA_HEADS, GLA_HEAD_K) * (GLA_HEAD_K ** -0.5)
    k = k.reshape(bsz, t, GLA_HEADS, GLA_HEAD_K)
    v = v.reshape(bsz, t, GLA_HEADS, GLA_HEAD_V)
    log_gate = jax.nn.log_sigmoid((g_lr @ gla_gate_w2 + gla_gate_b).astype(f32)) / GLA_GATE_TAU
    log_gate = log_gate.reshape(bsz, t, GLA_HEADS, GLA_HEAD_K)
    o = gla_chunked(q, k, v, log_gate)
    o = rms_norm(o, gla_norm_g) * jax.nn.silu(r.astype(f32).reshape(bsz, t, GLA_HEADS, GLA_HEAD_V))
    gla_out = o.reshape(bsz, t, GLA_DV).astype(hn.dtype)
    y = s5_scan(u, s5_a_re, s5_a_im, s5_b_re, s5_b_im, s5_c_re, s5_c_im, s5_d, s5_log_step)
    z = jax.nn.gelu(y)
    z = z * jax.nn.sigmoid(z @ s5_glu_w.astype(f32) + s5_glu_b.astype(f32))
    s5_out = rms_norm(z, s5_norm_g).astype(hn.dtype)
    return jnp.concatenate([gla_out, s5_out], axis=-1) @ w_out


def moe_ffn(hn, router_w, router_b, w1, b1, w2, b2):
    bsz, t, d = hn.shape
    n_tok = bsz * t
    xt = hn.reshape(n_tok, d)
    logits = (xt @ router_w + router_b).astype(jnp.float32)
    top_val, top_idx = lax.top_k(logits, TOP_K)
    gates = jax.nn.softmax(top_val, axis=-1)
    n_assign = n_tok * TOP_K
    flat_e = top_idx.reshape(-1).astype(jnp.int32)
    flat_tok = jnp.arange(n_assign, dtype=jnp.int32) // TOP_K
    order = jnp.argsort(flat_e)
    sorted_e = flat_e[order]
    counts = jnp.bincount(flat_e, length=N_EXPERTS)
    starts = jnp.cumsum(counts) - counts
    padded = (counts + MOE_BLOCK - 1) // MOE_BLOCK * MOE_BLOCK
    pad_ends = jnp.cumsum(padded)
    pad_starts = pad_ends - padded
    dest = pad_starts[sorted_e] + (jnp.arange(n_assign, dtype=jnp.int32) - starts[sorted_e])
    num_blocks = (n_assign + N_EXPERTS * (MOE_BLOCK - 1) + MOE_BLOCK - 1) // MOE_BLOCK
    n_rows = num_blocks * MOE_BLOCK
    row_tok = jnp.full((n_rows,), n_tok, jnp.int32).at[dest].set(flat_tok[order])
    row_gate = jnp.zeros((n_rows,), jnp.float32).at[dest].set(gates.reshape(-1)[order])
    block_e = jnp.minimum(
        jnp.searchsorted(pad_ends, jnp.arange(num_blocks, dtype=jnp.int32) * MOE_BLOCK, side='right'),
        N_EXPERTS - 1)
    x_pad = jnp.concatenate([xt, jnp.zeros((1, d), xt.dtype)], axis=0)
    xs = x_pad[row_tok].reshape(num_blocks, MOE_BLOCK, d)

    def expert_block(args):
        xb, e = args
        hcat = xb @ w1[e] + b1[e]
        glu, lin = jnp.split(hcat, 2, axis=-1)
        glu = jnp.minimum(glu, SWIGLU_LIMIT)
        lin = jnp.clip(lin, -SWIGLU_LIMIT, SWIGLU_LIMIT)
        act = glu * jax.nn.sigmoid(SWIGLU_ALPHA * glu) * (lin + 1.0)
        return act @ w2[e] + b2[e]

    ys = lax.map(expert_block, (xs, block_e)).reshape(n_rows, d)
    out = jax.ops.segment_sum(ys * row_gate[:, None].astype(ys.dtype), row_tok, num_segments=n_tok + 1)[:n_tok]
    return out.reshape(bsz, t, d)


def setup_inputs(seed: int = 0) -> dict:
    key = jax.random.key(seed)
    ks = jax.random.split(key, 26)
    L = DEPTH

    def nrm(k, shape, scale):
        return scale * jax.random.normal(k, shape, jnp.float32)

    n_idx = jnp.arange(S5_STATE, dtype=jnp.float32)
    return {
        "x": nrm(ks[0], (BATCH, SEQ, D_MODEL), 1.0),
        "norm1_g": 1.0 + nrm(ks[1], (L, D_MODEL), 0.02),
        "w_in": nrm(ks[2], (L, D_MODEL, PROJ_WIDTH), D_MODEL ** -0.5),
        "gla_gate_w2": nrm(ks[3], (L, GLA_GATE_RANK, GLA_DK), GLA_GATE_RANK ** -0.5),
        "gla_gate_b": nrm(ks[4], (L, GLA_DK), 0.1),
        "gla_norm_g": 1.0 + nrm(ks[5], (L, GLA_HEAD_V), 0.02),
        "s5_a_re": -0.5 + nrm(ks[6], (L, S5_GROUPS, S5_STATE), 0.01),
        "s5_a_im": math.pi * n_idx + nrm(ks[7], (L, S5_GROUPS, S5_STATE), 0.01),
        "s5_b_re": nrm(ks[8], (L, S5_GROUPS, S5_STATE, S5_GROUP), (2 * S5_GROUP) ** -0.5),
        "s5_b_im": nrm(ks[9], (L, S5_GROUPS, S5_STATE, S5_GROUP), (2 * S5_GROUP) ** -0.5),
        "s5_c_re": nrm(ks[10], (L, S5_GROUPS, S5_GROUP, S5_STATE), S5_STATE ** -0.5),
        "s5_c_im": nrm(ks[11], (L, S5_GROUPS, S5_GROUP, S5_STATE), S5_STATE ** -0.5),
        "s5_d": nrm(ks[12], (L, S5_GROUPS, S5_GROUP), 0.5),
        "s5_log_step": jax.random.uniform(ks[13], (L, S5_GROUPS), jnp.float32,
                                          math.log(S5_DT_MIN), math.log(S5_DT_MAX)),
        "s5_glu_w": nrm(ks[14], (L, S5_WIDTH, S5_WIDTH), S5_WIDTH ** -0.5),
        "s5_glu_b": nrm(ks[15], (L, S5_WIDTH), 0.02),
        "s5_norm_g": 1.0 + nrm(ks[16], (L, S5_WIDTH), 0.02),
        "w_out": nrm(ks[17], (L, D_MIX, D_MODEL), D_MIX ** -0.5),
        "norm2_g": 1.0 + nrm(ks[18], (L, D_MODEL), 0.02),
        "router_w": nrm(ks[19], (L, D_MODEL, N_EXPERTS), D_MODEL ** -0.5),
        "router_b": nrm(ks[20], (L, N_EXPERTS), 0.01),
        "expert_w1": nrm(ks[21], (L, N_EXPERTS, D_MODEL, 2 * D_FF_EXPERT), D_MODEL ** -0.5),
        "expert_b1": nrm(ks[22], (L, N_EXPERTS, 2 * D_FF_EXPERT), 0.02),
        "expert_w2": nrm(ks[23], (L, N_EXPERTS, D_FF_EXPERT, D_MODEL), D_FF_EXPERT ** -0.5),
        "expert_b2": nrm(ks[24], (L, N_EXPERTS, D_MODEL), 0.02),
        "final_norm_g": 1.0 + nrm(ks[25], (D_MODEL,), 0.02),
    }


def reference(x, norm1_g, w_in, gla_gate_w2, gla_gate_b, gla_norm_g, s5_a_re, s5_a_im, s5_b_re, s5_b_im,
              s5_c_re, s5_c_im, s5_d, s5_log_step, s5_glu_w, s5_glu_b, s5_norm_g, w_out, norm2_g,
              router_w, router_b, expert_w1, expert_b1, expert_w2, expert_b2, final_norm_g):
    h = x
    for l in range(DEPTH):
        hn = rms_norm(h, norm1_g[l])
        h = h + hybrid_mixer(hn, w_in[l], gla_gate_w2[l], gla_gate_b[l], gla_norm_g[l],
                             s5_a_re[l], s5_a_im[l], s5_b_re[l], s5_b_im[l], s5_c_re[l], s5_c_im[l],
                             s5_d[l], s5_log_step[l], s5_glu_w[l], s5_glu_b[l], s5_norm_g[l], w_out[l])
        hn = rms_norm(h, norm2_g[l])
        h = h + moe_ffn(hn, router_w[l], router_b[l], expert_w1[l], expert_b1[l], expert_w2[l], expert_b2[l])
    return rms_norm(h, final_norm_g)
```

```python
import functools
import math

import jax
import jax.numpy as jnp
from jax import lax
from jax.experimental import pallas as pl
from jax.experimental.pallas import tpu as pltpu

F32 = jnp.float32
BF16 = jnp.bfloat16
HIGHEST = lax.Precision.HIGHEST

T = 8192
D = 2048
GLA_H = 4
GLA_DK = 128
GLA_DV = 256
GLA_CHUNK = 64
GLA_SUB = 16
GATE_RANK = 16
GATE_TAU = 16.0
S5_G = 64
S5_N = 16
S5_P = 64
S5_L = 16
S5_SEG = 8
N_EXP = 32
TOP_K = 4
D_FF = 2048
SWIGLU_ALPHA = 1.702
SWIGLU_LIMIT = 7.0
RMS_EPS = 1e-6

MOE_RB = 256
MOE_RMAX = 1536
MOE_TF = 256
MOE_NF = D_FF // MOE_TF
MOE_NV = -(-T * TOP_K // MOE_RMAX) + N_EXP
MOE_ROWS = T * TOP_K + N_EXP * MOE_RB
VMEM_LIMIT = 56 * 1024 * 1024

NN = (((1,), (0,)), ((), ()))
NT = (((1,), (1,)), ((), ()))
TN = (((0,), (0,)), ((), ()))


def _dot(a, b, dims=NN, precision=None):
    return lax.dot_general(a, b, dims, preferred_element_type=F32, precision=precision)


def _rms(x, g):
    return x * lax.rsqrt(jnp.mean(x * x, axis=-1, keepdims=True) + RMS_EPS) * g


def _sigmoid(x):
    return 1.0 / (1.0 + jnp.exp(-x))


def _const_spec(shape):
    nd = len(shape)
    return pl.BlockSpec(shape, lambda *_: (0,) * nd, pipeline_mode=pl.Buffered(1))


def _proj_kernel(x_ref, g1_ref, wq_ref, wg_ref, wu_ref, w2_ref, gb_ref, qkvr_ref, lg_ref, u_ref):
    hb = _rms(x_ref[...], g1_ref[...]).astype(BF16)
    p = _dot(hb, wq_ref[...])
    nq = GLA_H * GLA_DK
    qkvr_ref[:, :nq] = (p[:, :nq] * (GLA_DK ** -0.5)).astype(BF16)
    qkvr_ref[:, nq:] = p[:, nq:].astype(BF16)
    glr = _dot(hb, wg_ref[...])
    z = _dot(glr, w2_ref[...], precision=HIGHEST) + gb_ref[...]
    lg_ref[...] = (jnp.minimum(z, 0.0) - jnp.log(1.0 + jnp.exp(-jnp.abs(z)))) * (1.0 / GATE_TAU)
    u_ref[...] = _dot(hb, wu_ref[...]).astype(BF16)


def _proj(x, g1, wq, wg, wu, w2, gb):
    tm = 256
    nqkvr = wq.shape[1]
    return pl.pallas_call(
        _proj_kernel,
        grid=(T // tm,),
        in_specs=[
            pl.BlockSpec((tm, D), lambda i: (i, 0)),
            _const_spec((1, D)),
            _const_spec(wq.shape), _const_spec(wg.shape), _const_spec(wu.shape),
            _const_spec(w2.shape), _const_spec(gb.shape),
        ],
        out_specs=[
            pl.BlockSpec((tm, nqkvr), lambda i: (i, 0)),
            pl.BlockSpec((tm, GLA_H * GLA_DK), lambda i: (i, 0)),
            pl.BlockSpec((tm, S5_G * S5_N), lambda i: (i, 0)),
        ],
        out_shape=[
            jax.ShapeDtypeStruct((T, nqkvr), BF16),
            jax.ShapeDtypeStruct((T, GLA_H * GLA_DK), F32),
            jax.ShapeDtypeStruct((T, S5_G * S5_N), BF16),
        ],
        compiler_params=pltpu.CompilerParams(
            dimension_semantics=("arbitrary",), vmem_limit_bytes=VMEM_LIMIT),
        name="proj",
    )(x, g1, wq, wg, wu, w2, gb)


def _gla_kernel(q_ref, k_ref, v_ref, r_ref, lg_ref, gn_ref, tril_ref, esel_ref, o_ref, state_ref):
    C, S = GLA_CHUNK, GLA_SUB
    nsub = C // S

    @pl.when(pl.program_id(1) == 0)
    def _():
        state_ref[...] = jnp.zeros_like(state_ref)

    row = lax.broadcasted_iota(jnp.int32, (C, C), 0)
    col = lax.broadcasted_iota(jnp.int32, (C, C), 1)
    below = (col // S) < (row // S)
    diag = ((col // S) == (row // S)) & (col <= row)
    tril = tril_ref[...]
    esel = esel_ref[...]
    gn = gn_ref[...]

    def chunk(c, carry):
        rows = pl.ds(pl.multiple_of(c * C, C), C)
        q = q_ref[rows, :].astype(F32)
        k = k_ref[rows, :].astype(F32)
        v = v_ref[rows, :]
        b = _dot(tril, lg_ref[rows, :], precision=HIGHEST)
        bl = b[C - 1:C, :]
        st = state_ref[...]
        o = _dot((q * jnp.exp(b)).astype(BF16), st.astype(BF16), NT)
        s_rows = [jnp.zeros((S, C), F32)]
        for j in range(1, nsub):
            ref = b[j * S - 1:j * S, :]
            qj = (q[j * S:(j + 1) * S] * jnp.exp(b[j * S:(j + 1) * S] - ref)).astype(BF16)
            kj = (k * jnp.exp(jnp.minimum(ref - b, 0.0))).astype(BF16)
            s_rows.append(_dot(qj, kj, NT))
        s_off = jnp.concatenate(s_rows, axis=0)
        q3 = q.reshape(nsub, S, GLA_DK)
        k3 = k.reshape(nsub, S, GLA_DK)
        b3 = b.reshape(nsub, S, GLA_DK)
        xs = []
        for s in range(S):
            dec = jnp.exp(jnp.minimum(b3 - b3[:, s:s + 1, :], 0.0))
            xs.append((q3 * k3[:, s:s + 1, :] * dec).reshape(C, GLA_DK).astype(BF16))
        dsc = _dot(jnp.concatenate(xs, axis=1), esel)
        a = jnp.where(below, s_off, jnp.where(diag, dsc, 0.0))
        o = o + _dot(a.astype(BF16), v)
        kout = (k * jnp.exp(bl - b)).astype(BF16)
        state_ref[...] = st * jnp.exp(bl) + _dot(v, kout, TN)
        y = _rms(o, gn)
        r = r_ref[rows, :].astype(F32)
        o_ref[rows, :] = (y * (r * _sigmoid(r))).astype(BF16)
        return carry

    lax.fori_loop(0, q_ref.shape[0] // C, chunk, 0)


def _gla(qkvr, lg, gn):
    tb = 512
    C, S = GLA_CHUNK, GLA_SUB
    tril = jnp.tril(jnp.ones((C, C), F32))
    esel = (jnp.arange(S * GLA_DK)[:, None] // GLA_DK == jnp.arange(C)[None, :] % S).astype(BF16)
    return pl.pallas_call(
        _gla_kernel,
        grid=(GLA_H, T // tb),
        in_specs=[
            pl.BlockSpec((tb, GLA_DK), lambda h, i: (i, h)),
            pl.BlockSpec((tb, GLA_DK), lambda h, i: (i, GLA_H + h)),
            pl.BlockSpec((tb, GLA_DV), lambda h, i: (i, GLA_H + h)),
            pl.BlockSpec((tb, GLA_DV), lambda h, i: (i, 2 * GLA_H + h)),
            pl.BlockSpec((tb, GLA_DK), lambda h, i: (i, h)),
            _const_spec((1, GLA_DV)), _const_spec((C, C)), _const_spec((S * GLA_DK, C)),
        ],
        out_specs=pl.BlockSpec((tb, GLA_DV), lambda h, i: (i, h)),
        out_shape=jax.ShapeDtypeStruct((T, GLA_H * GLA_DV), BF16),
        scratch_shapes=[pltpu.VMEM((GLA_DV, GLA_DK), F32)],
        compiler_params=pltpu.CompilerParams(
            dimension_semantics=("arbitrary", "arbitrary"), vmem_limit_bytes=VMEM_LIMIT),
        name="gla",
    )(qkvr, qkvr, qkvr, qkvr, lg, gn, tril, esel)


def _s5_tables(a_re, a_im, b_re, b_im, c_re, c_im, d_skip, log_step):
    L, G, P, N = S5_L, S5_G, S5_P, S5_N
    delta = jnp.exp(log_step)[:, None]
    ar, ai = a_re * delta, a_im * delta

    def apow(tau):
        tau = jnp.asarray(tau, F32)[None, :, None]
        mag = jnp.exp(ar[:, None, :] * tau)
        ph = ai[:, None, :] * tau
        return mag * jnp.cos(ph), mag * jnp.sin(ph)

    p_re, p_im = apow(jnp.arange(L + 1))
    ab_re, ab_im = p_re[:, 1], p_im[:, 1]
    den = a_re * a_re + a_im * a_im
    cf_re = ((ab_re - 1.0) * a_re + ab_im * a_im) / den
    cf_im = (ab_im * a_re - (ab_re - 1.0) * a_im) / den
    bb_re = cf_re[:, :, None] * b_re - cf_im[:, :, None] * b_im
    bb_im = cf_re[:, :, None] * b_im + cf_im[:, :, None] * b_re
    e_re = p_re[:, :L, :, None] * bb_re[:, None] - p_im[:, :L, :, None] * bb_im[:, None]
    e_im = p_re[:, :L, :, None] * bb_im[:, None] + p_im[:, :L, :, None] * bb_re[:, None]
    kk = (jnp.einsum('gnp,gtpm->gtnm', c_re, e_re, precision=HIGHEST)
          - jnp.einsum('gnp,gtpm->gtnm', c_im, e_im, precision=HIGHEST))
    lag = jnp.arange(L)[None, :] - jnp.arange(L)[:, None]
    m5 = jnp.where((lag >= 0)[None, :, :, None, None], kk[:, jnp.clip(lag, 0, L - 1)], 0.0)
    eye = (jnp.eye(L)[:, :, None, None] * jnp.eye(N)[None, None]) * d_skip[:, None, None, :, None]
    mt = (m5 + eye).transpose(0, 1, 4, 2, 3).reshape(G, L * N, L * N)
    r_re, r_im = p_re[:, L - 1::-1][:, :L], p_im[:, L - 1::-1][:, :L]
    bs_re = (r_re[:, :, None, :] * bb_re.transpose(0, 2, 1)[:, None] - r_im[:, :, None, :] * bb_im.transpose(0, 2, 1)[:, None])
    bs_im = (r_re[:, :, None, :] * bb_im.transpose(0, 2, 1)[:, None] + r_im[:, :, None, :] * bb_re.transpose(0, 2, 1)[:, None])
    bs_re = bs_re.reshape(G, L * N, P)
    bs_im = bs_im.reshape(G, L * N, P)
    ct_re = c_re.transpose(0, 2, 1)[:, :, None, :]
    ct_im = c_im.transpose(0, 2, 1)[:, :, None, :]
    q_re = p_re[:, 1:].transpose(0, 2, 1)[:, :, :, None]
    q_im = p_im[:, 1:].transpose(0, 2, 1)[:, :, :, None]
    cre = (ct_re * q_re - ct_im * q_im).reshape(G, P, L * N)
    cim = -(ct_re * q_im + ct_im * q_re).reshape(G, P, L * N)

    def pair_rows(w):
        w = w.reshape(G // 2, 2, w.shape[1], w.shape[2])
        z = jnp.zeros_like(w[:, 0])
        return jnp.concatenate([jnp.concatenate([w[:, 0], z], axis=2),
                                jnp.concatenate([z, w[:, 1]], axis=2)], axis=1)

    def pair_vec(w):
        w = w.reshape(G // 2, 2, w.shape[1], P)
        return jnp.concatenate([w[:, 0], w[:, 1]], axis=-1)

    nchunk = T // L
    nv = nchunk // S5_SEG
    al_re, al_im = apow(jnp.asarray([L]))
    sg_re, sg_im = apow(jnp.asarray([L * nv * (2 ** i) for i in range(3)]))
    wv_re, wv_im = apow(jnp.arange(nv) * L)
    wv_re = jnp.repeat(wv_re, S5_SEG, axis=1)
    wv_im = jnp.repeat(wv_im, S5_SEG, axis=1)
    return dict(
        mt=mt.reshape(G // 2, 2, L * N, L * N).astype(BF16),
        bs_re=pair_rows(bs_re).astype(BF16), bs_im=pair_rows(bs_im).astype(BF16),
        cre=pair_rows(cre).astype(BF16), cim=pair_rows(cim).astype(BF16),
        al_re=pair_vec(al_re), al_im=pair_vec(al_im),
        sg_re=pair_vec(sg_re), sg_im=pair_vec(sg_im),
        wv_re=pair_vec(wv_re), wv_im=pair_vec(wv_im),
    )


def _s5_kernel(u_ref, mt_ref, bsr_ref, bsi_ref, cre_ref, cim_ref, alr_ref, ali_ref, sgr_ref, sgi_ref,
               wvr_ref, wvi_ref, y_ref, sre_ref, sim_ref, gre_ref, gim_ref):
    R = S5_SEG
    nrow = u_ref.shape[1]
    nv = nrow // R
    half = S5_L * S5_N
    u = u_ref[0]
    sre_ref[...] = _dot(u, bsr_ref[0])
    sim_ref[...] = _dot(u, bsi_ref[0])
    ar = jnp.broadcast_to(alr_ref[0], (R, 2 * S5_P))
    ai = jnp.broadcast_to(ali_ref[0], (R, 2 * S5_P))

    def step(v, h):
        hr, hi = h
        rows = pl.ds(pl.multiple_of(v * R, R), R)
        gre_ref[rows, :] = hr
        gim_ref[rows, :] = hi
        return (ar * hr - ai * hi + sre_ref[rows, :], ar * hi + ai * hr + sim_ref[rows, :])

    zero = jnp.zeros((R, 2 * S5_P), F32)
    er, ei = lax.fori_loop(0, nv, step, (zero, zero))
    sub = lax.broadcasted_iota(jnp.int32, (R, 2 * S5_P), 0)

    def shift(x, d):
        return jnp.where(sub >= d, pltpu.roll(x, d, 0), 0.0)

    zr, zi = shift(er, 1), shift(ei, 1)
    for i in range(3):
        mr, mi = sgr_ref[0, i:i + 1, :], sgi_ref[0, i:i + 1, :]
        pr, pi = shift(zr, 2 ** i), shift(zi, 2 ** i)
        zr, zi = zr + mr * pr - mi * pi, zi + mr * pi + mi * pr
    zr = jnp.tile(zr, (nv, 1))
    zi = jnp.tile(zi, (nv, 1))
    wr, wi = wvr_ref[0], wvi_ref[0]
    g_re = gre_ref[...] + wr * zr - wi * zi
    g_im = gim_ref[...] + wr * zi + wi * zr
    yi = _dot(g_re.astype(BF16), cre_ref[0]) + _dot(g_im.astype(BF16), cim_ref[0])
    y_ref[0, :, :half] = _dot(u[:, :half], mt_ref[0, 0]) + yi[:, :half]
    y_ref[0, :, half:] = _dot(u[:, half:], mt_ref[0, 1]) + yi[:, half:]


def _s5(u, tb):
    L, R = S5_L, S5_SEG
    nchunk = T // L
    nv = nchunk // R
    npair = S5_G // 2
    wlane = 2 * L * S5_N
    up = (u.reshape(R, nv, L, npair, 2, S5_N).transpose(3, 1, 0, 4, 2, 5).reshape(npair, nchunk, wlane))

    def pair_spec(a):
        return pl.BlockSpec((1,) + a.shape[1:], lambda g: (g,) + (0,) * (a.ndim - 1))

    names = ["mt", "bs_re", "bs_im", "cre", "cim", "al_re", "al_im", "sg_re", "sg_im", "wv_re", "wv_im"]
    ws = [tb[n] for n in names]
    y = pl.pallas_call(
        _s5_kernel,
        grid=(npair,),
        in_specs=[pair_spec(up)] + [pair_spec(w) for w in ws],
        out_specs=pl.BlockSpec((1, nchunk, wlane), lambda g: (g, 0, 0)),
        out_shape=jax.ShapeDtypeStruct((npair, nchunk, wlane), F32),
        scratch_shapes=[pltpu.VMEM((nchunk, 2 * S5_P), F32)] * 4,
        compiler_params=pltpu.CompilerParams(
            dimension_semantics=("arbitrary",), vmem_limit_bytes=VMEM_LIMIT),
        name="s5",
    )(up, *ws)
    return (y.reshape(npair, nv, R, 2, L, S5_N).transpose(2, 1, 4, 0, 3, 5).reshape(T, S5_G * S5_N))


def _post_kernel(y_ref, gla_ref, x_ref, gw_ref, gb_ref, sg_ref, wo_ref, n2_ref, rw_ref, rb_ref,
                 h_ref, hn_ref, idx_ref, gate_ref):
    y = y_ref[...]
    z = 0.5 * y * (1.0 + jnp.tanh(math.sqrt(2.0 / math.pi) * (y + 0.044715 * (y * y * y))))
    z = z * _sigmoid(_dot(z.astype(BF16), gw_ref[...]) + gb_ref[...])
    s5o = _rms(z, sg_ref[...]).astype(BF16)
    half = GLA_H * GLA_DV
    h = x_ref[...] + _dot(gla_ref[...], wo_ref[:half, :]) + _dot(s5o, wo_ref[half:, :])
    h_ref[...] = h
    hn = _rms(h, n2_ref[...])
    hn_ref[...] = hn
    lt = _dot(rw_ref[...], hn, NT, precision=HIGHEST) + rb_ref[...]
    eid = lax.broadcasted_iota(jnp.int32, lt.shape, 0).astype(F32)
    vals, idxs = [], []
    for _ in range(TOP_K):
        m = jnp.max(lt, axis=0, keepdims=True)
        sel = jnp.min(jnp.where(lt == m, eid, float(N_EXP)), axis=0, keepdims=True)
        vals.append(m)
        idxs.append(sel)
        lt = jnp.where(eid == sel, -jnp.inf, lt)
    ex = [jnp.exp(vv - vals[0]) for vv in vals]
    inv = 1.0 / (ex[0] + ex[1] + ex[2] + ex[3])
    idx_ref[...] = jnp.concatenate(idxs, axis=0).astype(jnp.int32)
    gate_ref[...] = jnp.concatenate([e * inv for e in ex], axis=0)


def _post(y, gla, x, gw, gb, sg, wo, n2, rwt, rb):
    tm = 256
    width = S5_G * S5_N
    return pl.pallas_call(
        _post_kernel,
        grid=(T // tm,),
        in_specs=[
            pl.BlockSpec((tm, width), lambda i: (i, 0)),
            pl.BlockSpec((tm, GLA_H * GLA_DV), lambda i: (i, 0)),
            pl.BlockSpec((tm, D), lambda i: (i, 0)),
            _const_spec(gw.shape), _const_spec(gb.shape), _const_spec(sg.shape), _const_spec(wo.shape),
            _const_spec(n2.shape), _const_spec(rwt.shape), _const_spec(rb.shape),
        ],
        out_specs=[
            pl.BlockSpec((tm, D), lambda i: (i, 0)),
            pl.BlockSpec((tm, D), lambda i: (i, 0)),
            pl.BlockSpec((TOP_K, tm), lambda i: (0, i)),
            pl.BlockSpec((TOP_K, tm), lambda i: (0, i)),
        ],
        out_shape=[
            jax.ShapeDtypeStruct((T, D), F32),
            jax.ShapeDtypeStruct((T, D), F32),
            jax.ShapeDtypeStruct((TOP_K, T), jnp.int32),
            jax.ShapeDtypeStruct((TOP_K, T), F32),
        ],
        compiler_params=pltpu.CompilerParams(
            dimension_semantics=("arbitrary",), vmem_limit_bytes=VMEM_LIMIT),
        name="post",
    )(y, gla, x, gw, gb, sg, wo, n2, rwt, rb)


def _route(top_idx):
    e_flat = top_idx.reshape(-1)
    onehot = (e_flat[:, None] == jnp.arange(N_EXP, dtype=jnp.int32)[None, :]).astype(jnp.int32)
    csum = jnp.cumsum(onehot, axis=0)
    rank = jnp.sum(csum * onehot, axis=1) - 1
    counts = csum[-1]
    padded = (counts + MOE_RB - 1) // MOE_RB * MOE_RB
    pad_end = jnp.cumsum(padded)
    pad_start = pad_end - padded
    dest = (pad_start[e_flat] + rank).astype(jnp.int32)
    nvis_e = (counts + MOE_RMAX - 1) // MOE_RMAX
    vis_end = jnp.cumsum(nvis_e)
    n_vis = vis_end[-1]
    v = jnp.arange(MOE_NV, dtype=jnp.int32)
    vc = jnp.minimum(v, n_vis - 1)
    ve = jnp.minimum(jnp.searchsorted(vis_end, vc, side='right'), N_EXP - 1).astype(jnp.int32)
    local = vc - (vis_end[ve] - nvis_e[ve])
    vcnt = jnp.where(v < n_vis, jnp.minimum(MOE_RMAX, counts[ve] - local * MOE_RMAX), 0)
    vrow = pad_start[ve] + local * MOE_RMAX
    return dict(dest=dest, fill_from=(pad_start + counts).astype(jnp.int32), fill_to=pad_end.astype(jnp.int32),
                tail_blk=(pad_end[-1:] // MOE_RB).astype(jnp.int32),
                vis_e=ve, vis_cnt=vcnt.astype(jnp.int32), vis_row=vrow.astype(jnp.int32),
                n_vis=n_vis.reshape(1).astype(jnp.int32))


def _row_copy(src_ref, srow, dst_ref, drow, sem):
    return pltpu.make_async_copy(src_ref.at[pl.ds(srow, 1), :], dst_ref.at[pl.ds(drow, 1), :], sem)


def _tail_fill(src_ref, dst_ref, tail_ref, sem):
    def cp(b):
        return pltpu.make_async_copy(src_ref.at[pl.ds(0, MOE_RB), :],
                                     dst_ref.at[pl.ds(pl.multiple_of(b * MOE_RB, MOE_RB), MOE_RB), :], sem)

    def start(b, c):
        cp(b).start()
        return c
    lax.fori_loop(tail_ref[0], MOE_ROWS // MOE_RB, start, 0)

    def finish(b, c):
        cp(b).wait()
        return c
    lax.fori_loop(tail_ref[0], MOE_ROWS // MOE_RB, finish, 0)


def _scatter_kernel(dest_ref, from_ref, to_ref, tail_ref, hn_ref, xs_ref, zero_ref, sem, zsem):
    i = pl.program_id(0)
    tm = hn_ref.shape[0]

    @pl.when(i == 0)
    def _():
        zero_ref[...] = jnp.zeros_like(zero_ref)
        _tail_fill(zero_ref, xs_ref, tail_ref, zsem)

        def per_expert(e, c):
            def fill(r, c2):
                _row_copy(zero_ref, 0, xs_ref, r, zsem).start()
                return c2
            lax.fori_loop(from_ref[e], to_ref[e], fill, 0)

            def drain(r, c2):
                _row_copy(zero_ref, 0, xs_ref, r, zsem).wait()
                return c2
            lax.fori_loop(from_ref[e], to_ref[e], drain, 0)
            return c
        lax.fori_loop(0, N_EXP, per_expert, 0)

    def issue(t, c):
        for k in range(TOP_K):
            _row_copy(hn_ref, t, xs_ref, dest_ref[k * T + i * tm + t], sem).start()
        return c
    lax.fori_loop(0, tm, issue, 0)

    def drain(t, c):
        for k in range(TOP_K):
            _row_copy(hn_ref, t, xs_ref, dest_ref[k * T + i * tm + t], sem).wait()
        return c
    lax.fori_loop(0, tm, drain, 0)


def _scatter(hn, rt):
    tm = 256
    return pl.pallas_call(
        _scatter_kernel,
        grid_spec=pltpu.PrefetchScalarGridSpec(
            num_scalar_prefetch=4,
            grid=(T // tm,),
            in_specs=[pl.BlockSpec((tm, D), lambda i, *_: (i, 0))],
            out_specs=pl.BlockSpec(memory_space=pl.ANY),
            scratch_shapes=[pltpu.VMEM((MOE_RB, D), F32), pltpu.SemaphoreType.DMA, pltpu.SemaphoreType.DMA],
        ),
        out_shape=jax.ShapeDtypeStruct((MOE_ROWS, D), F32),
        compiler_params=pltpu.CompilerParams(
            dimension_semantics=("arbitrary",), vmem_limit_bytes=VMEM_LIMIT),
        name="scatter",
    )(rt["dest"], rt["fill_from"], rt["fill_to"], rt["tail_blk"], hn)


def _experts_kernel(ve_ref, vcnt_ref, vrow_ref, nvis_ref, tail_ref, xs_ref, w1g_ref, w1l_ref, w2_ref, b1g_ref,
                    b1l_ref, b2_ref, ys_ref, xin_ref, xb_ref, acc_ref, isem, osem):
    v = pl.program_id(0)
    j = pl.program_id(1)
    RB = MOE_RB
    active = v < nvis_ref[0]
    nblk = (vcnt_ref[v] + RB - 1) // RB
    row0 = vrow_ref[v]

    def blk_in(i):
        return pltpu.make_async_copy(xs_ref.at[pl.ds(pl.multiple_of(row0 + i * RB, RB), RB), :],
                                     xin_ref.at[pl.ds(pl.multiple_of(i * RB, RB), RB), :], isem.at[i])

    def blk_out(i):
        return pltpu.make_async_copy(acc_ref.at[pl.ds(pl.multiple_of(i * RB, RB), RB), :],
                                     ys_ref.at[pl.ds(pl.multiple_of(row0 + i * RB, RB), RB), :], osem)

    @pl.when((v == 0) & (j == 0))
    def _():
        acc_ref[pl.ds(0, RB), :] = jnp.zeros((RB, D), F32)
        _tail_fill(acc_ref, ys_ref, tail_ref, osem)

    @pl.when(active & (j == 0))
    def _():
        def start(i, c):
            blk_in(i).start()
            return c
        lax.fori_loop(0, nblk, start, 0)

        def finish(i, c):
            blk_in(i).wait()
            rows = pl.ds(pl.multiple_of(i * RB, RB), RB)
            xb_ref[rows, :] = xin_ref[rows, :].astype(BF16)
            return c
        lax.fori_loop(0, nblk, finish, 0)

    @pl.when(active)
    def _():
        w1g = w1g_ref[0].astype(BF16)
        w1l = w1l_ref[0].astype(BF16)
        w2 = w2_ref[0].astype(BF16)
        b1g = b1g_ref[0]
        b1l = b1l_ref[0]

        def block(i, c):
            rows = pl.ds(pl.multiple_of(i * RB, RB), RB)
            x = xb_ref[rows, :]
            glu = jnp.minimum(_dot(x, w1g) + b1g, SWIGLU_LIMIT)
            lin = jnp.clip(_dot(x, w1l) + b1l, -SWIGLU_LIMIT, SWIGLU_LIMIT)
            act = glu * _sigmoid(SWIGLU_ALPHA * glu) * (lin + 1.0)
            part = _dot(act.astype(BF16), w2)

            @pl.when(j == 0)
            def _():
                acc_ref[rows, :] = part + b2_ref[0]

            @pl.when(j > 0)
            def _():
                acc_ref[rows, :] += part
            return c
        lax.fori_loop(0, nblk, block, 0)

    @pl.when(active & (j == MOE_NF - 1))
    def _():
        def start(i, c):
            blk_out(i).start()
            return c
        lax.fori_loop(0, nblk, start, 0)

        def finish(i, c):
            blk_out(i).wait()
            return c
        lax.fori_loop(0, nblk, finish, 0)


def _experts(xs, rt, w1, b1, w2, b2):
    b1 = b1.reshape(N_EXP, 1, 2 * D_FF)
    b2 = b2.reshape(N_EXP, 1, D)

    def jj(v, j, ve, vcnt, vrow, nvis, tail):
        return jnp.where(v < nvis[0], j, MOE_NF - 1)

    return pl.pallas_call(
        _experts_kernel,
        grid_spec=pltpu.PrefetchScalarGridSpec(
            num_scalar_prefetch=5,
            grid=(MOE_NV, MOE_NF),
            in_specs=[
                pl.BlockSpec(memory_space=pl.ANY),
                pl.BlockSpec((1, D, MOE_TF), lambda v, j, ve, *s: (ve[v], 0, jj(v, j, ve, *s))),
                pl.BlockSpec((1, D, MOE_TF), lambda v, j, ve, *s: (ve[v], 0, MOE_NF + jj(v, j, ve, *s))),
                pl.BlockSpec((1, MOE_TF, D), lambda v, j, ve, *s: (ve[v], jj(v, j, ve, *s), 0)),
                pl.BlockSpec((1, 1, MOE_TF), lambda v, j, ve, *s: (ve[v], 0, jj(v, j, ve, *s))),
                pl.BlockSpec((1, 1, MOE_TF), lambda v, j, ve, *s: (ve[v], 0, MOE_NF + jj(v, j, ve, *s))),
                pl.BlockSpec((1, 1, D), lambda v, j, ve, *s: (ve[v], 0, 0)),
            ],
            out_specs=pl.BlockSpec(memory_space=pl.ANY),
            scratch_shapes=[
                pltpu.VMEM((MOE_RMAX, D), F32),
                pltpu.VMEM((MOE_RMAX, D), BF16),
                pltpu.VMEM((MOE_RMAX, D), F32),
                pltpu.SemaphoreType.DMA((MOE_RMAX // MOE_RB,)),
                pltpu.SemaphoreType.DMA,
            ],
        ),
        out_shape=jax.ShapeDtypeStruct((MOE_ROWS, D), F32),
        compiler_params=pltpu.CompilerParams(
            dimension_semantics=("arbitrary", "arbitrary"), vmem_limit_bytes=VMEM_LIMIT),
        name="experts",
    )(rt["vis_e"], rt["vis_cnt"], rt["vis_row"], rt["n_vis"], rt["tail_blk"], xs, w1, w1, w2, b1, b1, b2)


def _combine_kernel(dest_ref, ys_ref, h_ref, gate_ref, gf_ref, o_ref, buf_ref, sem):
    i = pl.program_id(0)
    tm = h_ref.shape[0]

    def issue(t, c):
        for k in range(TOP_K):
            _row_copy(ys_ref, dest_ref[k * T + i * tm + t], buf_ref.at[k], t, sem).start()
        return c
    lax.fori_loop(0, tm, issue, 0)

    def drain(t, c):
        for k in range(TOP_K):
            _row_copy(ys_ref, dest_ref[k * T + i * tm + t], buf_ref.at[k], t, sem).wait()
        return c
    lax.fori_loop(0, tm, drain, 0)

    h = h_ref[...]
    for k in range(TOP_K):
        h = h + gate_ref[:, k:k + 1] * buf_ref[k]
    o_ref[...] = _rms(h, gf_ref[...])


def _combine(ys, h, gates_t, gf, rt):
    tm = 256
    return pl.pallas_call(
        _combine_kernel,
        grid_spec=pltpu.PrefetchScalarGridSpec(
            num_scalar_prefetch=1,
            grid=(T // tm,),
            in_specs=[
                pl.BlockSpec(memory_space=pl.ANY),
                pl.BlockSpec((tm, D), lambda i, *_: (i, 0)),
                pl.BlockSpec((tm, TOP_K), lambda i, *_: (i, 0)),
                pl.BlockSpec((1, D), lambda i, *_: (0, 0)),
            ],
            out_specs=pl.BlockSpec((tm, D), lambda i, *_: (i, 0)),
            scratch_shapes=[pltpu.VMEM((TOP_K, tm, D), F32), pltpu.SemaphoreType.DMA],
        ),
        out_shape=jax.ShapeDtypeStruct((T, D), F32),
        compiler_params=pltpu.CompilerParams(
            dimension_semantics=("arbitrary",), vmem_limit_bytes=VMEM_LIMIT),
        name="combine",
    )(rt["dest"], ys, h, gates_t, gf)


def kernel(x, norm1_g, w_in, gla_gate_w2, gla_gate_b, gla_norm_g, s5_a_re, s5_a_im, s5_b_re, s5_b_im, s5_c_re, s5_c_im, s5_d, s5_log_step, s5_glu_w, s5_glu_b, s5_norm_g, w_out, norm2_g, router_w, router_b, expert_w1, expert_b1, expert_w2, expert_b2, final_norm_g):
    assert x.shape == (1, T, D) and w_in.shape[0] == 1
    xt = x.reshape(T, D)
    nqkvr = 2 * GLA_H * GLA_DK + 2 * GLA_H * GLA_DV
    wi = w_in[0]
    wq = wi[:, :nqkvr].astype(BF16)
    wg = jnp.pad(wi[:, nqkvr:nqkvr + GATE_RANK], ((0, 0), (0, 128 - GATE_RANK))).astype(BF16)
    wu = wi[:, nqkvr + GATE_RANK:].astype(BF16)
    w2g = jnp.pad(gla_gate_w2[0], ((0, 128 - GATE_RANK), (0, 0)))
    qkvr, lg, u = _proj(xt, norm1_g, wq, wg, wu, w2g, gla_gate_b)
    gla = _gla(qkvr, lg, gla_norm_g)
    tables = _s5_tables(s5_a_re[0], s5_a_im[0], s5_b_re[0], s5_b_im[0], s5_c_re[0], s5_c_im[0],
                        s5_d[0], s5_log_step[0])
    y = _s5(u, tables)
    h, hn, top_idx, gates = _post(y, gla, xt, s5_glu_w[0].astype(BF16), s5_glu_b, s5_norm_g,
                                  w_out[0].astype(BF16), norm2_g, router_w[0].T, router_b.reshape(N_EXP, 1))
    rt = _route(top_idx)
    xs = _scatter(hn, rt)
    ys = _experts(xs, rt, expert_w1[0], expert_b1[0], expert_w2[0], expert_b2[0])
    out = _combine(ys, h, gates.T, final_norm_g.reshape(1, D), rt)
    return out.reshape(1, T, D)
```

```python
import functools
import math

import jax
import jax.numpy as jnp
from jax import lax
from jax.experimental import pallas as pl
from jax.experimental.pallas import tpu as pltpu

F32 = jnp.float32
BF16 = jnp.bfloat16
HIGHEST = lax.Precision.HIGHEST

T = 8192
D = 2048
GLA_H = 4
GLA_DK = 128
GLA_DV = 256
GLA_CHUNK = 64
GLA_SUB = 16
GATE_RANK = 16
GATE_TAU = 16.0
S5_G = 64
S5_N = 16
S5_P = 64
S5_L = 16
S5_GT = 8
N_EXP = 32
TOP_K = 4
D_FF = 2048
SWIGLU_ALPHA = 1.702
SWIGLU_LIMIT = 7.0
RMS_EPS = 1e-6

MOE_RB = 256
MOE_RMAX = 1536
MOE_TF = 512
MOE_NF = D_FF // MOE_TF
MOE_NV = -(-T * TOP_K // MOE_RMAX) + N_EXP
MOE_ROWS = T * TOP_K + N_EXP * MOE_RB
VMEM_LIMIT = 56 * 1024 * 1024

NN = (((1,), (0,)), ((), ()))
NT = (((1,), (1,)), ((), ()))
TN = (((0,), (0,)), ((), ()))


def _dot(a, b, dims=NN, precision=None):
    return lax.dot_general(a, b, dims, preferred_element_type=F32, precision=precision)


def _rms(x, g):
    return x * lax.rsqrt(jnp.mean(x * x, axis=-1, keepdims=True) + RMS_EPS) * g


def _sigmoid(x):
    return 1.0 / (1.0 + jnp.exp(-x))


def _const_spec(shape):
    nd = len(shape)
    return pl.BlockSpec(shape, lambda *_: (0,) * nd, pipeline_mode=pl.Buffered(1))


def _proj_kernel(x_ref, g1_ref, wq_ref, wg_ref, wu_ref, w2_ref, gb_ref, qkvr_ref, lg_ref, u_ref):
    hb = _rms(x_ref[...], g1_ref[...]).astype(BF16)
    p = _dot(hb, wq_ref[...])
    nq = GLA_H * GLA_DK
    qkvr_ref[:, :nq] = (p[:, :nq] * (GLA_DK ** -0.5)).astype(BF16)
    qkvr_ref[:, nq:] = p[:, nq:].astype(BF16)
    glr = _dot(hb, wg_ref[...])
    z = _dot(glr, w2_ref[...], precision=HIGHEST) + gb_ref[...]
    lg_ref[...] = (jnp.minimum(z, 0.0) - jnp.log(1.0 + jnp.exp(-jnp.abs(z)))) * (1.0 / GATE_TAU)
    u_ref[...] = _dot(hb, wu_ref[...])


def _proj(x, g1, wq, wg, wu, w2, gb):
    tm = 256
    nqkvr = wq.shape[1]
    return pl.pallas_call(
        _proj_kernel,
        grid=(T // tm,),
        in_specs=[
            pl.BlockSpec((tm, D), lambda i: (i, 0)),
            _const_spec((1, D)),
            _const_spec(wq.shape), _const_spec(wg.shape), _const_spec(wu.shape),
            _const_spec(w2.shape), _const_spec(gb.shape),
        ],
        out_specs=[
            pl.BlockSpec((tm, nqkvr), lambda i: (i, 0)),
            pl.BlockSpec((tm, GLA_H * GLA_DK), lambda i: (i, 0)),
            pl.BlockSpec((tm, S5_G * S5_N), lambda i: (i, 0)),
        ],
        out_shape=[
            jax.ShapeDtypeStruct((T, nqkvr), BF16),
            jax.ShapeDtypeStruct((T, GLA_H * GLA_DK), F32),
            jax.ShapeDtypeStruct((T, S5_G * S5_N), F32),
        ],
        compiler_params=pltpu.CompilerParams(
            dimension_semantics=("arbitrary",), vmem_limit_bytes=VMEM_LIMIT),
        name="proj",
    )(x, g1, wq, wg, wu, w2, gb)


def _gla_kernel(q_ref, k_ref, v_ref, r_ref, lg_ref, gn_ref, tril_ref, esel_ref, o_ref, state_ref):
    C, S = GLA_CHUNK, GLA_SUB
    nsub = C // S

    @pl.when(pl.program_id(1) == 0)
    def _():
        state_ref[...] = jnp.zeros_like(state_ref)

    row = lax.broadcasted_iota(jnp.int32, (C, C), 0)
    col = lax.broadcasted_iota(jnp.int32, (C, C), 1)
    below = (col // S) < (row // S)
    diag = ((col // S) == (row // S)) & (col <= row)
    tril = tril_ref[...]
    esel = esel_ref[...]
    gn = gn_ref[...]

    def chunk(c, carry):
        rows = pl.ds(pl.multiple_of(c * C, C), C)
        q = q_ref[rows, :].astype(F32)
        k = k_ref[rows, :].astype(F32)
        v = v_ref[rows, :]
        b = _dot(tril, lg_ref[rows, :], precision=HIGHEST)
        bl = b[C - 1:C, :]
        st = state_ref[...]
        o = _dot((q * jnp.exp(b)).astype(BF16), st.astype(BF16), NT)
        s_rows = [jnp.zeros((S, C), F32)]
        for j in range(1, nsub):
            ref = b[j * S - 1:j * S, :]
            qj = (q[j * S:(j + 1) * S] * jnp.exp(b[j * S:(j + 1) * S] - ref)).astype(BF16)
            kj = (k * jnp.exp(jnp.minimum(ref - b, 0.0))).astype(BF16)
            s_rows.append(_dot(qj, kj, NT))
        s_off = jnp.concatenate(s_rows, axis=0)
        q3 = q.reshape(nsub, S, GLA_DK)
        k3 = k.reshape(nsub, S, GLA_DK)
        b3 = b.reshape(nsub, S, GLA_DK)
        xs = []
        for s in range(S):
            dec = jnp.exp(jnp.minimum(b3 - b3[:, s:s + 1, :], 0.0))
            xs.append((q3 * k3[:, s:s + 1, :] * dec).reshape(C, GLA_DK).astype(BF16))
        dsc = _dot(jnp.concatenate(xs, axis=1), esel)
        a = jnp.where(below, s_off, jnp.where(diag, dsc, 0.0))
        o = o + _dot(a.astype(BF16), v)
        kout = (k * jnp.exp(bl - b)).astype(BF16)
        state_ref[...] = st * jnp.exp(bl) + _dot(v, kout, TN)
        y = _rms(o, gn)
        r = r_ref[rows, :].astype(F32)
        o_ref[rows, :] = (y * (r * _sigmoid(r))).astype(BF16)
        return carry

    lax.fori_loop(0, q_ref.shape[0] // C, chunk, 0)


def _gla(qkvr, lg, gn):
    tb = 512
    C, S = GLA_CHUNK, GLA_SUB
    tril = jnp.tril(jnp.ones((C, C), F32))
    esel = (jnp.arange(S * GLA_DK)[:, None] // GLA_DK == jnp.arange(C)[None, :] % S).astype(BF16)
    return pl.pallas_call(
        _gla_kernel,
        grid=(GLA_H, T // tb),
        in_specs=[
            pl.BlockSpec((tb, GLA_DK), lambda h, i: (i, h)),
            pl.BlockSpec((tb, GLA_DK), lambda h, i: (i, GLA_H + h)),
            pl.BlockSpec((tb, GLA_DV), lambda h, i: (i, GLA_H + h)),
            pl.BlockSpec((tb, GLA_DV), lambda h, i: (i, 2 * GLA_H + h)),
            pl.BlockSpec((tb, GLA_DK), lambda h, i: (i, h)),
            _const_spec((1, GLA_DV)), _const_spec((C, C)), _const_spec((S * GLA_DK, C)),
        ],
        out_specs=pl.BlockSpec((tb, GLA_DV), lambda h, i: (i, h)),
        out_shape=jax.ShapeDtypeStruct((T, GLA_H * GLA_DV), BF16),
        scratch_shapes=[pltpu.VMEM((GLA_DV, GLA_DK), F32)],
        compiler_params=pltpu.CompilerParams(
            dimension_semantics=("arbitrary", "arbitrary"), vmem_limit_bytes=VMEM_LIMIT),
        name="gla",
    )(qkvr, qkvr, qkvr, qkvr, lg, gn, tril, esel)


def _s5_tables(a_re, a_im, b_re, b_im, c_re, c_im, d_skip, log_step):
    L, G, P, N = S5_L, S5_G, S5_P, S5_N
    delta = jnp.exp(log_step)[:, None]
    ar, ai = a_re * delta, a_im * delta

    def apow(tau):
        tau = jnp.asarray(tau, F32)[None, :, None]
        mag = jnp.exp(ar[:, None, :] * tau)
        ph = ai[:, None, :] * tau
        return mag * jnp.cos(ph), mag * jnp.sin(ph)

    p_re, p_im = apow(jnp.arange(L + 1))
    ab_re, ab_im = p_re[:, 1], p_im[:, 1]
    den = a_re * a_re + a_im * a_im
    cf_re = ((ab_re - 1.0) * a_re + ab_im * a_im) / den
    cf_im = (ab_im * a_re - (ab_re - 1.0) * a_im) / den
    bb_re = cf_re[:, :, None] * b_re - cf_im[:, :, None] * b_im
    bb_im = cf_re[:, :, None] * b_im + cf_im[:, :, None] * b_re
    e_re = p_re[:, :L, :, None] * bb_re[:, None] - p_im[:, :L, :, None] * bb_im[:, None]
    e_im = p_re[:, :L, :, None] * bb_im[:, None] + p_im[:, :L, :, None] * bb_re[:, None]
    kk = (jnp.einsum('gnp,gtpm->gtnm', c_re, e_re, precision=HIGHEST)
          - jnp.einsum('gnp,gtpm->gtnm', c_im, e_im, precision=HIGHEST))
    kk = kk.at[:, 0].add(jnp.eye(N, dtype=F32)[None] * d_skip[:, :, None])
    r_re, r_im = apow(L - 1 - jnp.arange(L))
    bs_re = (r_re[:, :, None, :] * bb_re.transpose(0, 2, 1)[:, None] - r_im[:, :, None, :] * bb_im.transpose(0, 2, 1)[:, None])
    bs_im = (r_re[:, :, None, :] * bb_im.transpose(0, 2, 1)[:, None] + r_im[:, :, None, :] * bb_re.transpose(0, 2, 1)[:, None])
    ct_re = c_re.transpose(0, 2, 1)[:, :, None, :]
    ct_im = c_im.transpose(0, 2, 1)[:, :, None, :]
    q_re = p_re[:, 1:].transpose(0, 2, 1)[:, :, :, None]
    q_im = p_im[:, 1:].transpose(0, 2, 1)[:, :, :, None]
    cre = ct_re * q_re - ct_im * q_im
    cim = -(ct_re * q_im + ct_im * q_re)
    GT = S5_GT
    nt = G // GT
    eye = jnp.eye(GT, dtype=F32)

    def tile_groups(w):
        return w.reshape((nt, GT) + w.shape[1:])

    kt = tile_groups(kk).transpose(0, 1, 4, 2, 3)
    bd = (kt[:, :, :, :, None, :] * eye[None, :, None, None, :, None]).reshape(nt, GT * N, L * GT * N)

    def in_table(w):
        w = tile_groups(w).transpose(0, 2, 1, 3, 4)
        return (w[:, :, :, :, None, :] * eye[None, None, :, None, :, None]).reshape(nt, L * GT * N, GT * P)

    def out_table(w):
        w = tile_groups(w)
        return (w[:, :, :, :, None, :] * eye[None, :, None, None, :, None]).reshape(nt, GT * P, L * GT * N)

    nstep = (T // L - 1).bit_length()
    sc_re, sc_im = apow(jnp.asarray([L * 2 ** i for i in range(nstep)]))

    def lanes(w):
        return tile_groups(w).transpose(0, 2, 1, 3).reshape(nt, w.shape[1], GT * P)

    return dict(
        bd=bd.astype(BF16),
        bs_re=in_table(bs_re).astype(BF16), bs_im=in_table(bs_im).astype(BF16),
        cre=out_table(cre).astype(BF16), cim=out_table(cim).astype(BF16),
        sc_re=lanes(sc_re), sc_im=lanes(sc_im),
    )


def _s5_kernel(u_ref, bd_ref, bsr_ref, bsi_ref, cre_ref, cim_ref, scr_ref, sci_ref, y_ref, m_ref):
    L = S5_L
    W = S5_GT * S5_N
    nch = u_ref.shape[0] // L

    @pl.when(pl.program_id(0) == 0)
    def _():
        m_ref[...] = jnp.zeros_like(m_ref)

    for s in range(L):
        m_ref[s * W:(s + 1) * W, s * W:] = bd_ref[0, :, :(L - s) * W]
    ucat = jnp.concatenate([u_ref[pl.ds(s, nch, stride=L), :].astype(BF16) for s in range(L)], axis=1)
    y = _dot(ucat, m_ref[...])
    hr = _dot(ucat, bsr_ref[0])
    hi = _dot(ucat, bsi_ref[0])
    row = lax.broadcasted_iota(jnp.int32, hr.shape, 0)

    def shift(x, d):
        if d % 8 == 0:
            return jnp.concatenate([jnp.zeros((d, x.shape[1]), F32), x[:nch - d]], axis=0)
        return jnp.where(row >= d, pltpu.roll(x, d, 0), 0.0)

    for i in range(scr_ref.shape[1]):
        mr, mi = scr_ref[0, i:i + 1, :], sci_ref[0, i:i + 1, :]
        pr, pi = shift(hr, 2 ** i), shift(hi, 2 ** i)
        hr, hi = hr + mr * pr - mi * pi, hi + mr * pi + mi * pr
    gr, gi = shift(hr, 1), shift(hi, 1)
    y = y + _dot(gr.astype(BF16), cre_ref[0]) + _dot(gi.astype(BF16), cim_ref[0])
    for t in range(L):
        y_ref[pl.ds(t, nch, stride=L), :] = y[:, t * W:(t + 1) * W]


def _s5(u, tb):
    W = S5_GT * S5_N
    nt = S5_G // S5_GT

    def tile_spec(a):
        return pl.BlockSpec((1,) + a.shape[1:], lambda g: (g,) + (0,) * (a.ndim - 1))

    ws = [tb[n] for n in ["bd", "bs_re", "bs_im", "cre", "cim", "sc_re", "sc_im"]]
    return pl.pallas_call(
        _s5_kernel,
        grid=(nt,),
        in_specs=[pl.BlockSpec((T, W), lambda g: (0, g))] + [tile_spec(w) for w in ws],
        out_specs=pl.BlockSpec((T, W), lambda g: (0, g)),
        out_shape=jax.ShapeDtypeStruct((T, S5_G * S5_N), F32),
        scratch_shapes=[pltpu.VMEM((S5_L * W, S5_L * W), BF16)],
        compiler_params=pltpu.CompilerParams(
            dimension_semantics=("arbitrary",), vmem_limit_bytes=VMEM_LIMIT),
        name="s5",
    )(u, *ws)


def _post_kernel(y_ref, gla_ref, x_ref, gw_ref, gb_ref, sg_ref, wo_ref, n2_ref, rw_ref, rb_ref,
                 h_ref, hn_ref, idx_ref, gate_ref):
    y = y_ref[...]
    z = 0.5 * y * (1.0 + jnp.tanh(math.sqrt(2.0 / math.pi) * (y + 0.044715 * (y * y * y))))
    z = z * _sigmoid(_dot(z.astype(BF16), gw_ref[...]) + gb_ref[...])
    s5o = _rms(z, sg_ref[...]).astype(BF16)
    half = GLA_H * GLA_DV
    h = x_ref[...] + _dot(gla_ref[...], wo_ref[:half, :]) + _dot(s5o, wo_ref[half:, :])
    h_ref[...] = h
    hn = _rms(h, n2_ref[...])
    hn_ref[...] = hn
    lt = _dot(rw_ref[...], hn, NT, precision=HIGHEST) + rb_ref[...]
    eid = lax.broadcasted_iota(jnp.int32, lt.shape, 0).astype(F32)
    vals, idxs = [], []
    for _ in range(TOP_K):
        m = jnp.max(lt, axis=0, keepdims=True)
        sel = jnp.min(jnp.where(lt == m, eid, float(N_EXP)), axis=0, keepdims=True)
        vals.append(m)
        idxs.append(sel)
        lt = jnp.where(eid == sel, -jnp.inf, lt)
    ex = [jnp.exp(vv - vals[0]) for vv in vals]
    inv = 1.0 / (ex[0] + ex[1] + ex[2] + ex[3])
    idx_ref[...] = jnp.concatenate(idxs, axis=0).astype(jnp.int32)
    gate_ref[...] = jnp.concatenate([e * inv for e in ex], axis=0)


def _post(y, gla, x, gw, gb, sg, wo, n2, rwt, rb):
    tm = 256
    width = S5_G * S5_N
    return pl.pallas_call(
        _post_kernel,
        grid=(T // tm,),
        in_specs=[
            pl.BlockSpec((tm, width), lambda i: (i, 0)),
            pl.BlockSpec((tm, GLA_H * GLA_DV), lambda i: (i, 0)),
            pl.BlockSpec((tm, D), lambda i: (i, 0)),
            _const_spec(gw.shape), _const_spec(gb.shape), _const_spec(sg.shape), _const_spec(wo.shape),
            _const_spec(n2.shape), _const_spec(rwt.shape), _const_spec(rb.shape),
        ],
        out_specs=[
            pl.BlockSpec((tm, D), lambda i: (i, 0)),
            pl.BlockSpec((tm, D), lambda i: (i, 0)),
            pl.BlockSpec((TOP_K, tm), lambda i: (0, i)),
            pl.BlockSpec((TOP_K, tm), lambda i: (0, i)),
        ],
        out_shape=[
            jax.ShapeDtypeStruct((T, D), F32),
            jax.ShapeDtypeStruct((T, D), F32),
            jax.ShapeDtypeStruct((TOP_K, T), jnp.int32),
            jax.ShapeDtypeStruct((TOP_K, T), F32),
        ],
        compiler_params=pltpu.CompilerParams(
            dimension_semantics=("arbitrary",), vmem_limit_bytes=VMEM_LIMIT),
        name="post",
    )(y, gla, x, gw, gb, sg, wo, n2, rwt, rb)


def _route(top_idx):
    e_flat = top_idx.reshape(-1)
    onehot = (e_flat[:, None] == jnp.arange(N_EXP, dtype=jnp.int32)[None, :]).astype(jnp.int32)
    csum = jnp.cumsum(onehot, axis=0)
    rank = jnp.sum(csum * onehot, axis=1) - 1
    counts = csum[-1]
    padded = (counts + MOE_RB - 1) // MOE_RB * MOE_RB
    pad_end = jnp.cumsum(padded)
    pad_start = pad_end - padded
    dest = (pad_start[e_flat] + rank).astype(jnp.int32)
    nvis_e = (counts + MOE_RMAX - 1) // MOE_RMAX
    vis_end = jnp.cumsum(nvis_e)
    n_vis = vis_end[-1]
    v = jnp.arange(MOE_NV, dtype=jnp.int32)
    vc = jnp.minimum(v, n_vis - 1)
    ve = jnp.minimum(jnp.searchsorted(vis_end, vc, side='right'), N_EXP - 1).astype(jnp.int32)
    local = vc - (vis_end[ve] - nvis_e[ve])
    vcnt = jnp.where(v < n_vis, jnp.minimum(MOE_RMAX, counts[ve] - local * MOE_RMAX), 0)
    vrow = pad_start[ve] + local * MOE_RMAX
    return dict(dest=dest, fill_from=(pad_start + counts).astype(jnp.int32), fill_to=pad_end.astype(jnp.int32),
                tail_blk=(pad_end[-1:] // MOE_RB).astype(jnp.int32),
                vis_e=ve, vis_cnt=vcnt.astype(jnp.int32), vis_row=vrow.astype(jnp.int32),
                n_vis=n_vis.reshape(1).astype(jnp.int32))


def _row_copy(src_ref, srow, dst_ref, drow, sem):
    return pltpu.make_async_copy(src_ref.at[pl.ds(srow, 1), :], dst_ref.at[pl.ds(drow, 1), :], sem)


def _tail_fill(src_ref, dst_ref, tail_ref, sem):
    def cp(b):
        return pltpu.make_async_copy(src_ref.at[pl.ds(0, MOE_RB), :],
                                     dst_ref.at[pl.ds(pl.multiple_of(b * MOE_RB, MOE_RB), MOE_RB), :], sem)

    def start(b, c):
        cp(b).start()
        return c
    lax.fori_loop(tail_ref[0], MOE_ROWS // MOE_RB, start, 0)

    def finish(b, c):
        cp(b).wait()
        return c
    lax.fori_loop(tail_ref[0], MOE_ROWS // MOE_RB, finish, 0)


def _scatter_kernel(dest_ref, from_ref, to_ref, tail_ref, hn_ref, xs_ref, zero_ref, sem, zsem):
    i = pl.program_id(0)
    tm = hn_ref.shape[0]

    @pl.when(i == 0)
    def _():
        zero_ref[...] = jnp.zeros_like(zero_ref)
        _tail_fill(zero_ref, xs_ref, tail_ref, zsem)

        def per_expert(e, c):
            def fill(r, c2):
                _row_copy(zero_ref, 0, xs_ref, r, zsem).start()
                return c2
            lax.fori_loop(from_ref[e], to_ref[e], fill, 0)

            def drain(r, c2):
                _row_copy(zero_ref, 0, xs_ref, r, zsem).wait()
                return c2
            lax.fori_loop(from_ref[e], to_ref[e], drain, 0)
            return c
        lax.fori_loop(0, N_EXP, per_expert, 0)

    def issue(t, c):
        for k in range(TOP_K):
            _row_copy(hn_ref, t, xs_ref, dest_ref[k * T + i * tm + t], sem).start()
        return c
    lax.fori_loop(0, tm, issue, 0)

    def drain(t, c):
        for k in range(TOP_K):
            _row_copy(hn_ref, t, xs_ref, dest_ref[k * T + i * tm + t], sem).wait()
        return c
    lax.fori_loop(0, tm, drain, 0)


def _scatter(hn, rt):
    tm = 256
    return pl.pallas_call(
        _scatter_kernel,
        grid_spec=pltpu.PrefetchScalarGridSpec(
            num_scalar_prefetch=4,
            grid=(T // tm,),
            in_specs=[pl.BlockSpec((tm, D), lambda i, *_: (i, 0))],
            out_specs=pl.BlockSpec(memory_space=pl.ANY),
            scratch_shapes=[pltpu.VMEM((MOE_RB, D), F32), pltpu.SemaphoreType.DMA, pltpu.SemaphoreType.DMA],
        ),
        out_shape=jax.ShapeDtypeStruct((MOE_ROWS, D), F32),
        compiler_params=pltpu.CompilerParams(
            dimension_semantics=("arbitrary",), vmem_limit_bytes=VMEM_LIMIT),
        name="scatter",
    )(rt["dest"], rt["fill_from"], rt["fill_to"], rt["tail_blk"], hn)


def _experts_kernel(ve_ref, vcnt_ref, vrow_ref, nvis_ref, tail_ref, xs_ref, w1g_ref, w1l_ref, w2_ref, b1g_ref,
                    b1l_ref, b2_ref, ys_ref, xin_ref, xb_ref, acc_ref, isem, osem):
    v = pl.program_id(0)
    j = pl.program_id(1)
    RB = MOE_RB
    active = v < nvis_ref[0]
    nblk = (vcnt_ref[v] + RB - 1) // RB
    row0 = vrow_ref[v]

    def blk_in(i, slot):
        return pltpu.make_async_copy(xs_ref.at[pl.ds(pl.multiple_of(row0 + i * RB, RB), RB), :],
                                     xin_ref.at[slot], isem.at[slot])

    def blk_out(i):
        return pltpu.make_async_copy(acc_ref.at[pl.ds(pl.multiple_of(i * RB, RB), RB), :],
                                     ys_ref.at[pl.ds(pl.multiple_of(row0 + i * RB, RB), RB), :], osem)

    @pl.when((v == 0) & (j == 0))
    def _():
        acc_ref[pl.ds(0, RB), :] = jnp.zeros((RB, D), F32)
        _tail_fill(acc_ref, ys_ref, tail_ref, osem)

    @pl.when(active & (j == 0))
    def _():
        blk_in(0, 0).start()

        def load(i, c):
            slot = i % 2
            blk_in(i, slot).wait()

            @pl.when(i + 1 < nblk)
            def _():
                blk_in(i + 1, 1 - slot).start()
            xb_ref[pl.ds(pl.multiple_of(i * RB, RB), RB), :] = xin_ref[slot].astype(BF16)
            return c
        lax.fori_loop(0, nblk, load, 0)

    @pl.when(active)
    def _():
        w1g = w1g_ref[0].astype(BF16)
        w1l = w1l_ref[0].astype(BF16)
        w2 = w2_ref[0].astype(BF16)
        b1g = b1g_ref[0]
        b1l = b1l_ref[0]

        def block(r0, n):
            rows = pl.ds(pl.multiple_of(r0, RB), n)
            x = xb_ref[rows, :]
            glu = jnp.minimum(_dot(x, w1g) + b1g, SWIGLU_LIMIT)
            lin = jnp.clip(_dot(x, w1l) + b1l, -SWIGLU_LIMIT, SWIGLU_LIMIT)
            act = glu * _sigmoid(SWIGLU_ALPHA * glu) * (lin + 1.0)
            part = _dot(act.astype(BF16), w2)

            @pl.when(j == 0)
            def _():
                acc_ref[rows, :] = part + b2_ref[0]

            @pl.when(j > 0)
            def _():
                acc_ref[rows, :] += part

        def pair(i, c):
            block(i * (2 * RB), 2 * RB)
            return c
        lax.fori_loop(0, nblk // 2, pair, 0)

        @pl.when(nblk % 2 == 1)
        def _():
            block((nblk - 1) * RB, RB)

    @pl.when(active & (j == MOE_NF - 1))
    def _():
        def start(i, c):
            blk_out(i).start()
            return c
        lax.fori_loop(0, nblk, start, 0)

        def finish(i, c):
            blk_out(i).wait()
            return c
        lax.fori_loop(0, nblk, finish, 0)


def _experts(xs, rt, w1, b1, w2, b2):
    b1 = b1.reshape(N_EXP, 1, 2 * D_FF)
    b2 = b2.reshape(N_EXP, 1, D)

    def jj(v, j, ve, vcnt, vrow, nvis, tail):
        return jnp.where(v < nvis[0], j, MOE_NF - 1)

    return pl.pallas_call(
        _experts_kernel,
        grid_spec=pltpu.PrefetchScalarGridSpec(
            num_scalar_prefetch=5,
            grid=(MOE_NV, MOE_NF),
            in_specs=[
                pl.BlockSpec(memory_space=pl.ANY),
                pl.BlockSpec((1, D, MOE_TF), lambda v, j, ve, *s: (ve[v], 0, jj(v, j, ve, *s))),
                pl.BlockSpec((1, D, MOE_TF), lambda v, j, ve, *s: (ve[v], 0, MOE_NF + jj(v, j, ve, *s))),
                pl.BlockSpec((1, MOE_TF, D), lambda v, j, ve, *s: (ve[v], jj(v, j, ve, *s), 0)),
                pl.BlockSpec((1, 1, MOE_TF), lambda v, j, ve, *s: (ve[v], 0, jj(v, j, ve, *s))),
                pl.BlockSpec((1, 1, MOE_TF), lambda v, j, ve, *s: (ve[v], 0, MOE_NF + jj(v, j, ve, *s))),
                pl.BlockSpec((1, 1, D), lambda v, j, ve, *s: (ve[v], 0, 0)),
            ],
            out_specs=pl.BlockSpec(memory_space=pl.ANY),
            scratch_shapes=[
                pltpu.VMEM((2, MOE_RB, D), F32),
                pltpu.VMEM((MOE_RMAX, D), BF16),
                pltpu.VMEM((MOE_RMAX, D), F32),
                pltpu.SemaphoreType.DMA((2,)),
                pltpu.SemaphoreType.DMA,
            ],
        ),
        out_shape=jax.ShapeDtypeStruct((MOE_ROWS, D), F32),
        compiler_params=pltpu.CompilerParams(
            dimension_semantics=("arbitrary", "arbitrary"), vmem_limit_bytes=VMEM_LIMIT),
        name="experts",
    )(rt["vis_e"], rt["vis_cnt"], rt["vis_row"], rt["n_vis"], rt["tail_blk"], xs, w1, w1, w2, b1, b1, b2)


def _combine_kernel(dest_ref, ys_ref, h_ref, gate_ref, gf_ref, o_ref, buf_ref, sem):
    i = pl.program_id(0)
    tm = h_ref.shape[0]

    def issue(t, c):
        for k in range(TOP_K):
            _row_copy(ys_ref, dest_ref[k * T + i * tm + t], buf_ref.at[k], t, sem).start()
        return c
    lax.fori_loop(0, tm, issue, 0)

    def drain(t, c):
        for k in range(TOP_K):
            _row_copy(ys_ref, dest_ref[k * T + i * tm + t], buf_ref.at[k], t, sem).wait()
        return c
    lax.fori_loop(0, tm, drain, 0)

    h = h_ref[...]
    for k in range(TOP_K):
        h = h + gate_ref[:, k:k + 1] * buf_ref[k]
    o_ref[...] = _rms(h, gf_ref[...])


def _combine(ys, h, gates_t, gf, rt):
    tm = 256
    return pl.pallas_call(
        _combine_kernel,
        grid_spec=pltpu.PrefetchScalarGridSpec(
            num_scalar_prefetch=1,
            grid=(T // tm,),
            in_specs=[
                pl.BlockSpec(memory_space=pl.ANY),
                pl.BlockSpec((tm, D), lambda i, *_: (i, 0)),
                pl.BlockSpec((tm, TOP_K), lambda i, *_: (i, 0)),
                pl.BlockSpec((1, D), lambda i, *_: (0, 0)),
            ],
            out_specs=pl.BlockSpec((tm, D), lambda i, *_: (i, 0)),
            scratch_shapes=[pltpu.VMEM((TOP_K, tm, D), F32), pltpu.SemaphoreType.DMA],
        ),
        out_shape=jax.ShapeDtypeStruct((T, D), F32),
        compiler_params=pltpu.CompilerParams(
            dimension_semantics=("arbitrary",), vmem_limit_bytes=VMEM_LIMIT),
        name="combine",
    )(rt["dest"], ys, h, gates_t, gf)


def kernel(x, norm1_g, w_in, gla_gate_w2, gla_gate_b, gla_norm_g, s5_a_re, s5_a_im, s5_b_re, s5_b_im, s5_c_re, s5_c_im, s5_d, s5_log_step, s5_glu_w, s5_glu_b, s5_norm_g, w_out, norm2_g, router_w, router_b, expert_w1, expert_b1, expert_w2, expert_b2, final_norm_g):
    assert x.shape == (1, T, D) and w_in.shape[0] == 1
    xt = x.reshape(T, D)
    nqkvr = 2 * GLA_H * GLA_DK + 2 * GLA_H * GLA_DV
    wi = w_in[0]
    wq = wi[:, :nqkvr].astype(BF16)
    wg = jnp.pad(wi[:, nqkvr:nqkvr + GATE_RANK], ((0, 0), (0, 128 - GATE_RANK))).astype(BF16)
    wu = wi[:, nqkvr + GATE_RANK:].astype(BF16)
    w2g = jnp.pad(gla_gate_w2[0], ((0, 128 - GATE_RANK), (0, 0)))
    qkvr, lg, u = _proj(xt, norm1_g, wq, wg, wu, w2g, gla_gate_b)
    gla = _gla(qkvr, lg, gla_norm_g)
    tables = _s5_tables(s5_a_re[0], s5_a_im[0], s5_b_re[0], s5_b_im[0], s5_c_re[0], s5_c_im[0],
                        s5_d[0], s5_log_step[0])
    y = _s5(u, tables)
    h, hn, top_idx, gates = _post(y, gla, xt, s5_glu_w[0].astype(BF16), s5_glu_b, s5_norm_g,
                                  w_out[0].astype(BF16), norm2_g, router_w[0].T, router_b.reshape(N_EXP, 1))
    rt = _route(top_idx)
    xs = _scatter(hn, rt)
    ys = _experts(xs, rt, expert_w1[0], expert_b1[0], expert_w2[0], expert_b2[0])
    out = _combine(ys, h, gates.T, final_norm_g.reshape(1, D), rt)
    return out.reshape(1, T, D)
```

```python
import functools
import math

import jax
import jax.numpy as jnp
from jax import lax
from jax.experimental import pallas as pl
from jax.experimental.pallas import tpu as pltpu

F32 = jnp.float32
BF16 = jnp.bfloat16
HIGHEST = lax.Precision.HIGHEST

T = 8192
D = 2048
GLA_H = 4
GLA_DK = 128
GLA_DV = 256
GLA_CHUNK = 64
GLA_SUB = 16
GATE_RANK = 16
GATE_TAU = 16.0
S5_G = 64
S5_N = 16
S5_P = 64
S5_L = 16
S5_GT = 8
N_EXP = 32
TOP_K = 4
D_FF = 2048
SWIGLU_ALPHA = 1.702
SWIGLU_LIMIT = 7.0
RMS_EPS = 1e-6

MOE_RB = 256
MOE_RMAX = 1536
MOE_TF = 512
MOE_NF = D_FF // MOE_TF
MOE_NV = -(-T * TOP_K // MOE_RMAX) + N_EXP
MOE_ROWS = T * TOP_K + N_EXP * MOE_RB
VMEM_LIMIT = 56 * 1024 * 1024

NN = (((1,), (0,)), ((), ()))
NT = (((1,), (1,)), ((), ()))
TN = (((0,), (0,)), ((), ()))


def _dot(a, b, dims=NN, precision=None):
    return lax.dot_general(a, b, dims, preferred_element_type=F32, precision=precision)


def _rms(x, g):
    return x * lax.rsqrt(jnp.mean(x * x, axis=-1, keepdims=True) + RMS_EPS) * g


def _sigmoid(x):
    return 1.0 / (1.0 + jnp.exp(-x))


def _const_spec(shape):
    nd = len(shape)
    return pl.BlockSpec(shape, lambda *_: (0,) * nd, pipeline_mode=pl.Buffered(1))


def _proj_kernel(x_ref, g1_ref, wq_ref, wg_ref, wu_ref, w2_ref, gb_ref, qkvr_ref, lg_ref, u_ref):
    hb = _rms(x_ref[...], g1_ref[...]).astype(BF16)
    p = _dot(hb, wq_ref[...])
    nq = GLA_H * GLA_DK
    qkvr_ref[:, :nq] = (p[:, :nq] * (GLA_DK ** -0.5)).astype(BF16)
    qkvr_ref[:, nq:] = p[:, nq:].astype(BF16)
    glr = _dot(hb, wg_ref[...])
    z = _dot(glr, w2_ref[...], precision=HIGHEST) + gb_ref[...]
    lg_ref[...] = (jnp.minimum(z, 0.0) - jnp.log(1.0 + jnp.exp(-jnp.abs(z)))) * (1.0 / GATE_TAU)
    u_ref[...] = _dot(hb, wu_ref[...])


def _proj(x, g1, wq, wg, wu, w2, gb):
    tm = 256
    nqkvr = wq.shape[1]
    return pl.pallas_call(
        _proj_kernel,
        grid=(T // tm,),
        in_specs=[
            pl.BlockSpec((tm, D), lambda i: (i, 0)),
            _const_spec((1, D)),
            _const_spec(wq.shape), _const_spec(wg.shape), _const_spec(wu.shape),
            _const_spec(w2.shape), _const_spec(gb.shape),
        ],
        out_specs=[
            pl.BlockSpec((tm, nqkvr), lambda i: (i, 0)),
            pl.BlockSpec((tm, GLA_H * GLA_DK), lambda i: (i, 0)),
            pl.BlockSpec((tm, S5_G * S5_N), lambda i: (i, 0)),
        ],
        out_shape=[
            jax.ShapeDtypeStruct((T, nqkvr), BF16),
            jax.ShapeDtypeStruct((T, GLA_H * GLA_DK), F32),
            jax.ShapeDtypeStruct((T, S5_G * S5_N), F32),
        ],
        compiler_params=pltpu.CompilerParams(
            dimension_semantics=("arbitrary",), vmem_limit_bytes=VMEM_LIMIT),
        name="proj",
    )(x, g1, wq, wg, wu, w2, gb)


def _gla_kernel(q_ref, k_ref, v_ref, r_ref, lg_ref, gn_ref, tril_ref, esel_ref, o_ref, state_ref):
    C, S = GLA_CHUNK, GLA_SUB
    nsub = C // S

    @pl.when(pl.program_id(0) == 0)
    def _():
        state_ref[...] = jnp.zeros_like(state_ref)

    row = lax.broadcasted_iota(jnp.int32, (C, C), 0)
    col = lax.broadcasted_iota(jnp.int32, (C, C), 1)
    below = (col // S) < (row // S)
    diag = ((col // S) == (row // S)) & (col <= row)
    tril = tril_ref[...]
    esel = esel_ref[...]
    gn = gn_ref[...]

    def head(h, rows, b):
        kcols = slice(h * GLA_DK, (h + 1) * GLA_DK)
        vcols = slice(h * GLA_DV, (h + 1) * GLA_DV)
        q = q_ref[rows, kcols].astype(F32)
        k = k_ref[rows, kcols].astype(F32)
        v = v_ref[rows, vcols]
        bl = b[C - 1:C, :]
        st = state_ref[h]
        o = _dot((q * jnp.exp(b)).astype(BF16), st.astype(BF16), NT)
        s_rows = [jnp.zeros((S, C), F32)]
        for j in range(1, nsub):
            ref = b[j * S - 1:j * S, :]
            qj = (q[j * S:(j + 1) * S] * jnp.exp(b[j * S:(j + 1) * S] - ref)).astype(BF16)
            kj = (k * jnp.exp(jnp.minimum(ref - b, 0.0))).astype(BF16)
            s_rows.append(_dot(qj, kj, NT))
        s_off = jnp.concatenate(s_rows, axis=0)
        q3 = q.reshape(nsub, S, GLA_DK)
        k3 = k.reshape(nsub, S, GLA_DK)
        b3 = b.reshape(nsub, S, GLA_DK)
        xs = []
        for s in range(S):
            dec = jnp.exp(jnp.minimum(b3 - b3[:, s:s + 1, :], 0.0))
            xs.append((q3 * k3[:, s:s + 1, :] * dec).reshape(C, GLA_DK).astype(BF16))
        dsc = _dot(jnp.concatenate(xs, axis=1), esel)
        a = jnp.where(below, s_off, jnp.where(diag, dsc, 0.0))
        o = o + _dot(a.astype(BF16), v)
        kout = (k * jnp.exp(bl - b)).astype(BF16)
        state_ref[h] = st * jnp.exp(bl) + _dot(v, kout, TN)
        y = _rms(o, gn)
        r = r_ref[rows, vcols].astype(F32)
        o_ref[rows, vcols] = (y * (r * _sigmoid(r))).astype(BF16)

    def chunk(c, carry):
        rows = pl.ds(pl.multiple_of(c * C, C), C)
        b = _dot(tril, lg_ref[rows, :], precision=HIGHEST)
        for h in range(GLA_H):
            head(h, rows, b[:, h * GLA_DK:(h + 1) * GLA_DK])
        return carry

    lax.fori_loop(0, q_ref.shape[0] // C, chunk, 0)


def _gla(qkvr, lg, gn):
    tb = 512
    C, S = GLA_CHUNK, GLA_SUB
    nk, nv = GLA_H * GLA_DK, GLA_H * GLA_DV
    tril = jnp.tril(jnp.ones((C, C), F32))
    esel = (jnp.arange(S * GLA_DK)[:, None] // GLA_DK == jnp.arange(C)[None, :] % S).astype(BF16)
    return pl.pallas_call(
        _gla_kernel,
        grid=(T // tb,),
        in_specs=[
            pl.BlockSpec((tb, nk), lambda i: (i, 0)),
            pl.BlockSpec((tb, nk), lambda i: (i, 1)),
            pl.BlockSpec((tb, nv), lambda i: (i, 1)),
            pl.BlockSpec((tb, nv), lambda i: (i, 2)),
            pl.BlockSpec((tb, nk), lambda i: (i, 0)),
            _const_spec((1, GLA_DV)), _const_spec((C, C)), _const_spec((S * GLA_DK, C)),
        ],
        out_specs=pl.BlockSpec((tb, nv), lambda i: (i, 0)),
        out_shape=jax.ShapeDtypeStruct((T, nv), BF16),
        scratch_shapes=[pltpu.VMEM((GLA_H, GLA_DV, GLA_DK), F32)],
        compiler_params=pltpu.CompilerParams(
            dimension_semantics=("arbitrary",), vmem_limit_bytes=VMEM_LIMIT),
        name="gla",
    )(qkvr, qkvr, qkvr, qkvr, lg, gn, tril, esel)


def _s5_tables(a_re, a_im, b_re, b_im, c_re, c_im, d_skip, log_step):
    L, G, P, N = S5_L, S5_G, S5_P, S5_N
    delta = jnp.exp(log_step)[:, None]
    ar, ai = a_re * delta, a_im * delta

    def apow(tau):
        tau = jnp.asarray(tau, F32)[None, :, None]
        mag = jnp.exp(ar[:, None, :] * tau)
        ph = ai[:, None, :] * tau
        return mag * jnp.cos(ph), mag * jnp.sin(ph)

    p_re, p_im = apow(jnp.arange(L + 1))
    ab_re, ab_im = p_re[:, 1], p_im[:, 1]
    den = a_re * a_re + a_im * a_im
    cf_re = ((ab_re - 1.0) * a_re + ab_im * a_im) / den
    cf_im = (ab_im * a_re - (ab_re - 1.0) * a_im) / den
    bb_re = cf_re[:, :, None] * b_re - cf_im[:, :, None] * b_im
    bb_im = cf_re[:, :, None] * b_im + cf_im[:, :, None] * b_re
    e_re = p_re[:, :L, :, None] * bb_re[:, None] - p_im[:, :L, :, None] * bb_im[:, None]
    e_im = p_re[:, :L, :, None] * bb_im[:, None] + p_im[:, :L, :, None] * bb_re[:, None]
    kk = (jnp.einsum('gnp,gtpm->gtnm', c_re, e_re, precision=HIGHEST)
          - jnp.einsum('gnp,gtpm->gtnm', c_im, e_im, precision=HIGHEST))
    kk = kk.at[:, 0].add(jnp.eye(N, dtype=F32)[None] * d_skip[:, :, None])
    r_re, r_im = apow(L - 1 - jnp.arange(L))
    bs_re = (r_re[:, :, None, :] * bb_re.transpose(0, 2, 1)[:, None] - r_im[:, :, None, :] * bb_im.transpose(0, 2, 1)[:, None])
    bs_im = (r_re[:, :, None, :] * bb_im.transpose(0, 2, 1)[:, None] + r_im[:, :, None, :] * bb_re.transpose(0, 2, 1)[:, None])
    ct_re = c_re.transpose(0, 2, 1)[:, :, None, :]
    ct_im = c_im.transpose(0, 2, 1)[:, :, None, :]
    q_re = p_re[:, 1:].transpose(0, 2, 1)[:, :, :, None]
    q_im = p_im[:, 1:].transpose(0, 2, 1)[:, :, :, None]
    cre = ct_re * q_re - ct_im * q_im
    cim = -(ct_re * q_im + ct_im * q_re)
    GT = S5_GT
    nt = G // GT

    def tile_groups(w):
        return w.reshape((nt, GT) + w.shape[1:])

    kc = tile_groups(kk).transpose(0, 1, 4, 2, 3).reshape(nt, GT * N, L * N)

    def in_table(w):
        return tile_groups(w).transpose(0, 2, 1, 3, 4).reshape(nt, L * GT * N, P)

    def out_table(w):
        return w.reshape(nt, GT * P, L * N)

    nstep = (T // L - 1).bit_length()
    sc_re, sc_im = apow(jnp.asarray([L * 2 ** i for i in range(nstep)]))

    def lanes(w):
        return tile_groups(w).transpose(0, 2, 1, 3).reshape(nt, w.shape[1], GT * P)

    r1 = jnp.arange(L * N)[:, None]
    c1 = jnp.arange(L * GT * N)[None, :]
    wide_tn = ((r1 // N == c1 // (GT * N)) & (r1 % N == c1 % N)).astype(BF16)
    wide_p = (jnp.arange(P)[:, None] == jnp.arange(GT * P)[None, :] % P).astype(BF16)
    return dict(
        kc=kc.astype(BF16),
        bs_re=in_table(bs_re).astype(BF16), bs_im=in_table(bs_im).astype(BF16),
        cre=out_table(cre).astype(BF16), cim=out_table(cim).astype(BF16),
        sc_re=lanes(sc_re), sc_im=lanes(sc_im), wide_tn=wide_tn, wide_p=wide_p,
    )


def _s5_kernel(u_ref, kc_ref, bsr_ref, bsi_ref, cre_ref, cim_ref, scr_ref, sci_ref, wtn_ref, wp_ref,
               y_ref, m_ref):
    L, N, P = S5_L, S5_N, S5_P
    W = S5_GT * N
    nch = u_ref.shape[0] // L

    @pl.when(pl.program_id(0) == 0)
    def _():
        m_ref[...] = jnp.zeros_like(m_ref)

    def widen(compact, wide, row_group, col_group):
        full = _dot(compact, wide)
        r = lax.broadcasted_iota(jnp.int32, full.shape, 0)
        c = lax.broadcasted_iota(jnp.int32, full.shape, 1)
        return jnp.where(row_group(r) == col_group(c), full, 0.0).astype(BF16)

    wtn, wp = wtn_ref[...], wp_ref[...]
    bd = widen(kc_ref[0], wtn, lambda r: r // N, lambda c: (c % W) // N)
    bsr = widen(bsr_ref[0], wp, lambda r: (r % W) // N, lambda c: c // P)
    bsi = widen(bsi_ref[0], wp, lambda r: (r % W) // N, lambda c: c // P)
    cre = widen(cre_ref[0], wtn, lambda r: r // P, lambda c: (c % W) // N)
    cim = widen(cim_ref[0], wtn, lambda r: r // P, lambda c: (c % W) // N)
    for s in range(L):
        m_ref[s * W:(s + 1) * W, s * W:] = bd[:, :(L - s) * W]
    ucat = jnp.concatenate([u_ref[pl.ds(s, nch, stride=L), :].astype(BF16) for s in range(L)], axis=1)
    y = _dot(ucat, m_ref[...])
    hr = _dot(ucat, bsr)
    hi = _dot(ucat, bsi)
    row = lax.broadcasted_iota(jnp.int32, hr.shape, 0)

    def shift(x, d):
        if d % 8 == 0:
            return jnp.concatenate([jnp.zeros((d, x.shape[1]), F32), x[:nch - d]], axis=0)
        return jnp.where(row >= d, pltpu.roll(x, d, 0), 0.0)

    for i in range(scr_ref.shape[1]):
        mr, mi = scr_ref[0, i:i + 1, :], sci_ref[0, i:i + 1, :]
        pr, pi = shift(hr, 2 ** i), shift(hi, 2 ** i)
        hr, hi = hr + mr * pr - mi * pi, hi + mr * pi + mi * pr
    gr, gi = shift(hr, 1), shift(hi, 1)
    y = y + _dot(gr.astype(BF16), cre) + _dot(gi.astype(BF16), cim)
    for t in range(L):
        y_ref[pl.ds(t, nch, stride=L), :] = y[:, t * W:(t + 1) * W]


def _s5(u, tb):
    W = S5_GT * S5_N
    nt = S5_G // S5_GT

    def tile_spec(a):
        return pl.BlockSpec((1,) + a.shape[1:], lambda g: (g,) + (0,) * (a.ndim - 1))

    ws = [tb[n] for n in ["kc", "bs_re", "bs_im", "cre", "cim", "sc_re", "sc_im"]]
    consts = [tb["wide_tn"], tb["wide_p"]]
    return pl.pallas_call(
        _s5_kernel,
        grid=(nt,),
        in_specs=([pl.BlockSpec((T, W), lambda g: (0, g))] + [tile_spec(w) for w in ws]
                  + [_const_spec(c.shape) for c in consts]),
        out_specs=pl.BlockSpec((T, W), lambda g: (0, g)),
        out_shape=jax.ShapeDtypeStruct((T, S5_G * S5_N), F32),
        scratch_shapes=[pltpu.VMEM((S5_L * W, S5_L * W), BF16)],
        compiler_params=pltpu.CompilerParams(
            dimension_semantics=("arbitrary",), vmem_limit_bytes=VMEM_LIMIT),
        name="s5",
    )(u, *ws, *consts)


def _post_kernel(y_ref, gla_ref, x_ref, gw_ref, gb_ref, sg_ref, wo_ref, n2_ref, rw_ref, rb_ref, before_ref,
                 h_ref, hn_ref, idx_ref, gate_ref, rank_ref, cnt_ref, run_ref):
    @pl.when(pl.program_id(0) == 0)
    def _():
        run_ref[...] = jnp.zeros_like(run_ref)

    y = y_ref[...]
    z = 0.5 * y * (1.0 + jnp.tanh(math.sqrt(2.0 / math.pi) * (y + 0.044715 * (y * y * y))))
    z = z * _sigmoid(_dot(z.astype(BF16), gw_ref[...]) + gb_ref[...])
    s5o = _rms(z, sg_ref[...]).astype(BF16)
    half = GLA_H * GLA_DV
    h = x_ref[...] + _dot(gla_ref[...], wo_ref[:half, :]) + _dot(s5o, wo_ref[half:, :])
    h_ref[...] = h
    hn = _rms(h, n2_ref[...])
    hn_ref[...] = hn
    hi = hn.astype(BF16)
    lo = (hn - hi.astype(F32)).astype(BF16)
    rw = rw_ref[...]
    rw_hi = rw.astype(BF16)
    rw_lo = (rw - rw_hi.astype(F32)).astype(BF16)
    lg = _dot(hi, rw_hi) + (_dot(lo, rw_hi) + _dot(hi, rw_lo))
    lt = lg.T[:N_EXP] + rb_ref[...]
    eid = lax.broadcasted_iota(jnp.int32, lt.shape, 0).astype(F32)
    vals, idxs = [], []
    for _ in range(TOP_K):
        m = jnp.max(lt, axis=0, keepdims=True)
        sel = jnp.min(jnp.where(lt == m, eid, float(N_EXP)), axis=0, keepdims=True)
        vals.append(m)
        idxs.append(sel)
        lt = jnp.where(eid == sel, -jnp.inf, lt)
    ex = [jnp.exp(vv - vals[0]) for vv in vals]
    inv = 1.0 / (ex[0] + ex[1] + ex[2] + ex[3])
    idx_ref[...] = jnp.concatenate(idxs, axis=0).astype(jnp.int32)
    gate_ref[...] = jnp.concatenate([e * inv for e in ex], axis=0)
    run = run_ref[:, :1]
    ranks = []
    for sel in idxs:
        onehot = jnp.where(eid == sel, 1.0, 0.0)
        earlier = _dot(onehot.astype(BF16), before_ref[...])
        ranks.append(jnp.sum(onehot * (run + earlier), axis=0, keepdims=True))
        run = run + jnp.sum(onehot, axis=1, keepdims=True)
    rank_ref[...] = jnp.concatenate(ranks, axis=0).astype(jnp.int32)
    run_ref[...] = jnp.broadcast_to(run, run_ref.shape)
    cnt_ref[...] = jnp.broadcast_to(run, cnt_ref.shape)


def _post(y, gla, x, gw, gb, sg, wo, n2, rwt, rb):
    tm = 256
    width = S5_G * S5_N
    before = (jnp.arange(tm)[:, None] < jnp.arange(tm)[None, :]).astype(BF16)
    return pl.pallas_call(
        _post_kernel,
        grid=(T // tm,),
        in_specs=[
            pl.BlockSpec((tm, width), lambda i: (i, 0)),
            pl.BlockSpec((tm, GLA_H * GLA_DV), lambda i: (i, 0)),
            pl.BlockSpec((tm, D), lambda i: (i, 0)),
            _const_spec(gw.shape), _const_spec(gb.shape), _const_spec(sg.shape), _const_spec(wo.shape),
            _const_spec(n2.shape), _const_spec(rwt.shape), _const_spec(rb.shape), _const_spec(before.shape),
        ],
        out_specs=[
            pl.BlockSpec((tm, D), lambda i: (i, 0)),
            pl.BlockSpec((tm, D), lambda i: (i, 0)),
            pl.BlockSpec((TOP_K, tm), lambda i: (0, i)),
            pl.BlockSpec((TOP_K, tm), lambda i: (0, i)),
            pl.BlockSpec((TOP_K, tm), lambda i: (0, i)),
            pl.BlockSpec((N_EXP, 128), lambda i: (0, 0)),
        ],
        out_shape=[
            jax.ShapeDtypeStruct((T, D), F32),
            jax.ShapeDtypeStruct((T, D), F32),
            jax.ShapeDtypeStruct((TOP_K, T), jnp.int32),
            jax.ShapeDtypeStruct((TOP_K, T), F32),
            jax.ShapeDtypeStruct((TOP_K, T), jnp.int32),
            jax.ShapeDtypeStruct((N_EXP, 128), F32),
        ],
        scratch_shapes=[pltpu.VMEM((N_EXP, 128), F32)],
        compiler_params=pltpu.CompilerParams(
            dimension_semantics=("arbitrary",), vmem_limit_bytes=VMEM_LIMIT),
        name="post",
    )(y, gla, x, gw, gb, sg, wo, n2, rwt, rb, before)


def _route(top_idx, rank, counts):
    e_flat = top_idx.reshape(-1)
    rank = rank.reshape(-1)
    counts = counts[:, 0].astype(jnp.int32)
    padded = (counts + MOE_RB - 1) // MOE_RB * MOE_RB
    pad_end = jnp.cumsum(padded)
    pad_start = pad_end - padded
    dest = (pad_start[e_flat] + rank).astype(jnp.int32)
    nvis_e = (counts + MOE_RMAX - 1) // MOE_RMAX
    vis_end = jnp.cumsum(nvis_e)
    n_vis = vis_end[-1]
    v = jnp.arange(MOE_NV, dtype=jnp.int32)
    vc = jnp.minimum(v, n_vis - 1)
    ve = jnp.minimum(jnp.searchsorted(vis_end, vc, side='right'), N_EXP - 1).astype(jnp.int32)
    local = vc - (vis_end[ve] - nvis_e[ve])
    vcnt = jnp.where(v < n_vis, jnp.minimum(MOE_RMAX, counts[ve] - local * MOE_RMAX), 0)
    vrow = pad_start[ve] + local * MOE_RMAX
    return dict(dest=dest, fill_from=(pad_start + counts).astype(jnp.int32), fill_to=pad_end.astype(jnp.int32),
                tail_blk=(pad_end[-1:] // MOE_RB).astype(jnp.int32),
                vis_e=ve, vis_cnt=vcnt.astype(jnp.int32), vis_row=vrow.astype(jnp.int32),
                n_vis=n_vis.reshape(1).astype(jnp.int32))


def _row_copy(src_ref, srow, dst_ref, drow, sem):
    return pltpu.make_async_copy(src_ref.at[pl.ds(srow, 1), :], dst_ref.at[pl.ds(drow, 1), :], sem)


def _tail_fill(src_ref, dst_ref, tail_ref, sem):
    def cp(b):
        return pltpu.make_async_copy(src_ref.at[pl.ds(0, MOE_RB), :],
                                     dst_ref.at[pl.ds(pl.multiple_of(b * MOE_RB, MOE_RB), MOE_RB), :], sem)

    def start(b, c):
        cp(b).start()
        return c
    lax.fori_loop(tail_ref[0], MOE_ROWS // MOE_RB, start, 0)

    def finish(b, c):
        cp(b).wait()
        return c
    lax.fori_loop(tail_ref[0], MOE_ROWS // MOE_RB, finish, 0)


def _scatter_kernel(dest_ref, from_ref, to_ref, tail_ref, hn_ref, xs_ref, zero_ref, sem, zsem):
    i = pl.program_id(0)
    tm = hn_ref.shape[0]

    @pl.when(i == 0)
    def _():
        zero_ref[...] = jnp.zeros_like(zero_ref)
        _tail_fill(zero_ref, xs_ref, tail_ref, zsem)

        def per_expert(e, c):
            def fill(r, c2):
                _row_copy(zero_ref, 0, xs_ref, r, zsem).start()
                return c2
            lax.fori_loop(from_ref[e], to_ref[e], fill, 0)

            def drain(r, c2):
                _row_copy(zero_ref, 0, xs_ref, r, zsem).wait()
                return c2
            lax.fori_loop(from_ref[e], to_ref[e], drain, 0)
            return c
        lax.fori_loop(0, N_EXP, per_expert, 0)

    def issue(t, c):
        for k in range(TOP_K):
            _row_copy(hn_ref, t, xs_ref, dest_ref[k * T + i * tm + t], sem).start()
        return c
    lax.fori_loop(0, tm, issue, 0)

    def drain(t, c):
        for k in range(TOP_K):
            _row_copy(hn_ref, t, xs_ref, dest_ref[k * T + i * tm + t], sem).wait()
        return c
    lax.fori_loop(0, tm, drain, 0)


def _scatter(hn, rt):
    tm = 256
    return pl.pallas_call(
        _scatter_kernel,
        grid_spec=pltpu.PrefetchScalarGridSpec(
            num_scalar_prefetch=4,
            grid=(T // tm,),
            in_specs=[pl.BlockSpec((tm, D), lambda i, *_: (i, 0))],
            out_specs=pl.BlockSpec(memory_space=pl.ANY),
            scratch_shapes=[pltpu.VMEM((MOE_RB, D), F32), pltpu.SemaphoreType.DMA, pltpu.SemaphoreType.DMA],
        ),
        out_shape=jax.ShapeDtypeStruct((MOE_ROWS, D), F32),
        compiler_params=pltpu.CompilerParams(
            dimension_semantics=("arbitrary",), vmem_limit_bytes=VMEM_LIMIT),
        name="scatter",
    )(rt["dest"], rt["fill_from"], rt["fill_to"], rt["tail_blk"], hn)


def _experts_kernel(ve_ref, vcnt_ref, vrow_ref, nvis_ref, tail_ref, xs_ref, w1g_ref, w1l_ref, w2_ref, b1g_ref,
                    b1l_ref, b2_ref, ys_ref, xin_ref, xb_ref, acc_ref, isem, osem):
    v = pl.program_id(0)
    j = pl.program_id(1)
    RB = MOE_RB
    active = v < nvis_ref[0]
    nblk = (vcnt_ref[v] + RB - 1) // RB
    row0 = vrow_ref[v]

    def blk_in(i, slot):
        return pltpu.make_async_copy(xs_ref.at[pl.ds(pl.multiple_of(row0 + i * RB, RB), RB), :],
                                     xin_ref.at[slot], isem.at[slot])

    def blk_out(i):
        return pltpu.make_async_copy(acc_ref.at[pl.ds(pl.multiple_of(i * RB, RB), RB), :],
                                     ys_ref.at[pl.ds(pl.multiple_of(row0 + i * RB, RB), RB), :], osem)

    @pl.when((v == 0) & (j == 0))
    def _():
        acc_ref[pl.ds(0, RB), :] = jnp.zeros((RB, D), F32)
        _tail_fill(acc_ref, ys_ref, tail_ref, osem)

    @pl.when(active & (j == 0))
    def _():
        blk_in(0, 0).start()

        def load(i, c):
            slot = i % 2
            blk_in(i, slot).wait()

            @pl.when(i + 1 < nblk)
            def _():
                blk_in(i + 1, 1 - slot).start()
            xb_ref[pl.ds(pl.multiple_of(i * RB, RB), RB), :] = xin_ref[slot].astype(BF16)
            return c
        lax.fori_loop(0, nblk, load, 0)

    @pl.when(active)
    def _():
        w1g = w1g_ref[0].astype(BF16)
        w1l = w1l_ref[0].astype(BF16)
        w2 = w2_ref[0].astype(BF16)
        b1g = b1g_ref[0]
        b1l = b1l_ref[0]

        def block(r0, n):
            rows = pl.ds(pl.multiple_of(r0, RB), n)
            x = xb_ref[rows, :]
            glu = jnp.minimum(_dot(x, w1g) + b1g, SWIGLU_LIMIT)
            lin = jnp.clip(_dot(x, w1l) + b1l, -SWIGLU_LIMIT, SWIGLU_LIMIT)
            act = glu * _sigmoid(SWIGLU_ALPHA * glu) * (lin + 1.0)
            part = _dot(act.astype(BF16), w2)

            @pl.when(j == 0)
            def _():
                acc_ref[rows, :] = part + b2_ref[0]

            @pl.when(j > 0)
            def _():
                acc_ref[rows, :] += part

        def pair(i, c):
            block(i * (2 * RB), 2 * RB)
            return c
        lax.fori_loop(0, nblk // 2, pair, 0)

        @pl.when(nblk % 2 == 1)
        def _():
            block((nblk - 1) * RB, RB)

    @pl.when(active & (j == MOE_NF - 1))
    def _():
        def start(i, c):
            blk_out(i).start()
            return c
        lax.fori_loop(0, nblk, start, 0)

        def finish(i, c):
            blk_out(i).wait()
            return c
        lax.fori_loop(0, nblk, finish, 0)


def _experts(xs, rt, w1, b1, w2, b2):
    b1 = b1.reshape(N_EXP, 1, 2 * D_FF)
    b2 = b2.reshape(N_EXP, 1, D)

    def jj(v, j, ve, vcnt, vrow, nvis, tail):
        return jnp.where(v < nvis[0], j, MOE_NF - 1)

    return pl.pallas_call(
        _experts_kernel,
        grid_spec=pltpu.PrefetchScalarGridSpec(
            num_scalar_prefetch=5,
            grid=(MOE_NV, MOE_NF),
            in_specs=[
                pl.BlockSpec(memory_space=pl.ANY),
                pl.BlockSpec((1, D, MOE_TF), lambda v, j, ve, *s: (ve[v], 0, jj(v, j, ve, *s))),
                pl.BlockSpec((1, D, MOE_TF), lambda v, j, ve, *s: (ve[v], 0, MOE_NF + jj(v, j, ve, *s))),
                pl.BlockSpec((1, MOE_TF, D), lambda v, j, ve, *s: (ve[v], jj(v, j, ve, *s), 0)),
                pl.BlockSpec((1, 1, MOE_TF), lambda v, j, ve, *s: (ve[v], 0, jj(v, j, ve, *s))),
                pl.BlockSpec((1, 1, MOE_TF), lambda v, j, ve, *s: (ve[v], 0, MOE_NF + jj(v, j, ve, *s))),
                pl.BlockSpec((1, 1, D), lambda v, j, ve, *s: (ve[v], 0, 0)),
            ],
            out_specs=pl.BlockSpec(memory_space=pl.ANY),
            scratch_shapes=[
                pltpu.VMEM((2, MOE_RB, D), F32),
                pltpu.VMEM((MOE_RMAX, D), BF16),
                pltpu.VMEM((MOE_RMAX, D), F32),
                pltpu.SemaphoreType.DMA((2,)),
                pltpu.SemaphoreType.DMA,
            ],
        ),
        out_shape=jax.ShapeDtypeStruct((MOE_ROWS, D), F32),
        compiler_params=pltpu.CompilerParams(
            dimension_semantics=("arbitrary", "arbitrary"), vmem_limit_bytes=VMEM_LIMIT),
        name="experts",
    )(rt["vis_e"], rt["vis_cnt"], rt["vis_row"], rt["n_vis"], rt["tail_blk"], xs, w1, w1, w2, b1, b1, b2)


def _combine_kernel(dest_ref, ys_ref, h_ref, gate_ref, gf_ref, o_ref, buf_ref, sem):
    i = pl.program_id(0)
    tm = h_ref.shape[0]

    def issue(t, c):
        for k in range(TOP_K):
            _row_copy(ys_ref, dest_ref[k * T + i * tm + t], buf_ref.at[k], t, sem).start()
        return c
    lax.fori_loop(0, tm, issue, 0)

    def drain(t, c):
        for k in range(TOP_K):
            _row_copy(ys_ref, dest_ref[k * T + i * tm + t], buf_ref.at[k], t, sem).wait()
        return c
    lax.fori_loop(0, tm, drain, 0)

    h = h_ref[...]
    for k in range(TOP_K):
        h = h + gate_ref[:, k:k + 1] * buf_ref[k]
    o_ref[...] = _rms(h, gf_ref[...])


def _combine(ys, h, gates_t, gf, rt):
    tm = 256
    return pl.pallas_call(
        _combine_kernel,
        grid_spec=pltpu.PrefetchScalarGridSpec(
            num_scalar_prefetch=1,
            grid=(T // tm,),
            in_specs=[
                pl.BlockSpec(memory_space=pl.ANY),
                pl.BlockSpec((tm, D), lambda i, *_: (i, 0)),
                pl.BlockSpec((tm, TOP_K), lambda i, *_: (i, 0)),
                pl.BlockSpec((1, D), lambda i, *_: (0, 0)),
            ],
            out_specs=pl.BlockSpec((tm, D), lambda i, *_: (i, 0)),
            scratch_shapes=[pltpu.VMEM((TOP_K, tm, D), F32), pltpu.SemaphoreType.DMA],
        ),
        out_shape=jax.ShapeDtypeStruct((T, D), F32),
        compiler_params=pltpu.CompilerParams(
            dimension_semantics=("arbitrary",), vmem_limit_bytes=VMEM_LIMIT),
        name="combine",
    )(rt["dest"], ys, h, gates_t, gf)


def kernel(x, norm1_g, w_in, gla_gate_w2, gla_gate_b, gla_norm_g, s5_a_re, s5_a_im, s5_b_re, s5_b_im, s5_c_re, s5_c_im, s5_d, s5_log_step, s5_glu_w, s5_glu_b, s5_norm_g, w_out, norm2_g, router_w, router_b, expert_w1, expert_b1, expert_w2, expert_b2, final_norm_g):
    assert x.shape == (1, T, D) and w_in.shape[0] == 1
    xt = x.reshape(T, D)
    nqkvr = 2 * GLA_H * GLA_DK + 2 * GLA_H * GLA_DV
    wi = w_in[0]
    wq = wi[:, :nqkvr].astype(BF16)
    wg = jnp.pad(wi[:, nqkvr:nqkvr + GATE_RANK], ((0, 0), (0, 128 - GATE_RANK))).astype(BF16)
    wu = wi[:, nqkvr + GATE_RANK:].astype(BF16)
    w2g = jnp.pad(gla_gate_w2[0], ((0, 128 - GATE_RANK), (0, 0)))
    qkvr, lg, u = _proj(xt, norm1_g, wq, wg, wu, w2g, gla_gate_b)
    gla = _gla(qkvr, lg, gla_norm_g)
    tables = _s5_tables(s5_a_re[0], s5_a_im[0], s5_b_re[0], s5_b_im[0], s5_c_re[0], s5_c_im[0],
                        s5_d[0], s5_log_step[0])
    y = _s5(u, tables)
    h, hn, top_idx, gates, rank, counts = _post(
        y, gla, xt, s5_glu_w[0].astype(BF16), s5_glu_b, s5_norm_g, w_out[0].astype(BF16), norm2_g,
        jnp.pad(router_w[0], ((0, 0), (0, 128 - N_EXP))), router_b.reshape(N_EXP, 1))
    rt = _route(top_idx, rank, counts)
    xs = _scatter(hn, rt)
    ys = _experts(xs, rt, expert_w1[0], expert_b1[0], expert_w2[0], expert_b2[0])
    out = _combine(ys, h, gates.T, final_norm_g.reshape(1, D), rt)
    return out.reshape(1, T, D)
```

```python
import functools
import math

import jax
import jax.numpy as jnp
from jax import lax
from jax.experimental import pallas as pl
from jax.experimental.pallas import tpu as pltpu

F32 = jnp.float32
BF16 = jnp.bfloat16
HIGHEST = lax.Precision.HIGHEST

T = 8192
D = 2048
GLA_H = 4
GLA_DK = 128
GLA_DV = 256
GLA_CHUNK = 64
GLA_SUB = 16
GATE_RANK = 16
GATE_TAU = 16.0
S5_G = 64
S5_N = 16
S5_P = 64
S5_L = 16
S5_GT = 8
N_EXP = 32
TOP_K = 4
D_FF = 2048
SWIGLU_ALPHA = 1.702
SWIGLU_LIMIT = 7.0
RMS_EPS = 1e-6

MOE_RB = 256
MOE_RMAX = 1536
MOE_TF = 512
MOE_NF = D_FF // MOE_TF
MOE_NV = -(-T * TOP_K // MOE_RMAX) + N_EXP
MOE_ROWS = T * TOP_K + N_EXP * MOE_RB
VMEM_LIMIT = 56 * 1024 * 1024

NN = (((1,), (0,)), ((), ()))
NT = (((1,), (1,)), ((), ()))
TN = (((0,), (0,)), ((), ()))


def _dot(a, b, dims=NN, precision=None):
    return lax.dot_general(a, b, dims, preferred_element_type=F32, precision=precision)


def _rms(x, g):
    return x * lax.rsqrt(jnp.mean(x * x, axis=-1, keepdims=True) + RMS_EPS) * g


def _sigmoid(x):
    return 1.0 / (1.0 + jnp.exp(-x))


def _const_spec(shape):
    nd = len(shape)
    return pl.BlockSpec(shape, lambda *_: (0,) * nd, pipeline_mode=pl.Buffered(1))


def _proj_kernel(x_ref, g1_ref, wq_ref, wg_ref, wu_ref, w2_ref, gb_ref, qkvr_ref, lg_ref, u_ref):
    hb = _rms(x_ref[...], g1_ref[...]).astype(BF16)
    p = _dot(hb, wq_ref[...])
    nq = GLA_H * GLA_DK
    qkvr_ref[:, :nq] = (p[:, :nq] * (GLA_DK ** -0.5)).astype(BF16)
    qkvr_ref[:, nq:] = p[:, nq:].astype(BF16)
    glr = _dot(hb, wg_ref[...])
    z = _dot(glr, w2_ref[...], precision=HIGHEST) + gb_ref[...]
    lg_ref[...] = (jnp.minimum(z, 0.0) - jnp.log(1.0 + jnp.exp(-jnp.abs(z)))) * (1.0 / GATE_TAU)
    u_ref[...] = _dot(hb, wu_ref[...])


def _proj(x, g1, wq, wg, wu, w2, gb):
    tm = 256
    nqkvr = wq.shape[1]
    return pl.pallas_call(
        _proj_kernel,
        grid=(T // tm,),
        in_specs=[
            pl.BlockSpec((tm, D), lambda i: (i, 0)),
            _const_spec((1, D)),
            _const_spec(wq.shape), _const_spec(wg.shape), _const_spec(wu.shape),
            _const_spec(w2.shape), _const_spec(gb.shape),
        ],
        out_specs=[
            pl.BlockSpec((tm, nqkvr), lambda i: (i, 0)),
            pl.BlockSpec((tm, GLA_H * GLA_DK), lambda i: (i, 0)),
            pl.BlockSpec((tm, S5_G * S5_N), lambda i: (i, 0)),
        ],
        out_shape=[
            jax.ShapeDtypeStruct((T, nqkvr), BF16),
            jax.ShapeDtypeStruct((T, GLA_H * GLA_DK), F32),
            jax.ShapeDtypeStruct((T, S5_G * S5_N), F32),
        ],
        compiler_params=pltpu.CompilerParams(
            dimension_semantics=("arbitrary",), vmem_limit_bytes=VMEM_LIMIT),
        name="proj",
    )(x, g1, wq, wg, wu, w2, gb)


def _gla_kernel(q_ref, k_ref, v_ref, r_ref, lg_ref, gn_ref, tril_ref, esel_ref, o_ref, state_ref):
    C, S = GLA_CHUNK, GLA_SUB
    nsub = C // S

    @pl.when(pl.program_id(0) == 0)
    def _():
        state_ref[...] = jnp.zeros_like(state_ref)

    row = lax.broadcasted_iota(jnp.int32, (C, C), 0)
    col = lax.broadcasted_iota(jnp.int32, (C, C), 1)
    below = (col // S) < (row // S)
    diag = ((col // S) == (row // S)) & (col <= row)
    tril = tril_ref[...]
    esel = esel_ref[...]
    gn = gn_ref[...]

    def head(h, rows, b):
        kcols = slice(h * GLA_DK, (h + 1) * GLA_DK)
        vcols = slice(h * GLA_DV, (h + 1) * GLA_DV)
        q = q_ref[rows, kcols].astype(F32)
        k = k_ref[rows, kcols].astype(F32)
        v = v_ref[rows, vcols]
        bl = b[C - 1:C, :]
        st = state_ref[h]
        o = _dot((q * jnp.exp(b)).astype(BF16), st.astype(BF16), NT)
        s_rows = [jnp.zeros((S, C), F32)]
        for j in range(1, nsub):
            ref = b[j * S - 1:j * S, :]
            qj = (q[j * S:(j + 1) * S] * jnp.exp(b[j * S:(j + 1) * S] - ref)).astype(BF16)
            kj = (k * jnp.exp(jnp.minimum(ref - b, 0.0))).astype(BF16)
            s_rows.append(_dot(qj, kj, NT))
        s_off = jnp.concatenate(s_rows, axis=0)
        q3 = q.reshape(nsub, S, GLA_DK)
        k3 = k.reshape(nsub, S, GLA_DK)
        b3 = b.reshape(nsub, S, GLA_DK)
        xs = []
        for s in range(S):
            dec = jnp.exp(jnp.minimum(b3 - b3[:, s:s + 1, :], 0.0))
            xs.append((q3 * k3[:, s:s + 1, :] * dec).reshape(C, GLA_DK).astype(BF16))
        dsc = _dot(jnp.concatenate(xs, axis=1), esel)
        a = jnp.where(below, s_off, jnp.where(diag, dsc, 0.0))
        o = o + _dot(a.astype(BF16), v)
        kout = (k * jnp.exp(bl - b)).astype(BF16)
        state_ref[h] = st * jnp.exp(bl) + _dot(v, kout, TN)
        y = _rms(o, gn)
        r = r_ref[rows, vcols].astype(F32)
        o_ref[rows, vcols] = (y * (r * _sigmoid(r))).astype(BF16)

    def chunk(c, carry):
        rows = pl.ds(pl.multiple_of(c * C, C), C)
        b = _dot(tril, lg_ref[rows, :], precision=HIGHEST)
        for h in range(GLA_H):
            head(h, rows, b[:, h * GLA_DK:(h + 1) * GLA_DK])
        return carry

    lax.fori_loop(0, q_ref.shape[0] // C, chunk, 0)


def _gla(qkvr, lg, gn):
    tb = 512
    C, S = GLA_CHUNK, GLA_SUB
    nk, nv = GLA_H * GLA_DK, GLA_H * GLA_DV
    tril = jnp.tril(jnp.ones((C, C), F32))
    esel = (jnp.arange(S * GLA_DK)[:, None] // GLA_DK == jnp.arange(C)[None, :] % S).astype(BF16)
    return pl.pallas_call(
        _gla_kernel,
        grid=(T // tb,),
        in_specs=[
            pl.BlockSpec((tb, nk), lambda i: (i, 0)),
            pl.BlockSpec((tb, nk), lambda i: (i, 1)),
            pl.BlockSpec((tb, nv), lambda i: (i, 1)),
            pl.BlockSpec((tb, nv), lambda i: (i, 2)),
            pl.BlockSpec((tb, nk), lambda i: (i, 0)),
            _const_spec((1, GLA_DV)), _const_spec((C, C)), _const_spec((S * GLA_DK, C)),
        ],
        out_specs=pl.BlockSpec((tb, nv), lambda i: (i, 0)),
        out_shape=jax.ShapeDtypeStruct((T, nv), BF16),
        scratch_shapes=[pltpu.VMEM((GLA_H, GLA_DV, GLA_DK), F32)],
        compiler_params=pltpu.CompilerParams(
            dimension_semantics=("arbitrary",), vmem_limit_bytes=VMEM_LIMIT),
        name="gla",
    )(qkvr, qkvr, qkvr, qkvr, lg, gn, tril, esel)


def _s5_tables(a_re, a_im, b_re, b_im, c_re, c_im, d_skip, log_step):
    L, G, P, N = S5_L, S5_G, S5_P, S5_N
    delta = jnp.exp(log_step)[:, None]
    ar, ai = a_re * delta, a_im * delta

    def apow(tau):
        tau = jnp.asarray(tau, F32)[None, :, None]
        mag = jnp.exp(ar[:, None, :] * tau)
        ph = ai[:, None, :] * tau
        return mag * jnp.cos(ph), mag * jnp.sin(ph)

    p_re, p_im = apow(jnp.arange(L + 1))
    ab_re, ab_im = p_re[:, 1], p_im[:, 1]
    den = a_re * a_re + a_im * a_im
    cf_re = ((ab_re - 1.0) * a_re + ab_im * a_im) / den
    cf_im = (ab_im * a_re - (ab_re - 1.0) * a_im) / den
    bb_re = cf_re[:, :, None] * b_re - cf_im[:, :, None] * b_im
    bb_im = cf_re[:, :, None] * b_im + cf_im[:, :, None] * b_re
    e_re = p_re[:, :L, :, None] * bb_re[:, None] - p_im[:, :L, :, None] * bb_im[:, None]
    e_im = p_re[:, :L, :, None] * bb_im[:, None] + p_im[:, :L, :, None] * bb_re[:, None]
    kk = (jnp.einsum('gnp,gtpm->gtnm', c_re, e_re, precision=HIGHEST)
          - jnp.einsum('gnp,gtpm->gtnm', c_im, e_im, precision=HIGHEST))
    kk = kk.at[:, 0].add(jnp.eye(N, dtype=F32)[None] * d_skip[:, :, None])
    r_re, r_im = apow(L - 1 - jnp.arange(L))
    bs_re = (r_re[:, :, None, :] * bb_re.transpose(0, 2, 1)[:, None] - r_im[:, :, None, :] * bb_im.transpose(0, 2, 1)[:, None])
    bs_im = (r_re[:, :, None, :] * bb_im.transpose(0, 2, 1)[:, None] + r_im[:, :, None, :] * bb_re.transpose(0, 2, 1)[:, None])
    ct_re = c_re.transpose(0, 2, 1)[:, :, None, :]
    ct_im = c_im.transpose(0, 2, 1)[:, :, None, :]
    q_re = p_re[:, 1:].transpose(0, 2, 1)[:, :, :, None]
    q_im = p_im[:, 1:].transpose(0, 2, 1)[:, :, :, None]
    cre = ct_re * q_re - ct_im * q_im
    cim = -(ct_re * q_im + ct_im * q_re)
    GT = S5_GT
    nt = G // GT

    def tile_groups(w):
        return w.reshape((nt, GT) + w.shape[1:])

    kc = tile_groups(kk).transpose(0, 1, 4, 2, 3).reshape(nt, GT * N, L * N)

    def in_table(w):
        return tile_groups(w).transpose(0, 2, 1, 3, 4).reshape(nt, L * GT * N, P)

    def out_table(w):
        return w.reshape(nt, GT * P, L * N)

    nstep = (T // L - 1).bit_length()
    sc_re, sc_im = apow(jnp.asarray([L * 2 ** i for i in range(nstep)]))

    def lanes(w):
        return tile_groups(w).transpose(0, 2, 1, 3).reshape(nt, w.shape[1], GT * P)

    r1 = jnp.arange(L * N)[:, None]
    c1 = jnp.arange(L * GT * N)[None, :]
    wide_tn = ((r1 // N == c1 // (GT * N)) & (r1 % N == c1 % N)).astype(BF16)
    wide_p = (jnp.arange(P)[:, None] == jnp.arange(GT * P)[None, :] % P).astype(BF16)
    return dict(
        kc=kc.astype(BF16),
        bs_re=in_table(bs_re).astype(BF16), bs_im=in_table(bs_im).astype(BF16),
        cre=out_table(cre).astype(BF16), cim=out_table(cim).astype(BF16),
        sc_re=lanes(sc_re), sc_im=lanes(sc_im), wide_tn=wide_tn, wide_p=wide_p,
    )


def _s5_kernel(u_ref, kc_ref, bsr_ref, bsi_ref, cre_ref, cim_ref, scr_ref, sci_ref, wtn_ref, wp_ref,
               y_ref, m_ref):
    L, N, P = S5_L, S5_N, S5_P
    W = S5_GT * N
    nch = u_ref.shape[0] // L

    @pl.when(pl.program_id(0) == 0)
    def _():
        m_ref[...] = jnp.zeros_like(m_ref)

    def widen(compact, wide, row_group, col_group):
        full = _dot(compact, wide)
        r = lax.broadcasted_iota(jnp.int32, full.shape, 0)
        c = lax.broadcasted_iota(jnp.int32, full.shape, 1)
        return jnp.where(row_group(r) == col_group(c), full, 0.0).astype(BF16)

    wtn, wp = wtn_ref[...], wp_ref[...]
    bd = widen(kc_ref[0], wtn, lambda r: r // N, lambda c: (c % W) // N)
    bsr = widen(bsr_ref[0], wp, lambda r: (r % W) // N, lambda c: c // P)
    bsi = widen(bsi_ref[0], wp, lambda r: (r % W) // N, lambda c: c // P)
    cre = widen(cre_ref[0], wtn, lambda r: r // P, lambda c: (c % W) // N)
    cim = widen(cim_ref[0], wtn, lambda r: r // P, lambda c: (c % W) // N)
    for s in range(L):
        m_ref[s * W:(s + 1) * W, s * W:] = bd[:, :(L - s) * W]
    ucat = jnp.concatenate([u_ref[pl.ds(s, nch, stride=L), :].astype(BF16) for s in range(L)], axis=1)
    y = _dot(ucat, m_ref[...])
    hr = _dot(ucat, bsr)
    hi = _dot(ucat, bsi)
    row = lax.broadcasted_iota(jnp.int32, hr.shape, 0)

    def shift(x, d):
        if d % 8 == 0:
            return jnp.concatenate([jnp.zeros((d, x.shape[1]), F32), x[:nch - d]], axis=0)
        return jnp.where(row >= d, pltpu.roll(x, d, 0), 0.0)

    for i in range(scr_ref.shape[1]):
        mr, mi = scr_ref[0, i:i + 1, :], sci_ref[0, i:i + 1, :]
        pr, pi = shift(hr, 2 ** i), shift(hi, 2 ** i)
        hr, hi = hr + mr * pr - mi * pi, hi + mr * pi + mi * pr
    gr, gi = shift(hr, 1), shift(hi, 1)
    y = y + _dot(gr.astype(BF16), cre) + _dot(gi.astype(BF16), cim)
    for t in range(L):
        y_ref[pl.ds(t, nch, stride=L), :] = y[:, t * W:(t + 1) * W]


def _s5(u, tb):
    W = S5_GT * S5_N
    nt = S5_G // S5_GT

    def tile_spec(a):
        return pl.BlockSpec((1,) + a.shape[1:], lambda g: (g,) + (0,) * (a.ndim - 1))

    ws = [tb[n] for n in ["kc", "bs_re", "bs_im", "cre", "cim", "sc_re", "sc_im"]]
    consts = [tb["wide_tn"], tb["wide_p"]]
    return pl.pallas_call(
        _s5_kernel,
        grid=(nt,),
        in_specs=([pl.BlockSpec((T, W), lambda g: (0, g))] + [tile_spec(w) for w in ws]
                  + [_const_spec(c.shape) for c in consts]),
        out_specs=pl.BlockSpec((T, W), lambda g: (0, g)),
        out_shape=jax.ShapeDtypeStruct((T, S5_G * S5_N), F32),
        scratch_shapes=[pltpu.VMEM((S5_L * W, S5_L * W), BF16)],
        compiler_params=pltpu.CompilerParams(
            dimension_semantics=("arbitrary",), vmem_limit_bytes=VMEM_LIMIT),
        name="s5",
    )(u, *ws, *consts)


def _post_kernel(y_ref, gla_ref, x_ref, gw_ref, gb_ref, sg_ref, wo_ref, n2_ref, rw_ref, rb_ref, before_ref,
                 h_ref, hn_ref, idx_ref, gate_ref, rank_ref, cnt_ref, run_ref):
    @pl.when(pl.program_id(0) == 0)
    def _():
        run_ref[...] = jnp.zeros_like(run_ref)

    y = y_ref[...]
    z = 0.5 * y * (1.0 + jnp.tanh(math.sqrt(2.0 / math.pi) * (y + 0.044715 * (y * y * y))))
    z = z * _sigmoid(_dot(z.astype(BF16), gw_ref[...]) + gb_ref[...])
    s5o = _rms(z, sg_ref[...]).astype(BF16)
    half = GLA_H * GLA_DV
    h = x_ref[...] + _dot(gla_ref[...], wo_ref[:half, :]) + _dot(s5o, wo_ref[half:, :])
    h_ref[...] = h
    hn = _rms(h, n2_ref[...])
    hn_ref[...] = hn
    hi = hn.astype(BF16)
    lo = (hn - hi.astype(F32)).astype(BF16)
    rw = rw_ref[...]
    rw_hi = rw.astype(BF16)
    rw_lo = (rw - rw_hi.astype(F32)).astype(BF16)
    lg = _dot(hi, rw_hi) + (_dot(lo, rw_hi) + _dot(hi, rw_lo))
    lt = lg.T[:N_EXP] + rb_ref[...]
    eid = lax.broadcasted_iota(jnp.int32, lt.shape, 0).astype(F32)
    vals, idxs = [], []
    for _ in range(TOP_K):
        m = jnp.max(lt, axis=0, keepdims=True)
        sel = jnp.min(jnp.where(lt == m, eid, float(N_EXP)), axis=0, keepdims=True)
        vals.append(m)
        idxs.append(sel)
        lt = jnp.where(eid == sel, -jnp.inf, lt)
    ex = [jnp.exp(vv - vals[0]) for vv in vals]
    inv = 1.0 / (ex[0] + ex[1] + ex[2] + ex[3])
    idx_ref[...] = jnp.concatenate(idxs, axis=0).astype(jnp.int32)
    gate_ref[...] = jnp.concatenate([e * inv for e in ex], axis=0)
    run = run_ref[:, :1]
    ranks = []
    for sel in idxs:
        onehot = jnp.where(eid == sel, 1.0, 0.0)
        earlier = _dot(onehot.astype(BF16), before_ref[...])
        ranks.append(jnp.sum(onehot * (run + earlier), axis=0, keepdims=True))
        run = run + jnp.sum(onehot, axis=1, keepdims=True)
    rank_ref[...] = jnp.concatenate(ranks, axis=0).astype(jnp.int32)
    run_ref[...] = jnp.broadcast_to(run, run_ref.shape)
    cnt_ref[...] = jnp.broadcast_to(run, cnt_ref.shape)


def _post(y, gla, x, gw, gb, sg, wo, n2, rwt, rb):
    tm = 256
    width = S5_G * S5_N
    before = (jnp.arange(tm)[:, None] < jnp.arange(tm)[None, :]).astype(BF16)
    return pl.pallas_call(
        _post_kernel,
        grid=(T // tm,),
        in_specs=[
            pl.BlockSpec((tm, width), lambda i: (i, 0)),
            pl.BlockSpec((tm, GLA_H * GLA_DV), lambda i: (i, 0)),
            pl.BlockSpec((tm, D), lambda i: (i, 0)),
            _const_spec(gw.shape), _const_spec(gb.shape), _const_spec(sg.shape), _const_spec(wo.shape),
            _const_spec(n2.shape), _const_spec(rwt.shape), _const_spec(rb.shape), _const_spec(before.shape),
        ],
        out_specs=[
            pl.BlockSpec((tm, D), lambda i: (i, 0)),
            pl.BlockSpec((tm, D), lambda i: (i, 0)),
            pl.BlockSpec((TOP_K, tm), lambda i: (0, i)),
            pl.BlockSpec((TOP_K, tm), lambda i: (0, i)),
            pl.BlockSpec((TOP_K, tm), lambda i: (0, i)),
            pl.BlockSpec((N_EXP, 128), lambda i: (0, 0)),
        ],
        out_shape=[
            jax.ShapeDtypeStruct((T, D), F32),
            jax.ShapeDtypeStruct((T, D), F32),
            jax.ShapeDtypeStruct((TOP_K, T), jnp.int32),
            jax.ShapeDtypeStruct((TOP_K, T), F32),
            jax.ShapeDtypeStruct((TOP_K, T), jnp.int32),
            jax.ShapeDtypeStruct((N_EXP, 128), F32),
        ],
        scratch_shapes=[pltpu.VMEM((N_EXP, 128), F32)],
        compiler_params=pltpu.CompilerParams(
            dimension_semantics=("arbitrary",), vmem_limit_bytes=VMEM_LIMIT),
        name="post",
    )(y, gla, x, gw, gb, sg, wo, n2, rwt, rb, before)


def _route(top_idx, rank, counts):
    e_flat = top_idx.reshape(-1)
    rank = rank.reshape(-1)
    counts = counts[:, 0].astype(jnp.int32)
    padded = (counts + MOE_RB - 1) // MOE_RB * MOE_RB
    pad_end = jnp.cumsum(padded)
    pad_start = pad_end - padded
    dest = (pad_start[e_flat] + rank).astype(jnp.int32)
    nvis_e = (counts + MOE_RMAX - 1) // MOE_RMAX
    vis_end = jnp.cumsum(nvis_e)
    n_vis = vis_end[-1]
    v = jnp.arange(MOE_NV, dtype=jnp.int32)
    vc = jnp.minimum(v, n_vis - 1)
    ve = jnp.minimum(jnp.searchsorted(vis_end, vc, side='right'), N_EXP - 1).astype(jnp.int32)
    local = vc - (vis_end[ve] - nvis_e[ve])
    vcnt = jnp.where(v < n_vis, jnp.minimum(MOE_RMAX, counts[ve] - local * MOE_RMAX), 0)
    vrow = pad_start[ve] + local * MOE_RMAX
    return dict(dest=dest, fill_from=(pad_start + counts).astype(jnp.int32), fill_to=pad_end.astype(jnp.int32),
                tail_blk=(pad_end[-1:] // MOE_RB).astype(jnp.int32),
                vis_e=ve, vis_cnt=vcnt.astype(jnp.int32), vis_row=vrow.astype(jnp.int32),
                n_vis=n_vis.reshape(1).astype(jnp.int32))


def _row_copy(src_ref, srow, dst_ref, drow, sem):
    return pltpu.make_async_copy(src_ref.at[pl.ds(srow, 1), :], dst_ref.at[pl.ds(drow, 1), :], sem)


def _tail_fill(src_ref, dst_ref, tail_ref, sem):
    def cp(b):
        return pltpu.make_async_copy(src_ref.at[pl.ds(0, MOE_RB), :],
                                     dst_ref.at[pl.ds(pl.multiple_of(b * MOE_RB, MOE_RB), MOE_RB), :], sem)

    def start(b, c):
        cp(b).start()
        return c
    lax.fori_loop(tail_ref[0], MOE_ROWS // MOE_RB, start, 0)

    def finish(b, c):
        cp(b).wait()
        return c
    lax.fori_loop(tail_ref[0], MOE_ROWS // MOE_RB, finish, 0)


def _scatter_kernel(dest_ref, from_ref, to_ref, tail_ref, hn_ref, xs_ref, zero_ref, sem, zsem):
    i = pl.program_id(0)
    tm = hn_ref.shape[0]

    @pl.when(i == 0)
    def _():
        zero_ref[...] = jnp.zeros_like(zero_ref)
        _tail_fill(zero_ref, xs_ref, tail_ref, zsem)

        def per_expert(e, c):
            def fill(r, c2):
                _row_copy(zero_ref, 0, xs_ref, r, zsem).start()
                return c2
            lax.fori_loop(from_ref[e], to_ref[e], fill, 0)

            def drain(r, c2):
                _row_copy(zero_ref, 0, xs_ref, r, zsem).wait()
                return c2
            lax.fori_loop(from_ref[e], to_ref[e], drain, 0)
            return c
        lax.fori_loop(0, N_EXP, per_expert, 0)

    def issue(t, c):
        for k in range(TOP_K):
            _row_copy(hn_ref, t, xs_ref, dest_ref[k * T + i * tm + t], sem).start()
        return c
    lax.fori_loop(0, tm, issue, 0)

    def drain(t, c):
        for k in range(TOP_K):
            _row_copy(hn_ref, t, xs_ref, dest_ref[k * T + i * tm + t], sem).wait()
        return c
    lax.fori_loop(0, tm, drain, 0)


def _scatter(hn, rt):
    tm = 256
    return pl.pallas_call(
        _scatter_kernel,
        grid_spec=pltpu.PrefetchScalarGridSpec(
            num_scalar_prefetch=4,
            grid=(T // tm,),
            in_specs=[pl.BlockSpec((tm, D), lambda i, *_: (i, 0))],
            out_specs=pl.BlockSpec(memory_space=pl.ANY),
            scratch_shapes=[pltpu.VMEM((MOE_RB, D), F32), pltpu.SemaphoreType.DMA, pltpu.SemaphoreType.DMA],
        ),
        out_shape=jax.ShapeDtypeStruct((MOE_ROWS, D), F32),
        compiler_params=pltpu.CompilerParams(
            dimension_semantics=("arbitrary",), vmem_limit_bytes=VMEM_LIMIT),
        name="scatter",
    )(rt["dest"], rt["fill_from"], rt["fill_to"], rt["tail_blk"], hn)


def _experts_kernel(ve_ref, vcnt_ref, vrow_ref, nvis_ref, tail_ref, xs_ref, w1g_ref, w1l_ref, w2_ref, b1g_ref,
                    b1l_ref, b2_ref, ys_ref, xin_ref, xb_ref, acc_ref, isem, osem):
    v = pl.program_id(0)
    j = pl.program_id(1)
    RB = MOE_RB
    BIG = 2 * RB
    nvis = nvis_ref[0]
    active = v < nvis
    first = j == 0
    last = j == MOE_NF - 1

    def geometry(vv):
        nblk = (vcnt_ref[vv] + RB - 1) // RB
        return vrow_ref[vv], nblk // 2, nblk % 2

    row0, nbig, rem = geometry(v)

    def x_copy(vrow, r0, n, slot):
        return pltpu.make_async_copy(xs_ref.at[pl.ds(pl.multiple_of(vrow + r0, RB), n), :],
                                     xin_ref.at[slot, pl.ds(0, n), :], isem.at[slot])

    def y_copy(r0, n):
        return pltpu.make_async_copy(acc_ref.at[pl.ds(pl.multiple_of(r0, RB), n), :],
                                     ys_ref.at[pl.ds(pl.multiple_of(row0 + r0, RB), n), :], osem)

    def start_first(vv):
        vrow, nb, _ = geometry(vv)

        @pl.when(nb > 0)
        def _():
            x_copy(vrow, 0, BIG, 0).start()

        @pl.when(nb == 0)
        def _():
            x_copy(vrow, 0, RB, 0).start()

    @pl.when((v == 0) & first)
    def _():
        acc_ref[...] = jnp.zeros_like(acc_ref)
        _tail_fill(acc_ref, ys_ref, tail_ref, osem)
        start_first(0)

    def block(r0, n, slot, start_next):
        rows = pl.ds(pl.multiple_of(r0, RB), n)

        @pl.when(first)
        def _():
            x_copy(row0, r0, n, slot).wait()
            start_next()
            xb_ref[rows, :] = xin_ref[slot, pl.ds(0, n), :].astype(BF16)

        x = xb_ref[rows, :]
        glu = jnp.minimum(_dot(x, w1g_ref[0].astype(BF16)) + b1g_ref[0], SWIGLU_LIMIT)
        lin = jnp.clip(_dot(x, w1l_ref[0].astype(BF16)) + b1l_ref[0], -SWIGLU_LIMIT, SWIGLU_LIMIT)
        act = glu * _sigmoid(SWIGLU_ALPHA * glu) * (lin + 1.0)
        part = _dot(act.astype(BF16), w2_ref[0].astype(BF16))

        acc_ref[rows, :] = part + jnp.where(first, b2_ref[0], acc_ref[rows, :])

        @pl.when(last)
        def _():
            y_copy(r0, n).start()

    @pl.when(active)
    def _():
        def big(i, c):
            slot = i % 2

            def start_next():
                @pl.when(i + 1 < nbig)
                def _():
                    x_copy(row0, (i + 1) * BIG, BIG, 1 - slot).start()

                @pl.when((i + 1 == nbig) & (rem == 1))
                def _():
                    x_copy(row0, nbig * BIG, RB, 1 - slot).start()

            block(i * BIG, BIG, slot, start_next)
            return c
        lax.fori_loop(0, nbig, big, 0)

        @pl.when(rem == 1)
        def _():
            block(nbig * BIG, RB, nbig % 2, lambda: None)

    @pl.when(active & last)
    def _():
        def finish(i, c):
            y_copy(i * BIG, BIG).wait()
            return c
        lax.fori_loop(0, nbig, finish, 0)

        @pl.when(rem == 1)
        def _():
            y_copy(nbig * BIG, RB).wait()

        @pl.when(v + 1 < nvis)
        def _():
            start_first(v + 1)


def _experts(xs, rt, w1, b1, w2, b2):
    b1 = b1.reshape(N_EXP, 1, 2 * D_FF)
    b2 = b2.reshape(N_EXP, 1, D)

    def jj(v, j, ve, vcnt, vrow, nvis, tail):
        return jnp.where(v < nvis[0], j, MOE_NF - 1)

    return pl.pallas_call(
        _experts_kernel,
        grid_spec=pltpu.PrefetchScalarGridSpec(
            num_scalar_prefetch=5,
            grid=(MOE_NV, MOE_NF),
            in_specs=[
                pl.BlockSpec(memory_space=pl.ANY),
                pl.BlockSpec((1, D, MOE_TF), lambda v, j, ve, *s: (ve[v], 0, jj(v, j, ve, *s))),
                pl.BlockSpec((1, D, MOE_TF), lambda v, j, ve, *s: (ve[v], 0, MOE_NF + jj(v, j, ve, *s))),
                pl.BlockSpec((1, MOE_TF, D), lambda v, j, ve, *s: (ve[v], jj(v, j, ve, *s), 0)),
                pl.BlockSpec((1, 1, MOE_TF), lambda v, j, ve, *s: (ve[v], 0, jj(v, j, ve, *s))),
                pl.BlockSpec((1, 1, MOE_TF), lambda v, j, ve, *s: (ve[v], 0, MOE_NF + jj(v, j, ve, *s))),
                pl.BlockSpec((1, 1, D), lambda v, j, ve, *s: (ve[v], 0, 0)),
            ],
            out_specs=pl.BlockSpec(memory_space=pl.ANY),
            scratch_shapes=[
                pltpu.VMEM((2, 2 * MOE_RB, D), F32),
                pltpu.VMEM((MOE_RMAX, D), BF16),
                pltpu.VMEM((MOE_RMAX, D), F32),
                pltpu.SemaphoreType.DMA((2,)),
                pltpu.SemaphoreType.DMA,
            ],
        ),
        out_shape=jax.ShapeDtypeStruct((MOE_ROWS, D), F32),
        compiler_params=pltpu.CompilerParams(
            dimension_semantics=("arbitrary", "arbitrary"), vmem_limit_bytes=VMEM_LIMIT),
        name="experts",
    )(rt["vis_e"], rt["vis_cnt"], rt["vis_row"], rt["n_vis"], rt["tail_blk"], xs, w1, w1, w2, b1, b1, b2)


def _combine_kernel(dest_ref, ys_ref, h_ref, gate_ref, gf_ref, o_ref, buf_ref, sem):
    i = pl.program_id(0)
    tm = h_ref.shape[0]

    def issue(t, c):
        for k in range(TOP_K):
            _row_copy(ys_ref, dest_ref[k * T + i * tm + t], buf_ref.at[k], t, sem).start()
        return c
    lax.fori_loop(0, tm, issue, 0)

    def drain(t, c):
        for k in range(TOP_K):
            _row_copy(ys_ref, dest_ref[k * T + i * tm + t], buf_ref.at[k], t, sem).wait()
        return c
    lax.fori_loop(0, tm, drain, 0)

    h = h_ref[...]
    for k in range(TOP_K):
        h = h + gate_ref[:, k:k + 1] * buf_ref[k]
    o_ref[...] = _rms(h, gf_ref[...])


def _combine(ys, h, gates_t, gf, rt):
    tm = 256
    return pl.pallas_call(
        _combine_kernel,
        grid_spec=pltpu.PrefetchScalarGridSpec(
            num_scalar_prefetch=1,
            grid=(T // tm,),
            in_specs=[
                pl.BlockSpec(memory_space=pl.ANY),
                pl.BlockSpec((tm, D), lambda i, *_: (i, 0)),
                pl.BlockSpec((tm, TOP_K), lambda i, *_: (i, 0)),
                pl.BlockSpec((1, D), lambda i, *_: (0, 0)),
            ],
            out_specs=pl.BlockSpec((tm, D), lambda i, *_: (i, 0)),
            scratch_shapes=[pltpu.VMEM((TOP_K, tm, D), F32), pltpu.SemaphoreType.DMA],
        ),
        out_shape=jax.ShapeDtypeStruct((T, D), F32),
        compiler_params=pltpu.CompilerParams(
            dimension_semantics=("arbitrary",), vmem_limit_bytes=VMEM_LIMIT),
        name="combine",
    )(rt["dest"], ys, h, gates_t, gf)


def kernel(x, norm1_g, w_in, gla_gate_w2, gla_gate_b, gla_norm_g, s5_a_re, s5_a_im, s5_b_re, s5_b_im, s5_c_re, s5_c_im, s5_d, s5_log_step, s5_glu_w, s5_glu_b, s5_norm_g, w_out, norm2_g, router_w, router_b, expert_w1, expert_b1, expert_w2, expert_b2, final_norm_g):
    assert x.shape == (1, T, D) and w_in.shape[0] == 1
    xt = x.reshape(T, D)
    nqkvr = 2 * GLA_H * GLA_DK + 2 * GLA_H * GLA_DV
    wi = w_in[0]
    wq = wi[:, :nqkvr].astype(BF16)
    wg = jnp.pad(wi[:, nqkvr:nqkvr + GATE_RANK], ((0, 0), (0, 128 - GATE_RANK))).astype(BF16)
    wu = wi[:, nqkvr + GATE_RANK:].astype(BF16)
    w2g = jnp.pad(gla_gate_w2[0], ((0, 128 - GATE_RANK), (0, 0)))
    qkvr, lg, u = _proj(xt, norm1_g, wq, wg, wu, w2g, gla_gate_b)
    gla = _gla(qkvr, lg, gla_norm_g)
    tables = _s5_tables(s5_a_re[0], s5_a_im[0], s5_b_re[0], s5_b_im[0], s5_c_re[0], s5_c_im[0],
                        s5_d[0], s5_log_step[0])
    y = _s5(u, tables)
    h, hn, top_idx, gates, rank, counts = _post(
        y, gla, xt, s5_glu_w[0].astype(BF16), s5_glu_b, s5_norm_g, w_out[0].astype(BF16), norm2_g,
        jnp.pad(router_w[0], ((0, 0), (0, 128 - N_EXP))), router_b.reshape(N_EXP, 1))
    rt = _route(top_idx, rank, counts)
    xs = _scatter(hn, rt)
    ys = _experts(xs, rt, expert_w1[0], expert_b1[0], expert_w2[0], expert_b2[0])
    out = _combine(ys, h, gates.T, final_norm_g.reshape(1, D), rt)
    return out.reshape(1, T, D)
```

```python
import functools
import math

import jax
import jax.numpy as jnp
from jax import lax
from jax.experimental import pallas as pl
from jax.experimental.pallas import tpu as pltpu

F32 = jnp.float32
BF16 = jnp.bfloat16
HIGHEST = lax.Precision.HIGHEST

T = 8192
D = 2048
GLA_H = 4
GLA_DK = 128
GLA_DV = 256
GLA_CHUNK = 64
GLA_SUB = 16
GATE_RANK = 16
GATE_TAU = 16.0
S5_G = 64
S5_N = 16
S5_P = 64
S5_L = 16
S5_GT = 8
N_EXP = 32
TOP_K = 4
D_FF = 2048
SWIGLU_ALPHA = 1.702
SWIGLU_LIMIT = 7.0
RMS_EPS = 1e-6

MOE_RB = 128
MOE_RMAX = 1536
MOE_TF = 512
MOE_NF = D_FF // MOE_TF
MOE_NV = -(-T * TOP_K // MOE_RMAX) + N_EXP
MOE_ROWS = T * TOP_K + N_EXP * MOE_RB
VMEM_LIMIT = 56 * 1024 * 1024

NN = (((1,), (0,)), ((), ()))
NT = (((1,), (1,)), ((), ()))
TN = (((0,), (0,)), ((), ()))


def _dot(a, b, dims=NN, precision=None):
    return lax.dot_general(a, b, dims, preferred_element_type=F32, precision=precision)


def _rms(x, g):
    return x * lax.rsqrt(jnp.mean(x * x, axis=-1, keepdims=True) + RMS_EPS) * g


def _sigmoid(x):
    return 1.0 / (1.0 + jnp.exp(-x))


def _const_spec(shape):
    nd = len(shape)
    return pl.BlockSpec(shape, lambda *_: (0,) * nd, pipeline_mode=pl.Buffered(1))


def _proj_kernel(x_ref, g1_ref, wq_ref, wg_ref, wu_ref, w2_ref, gb_ref, qkvr_ref, lg_ref, u_ref):
    hb = _rms(x_ref[...], g1_ref[...]).astype(BF16)
    p = _dot(hb, wq_ref[...])
    nq = GLA_H * GLA_DK
    qkvr_ref[:, :nq] = (p[:, :nq] * (GLA_DK ** -0.5)).astype(BF16)
    qkvr_ref[:, nq:] = p[:, nq:].astype(BF16)
    glr = _dot(hb, wg_ref[...])
    z = _dot(glr, w2_ref[...], precision=HIGHEST) + gb_ref[...]
    lg_ref[...] = (jnp.minimum(z, 0.0) - jnp.log(1.0 + jnp.exp(-jnp.abs(z)))) * (1.0 / GATE_TAU)
    u_ref[...] = _dot(hb, wu_ref[...])


def _proj(x, g1, wq, wg, wu, w2, gb):
    tm = 256
    nqkvr = wq.shape[1]
    return pl.pallas_call(
        _proj_kernel,
        grid=(T // tm,),
        in_specs=[
            pl.BlockSpec((tm, D), lambda i: (i, 0)),
            _const_spec((1, D)),
            _const_spec(wq.shape), _const_spec(wg.shape), _const_spec(wu.shape),
            _const_spec(w2.shape), _const_spec(gb.shape),
        ],
        out_specs=[
            pl.BlockSpec((tm, nqkvr), lambda i: (i, 0)),
            pl.BlockSpec((tm, GLA_H * GLA_DK), lambda i: (i, 0)),
            pl.BlockSpec((tm, S5_G * S5_N), lambda i: (i, 0)),
        ],
        out_shape=[
            jax.ShapeDtypeStruct((T, nqkvr), BF16),
            jax.ShapeDtypeStruct((T, GLA_H * GLA_DK), F32),
            jax.ShapeDtypeStruct((T, S5_G * S5_N), F32),
        ],
        compiler_params=pltpu.CompilerParams(
            dimension_semantics=("arbitrary",), vmem_limit_bytes=VMEM_LIMIT),
        name="proj",
    )(x, g1, wq, wg, wu, w2, gb)


def _gla_kernel(q_ref, k_ref, v_ref, r_ref, lg_ref, gn_ref, tril_ref, esel_ref, o_ref, state_ref):
    C, S = GLA_CHUNK, GLA_SUB
    nsub = C // S

    @pl.when(pl.program_id(0) == 0)
    def _():
        state_ref[...] = jnp.zeros_like(state_ref)

    row = lax.broadcasted_iota(jnp.int32, (C, C), 0)
    col = lax.broadcasted_iota(jnp.int32, (C, C), 1)
    below = (col // S) < (row // S)
    diag = ((col // S) == (row // S)) & (col <= row)
    tril = tril_ref[...]
    esel = esel_ref[...]
    gn = gn_ref[...]

    def head(h, rows, b):
        kcols = slice(h * GLA_DK, (h + 1) * GLA_DK)
        vcols = slice(h * GLA_DV, (h + 1) * GLA_DV)
        q = q_ref[rows, kcols].astype(F32)
        k = k_ref[rows, kcols].astype(F32)
        v = v_ref[rows, vcols]
        bl = b[C - 1:C, :]
        st = state_ref[h]
        o = _dot((q * jnp.exp(b)).astype(BF16), st.astype(BF16), NT)
        s_rows = [jnp.zeros((S, C), F32)]
        for j in range(1, nsub):
            ref = b[j * S - 1:j * S, :]
            qj = (q[j * S:(j + 1) * S] * jnp.exp(b[j * S:(j + 1) * S] - ref)).astype(BF16)
            kj = (k * jnp.exp(jnp.minimum(ref - b, 0.0))).astype(BF16)
            s_rows.append(_dot(qj, kj, NT))
        s_off = jnp.concatenate(s_rows, axis=0)
        q3 = q.reshape(nsub, S, GLA_DK)
        k3 = k.reshape(nsub, S, GLA_DK)
        b3 = b.reshape(nsub, S, GLA_DK)
        xs = []
        for s in range(S):
            dec = jnp.exp(jnp.minimum(b3 - b3[:, s:s + 1, :], 0.0))
            xs.append((q3 * k3[:, s:s + 1, :] * dec).reshape(C, GLA_DK).astype(BF16))
        dsc = _dot(jnp.concatenate(xs, axis=1), esel)
        a = jnp.where(below, s_off, jnp.where(diag, dsc, 0.0))
        o = o + _dot(a.astype(BF16), v)
        kout = (k * jnp.exp(bl - b)).astype(BF16)
        state_ref[h] = st * jnp.exp(bl) + _dot(v, kout, TN)
        y = _rms(o, gn)
        r = r_ref[rows, vcols].astype(F32)
        o_ref[rows, vcols] = (y * (r * _sigmoid(r))).astype(BF16)

    def chunk(c, carry):
        rows = pl.ds(pl.multiple_of(c * C, C), C)
        b = _dot(tril, lg_ref[rows, :], precision=HIGHEST)
        for h in range(GLA_H):
            head(h, rows, b[:, h * GLA_DK:(h + 1) * GLA_DK])
        return carry

    lax.fori_loop(0, q_ref.shape[0] // C, chunk, 0)


def _gla(qkvr, lg, gn):
    tb = 512
    C, S = GLA_CHUNK, GLA_SUB
    nk, nv = GLA_H * GLA_DK, GLA_H * GLA_DV
    tril = jnp.tril(jnp.ones((C, C), F32))
    esel = (jnp.arange(S * GLA_DK)[:, None] // GLA_DK == jnp.arange(C)[None, :] % S).astype(BF16)
    return pl.pallas_call(
        _gla_kernel,
        grid=(T // tb,),
        in_specs=[
            pl.BlockSpec((tb, nk), lambda i: (i, 0)),
            pl.BlockSpec((tb, nk), lambda i: (i, 1)),
            pl.BlockSpec((tb, nv), lambda i: (i, 1)),
            pl.BlockSpec((tb, nv), lambda i: (i, 2)),
            pl.BlockSpec((tb, nk), lambda i: (i, 0)),
            _const_spec((1, GLA_DV)), _const_spec((C, C)), _const_spec((S * GLA_DK, C)),
        ],
        out_specs=pl.BlockSpec((tb, nv), lambda i: (i, 0)),
        out_shape=jax.ShapeDtypeStruct((T, nv), BF16),
        scratch_shapes=[pltpu.VMEM((GLA_H, GLA_DV, GLA_DK), F32)],
        compiler_params=pltpu.CompilerParams(
            dimension_semantics=("arbitrary",), vmem_limit_bytes=VMEM_LIMIT),
        name="gla",
    )(qkvr, qkvr, qkvr, qkvr, lg, gn, tril, esel)


def _s5_tables(a_re, a_im, b_re, b_im, c_re, c_im, d_skip, log_step):
    L, G, P, N = S5_L, S5_G, S5_P, S5_N
    delta = jnp.exp(log_step)[:, None]
    ar, ai = a_re * delta, a_im * delta

    def apow(tau):
        tau = jnp.asarray(tau, F32)[None, :, None]
        mag = jnp.exp(ar[:, None, :] * tau)
        ph = ai[:, None, :] * tau
        return mag * jnp.cos(ph), mag * jnp.sin(ph)

    p_re, p_im = apow(jnp.arange(L + 1))
    ab_re, ab_im = p_re[:, 1], p_im[:, 1]
    den = a_re * a_re + a_im * a_im
    cf_re = ((ab_re - 1.0) * a_re + ab_im * a_im) / den
    cf_im = (ab_im * a_re - (ab_re - 1.0) * a_im) / den
    bb_re = cf_re[:, :, None] * b_re - cf_im[:, :, None] * b_im
    bb_im = cf_re[:, :, None] * b_im + cf_im[:, :, None] * b_re
    e_re = p_re[:, :L, :, None] * bb_re[:, None] - p_im[:, :L, :, None] * bb_im[:, None]
    e_im = p_re[:, :L, :, None] * bb_im[:, None] + p_im[:, :L, :, None] * bb_re[:, None]
    kk = (jnp.einsum('gnp,gtpm->gtnm', c_re, e_re, precision=HIGHEST)
          - jnp.einsum('gnp,gtpm->gtnm', c_im, e_im, precision=HIGHEST))
    kk = kk.at[:, 0].add(jnp.eye(N, dtype=F32)[None] * d_skip[:, :, None])
    r_re, r_im = apow(L - 1 - jnp.arange(L))
    bs_re = (r_re[:, :, None, :] * bb_re.transpose(0, 2, 1)[:, None] - r_im[:, :, None, :] * bb_im.transpose(0, 2, 1)[:, None])
    bs_im = (r_re[:, :, None, :] * bb_im.transpose(0, 2, 1)[:, None] + r_im[:, :, None, :] * bb_re.transpose(0, 2, 1)[:, None])
    ct_re = c_re.transpose(0, 2, 1)[:, :, None, :]
    ct_im = c_im.transpose(0, 2, 1)[:, :, None, :]
    q_re = p_re[:, 1:].transpose(0, 2, 1)[:, :, :, None]
    q_im = p_im[:, 1:].transpose(0, 2, 1)[:, :, :, None]
    cre = ct_re * q_re - ct_im * q_im
    cim = -(ct_re * q_im + ct_im * q_re)
    GT = S5_GT
    nt = G // GT

    def tile_groups(w):
        return w.reshape((nt, GT) + w.shape[1:])

    kc = tile_groups(kk).transpose(0, 1, 4, 2, 3).reshape(nt, GT * N, L * N)

    def in_table(w):
        return tile_groups(w).transpose(0, 2, 1, 3, 4).reshape(nt, L * GT * N, P)

    def out_table(w):
        return w.reshape(nt, GT * P, L * N)

    nstep = (T // L - 1).bit_length()
    sc_re, sc_im = apow(jnp.asarray([L * 2 ** i for i in range(nstep)]))

    def lanes(w):
        return tile_groups(w).transpose(0, 2, 1, 3).reshape(nt, w.shape[1], GT * P)

    r1 = jnp.arange(L * N)[:, None]
    c1 = jnp.arange(L * GT * N)[None, :]
    wide_tn = ((r1 // N == c1 // (GT * N)) & (r1 % N == c1 % N)).astype(BF16)
    wide_p = (jnp.arange(P)[:, None] == jnp.arange(GT * P)[None, :] % P).astype(BF16)
    return dict(
        kc=kc.astype(BF16),
        bs_re=in_table(bs_re).astype(BF16), bs_im=in_table(bs_im).astype(BF16),
        cre=out_table(cre).astype(BF16), cim=out_table(cim).astype(BF16),
        sc_re=lanes(sc_re), sc_im=lanes(sc_im), wide_tn=wide_tn, wide_p=wide_p,
    )


def _s5_kernel(u_ref, kc_ref, bsr_ref, bsi_ref, cre_ref, cim_ref, scr_ref, sci_ref, wtn_ref, wp_ref,
               y_ref, m_ref):
    L, N, P = S5_L, S5_N, S5_P
    W = S5_GT * N
    nch = u_ref.shape[0] // L

    @pl.when(pl.program_id(0) == 0)
    def _():
        m_ref[...] = jnp.zeros_like(m_ref)

    def widen(compact, wide, row_group, col_group):
        full = _dot(compact, wide)
        r = lax.broadcasted_iota(jnp.int32, full.shape, 0)
        c = lax.broadcasted_iota(jnp.int32, full.shape, 1)
        return jnp.where(row_group(r) == col_group(c), full, 0.0).astype(BF16)

    wtn, wp = wtn_ref[...], wp_ref[...]
    bd = widen(kc_ref[0], wtn, lambda r: r // N, lambda c: (c % W) // N)
    bsr = widen(bsr_ref[0], wp, lambda r: (r % W) // N, lambda c: c // P)
    bsi = widen(bsi_ref[0], wp, lambda r: (r % W) // N, lambda c: c // P)
    cre = widen(cre_ref[0], wtn, lambda r: r // P, lambda c: (c % W) // N)
    cim = widen(cim_ref[0], wtn, lambda r: r // P, lambda c: (c % W) // N)
    for s in range(L):
        m_ref[s * W:(s + 1) * W, s * W:] = bd[:, :(L - s) * W]
    ucat = jnp.concatenate([u_ref[pl.ds(s, nch, stride=L), :].astype(BF16) for s in range(L)], axis=1)
    y = _dot(ucat, m_ref[...])
    hr = _dot(ucat, bsr)
    hi = _dot(ucat, bsi)
    row = lax.broadcasted_iota(jnp.int32, hr.shape, 0)

    def shift(x, d):
        if d % 8 == 0:
            return jnp.concatenate([jnp.zeros((d, x.shape[1]), F32), x[:nch - d]], axis=0)
        return jnp.where(row >= d, pltpu.roll(x, d, 0), 0.0)

    for i in range(scr_ref.shape[1]):
        mr, mi = scr_ref[0, i:i + 1, :], sci_ref[0, i:i + 1, :]
        pr, pi = shift(hr, 2 ** i), shift(hi, 2 ** i)
        hr, hi = hr + mr * pr - mi * pi, hi + mr * pi + mi * pr
    gr, gi = shift(hr, 1), shift(hi, 1)
    y = y + _dot(gr.astype(BF16), cre) + _dot(gi.astype(BF16), cim)
    for t in range(L):
        y_ref[pl.ds(t, nch, stride=L), :] = y[:, t * W:(t + 1) * W]


def _s5(u, tb):
    W = S5_GT * S5_N
    nt = S5_G // S5_GT

    def tile_spec(a):
        return pl.BlockSpec((1,) + a.shape[1:], lambda g: (g,) + (0,) * (a.ndim - 1))

    ws = [tb[n] for n in ["kc", "bs_re", "bs_im", "cre", "cim", "sc_re", "sc_im"]]
    consts = [tb["wide_tn"], tb["wide_p"]]
    return pl.pallas_call(
        _s5_kernel,
        grid=(nt,),
        in_specs=([pl.BlockSpec((T, W), lambda g: (0, g))] + [tile_spec(w) for w in ws]
                  + [_const_spec(c.shape) for c in consts]),
        out_specs=pl.BlockSpec((T, W), lambda g: (0, g)),
        out_shape=jax.ShapeDtypeStruct((T, S5_G * S5_N), F32),
        scratch_shapes=[pltpu.VMEM((S5_L * W, S5_L * W), BF16)],
        compiler_params=pltpu.CompilerParams(
            dimension_semantics=("arbitrary",), vmem_limit_bytes=VMEM_LIMIT),
        name="s5",
    )(u, *ws, *consts)


def _post_kernel(y_ref, gla_ref, x_ref, gw_ref, gb_ref, sg_ref, wo_ref, n2_ref, rw_ref, rb_ref, before_ref,
                 h_ref, hn_ref, idx_ref, gate_ref, rank_ref, cnt_ref, run_ref):
    @pl.when(pl.program_id(0) == 0)
    def _():
        run_ref[...] = jnp.zeros_like(run_ref)

    y = y_ref[...]
    z = 0.5 * y * (1.0 + jnp.tanh(math.sqrt(2.0 / math.pi) * (y + 0.044715 * (y * y * y))))
    z = z * _sigmoid(_dot(z.astype(BF16), gw_ref[...]) + gb_ref[...])
    s5o = _rms(z, sg_ref[...]).astype(BF16)
    half = GLA_H * GLA_DV
    h = x_ref[...] + _dot(gla_ref[...], wo_ref[:half, :]) + _dot(s5o, wo_ref[half:, :])
    h_ref[...] = h
    hn = _rms(h, n2_ref[...])
    hn_ref[...] = hn
    hi = hn.astype(BF16)
    lo = (hn - hi.astype(F32)).astype(BF16)
    rw = rw_ref[...]
    rw_hi = rw.astype(BF16)
    rw_lo = (rw - rw_hi.astype(F32)).astype(BF16)
    lg = _dot(hi, rw_hi) + (_dot(lo, rw_hi) + _dot(hi, rw_lo))
    lt = lg.T[:N_EXP] + rb_ref[...]
    eid = lax.broadcasted_iota(jnp.int32, lt.shape, 0).astype(F32)
    vals, idxs = [], []
    for _ in range(TOP_K):
        m = jnp.max(lt, axis=0, keepdims=True)
        sel = jnp.min(jnp.where(lt == m, eid, float(N_EXP)), axis=0, keepdims=True)
        vals.append(m)
        idxs.append(sel)
        lt = jnp.where(eid == sel, -jnp.inf, lt)
    ex = [jnp.exp(vv - vals[0]) for vv in vals]
    inv = 1.0 / (ex[0] + ex[1] + ex[2] + ex[3])
    idx_ref[...] = jnp.concatenate(idxs, axis=0).astype(jnp.int32)
    gate_ref[...] = jnp.concatenate([e * inv for e in ex], axis=0)
    run = run_ref[:, :1]
    ranks = []
    for sel in idxs:
        onehot = jnp.where(eid == sel, 1.0, 0.0)
        earlier = _dot(onehot.astype(BF16), before_ref[...])
        ranks.append(jnp.sum(onehot * (run + earlier), axis=0, keepdims=True))
        run = run + jnp.sum(onehot, axis=1, keepdims=True)
    rank_ref[...] = jnp.concatenate(ranks, axis=0).astype(jnp.int32)
    run_ref[...] = jnp.broadcast_to(run, run_ref.shape)
    cnt_ref[...] = jnp.broadcast_to(run, cnt_ref.shape)


def _post(y, gla, x, gw, gb, sg, wo, n2, rwt, rb):
    tm = 256
    width = S5_G * S5_N
    before = (jnp.arange(tm)[:, None] < jnp.arange(tm)[None, :]).astype(BF16)
    return pl.pallas_call(
        _post_kernel,
        grid=(T // tm,),
        in_specs=[
            pl.BlockSpec((tm, width), lambda i: (i, 0)),
            pl.BlockSpec((tm, GLA_H * GLA_DV), lambda i: (i, 0)),
            pl.BlockSpec((tm, D), lambda i: (i, 0)),
            _const_spec(gw.shape), _const_spec(gb.shape), _const_spec(sg.shape), _const_spec(wo.shape),
            _const_spec(n2.shape), _const_spec(rwt.shape), _const_spec(rb.shape), _const_spec(before.shape),
        ],
        out_specs=[
            pl.BlockSpec((tm, D), lambda i: (i, 0)),
            pl.BlockSpec((tm, D), lambda i: (i, 0)),
            pl.BlockSpec((TOP_K, tm), lambda i: (0, i)),
            pl.BlockSpec((TOP_K, tm), lambda i: (0, i)),
            pl.BlockSpec((TOP_K, tm), lambda i: (0, i)),
            pl.BlockSpec((N_EXP, 128), lambda i: (0, 0)),
        ],
        out_shape=[
            jax.ShapeDtypeStruct((T, D), F32),
            jax.ShapeDtypeStruct((T, D), F32),
            jax.ShapeDtypeStruct((TOP_K, T), jnp.int32),
            jax.ShapeDtypeStruct((TOP_K, T), F32),
            jax.ShapeDtypeStruct((TOP_K, T), jnp.int32),
            jax.ShapeDtypeStruct((N_EXP, 128), F32),
        ],
        scratch_shapes=[pltpu.VMEM((N_EXP, 128), F32)],
        compiler_params=pltpu.CompilerParams(
            dimension_semantics=("arbitrary",), vmem_limit_bytes=VMEM_LIMIT),
        name="post",
    )(y, gla, x, gw, gb, sg, wo, n2, rwt, rb, before)


def _route(top_idx, rank, counts):
    e_flat = top_idx.reshape(-1)
    rank = rank.reshape(-1)
    counts = counts[:, 0].astype(jnp.int32)
    padded = (counts + MOE_RB - 1) // MOE_RB * MOE_RB
    pad_end = jnp.cumsum(padded)
    pad_start = pad_end - padded
    dest = (pad_start[e_flat] + rank).astype(jnp.int32)
    nvis_e = (counts + MOE_RMAX - 1) // MOE_RMAX
    vis_end = jnp.cumsum(nvis_e)
    n_vis = vis_end[-1]
    v = jnp.arange(MOE_NV, dtype=jnp.int32)
    vc = jnp.minimum(v, n_vis - 1)
    ve = jnp.minimum(jnp.searchsorted(vis_end, vc, side='right'), N_EXP - 1).astype(jnp.int32)
    local = vc - (vis_end[ve] - nvis_e[ve])
    vcnt = jnp.where(v < n_vis, jnp.minimum(MOE_RMAX, counts[ve] - local * MOE_RMAX), 0)
    vrow = pad_start[ve] + local * MOE_RMAX
    return dict(dest=dest, fill_from=(pad_start + counts).astype(jnp.int32), fill_to=pad_end.astype(jnp.int32),
                tail_blk=(pad_end[-1:] // MOE_RB).astype(jnp.int32),
                vis_e=ve, vis_cnt=vcnt.astype(jnp.int32), vis_row=vrow.astype(jnp.int32),
                n_vis=n_vis.reshape(1).astype(jnp.int32))


def _row_copy(src_ref, srow, dst_ref, drow, sem):
    return pltpu.make_async_copy(src_ref.at[pl.ds(srow, 1), :], dst_ref.at[pl.ds(drow, 1), :], sem)


def _tail_fill(src_ref, dst_ref, tail_ref, sem):
    def cp(b):
        return pltpu.make_async_copy(src_ref.at[pl.ds(0, MOE_RB), :],
                                     dst_ref.at[pl.ds(pl.multiple_of(b * MOE_RB, MOE_RB), MOE_RB), :], sem)

    def start(b, c):
        cp(b).start()
        return c
    lax.fori_loop(tail_ref[0], MOE_ROWS // MOE_RB, start, 0)

    def finish(b, c):
        cp(b).wait()
        return c
    lax.fori_loop(tail_ref[0], MOE_ROWS // MOE_RB, finish, 0)


def _scatter_kernel(dest_ref, from_ref, to_ref, tail_ref, hn_ref, xs_ref, zero_ref, sem, zsem):
    i = pl.program_id(0)
    tm = hn_ref.shape[0]

    @pl.when(i == 0)
    def _():
        zero_ref[...] = jnp.zeros_like(zero_ref)
        _tail_fill(zero_ref, xs_ref, tail_ref, zsem)

        def per_expert(e, c):
            def fill(r, c2):
                _row_copy(zero_ref, 0, xs_ref, r, zsem).start()
                return c2
            lax.fori_loop(from_ref[e], to_ref[e], fill, 0)

            def drain(r, c2):
                _row_copy(zero_ref, 0, xs_ref, r, zsem).wait()
                return c2
            lax.fori_loop(from_ref[e], to_ref[e], drain, 0)
            return c
        lax.fori_loop(0, N_EXP, per_expert, 0)

    def issue(t, c):
        for k in range(TOP_K):
            _row_copy(hn_ref, t, xs_ref, dest_ref[k * T + i * tm + t], sem).start()
        return c
    lax.fori_loop(0, tm, issue, 0)

    def drain(t, c):
        for k in range(TOP_K):
            _row_copy(hn_ref, t, xs_ref, dest_ref[k * T + i * tm + t], sem).wait()
        return c
    lax.fori_loop(0, tm, drain, 0)


def _scatter(hn, rt):
    tm = 256
    return pl.pallas_call(
        _scatter_kernel,
        grid_spec=pltpu.PrefetchScalarGridSpec(
            num_scalar_prefetch=4,
            grid=(T // tm,),
            in_specs=[pl.BlockSpec((tm, D), lambda i, *_: (i, 0))],
            out_specs=pl.BlockSpec(memory_space=pl.ANY),
            scratch_shapes=[pltpu.VMEM((MOE_RB, D), F32), pltpu.SemaphoreType.DMA, pltpu.SemaphoreType.DMA],
        ),
        out_shape=jax.ShapeDtypeStruct((MOE_ROWS, D), F32),
        compiler_params=pltpu.CompilerParams(
            dimension_semantics=("arbitrary",), vmem_limit_bytes=VMEM_LIMIT),
        name="scatter",
    )(rt["dest"], rt["fill_from"], rt["fill_to"], rt["tail_blk"], hn)


def _experts_kernel(ve_ref, vcnt_ref, vrow_ref, nvis_ref, tail_ref, xs_ref, w1g_ref, w1l_ref, w2_ref, b1g_ref,
                    b1l_ref, b2_ref, ys_ref, xin_ref, xb_ref, acc_ref, isem, osem):
    v = pl.program_id(0)
    j = pl.program_id(1)
    RB = MOE_RB
    BIG, MID = 4 * RB, 2 * RB
    nvis = nvis_ref[0]
    active = v < nvis
    first = j == 0
    last = j == MOE_NF - 1

    def geometry(vv):
        nblk = (vcnt_ref[vv] + RB - 1) // RB
        return vrow_ref[vv], nblk // 4, (nblk // 2) % 2, nblk % 2

    row0, nbig, mid, small = geometry(v)
    mid_r0 = nbig * BIG
    small_r0 = mid_r0 + mid * MID

    def x_copy(vrow, r0, n, slot):
        return pltpu.make_async_copy(xs_ref.at[pl.ds(pl.multiple_of(vrow + r0, RB), n), :],
                                     xin_ref.at[slot, pl.ds(0, n), :], isem.at[slot])

    def y_copy(r0, n):
        return pltpu.make_async_copy(acc_ref.at[pl.ds(pl.multiple_of(r0, RB), n), :],
                                     ys_ref.at[pl.ds(pl.multiple_of(row0 + r0, RB), n), :], osem)

    def start_unit(vv, u):
        vrow, nb, md, sm = geometry(vv)

        @pl.when(u < nb)
        def _():
            x_copy(vrow, u * BIG, BIG, u % 2).start()

        @pl.when((u == nb) & (md == 1))
        def _():
            x_copy(vrow, nb * BIG, MID, u % 2).start()

        @pl.when((u == nb + md) & (sm == 1))
        def _():
            x_copy(vrow, nb * BIG + md * MID, RB, u % 2).start()

    @pl.when((v == 0) & first)
    def _():
        acc_ref[...] = jnp.zeros_like(acc_ref)
        _tail_fill(acc_ref, ys_ref, tail_ref, osem)
        start_unit(0, 0)

    def block(u, r0, n):
        rows = pl.ds(pl.multiple_of(r0, RB), n)
        slot = u % 2

        @pl.when(first)
        def _():
            x_copy(row0, r0, n, slot).wait()
            start_unit(v, u + 1)
            xb_ref[rows, :] = xin_ref[slot, pl.ds(0, n), :].astype(BF16)

        x = xb_ref[rows, :]
        glu = jnp.minimum(_dot(x, w1g_ref[0].astype(BF16)) + b1g_ref[0], SWIGLU_LIMIT)
        lin = jnp.clip(_dot(x, w1l_ref[0].astype(BF16)) + b1l_ref[0], -SWIGLU_LIMIT, SWIGLU_LIMIT)
        act = glu * _sigmoid(SWIGLU_ALPHA * glu) * (lin + 1.0)
        part = _dot(act.astype(BF16), w2_ref[0].astype(BF16))

        acc_ref[rows, :] = part + jnp.where(first, b2_ref[0], acc_ref[rows, :])

        @pl.when(last)
        def _():
            y_copy(r0, n).start()

    @pl.when(active)
    def _():
        def big(i, c):
            block(i, i * BIG, BIG)
            return c
        lax.fori_loop(0, nbig, big, 0)

        @pl.when(mid == 1)
        def _():
            block(nbig, mid_r0, MID)

        @pl.when(small == 1)
        def _():
            block(nbig + mid, small_r0, RB)

    @pl.when(active & last)
    def _():
        def finish(i, c):
            y_copy(i * BIG, BIG).wait()
            return c
        lax.fori_loop(0, nbig, finish, 0)

        @pl.when(mid == 1)
        def _():
            y_copy(mid_r0, MID).wait()

        @pl.when(small == 1)
        def _():
            y_copy(small_r0, RB).wait()

        @pl.when(v + 1 < nvis)
        def _():
            start_unit(v + 1, 0)


def _experts(xs, rt, w1, b1, w2, b2):
    b1 = b1.reshape(N_EXP, 1, 2 * D_FF)
    b2 = b2.reshape(N_EXP, 1, D)

    def jj(v, j, ve, vcnt, vrow, nvis, tail):
        return jnp.where(v < nvis[0], j, MOE_NF - 1)

    return pl.pallas_call(
        _experts_kernel,
        grid_spec=pltpu.PrefetchScalarGridSpec(
            num_scalar_prefetch=5,
            grid=(MOE_NV, MOE_NF),
            in_specs=[
                pl.BlockSpec(memory_space=pl.ANY),
                pl.BlockSpec((1, D, MOE_TF), lambda v, j, ve, *s: (ve[v], 0, jj(v, j, ve, *s))),
                pl.BlockSpec((1, D, MOE_TF), lambda v, j, ve, *s: (ve[v], 0, MOE_NF + jj(v, j, ve, *s))),
                pl.BlockSpec((1, MOE_TF, D), lambda v, j, ve, *s: (ve[v], jj(v, j, ve, *s), 0)),
                pl.BlockSpec((1, 1, MOE_TF), lambda v, j, ve, *s: (ve[v], 0, jj(v, j, ve, *s))),
                pl.BlockSpec((1, 1, MOE_TF), lambda v, j, ve, *s: (ve[v], 0, MOE_NF + jj(v, j, ve, *s))),
                pl.BlockSpec((1, 1, D), lambda v, j, ve, *s: (ve[v], 0, 0)),
            ],
            out_specs=pl.BlockSpec(memory_space=pl.ANY),
            scratch_shapes=[
                pltpu.VMEM((2, 4 * MOE_RB, D), F32),
                pltpu.VMEM((MOE_RMAX, D), BF16),
                pltpu.VMEM((MOE_RMAX, D), F32),
                pltpu.SemaphoreType.DMA((2,)),
                pltpu.SemaphoreType.DMA,
            ],
        ),
        out_shape=jax.ShapeDtypeStruct((MOE_ROWS, D), F32),
        compiler_params=pltpu.CompilerParams(
            dimension_semantics=("arbitrary", "arbitrary"), vmem_limit_bytes=VMEM_LIMIT),
        name="experts",
    )(rt["vis_e"], rt["vis_cnt"], rt["vis_row"], rt["n_vis"], rt["tail_blk"], xs, w1, w1, w2, b1, b1, b2)


def _combine_kernel(dest_ref, ys_ref, h_ref, gate_ref, gf_ref, o_ref, buf_ref, sem):
    i = pl.program_id(0)
    tm = h_ref.shape[0]

    def issue(t, c):
        for k in range(TOP_K):
            _row_copy(ys_ref, dest_ref[k * T + i * tm + t], buf_ref.at[k], t, sem).start()
        return c
    lax.fori_loop(0, tm, issue, 0)

    def drain(t, c):
        for k in range(TOP_K):
            _row_copy(ys_ref, dest_ref[k * T + i * tm + t], buf_ref.at[k], t, sem).wait()
        return c
    lax.fori_loop(0, tm, drain, 0)

    h = h_ref[...]
    for k in range(TOP_K):
        h = h + gate_ref[:, k:k + 1] * buf_ref[k]
    o_ref[...] = _rms(h, gf_ref[...])


def _combine(ys, h, gates_t, gf, rt):
    tm = 256
    return pl.pallas_call(
        _combine_kernel,
        grid_spec=pltpu.PrefetchScalarGridSpec(
            num_scalar_prefetch=1,
            grid=(T // tm,),
            in_specs=[
                pl.BlockSpec(memory_space=pl.ANY),
                pl.BlockSpec((tm, D), lambda i, *_: (i, 0)),
                pl.BlockSpec((tm, TOP_K), lambda i, *_: (i, 0)),
                pl.BlockSpec((1, D), lambda i, *_: (0, 0)),
            ],
            out_specs=pl.BlockSpec((tm, D), lambda i, *_: (i, 0)),
            scratch_shapes=[pltpu.VMEM((TOP_K, tm, D), F32), pltpu.SemaphoreType.DMA],
        ),
        out_shape=jax.ShapeDtypeStruct((T, D), F32),
        compiler_params=pltpu.CompilerParams(
            dimension_semantics=("arbitrary",), vmem_limit_bytes=VMEM_LIMIT),
        name="combine",
    )(rt["dest"], ys, h, gates_t, gf)


def kernel(x, norm1_g, w_in, gla_gate_w2, gla_gate_b, gla_norm_g, s5_a_re, s5_a_im, s5_b_re, s5_b_im, s5_c_re, s5_c_im, s5_d, s5_log_step, s5_glu_w, s5_glu_b, s5_norm_g, w_out, norm2_g, router_w, router_b, expert_w1, expert_b1, expert_w2, expert_b2, final_norm_g):
    assert x.shape == (1, T, D) and w_in.shape[0] == 1
    xt = x.reshape(T, D)
    nqkvr = 2 * GLA_H * GLA_DK + 2 * GLA_H * GLA_DV
    wi = w_in[0]
    wq = wi[:, :nqkvr].astype(BF16)
    wg = jnp.pad(wi[:, nqkvr:nqkvr + GATE_RANK], ((0, 0), (0, 128 - GATE_RANK))).astype(BF16)
    wu = wi[:, nqkvr + GATE_RANK:].astype(BF16)
    w2g = jnp.pad(gla_gate_w2[0], ((0, 128 - GATE_RANK), (0, 0)))
    qkvr, lg, u = _proj(xt, norm1_g, wq, wg, wu, w2g, gla_gate_b)
    gla = _gla(qkvr, lg, gla_norm_g)
    tables = _s5_tables(s5_a_re[0], s5_a_im[0], s5_b_re[0], s5_b_im[0], s5_c_re[0], s5_c_im[0],
                        s5_d[0], s5_log_step[0])
    y = _s5(u, tables)
    h, hn, top_idx, gates, rank, counts = _post(
        y, gla, xt, s5_glu_w[0].astype(BF16), s5_glu_b, s5_norm_g, w_out[0].astype(BF16), norm2_g,
        jnp.pad(router_w[0], ((0, 0), (0, 128 - N_EXP))), router_b.reshape(N_EXP, 1))
    rt = _route(top_idx, rank, counts)
    xs = _scatter(hn, rt)
    ys = _experts(xs, rt, expert_w1[0], expert_b1[0], expert_w2[0], expert_b2[0])
    out = _combine(ys, h, gates.T, final_norm_g.reshape(1, D), rt)
    return out.reshape(1, T, D)
```

```python
import functools
import math

import jax
import jax.numpy as jnp
from jax import lax
from jax.experimental import pallas as pl
from jax.experimental.pallas import tpu as pltpu

F32 = jnp.float32
BF16 = jnp.bfloat16
HIGHEST = lax.Precision.HIGHEST

T = 8192
D = 2048
GLA_H = 4
GLA_DK = 128
GLA_DV = 256
GLA_CHUNK = 64
GLA_SUB = 16
GATE_RANK = 16
GATE_TAU = 16.0
S5_G = 64
S5_N = 16
S5_P = 64
S5_L = 16
S5_GT = 8
N_EXP = 32
TOP_K = 4
D_FF = 2048
SWIGLU_ALPHA = 1.702
SWIGLU_LIMIT = 7.0
RMS_EPS = 1e-6

MOE_RB = 128
MOE_RMAX = 1536
MOE_TF = 512
MOE_NF = D_FF // MOE_TF
MOE_NV = -(-T * TOP_K // MOE_RMAX) + N_EXP
MOE_ROWS = T * TOP_K + N_EXP * MOE_RB
VMEM_LIMIT = 56 * 1024 * 1024

NN = (((1,), (0,)), ((), ()))
NT = (((1,), (1,)), ((), ()))
TN = (((0,), (0,)), ((), ()))


def _dot(a, b, dims=NN, precision=None):
    return lax.dot_general(a, b, dims, preferred_element_type=F32, precision=precision)


def _rms(x, g):
    return x * lax.rsqrt(jnp.mean(x * x, axis=-1, keepdims=True) + RMS_EPS) * g


def _sigmoid(x):
    return 1.0 / (1.0 + jnp.exp(-x))


def _pack_halves(lo_half, hi_half):
    a = lax.bitcast_convert_type(lo_half.astype(F32), jnp.int32)
    b = lax.bitcast_convert_type(hi_half.astype(F32), jnp.int32)
    return a | lax.shift_right_logical(b, 16)


def _unpack_halves(p):
    a = lax.bitcast_convert_type(p & jnp.int32(-65536), F32).astype(BF16)
    b = lax.bitcast_convert_type(lax.shift_left(p, 16), F32).astype(BF16)
    return a, b


def _const_spec(shape):
    nd = len(shape)
    return pl.BlockSpec(shape, lambda *_: (0,) * nd, pipeline_mode=pl.Buffered(1))


def _proj_kernel(x_ref, g1_ref, wq_ref, wg_ref, wu_ref, w2_ref, gb_ref, qkvr_ref, lg_ref, u_ref):
    hb = _rms(x_ref[...], g1_ref[...]).astype(BF16)
    p = _dot(hb, wq_ref[...])
    nq = GLA_H * GLA_DK
    qkvr_ref[:, :nq] = (p[:, :nq] * (GLA_DK ** -0.5)).astype(BF16)
    qkvr_ref[:, nq:] = p[:, nq:].astype(BF16)
    glr = _dot(hb, wg_ref[...])
    z = _dot(glr, w2_ref[...], precision=HIGHEST) + gb_ref[...]
    lg_ref[...] = (jnp.minimum(z, 0.0) - jnp.log(1.0 + jnp.exp(-jnp.abs(z)))) * (1.0 / GATE_TAU)
    u_ref[...] = _dot(hb, wu_ref[...])


def _proj(x, g1, wq, wg, wu, w2, gb):
    tm = 256
    nqkvr = wq.shape[1]
    return pl.pallas_call(
        _proj_kernel,
        grid=(T // tm,),
        in_specs=[
            pl.BlockSpec((tm, D), lambda i: (i, 0)),
            _const_spec((1, D)),
            _const_spec(wq.shape), _const_spec(wg.shape), _const_spec(wu.shape),
            _const_spec(w2.shape), _const_spec(gb.shape),
        ],
        out_specs=[
            pl.BlockSpec((tm, nqkvr), lambda i: (i, 0)),
            pl.BlockSpec((tm, GLA_H * GLA_DK), lambda i: (i, 0)),
            pl.BlockSpec((tm, S5_G * S5_N), lambda i: (i, 0)),
        ],
        out_shape=[
            jax.ShapeDtypeStruct((T, nqkvr), BF16),
            jax.ShapeDtypeStruct((T, GLA_H * GLA_DK), F32),
            jax.ShapeDtypeStruct((T, S5_G * S5_N), F32),
        ],
        compiler_params=pltpu.CompilerParams(
            dimension_semantics=("arbitrary",), vmem_limit_bytes=VMEM_LIMIT),
        name="proj",
    )(x, g1, wq, wg, wu, w2, gb)


def _gla_kernel(q_ref, k_ref, v_ref, r_ref, lg_ref, gn_ref, tril_ref, esel_ref, o_ref, state_ref):
    C, S = GLA_CHUNK, GLA_SUB
    nsub = C // S

    @pl.when(pl.program_id(0) == 0)
    def _():
        state_ref[...] = jnp.zeros_like(state_ref)

    row = lax.broadcasted_iota(jnp.int32, (C, C), 0)
    col = lax.broadcasted_iota(jnp.int32, (C, C), 1)
    below = (col // S) < (row // S)
    diag = ((col // S) == (row // S)) & (col <= row)
    tril = tril_ref[...]
    esel = esel_ref[...]
    gn = gn_ref[...]

    def head(h, rows, b):
        kcols = slice(h * GLA_DK, (h + 1) * GLA_DK)
        vcols = slice(h * GLA_DV, (h + 1) * GLA_DV)
        q = q_ref[rows, kcols].astype(F32)
        k = k_ref[rows, kcols].astype(F32)
        v = v_ref[rows, vcols]
        bl = b[C - 1:C, :]
        st = state_ref[h]
        o = _dot((q * jnp.exp(b)).astype(BF16), st.astype(BF16), NT)
        s_rows = [jnp.zeros((S, C), F32)]
        for j in range(1, nsub):
            ref = b[j * S - 1:j * S, :]
            qj = (q[j * S:(j + 1) * S] * jnp.exp(b[j * S:(j + 1) * S] - ref)).astype(BF16)
            kj = (k * jnp.exp(jnp.minimum(ref - b, 0.0))).astype(BF16)
            s_rows.append(_dot(qj, kj, NT))
        s_off = jnp.concatenate(s_rows, axis=0)
        q3 = q.reshape(nsub, S, GLA_DK)
        k3 = k.reshape(nsub, S, GLA_DK)
        b3 = b.reshape(nsub, S, GLA_DK)
        xs = []
        for s in range(S):
            dec = jnp.exp(jnp.minimum(b3 - b3[:, s:s + 1, :], 0.0))
            xs.append((q3 * k3[:, s:s + 1, :] * dec).reshape(C, GLA_DK).astype(BF16))
        dsc = _dot(jnp.concatenate(xs, axis=1), esel)
        a = jnp.where(below, s_off, jnp.where(diag, dsc, 0.0))
        o = o + _dot(a.astype(BF16), v)
        kout = (k * jnp.exp(bl - b)).astype(BF16)
        state_ref[h] = st * jnp.exp(bl) + _dot(v, kout, TN)
        y = _rms(o, gn)
        r = r_ref[rows, vcols].astype(F32)
        o_ref[rows, vcols] = (y * (r * _sigmoid(r))).astype(BF16)

    def chunk(c, carry):
        rows = pl.ds(pl.multiple_of(c * C, C), C)
        b = _dot(tril, lg_ref[rows, :], precision=HIGHEST)
        for h in range(GLA_H):
            head(h, rows, b[:, h * GLA_DK:(h + 1) * GLA_DK])
        return carry

    lax.fori_loop(0, q_ref.shape[0] // C, chunk, 0)


def _gla(qkvr, lg, gn):
    tb = 512
    C, S = GLA_CHUNK, GLA_SUB
    nk, nv = GLA_H * GLA_DK, GLA_H * GLA_DV
    tril = jnp.tril(jnp.ones((C, C), F32))
    esel = (jnp.arange(S * GLA_DK)[:, None] // GLA_DK == jnp.arange(C)[None, :] % S).astype(BF16)
    return pl.pallas_call(
        _gla_kernel,
        grid=(T // tb,),
        in_specs=[
            pl.BlockSpec((tb, nk), lambda i: (i, 0)),
            pl.BlockSpec((tb, nk), lambda i: (i, 1)),
            pl.BlockSpec((tb, nv), lambda i: (i, 1)),
            pl.BlockSpec((tb, nv), lambda i: (i, 2)),
            pl.BlockSpec((tb, nk), lambda i: (i, 0)),
            _const_spec((1, GLA_DV)), _const_spec((C, C)), _const_spec((S * GLA_DK, C)),
        ],
        out_specs=pl.BlockSpec((tb, nv), lambda i: (i, 0)),
        out_shape=jax.ShapeDtypeStruct((T, nv), BF16),
        scratch_shapes=[pltpu.VMEM((GLA_H, GLA_DV, GLA_DK), F32)],
        compiler_params=pltpu.CompilerParams(
            dimension_semantics=("arbitrary",), vmem_limit_bytes=VMEM_LIMIT),
        name="gla",
    )(qkvr, qkvr, qkvr, qkvr, lg, gn, tril, esel)


def _s5_tables(a_re, a_im, b_re, b_im, c_re, c_im, d_skip, log_step):
    L, G, P, N = S5_L, S5_G, S5_P, S5_N
    delta = jnp.exp(log_step)[:, None]
    ar, ai = a_re * delta, a_im * delta

    def apow(tau):
        tau = jnp.asarray(tau, F32)[None, :, None]
        mag = jnp.exp(ar[:, None, :] * tau)
        ph = ai[:, None, :] * tau
        return mag * jnp.cos(ph), mag * jnp.sin(ph)

    p_re, p_im = apow(jnp.arange(L + 1))
    ab_re, ab_im = p_re[:, 1], p_im[:, 1]
    den = a_re * a_re + a_im * a_im
    cf_re = ((ab_re - 1.0) * a_re + ab_im * a_im) / den
    cf_im = (ab_im * a_re - (ab_re - 1.0) * a_im) / den
    bb_re = cf_re[:, :, None] * b_re - cf_im[:, :, None] * b_im
    bb_im = cf_re[:, :, None] * b_im + cf_im[:, :, None] * b_re
    e_re = p_re[:, :L, :, None] * bb_re[:, None] - p_im[:, :L, :, None] * bb_im[:, None]
    e_im = p_re[:, :L, :, None] * bb_im[:, None] + p_im[:, :L, :, None] * bb_re[:, None]
    kk = (jnp.einsum('gnp,gtpm->gtnm', c_re, e_re, precision=HIGHEST)
          - jnp.einsum('gnp,gtpm->gtnm', c_im, e_im, precision=HIGHEST))
    kk = kk.at[:, 0].add(jnp.eye(N, dtype=F32)[None] * d_skip[:, :, None])
    r_re, r_im = apow(L - 1 - jnp.arange(L))
    bs_re = (r_re[:, :, None, :] * bb_re.transpose(0, 2, 1)[:, None] - r_im[:, :, None, :] * bb_im.transpose(0, 2, 1)[:, None])
    bs_im = (r_re[:, :, None, :] * bb_im.transpose(0, 2, 1)[:, None] + r_im[:, :, None, :] * bb_re.transpose(0, 2, 1)[:, None])
    ct_re = c_re.transpose(0, 2, 1)[:, :, None, :]
    ct_im = c_im.transpose(0, 2, 1)[:, :, None, :]
    q_re = p_re[:, 1:].transpose(0, 2, 1)[:, :, :, None]
    q_im = p_im[:, 1:].transpose(0, 2, 1)[:, :, :, None]
    cre = ct_re * q_re - ct_im * q_im
    cim = -(ct_re * q_im + ct_im * q_re)
    GT = S5_GT
    nt = G // GT

    def tile_groups(w):
        return w.reshape((nt, GT) + w.shape[1:])

    kc = tile_groups(kk).transpose(0, 1, 4, 2, 3).reshape(nt, GT * N, L * N)

    def in_table(w):
        return tile_groups(w).transpose(0, 2, 1, 3, 4).reshape(nt, L * GT * N, P)

    def out_table(w):
        return w.reshape(nt, GT * P, L * N)

    nstep = (T // L - 1).bit_length()
    sc_re, sc_im = apow(jnp.asarray([L * 2 ** i for i in range(nstep)]))

    def lanes(w):
        return tile_groups(w).transpose(0, 2, 1, 3).reshape(nt, w.shape[1], GT * P)

    r1 = jnp.arange(L * N)[:, None]
    c1 = jnp.arange(L * GT * N)[None, :]
    wide_tn = ((r1 // N == c1 // (GT * N)) & (r1 % N == c1 % N)).astype(BF16)
    wide_p = (jnp.arange(P)[:, None] == jnp.arange(GT * P)[None, :] % P).astype(BF16)
    return dict(
        kc=kc.astype(BF16),
        bs_re=in_table(bs_re).astype(BF16), bs_im=in_table(bs_im).astype(BF16),
        cre=out_table(cre).astype(BF16), cim=out_table(cim).astype(BF16),
        sc_re=lanes(sc_re), sc_im=lanes(sc_im), wide_tn=wide_tn, wide_p=wide_p,
    )


def _s5_kernel(u_ref, kc_ref, bsr_ref, bsi_ref, cre_ref, cim_ref, scr_ref, sci_ref, wtn_ref, wp_ref,
               y_ref, m_ref):
    L, N, P = S5_L, S5_N, S5_P
    W = S5_GT * N
    nch = u_ref.shape[0] // L

    @pl.when(pl.program_id(0) == 0)
    def _():
        m_ref[...] = jnp.zeros_like(m_ref)

    def widen(compact, wide, row_group, col_group):
        full = _dot(compact, wide)
        r = lax.broadcasted_iota(jnp.int32, full.shape, 0)
        c = lax.broadcasted_iota(jnp.int32, full.shape, 1)
        return jnp.where(row_group(r) == col_group(c), full, 0.0).astype(BF16)

    wtn, wp = wtn_ref[...], wp_ref[...]
    bd = widen(kc_ref[0], wtn, lambda r: r // N, lambda c: (c % W) // N)
    bsr = widen(bsr_ref[0], wp, lambda r: (r % W) // N, lambda c: c // P)
    bsi = widen(bsi_ref[0], wp, lambda r: (r % W) // N, lambda c: c // P)
    cre = widen(cre_ref[0], wtn, lambda r: r // P, lambda c: (c % W) // N)
    cim = widen(cim_ref[0], wtn, lambda r: r // P, lambda c: (c % W) // N)
    for s in range(L):
        m_ref[s * W:(s + 1) * W, s * W:] = bd[:, :(L - s) * W]
    ucat = jnp.concatenate([u_ref[pl.ds(s, nch, stride=L), :].astype(BF16) for s in range(L)], axis=1)
    hr = _dot(ucat, bsr)
    hi = _dot(ucat, bsi)
    row = lax.broadcasted_iota(jnp.int32, hr.shape, 0)

    def shift(x, d):
        if d % 8 == 0:
            return jnp.concatenate([jnp.zeros((d, x.shape[1]), F32), x[:nch - d]], axis=0)
        return jnp.where(row >= d, pltpu.roll(x, d, 0), 0.0)

    for i in range(scr_ref.shape[1]):
        mr, mi = scr_ref[0, i:i + 1, :], sci_ref[0, i:i + 1, :]
        pr, pi = shift(hr, 2 ** i), shift(hi, 2 ** i)
        hr, hi = hr + mr * pr - mi * pi, hi + mr * pi + mi * pr
    gr, gi = shift(hr, 1), shift(hi, 1)
    grb, gib = gr.astype(BF16), gi.astype(BF16)
    for tp in range(L // 2):
        cols = slice(2 * tp * W, (2 * tp + 2) * W)
        depth = (2 * tp + 2) * W
        y = _dot(ucat[:, :depth], m_ref[:depth, cols]) + _dot(grb, cre[:, cols]) + _dot(gib, cim[:, cols])
        for t in (2 * tp, 2 * tp + 1):
            y_ref[pl.ds(t, nch, stride=L), :] = y[:, (t - 2 * tp) * W:(t - 2 * tp + 1) * W]


def _s5(u, tb):
    W = S5_GT * S5_N
    nt = S5_G // S5_GT

    def tile_spec(a):
        return pl.BlockSpec((1,) + a.shape[1:], lambda g: (g,) + (0,) * (a.ndim - 1))

    ws = [tb[n] for n in ["kc", "bs_re", "bs_im", "cre", "cim", "sc_re", "sc_im"]]
    consts = [tb["wide_tn"], tb["wide_p"]]
    return pl.pallas_call(
        _s5_kernel,
        grid=(nt,),
        in_specs=([pl.BlockSpec((T, W), lambda g: (0, g))] + [tile_spec(w) for w in ws]
                  + [_const_spec(c.shape) for c in consts]),
        out_specs=pl.BlockSpec((T, W), lambda g: (0, g)),
        out_shape=jax.ShapeDtypeStruct((T, S5_G * S5_N), F32),
        scratch_shapes=[pltpu.VMEM((S5_L * W, S5_L * W), BF16)],
        compiler_params=pltpu.CompilerParams(
            dimension_semantics=("arbitrary",), vmem_limit_bytes=VMEM_LIMIT),
        name="s5",
    )(u, *ws, *consts)


def _post_kernel(y_ref, gla_ref, x_ref, gw_ref, gb_ref, sg_ref, wo_ref, n2_ref, rw_ref, rb_ref, before_ref,
                 h_ref, hn_ref, idx_ref, gate_ref, rank_ref, cnt_ref, run_ref):
    @pl.when(pl.program_id(0) == 0)
    def _():
        run_ref[...] = jnp.zeros_like(run_ref)

    y = y_ref[...]
    z = 0.5 * y * (1.0 + jnp.tanh(math.sqrt(2.0 / math.pi) * (y + 0.044715 * (y * y * y))))
    z = z * _sigmoid(_dot(z.astype(BF16), gw_ref[...]) + gb_ref[...])
    s5o = _rms(z, sg_ref[...]).astype(BF16)
    half = GLA_H * GLA_DV
    h = x_ref[...] + _dot(gla_ref[...], wo_ref[:half, :]) + _dot(s5o, wo_ref[half:, :])
    h_ref[...] = h
    hn = _rms(h, n2_ref[...])
    hi = hn.astype(BF16)
    hn_ref[...] = _pack_halves(hi[:, :D // 2], hi[:, D // 2:])
    lo = (hn - hi.astype(F32)).astype(BF16)
    rw = rw_ref[...]
    rw_hi = rw.astype(BF16)
    rw_lo = (rw - rw_hi.astype(F32)).astype(BF16)
    lg = _dot(hi, rw_hi) + (_dot(lo, rw_hi) + _dot(hi, rw_lo))
    lt = lg.T[:N_EXP] + rb_ref[...]
    eid = lax.broadcasted_iota(jnp.int32, lt.shape, 0).astype(F32)
    vals, idxs = [], []
    for _ in range(TOP_K):
        m = jnp.max(lt, axis=0, keepdims=True)
        sel = jnp.min(jnp.where(lt == m, eid, float(N_EXP)), axis=0, keepdims=True)
        vals.append(m)
        idxs.append(sel)
        lt = jnp.where(eid == sel, -jnp.inf, lt)
    ex = [jnp.exp(vv - vals[0]) for vv in vals]
    inv = 1.0 / (ex[0] + ex[1] + ex[2] + ex[3])
    idx_ref[...] = jnp.concatenate(idxs, axis=0).astype(jnp.int32)
    gate_ref[...] = jnp.concatenate([e * inv for e in ex], axis=0)
    run = run_ref[:, :1]
    ranks = []
    for sel in idxs:
        onehot = jnp.where(eid == sel, 1.0, 0.0)
        earlier = _dot(onehot.astype(BF16), before_ref[...])
        ranks.append(jnp.sum(onehot * (run + earlier), axis=0, keepdims=True))
        run = run + jnp.sum(onehot, axis=1, keepdims=True)
    rank_ref[...] = jnp.concatenate(ranks, axis=0).astype(jnp.int32)
    run_ref[...] = jnp.broadcast_to(run, run_ref.shape)
    cnt_ref[...] = jnp.broadcast_to(run, cnt_ref.shape)


def _post(y, gla, x, gw, gb, sg, wo, n2, rwt, rb):
    tm = 256
    width = S5_G * S5_N
    before = (jnp.arange(tm)[:, None] < jnp.arange(tm)[None, :]).astype(BF16)
    return pl.pallas_call(
        _post_kernel,
        grid=(T // tm,),
        in_specs=[
            pl.BlockSpec((tm, width), lambda i: (i, 0)),
            pl.BlockSpec((tm, GLA_H * GLA_DV), lambda i: (i, 0)),
            pl.BlockSpec((tm, D), lambda i: (i, 0)),
            _const_spec(gw.shape), _const_spec(gb.shape), _const_spec(sg.shape), _const_spec(wo.shape),
            _const_spec(n2.shape), _const_spec(rwt.shape), _const_spec(rb.shape), _const_spec(before.shape),
        ],
        out_specs=[
            pl.BlockSpec((tm, D), lambda i: (i, 0)),
            pl.BlockSpec((tm, D // 2), lambda i: (i, 0)),
            pl.BlockSpec((TOP_K, tm), lambda i: (0, i)),
            pl.BlockSpec((TOP_K, tm), lambda i: (0, i)),
            pl.BlockSpec((TOP_K, tm), lambda i: (0, i)),
            pl.BlockSpec((N_EXP, 128), lambda i: (0, 0)),
        ],
        out_shape=[
            jax.ShapeDtypeStruct((T, D), F32),
            jax.ShapeDtypeStruct((T, D // 2), jnp.int32),
            jax.ShapeDtypeStruct((TOP_K, T), jnp.int32),
            jax.ShapeDtypeStruct((TOP_K, T), F32),
            jax.ShapeDtypeStruct((TOP_K, T), jnp.int32),
            jax.ShapeDtypeStruct((N_EXP, 128), F32),
        ],
        scratch_shapes=[pltpu.VMEM((N_EXP, 128), F32)],
        compiler_params=pltpu.CompilerParams(
            dimension_semantics=("arbitrary",), vmem_limit_bytes=VMEM_LIMIT),
        name="post",
    )(y, gla, x, gw, gb, sg, wo, n2, rwt, rb, before)


def _route(top_idx, rank, counts):
    e_flat = top_idx.reshape(-1)
    rank = rank.reshape(-1)
    counts = counts[:, 0].astype(jnp.int32)
    padded = (counts + MOE_RB - 1) // MOE_RB * MOE_RB
    pad_end = jnp.cumsum(padded)
    pad_start = pad_end - padded
    dest = _dest(top_idx, rank.reshape(top_idx.shape), pad_start.astype(jnp.int32)).reshape(-1)
    nvis_e = (counts + MOE_RMAX - 1) // MOE_RMAX
    vis_end = jnp.cumsum(nvis_e)
    n_vis = vis_end[-1]
    v = jnp.arange(MOE_NV, dtype=jnp.int32)
    vc = jnp.minimum(v, n_vis - 1)
    ve = jnp.minimum(jnp.searchsorted(vis_end, vc, side='right'), N_EXP - 1).astype(jnp.int32)
    local = vc - (vis_end[ve] - nvis_e[ve])
    vcnt = jnp.where(v < n_vis, jnp.minimum(MOE_RMAX, counts[ve] - local * MOE_RMAX), 0)
    vrow = pad_start[ve] + local * MOE_RMAX
    return dict(dest=dest, fill_from=(pad_start + counts).astype(jnp.int32), fill_to=pad_end.astype(jnp.int32),
                tail_blk=(pad_end[-1:] // MOE_RB).astype(jnp.int32),
                vis_e=ve, vis_cnt=vcnt.astype(jnp.int32), vis_row=vrow.astype(jnp.int32),
                n_vis=n_vis.reshape(1).astype(jnp.int32))


def _dest_kernel(start_ref, idx_ref, rank_ref, dest_ref):
    e = idx_ref[...]
    d = rank_ref[...]
    for x in range(N_EXP):
        d = d + jnp.where(e == x, start_ref[x], 0)
    dest_ref[...] = d


def _dest(top_idx, rank, pad_start):
    full = pl.BlockSpec(top_idx.shape, lambda i, *_: (0, 0))
    return pl.pallas_call(
        _dest_kernel,
        grid_spec=pltpu.PrefetchScalarGridSpec(num_scalar_prefetch=1, grid=(1,), in_specs=[full, full], out_specs=full),
        out_shape=jax.ShapeDtypeStruct(top_idx.shape, jnp.int32),
        name="dest",
    )(pad_start, top_idx, rank)


def _row_copy(src_ref, srow, dst_ref, drow, sem):
    return pltpu.make_async_copy(src_ref.at[pl.ds(srow, 1), :], dst_ref.at[pl.ds(drow, 1), :], sem)


def _tail_fill(src_ref, dst_ref, tail_ref, sem):
    def cp(b):
        return pltpu.make_async_copy(src_ref.at[pl.ds(0, MOE_RB), :],
                                     dst_ref.at[pl.ds(pl.multiple_of(b * MOE_RB, MOE_RB), MOE_RB), :], sem)

    def start(b, c):
        cp(b).start()
        return c
    lax.fori_loop(tail_ref[0], MOE_ROWS // MOE_RB, start, 0)

    def finish(b, c):
        cp(b).wait()
        return c
    lax.fori_loop(tail_ref[0], MOE_ROWS // MOE_RB, finish, 0)


def _scatter_kernel(dest_ref, from_ref, to_ref, tail_ref, hn_ref, xs_ref, zero_ref, sem, zsem):
    i = pl.program_id(0)
    tm = hn_ref.shape[0]

    @pl.when(i == 0)
    def _():
        zero_ref[...] = jnp.zeros_like(zero_ref)
        _tail_fill(zero_ref, xs_ref, tail_ref, zsem)

        def per_expert(e, c):
            def fill(r, c2):
                _row_copy(zero_ref, 0, xs_ref, r, zsem).start()
                return c2
            lax.fori_loop(from_ref[e], to_ref[e], fill, 0)

            def drain(r, c2):
                _row_copy(zero_ref, 0, xs_ref, r, zsem).wait()
                return c2
            lax.fori_loop(from_ref[e], to_ref[e], drain, 0)
            return c
        lax.fori_loop(0, N_EXP, per_expert, 0)

    def issue(t, c):
        for k in range(TOP_K):
            _row_copy(hn_ref, t, xs_ref, dest_ref[k * T + i * tm + t], sem).start()
        return c
    lax.fori_loop(0, tm, issue, 0)

    def drain(t, c):
        for k in range(TOP_K):
            _row_copy(hn_ref, t, xs_ref, dest_ref[k * T + i * tm + t], sem).wait()
        return c
    lax.fori_loop(0, tm, drain, 0)


def _scatter(hn, rt):
    tm = 256
    return pl.pallas_call(
        _scatter_kernel,
        grid_spec=pltpu.PrefetchScalarGridSpec(
            num_scalar_prefetch=4,
            grid=(T // tm,),
            in_specs=[pl.BlockSpec((tm, D // 2), lambda i, *_: (i, 0))],
            out_specs=pl.BlockSpec(memory_space=pl.ANY),
            scratch_shapes=[pltpu.VMEM((MOE_RB, D // 2), jnp.int32), pltpu.SemaphoreType.DMA,
                            pltpu.SemaphoreType.DMA],
        ),
        out_shape=jax.ShapeDtypeStruct((MOE_ROWS, D // 2), jnp.int32),
        compiler_params=pltpu.CompilerParams(
            dimension_semantics=("arbitrary",), vmem_limit_bytes=VMEM_LIMIT),
        name="scatter",
    )(rt["dest"], rt["fill_from"], rt["fill_to"], rt["tail_blk"], hn)


def _experts_kernel(ve_ref, vcnt_ref, vrow_ref, nvis_ref, tail_ref, xs_ref, w1g_ref, w1l_ref, w2_ref, b1g_ref,
                    b1l_ref, b2_ref, ys_ref, xin_ref, xb_ref, acc_ref, isem, osem):
    v = pl.program_id(0)
    j = pl.program_id(1)
    RB = MOE_RB
    BIG, MID = 4 * RB, 2 * RB
    nvis = nvis_ref[0]
    active = v < nvis
    first = j == 0
    last = j == MOE_NF - 1

    def geometry(vv):
        nblk = (vcnt_ref[vv] + RB - 1) // RB
        return vrow_ref[vv], nblk // 4, (nblk // 2) % 2, nblk % 2

    row0, nbig, mid, small = geometry(v)
    mid_r0 = nbig * BIG
    small_r0 = mid_r0 + mid * MID

    def x_copy(vrow, r0, n, slot):
        return pltpu.make_async_copy(xs_ref.at[pl.ds(pl.multiple_of(vrow + r0, RB), n), :],
                                     xin_ref.at[slot, pl.ds(0, n), :], isem.at[slot])

    def y_copy(r0, n):
        return pltpu.make_async_copy(acc_ref.at[pl.ds(pl.multiple_of(r0, RB), n), :],
                                     ys_ref.at[pl.ds(pl.multiple_of(row0 + r0, RB), n), :], osem)

    def start_unit(vv, u):
        vrow, nb, md, sm = geometry(vv)

        @pl.when(u < nb)
        def _():
            x_copy(vrow, u * BIG, BIG, u % 2).start()

        @pl.when((u == nb) & (md == 1))
        def _():
            x_copy(vrow, nb * BIG, MID, u % 2).start()

        @pl.when((u == nb + md) & (sm == 1))
        def _():
            x_copy(vrow, nb * BIG + md * MID, RB, u % 2).start()

    @pl.when((v == 0) & first)
    def _():
        acc_ref[...] = jnp.zeros_like(acc_ref)
        _tail_fill(acc_ref, ys_ref, tail_ref, osem)
        start_unit(0, 0)

    def block(u, r0, n):
        rows = pl.ds(pl.multiple_of(r0, RB), n)
        slot = u % 2

        @pl.when(first)
        def _():
            x_copy(row0, r0, n, slot).wait()
            start_unit(v, u + 1)
            lo_half, hi_half = _unpack_halves(xin_ref[slot, pl.ds(0, n), :])
            xb_ref[rows, :D // 2] = lo_half
            xb_ref[rows, D // 2:] = hi_half

        x = xb_ref[rows, :]
        glu = jnp.minimum(_dot(x, w1g_ref[0].astype(BF16)) + b1g_ref[0], SWIGLU_LIMIT)
        lin = jnp.clip(_dot(x, w1l_ref[0].astype(BF16)) + b1l_ref[0], -SWIGLU_LIMIT, SWIGLU_LIMIT)
        act = glu * _sigmoid(SWIGLU_ALPHA * glu) * (lin + 1.0)
        part = _dot(act.astype(BF16), w2_ref[0].astype(BF16))

        acc_ref[rows, :] = part + jnp.where(first, b2_ref[0], acc_ref[rows, :])

        @pl.when(last)
        def _():
            y_copy(r0, n).start()

    @pl.when(active)
    def _():
        def big(i, c):
            block(i, i * BIG, BIG)
            return c
        lax.fori_loop(0, nbig, big, 0)

        @pl.when(mid == 1)
        def _():
            block(nbig, mid_r0, MID)

        @pl.when(small == 1)
        def _():
            block(nbig + mid, small_r0, RB)

    @pl.when(active & last)
    def _():
        def finish(i, c):
            y_copy(i * BIG, BIG).wait()
            return c
        lax.fori_loop(0, nbig, finish, 0)

        @pl.when(mid == 1)
        def _():
            y_copy(mid_r0, MID).wait()

        @pl.when(small == 1)
        def _():
            y_copy(small_r0, RB).wait()

        @pl.when(v + 1 < nvis)
        def _():
            start_unit(v + 1, 0)


def _experts(xs, rt, w1, b1, w2, b2):
    b1 = b1.reshape(N_EXP, 1, 2 * D_FF)
    b2 = b2.reshape(N_EXP, 1, D)

    def jj(v, j, ve, vcnt, vrow, nvis, tail):
        return jnp.where(v < nvis[0], j, MOE_NF - 1)

    return pl.pallas_call(
        _experts_kernel,
        grid_spec=pltpu.PrefetchScalarGridSpec(
            num_scalar_prefetch=5,
            grid=(MOE_NV, MOE_NF),
            in_specs=[
                pl.BlockSpec(memory_space=pl.ANY),
                pl.BlockSpec((1, D, MOE_TF), lambda v, j, ve, *s: (ve[v], 0, jj(v, j, ve, *s))),
                pl.BlockSpec((1, D, MOE_TF), lambda v, j, ve, *s: (ve[v], 0, MOE_NF + jj(v, j, ve, *s))),
                pl.BlockSpec((1, MOE_TF, D), lambda v, j, ve, *s: (ve[v], jj(v, j, ve, *s), 0)),
                pl.BlockSpec((1, 1, MOE_TF), lambda v, j, ve, *s: (ve[v], 0, jj(v, j, ve, *s))),
                pl.BlockSpec((1, 1, MOE_TF), lambda v, j, ve, *s: (ve[v], 0, MOE_NF + jj(v, j, ve, *s))),
                pl.BlockSpec((1, 1, D), lambda v, j, ve, *s: (ve[v], 0, 0)),
            ],
            out_specs=pl.BlockSpec(memory_space=pl.ANY),
            scratch_shapes=[
                pltpu.VMEM((2, 4 * MOE_RB, D // 2), jnp.int32),
                pltpu.VMEM((MOE_RMAX, D), BF16),
                pltpu.VMEM((MOE_RMAX, D), F32),
                pltpu.SemaphoreType.DMA((2,)),
                pltpu.SemaphoreType.DMA,
            ],
        ),
        out_shape=jax.ShapeDtypeStruct((MOE_ROWS, D), F32),
        compiler_params=pltpu.CompilerParams(
            dimension_semantics=("arbitrary", "arbitrary"), vmem_limit_bytes=VMEM_LIMIT),
        name="experts",
    )(rt["vis_e"], rt["vis_cnt"], rt["vis_row"], rt["n_vis"], rt["tail_blk"], xs, w1, w1, w2, b1, b1, b2)


def _combine_kernel(dest_ref, ys_ref, h_ref, gate_ref, gf_ref, o_ref, buf_ref, sem):
    i = pl.program_id(0)
    tm = h_ref.shape[0]

    def issue(t, c):
        for k in range(TOP_K):
            _row_copy(ys_ref, dest_ref[k * T + i * tm + t], buf_ref.at[k], t, sem).start()
        return c
    lax.fori_loop(0, tm, issue, 0)

    def drain(t, c):
        for k in range(TOP_K):
            _row_copy(ys_ref, dest_ref[k * T + i * tm + t], buf_ref.at[k], t, sem).wait()
        return c
    lax.fori_loop(0, tm, drain, 0)

    h = h_ref[...]
    for k in range(TOP_K):
        h = h + gate_ref[:, k:k + 1] * buf_ref[k]
    o_ref[...] = _rms(h, gf_ref[...])


def _combine(ys, h, gates_t, gf, rt):
    tm = 256
    return pl.pallas_call(
        _combine_kernel,
        grid_spec=pltpu.PrefetchScalarGridSpec(
            num_scalar_prefetch=1,
            grid=(T // tm,),
            in_specs=[
                pl.BlockSpec(memory_space=pl.ANY),
                pl.BlockSpec((tm, D), lambda i, *_: (i, 0)),
                pl.BlockSpec((tm, TOP_K), lambda i, *_: (i, 0)),
                pl.BlockSpec((1, D), lambda i, *_: (0, 0)),
            ],
            out_specs=pl.BlockSpec((tm, D), lambda i, *_: (i, 0)),
            scratch_shapes=[pltpu.VMEM((TOP_K, tm, D), F32), pltpu.SemaphoreType.DMA],
        ),
        out_shape=jax.ShapeDtypeStruct((T, D), F32),
        compiler_params=pltpu.CompilerParams(
            dimension_semantics=("arbitrary",), vmem_limit_bytes=VMEM_LIMIT),
        name="combine",
    )(rt["dest"], ys, h, gates_t, gf)


def kernel(x, norm1_g, w_in, gla_gate_w2, gla_gate_b, gla_norm_g, s5_a_re, s5_a_im, s5_b_re, s5_b_im, s5_c_re, s5_c_im, s5_d, s5_log_step, s5_glu_w, s5_glu_b, s5_norm_g, w_out, norm2_g, router_w, router_b, expert_w1, expert_b1, expert_w2, expert_b2, final_norm_g):
    assert x.shape == (1, T, D) and w_in.shape[0] == 1
    xt = x.reshape(T, D)
    nqkvr = 2 * GLA_H * GLA_DK + 2 * GLA_H * GLA_DV
    wi = w_in[0]
    wq = wi[:, :nqkvr].astype(BF16)
    wg = jnp.pad(wi[:, nqkvr:nqkvr + GATE_RANK], ((0, 0), (0, 128 - GATE_RANK))).astype(BF16)
    wu = wi[:, nqkvr + GATE_RANK:].astype(BF16)
    w2g = jnp.pad(gla_gate_w2[0], ((0, 128 - GATE_RANK), (0, 0)))
    qkvr, lg, u = _proj(xt, norm1_g, wq, wg, wu, w2g, gla_gate_b)
    gla = _gla(qkvr, lg, gla_norm_g)
    tables = _s5_tables(s5_a_re[0], s5_a_im[0], s5_b_re[0], s5_b_im[0], s5_c_re[0], s5_c_im[0],
                        s5_d[0], s5_log_step[0])
    y = _s5(u, tables)
    h, hn, top_idx, gates, rank, counts = _post(
        y, gla, xt, s5_glu_w[0].astype(BF16), s5_glu_b, s5_norm_g, w_out[0].astype(BF16), norm2_g,
        jnp.pad(router_w[0], ((0, 0), (0, 128 - N_EXP))), router_b.reshape(N_EXP, 1))
    rt = _route(top_idx, rank, counts)
    xs = _scatter(hn, rt)
    ys = _experts(xs, rt, expert_w1[0], expert_b1[0], expert_w2[0], expert_b2[0])
    out = _combine(ys, h, gates.T, final_norm_g.reshape(1, D), rt)
    return out.reshape(1, T, D)
```

```python
import functools
import math

import jax
import jax.numpy as jnp
from jax import lax
from jax.experimental import pallas as pl
from jax.experimental.pallas import tpu as pltpu

F32 = jnp.float32
BF16 = jnp.bfloat16
HIGHEST = lax.Precision.HIGHEST

T = 8192
D = 2048
GLA_H = 4
GLA_DK = 128
GLA_DV = 256
GLA_CHUNK = 64
GLA_SUB = 16
GATE_RANK = 16
GATE_TAU = 16.0
S5_G = 64
S5_N = 16
S5_P = 64
S5_L = 16
S5_GT = 8
N_EXP = 32
TOP_K = 4
D_FF = 2048
SWIGLU_ALPHA = 1.702
SWIGLU_LIMIT = 7.0
RMS_EPS = 1e-6

MOE_RB = 128
MOE_RMAX = 1536
MOE_TF = 512
MOE_NF = D_FF // MOE_TF
MOE_NV = -(-T * TOP_K // MOE_RMAX) + N_EXP
MOE_ROWS = T * TOP_K + N_EXP * MOE_RB
VMEM_LIMIT = 56 * 1024 * 1024

NN = (((1,), (0,)), ((), ()))
NT = (((1,), (1,)), ((), ()))
TN = (((0,), (0,)), ((), ()))


def _dot(a, b, dims=NN, precision=None):
    return lax.dot_general(a, b, dims, preferred_element_type=F32, precision=precision)


def _rms(x, g):
    return x * lax.rsqrt(jnp.mean(x * x, axis=-1, keepdims=True) + RMS_EPS) * g


def _sigmoid(x):
    return 1.0 / (1.0 + jnp.exp(-x))


def _pack_halves(lo_half, hi_half):
    a = lax.bitcast_convert_type(lo_half.astype(F32), jnp.int32)
    b = lax.bitcast_convert_type(hi_half.astype(F32), jnp.int32)
    return a | lax.shift_right_logical(b, 16)


def _unpack_halves(p):
    a = lax.bitcast_convert_type(p & jnp.int32(-65536), F32).astype(BF16)
    b = lax.bitcast_convert_type(lax.shift_left(p, 16), F32).astype(BF16)
    return a, b


def _const_spec(shape):
    nd = len(shape)
    return pl.BlockSpec(shape, lambda *_: (0,) * nd, pipeline_mode=pl.Buffered(1))


def _proj_kernel(x_ref, g1_ref, wq_ref, wg_ref, wu_ref, w2_ref, gb_ref, qkvr_ref, lg_ref, u_ref):
    hb = _rms(x_ref[...], g1_ref[...]).astype(BF16)
    p = _dot(hb, wq_ref[...])
    nq = GLA_H * GLA_DK
    qkvr_ref[:, :nq] = (p[:, :nq] * (GLA_DK ** -0.5)).astype(BF16)
    qkvr_ref[:, nq:] = p[:, nq:].astype(BF16)
    glr = _dot(hb, wg_ref[...])
    z = _dot(glr, w2_ref[...], precision=HIGHEST) + gb_ref[...]
    lg_ref[...] = (jnp.minimum(z, 0.0) - jnp.log(1.0 + jnp.exp(-jnp.abs(z)))) * (1.0 / GATE_TAU)
    u_ref[...] = _dot(hb, wu_ref[...])


def _proj(x, g1, wq, wg, wu, w2, gb):
    tm = 256
    nqkvr = wq.shape[1]
    return pl.pallas_call(
        _proj_kernel,
        grid=(T // tm,),
        in_specs=[
            pl.BlockSpec((tm, D), lambda i: (i, 0)),
            _const_spec((1, D)),
            _const_spec(wq.shape), _const_spec(wg.shape), _const_spec(wu.shape),
            _const_spec(w2.shape), _const_spec(gb.shape),
        ],
        out_specs=[
            pl.BlockSpec((tm, nqkvr), lambda i: (i, 0)),
            pl.BlockSpec((tm, GLA_H * GLA_DK), lambda i: (i, 0)),
            pl.BlockSpec((tm, S5_G * S5_N), lambda i: (i, 0)),
        ],
        out_shape=[
            jax.ShapeDtypeStruct((T, nqkvr), BF16),
            jax.ShapeDtypeStruct((T, GLA_H * GLA_DK), F32),
            jax.ShapeDtypeStruct((T, S5_G * S5_N), F32),
        ],
        compiler_params=pltpu.CompilerParams(
            dimension_semantics=("arbitrary",), vmem_limit_bytes=VMEM_LIMIT),
        name="proj",
    )(x, g1, wq, wg, wu, w2, gb)


def _gla_kernel(q_ref, k_ref, v_ref, r_ref, lg_ref, gn_ref, tril_ref, esel_ref, o_ref, state_ref):
    C, S = GLA_CHUNK, GLA_SUB
    nsub = C // S

    @pl.when(pl.program_id(0) == 0)
    def _():
        state_ref[...] = jnp.zeros_like(state_ref)

    row = lax.broadcasted_iota(jnp.int32, (C, C), 0)
    col = lax.broadcasted_iota(jnp.int32, (C, C), 1)
    below = (col // S) < (row // S)
    diag = ((col // S) == (row // S)) & (col <= row)
    tril = tril_ref[...]
    esel = esel_ref[...]
    gn = gn_ref[...]

    def head(h, rows, b):
        kcols = slice(h * GLA_DK, (h + 1) * GLA_DK)
        vcols = slice(h * GLA_DV, (h + 1) * GLA_DV)
        q = q_ref[rows, kcols].astype(F32)
        k = k_ref[rows, kcols].astype(F32)
        v = v_ref[rows, vcols]
        bl = b[C - 1:C, :]
        st = state_ref[h]
        o = _dot((q * jnp.exp2(b)).astype(BF16), st.astype(BF16), NT)
        s_rows = [jnp.zeros((S, C), F32)]
        for j in range(1, nsub):
            ref = b[j * S - 1:j * S, :]
            qj = (q[j * S:(j + 1) * S] * jnp.exp2(b[j * S:(j + 1) * S] - ref)).astype(BF16)
            kj = (k * jnp.exp2(jnp.minimum(ref - b, 0.0))).astype(BF16)
            s_rows.append(_dot(qj, kj, NT))
        s_off = jnp.concatenate(s_rows, axis=0)
        q3 = q.reshape(nsub, S, GLA_DK)
        k3 = k.reshape(nsub, S, GLA_DK)
        b3 = b.reshape(nsub, S, GLA_DK)
        xs = []
        for s in range(S):
            lo = (s // 8) * 8
            dec = jnp.exp2(jnp.minimum(b3[:, lo:] - b3[:, s:s + 1, :], 0.0))
            x = q3[:, lo:] * k3[:, s:s + 1, :] * dec
            if lo:
                x = jnp.concatenate([jnp.zeros((nsub, lo, GLA_DK), F32), x], axis=1)
            xs.append(x.reshape(C, GLA_DK).astype(BF16))
        dsc = _dot(jnp.concatenate(xs, axis=1), esel)
        a = jnp.where(below, s_off, jnp.where(diag, dsc, 0.0))
        o = o + _dot(a.astype(BF16), v)
        kout = (k * jnp.exp2(bl - b)).astype(BF16)
        state_ref[h] = st * jnp.exp2(bl) + _dot(v, kout, TN)
        y = _rms(o, gn)
        r = r_ref[rows, vcols].astype(F32)
        o_ref[rows, vcols] = (y * (r * _sigmoid(r))).astype(BF16)

    def chunk(c, carry):
        rows = pl.ds(pl.multiple_of(c * C, C), C)
        b = _dot(tril, lg_ref[rows, :], precision=HIGHEST) * math.log2(math.e)
        for h in range(GLA_H):
            head(h, rows, b[:, h * GLA_DK:(h + 1) * GLA_DK])
        return carry

    lax.fori_loop(0, q_ref.shape[0] // C, chunk, 0, unroll=4)


def _gla(qkvr, lg, gn):
    tb = 512
    C, S = GLA_CHUNK, GLA_SUB
    nk, nv = GLA_H * GLA_DK, GLA_H * GLA_DV
    tril = jnp.tril(jnp.ones((C, C), F32))
    esel = (jnp.arange(S * GLA_DK)[:, None] // GLA_DK == jnp.arange(C)[None, :] % S).astype(BF16)
    return pl.pallas_call(
        _gla_kernel,
        grid=(T // tb,),
        in_specs=[
            pl.BlockSpec((tb, nk), lambda i: (i, 0)),
            pl.BlockSpec((tb, nk), lambda i: (i, 1)),
            pl.BlockSpec((tb, nv), lambda i: (i, 1)),
            pl.BlockSpec((tb, nv), lambda i: (i, 2)),
            pl.BlockSpec((tb, nk), lambda i: (i, 0)),
            _const_spec((1, GLA_DV)), _const_spec((C, C)), _const_spec((S * GLA_DK, C)),
        ],
        out_specs=pl.BlockSpec((tb, nv), lambda i: (i, 0)),
        out_shape=jax.ShapeDtypeStruct((T, nv), BF16),
        scratch_shapes=[pltpu.VMEM((GLA_H, GLA_DV, GLA_DK), F32)],
        compiler_params=pltpu.CompilerParams(
            dimension_semantics=("arbitrary",), vmem_limit_bytes=VMEM_LIMIT),
        name="gla",
    )(qkvr, qkvr, qkvr, qkvr, lg, gn, tril, esel)


def _s5_tables(a_re, a_im, b_re, b_im, c_re, c_im, d_skip, log_step):
    L, G, P, N = S5_L, S5_G, S5_P, S5_N
    delta = jnp.exp(log_step)[:, None]
    ar, ai = a_re * delta, a_im * delta

    def apow(tau):
        tau = jnp.asarray(tau, F32)[None, :, None]
        mag = jnp.exp(ar[:, None, :] * tau)
        ph = ai[:, None, :] * tau
        return mag * jnp.cos(ph), mag * jnp.sin(ph)

    p_re, p_im = apow(jnp.arange(L + 1))
    ab_re, ab_im = p_re[:, 1], p_im[:, 1]
    den = a_re * a_re + a_im * a_im
    cf_re = ((ab_re - 1.0) * a_re + ab_im * a_im) / den
    cf_im = (ab_im * a_re - (ab_re - 1.0) * a_im) / den
    bb_re = cf_re[:, :, None] * b_re - cf_im[:, :, None] * b_im
    bb_im = cf_re[:, :, None] * b_im + cf_im[:, :, None] * b_re
    e_re = p_re[:, :L, :, None] * bb_re[:, None] - p_im[:, :L, :, None] * bb_im[:, None]
    e_im = p_re[:, :L, :, None] * bb_im[:, None] + p_im[:, :L, :, None] * bb_re[:, None]
    kk = (jnp.einsum('gnp,gtpm->gtnm', c_re, e_re, precision=HIGHEST)
          - jnp.einsum('gnp,gtpm->gtnm', c_im, e_im, precision=HIGHEST))
    kk = kk.at[:, 0].add(jnp.eye(N, dtype=F32)[None] * d_skip[:, :, None])
    r_re, r_im = apow(L - 1 - jnp.arange(L))
    bs_re = (r_re[:, :, None, :] * bb_re.transpose(0, 2, 1)[:, None] - r_im[:, :, None, :] * bb_im.transpose(0, 2, 1)[:, None])
    bs_im = (r_re[:, :, None, :] * bb_im.transpose(0, 2, 1)[:, None] + r_im[:, :, None, :] * bb_re.transpose(0, 2, 1)[:, None])
    ct_re = c_re.transpose(0, 2, 1)[:, :, None, :]
    ct_im = c_im.transpose(0, 2, 1)[:, :, None, :]
    q_re = p_re[:, 1:].transpose(0, 2, 1)[:, :, :, None]
    q_im = p_im[:, 1:].transpose(0, 2, 1)[:, :, :, None]
    cre = ct_re * q_re - ct_im * q_im
    cim = -(ct_re * q_im + ct_im * q_re)
    GT = S5_GT
    nt = G // GT

    def tile_groups(w):
        return w.reshape((nt, GT) + w.shape[1:])

    kc = tile_groups(kk).transpose(0, 1, 4, 2, 3).reshape(nt, GT * N, L * N)

    def in_table(w):
        return tile_groups(w).transpose(0, 2, 1, 3, 4).reshape(nt, L * GT * N, P)

    def out_table(w):
        return w.reshape(nt, GT * P, L * N)

    nstep = (T // L - 1).bit_length()
    sc_re, sc_im = apow(jnp.asarray([L * 2 ** i for i in range(nstep)]))

    def lanes(w):
        return tile_groups(w).transpose(0, 2, 1, 3).reshape(nt, w.shape[1], GT * P)

    r1 = jnp.arange(L * N)[:, None]
    c1 = jnp.arange(L * GT * N)[None, :]
    wide_tn = ((r1 // N == c1 // (GT * N)) & (r1 % N == c1 % N)).astype(BF16)
    wide_p = (jnp.arange(P)[:, None] == jnp.arange(GT * P)[None, :] % P).astype(BF16)
    return dict(
        kc=kc.astype(BF16),
        bs_re=in_table(bs_re).astype(BF16), bs_im=in_table(bs_im).astype(BF16),
        cre=out_table(cre).astype(BF16), cim=out_table(cim).astype(BF16),
        sc_re=lanes(sc_re), sc_im=lanes(sc_im), wide_tn=wide_tn, wide_p=wide_p,
    )


def _s5_kernel(u_ref, kc_ref, bsr_ref, bsi_ref, cre_ref, cim_ref, scr_ref, sci_ref, wtn_ref, wp_ref,
               y_ref, m_ref):
    L, N, P = S5_L, S5_N, S5_P
    W = S5_GT * N
    nch = u_ref.shape[0] // L

    @pl.when(pl.program_id(0) == 0)
    def _():
        m_ref[...] = jnp.zeros_like(m_ref)

    def widen(compact, wide, row_group, col_group):
        full = _dot(compact, wide)
        r = lax.broadcasted_iota(jnp.int32, full.shape, 0)
        c = lax.broadcasted_iota(jnp.int32, full.shape, 1)
        return jnp.where(row_group(r) == col_group(c), full, 0.0).astype(BF16)

    wtn, wp = wtn_ref[...], wp_ref[...]
    bd = widen(kc_ref[0], wtn, lambda r: r // N, lambda c: (c % W) // N)
    bsr = widen(bsr_ref[0], wp, lambda r: (r % W) // N, lambda c: c // P)
    bsi = widen(bsi_ref[0], wp, lambda r: (r % W) // N, lambda c: c // P)
    cre = widen(cre_ref[0], wtn, lambda r: r // P, lambda c: (c % W) // N)
    cim = widen(cim_ref[0], wtn, lambda r: r // P, lambda c: (c % W) // N)
    for s in range(L):
        m_ref[s * W:(s + 1) * W, s * W:] = bd[:, :(L - s) * W]
    ucat = jnp.concatenate([u_ref[pl.ds(s, nch, stride=L), :].astype(BF16) for s in range(L)], axis=1)
    hr = _dot(ucat, bsr)
    hi = _dot(ucat, bsi)
    row = lax.broadcasted_iota(jnp.int32, hr.shape, 0)

    def shift(x, d):
        if d % 8 == 0:
            return jnp.concatenate([jnp.zeros((d, x.shape[1]), F32), x[:nch - d]], axis=0)
        return jnp.where(row >= d, pltpu.roll(x, d, 0), 0.0)

    for i in range(scr_ref.shape[1]):
        mr, mi = scr_ref[0, i:i + 1, :], sci_ref[0, i:i + 1, :]
        pr, pi = shift(hr, 2 ** i), shift(hi, 2 ** i)
        hr, hi = hr + mr * pr - mi * pi, hi + mr * pi + mi * pr
    gr, gi = shift(hr, 1), shift(hi, 1)
    grb, gib = gr.astype(BF16), gi.astype(BF16)
    for tp in range(L // 2):
        cols = slice(2 * tp * W, (2 * tp + 2) * W)
        depth = (2 * tp + 2) * W
        y = _dot(ucat[:, :depth], m_ref[:depth, cols]) + _dot(grb, cre[:, cols]) + _dot(gib, cim[:, cols])
        for t in (2 * tp, 2 * tp + 1):
            y_ref[pl.ds(t, nch, stride=L), :] = y[:, (t - 2 * tp) * W:(t - 2 * tp + 1) * W]


def _s5(u, tb):
    W = S5_GT * S5_N
    nt = S5_G // S5_GT

    def tile_spec(a):
        return pl.BlockSpec((1,) + a.shape[1:], lambda g: (g,) + (0,) * (a.ndim - 1))

    ws = [tb[n] for n in ["kc", "bs_re", "bs_im", "cre", "cim", "sc_re", "sc_im"]]
    consts = [tb["wide_tn"], tb["wide_p"]]
    return pl.pallas_call(
        _s5_kernel,
        grid=(nt,),
        in_specs=([pl.BlockSpec((T, W), lambda g: (0, g))] + [tile_spec(w) for w in ws]
                  + [_const_spec(c.shape) for c in consts]),
        out_specs=pl.BlockSpec((T, W), lambda g: (0, g)),
        out_shape=jax.ShapeDtypeStruct((T, S5_G * S5_N), F32),
        scratch_shapes=[pltpu.VMEM((S5_L * W, S5_L * W), BF16)],
        compiler_params=pltpu.CompilerParams(
            dimension_semantics=("arbitrary",), vmem_limit_bytes=VMEM_LIMIT),
        name="s5",
    )(u, *ws, *consts)


def _post_kernel(y_ref, gla_ref, x_ref, gw_ref, gb_ref, sg_ref, wo_ref, n2_ref, rw_ref, rb_ref, before_ref,
                 h_ref, hn_ref, idx_ref, gate_ref, rank_ref, cnt_ref, run_ref):
    @pl.when(pl.program_id(0) == 0)
    def _():
        run_ref[...] = jnp.zeros_like(run_ref)

    y = y_ref[...]
    z = 0.5 * y * (1.0 + jnp.tanh(math.sqrt(2.0 / math.pi) * (y + 0.044715 * (y * y * y))))
    z = z * _sigmoid(_dot(z.astype(BF16), gw_ref[...]) + gb_ref[...])
    s5o = _rms(z, sg_ref[...]).astype(BF16)
    half = GLA_H * GLA_DV
    h = x_ref[...] + _dot(gla_ref[...], wo_ref[:half, :]) + _dot(s5o, wo_ref[half:, :])
    h_ref[...] = h
    hn = _rms(h, n2_ref[...])
    hi = hn.astype(BF16)
    hn_ref[...] = _pack_halves(hi[:, :D // 2], hi[:, D // 2:])
    lo = (hn - hi.astype(F32)).astype(BF16)
    rw = rw_ref[...]
    rw_hi = rw.astype(BF16)
    rw_lo = (rw - rw_hi.astype(F32)).astype(BF16)
    lg = _dot(hi, rw_hi) + (_dot(lo, rw_hi) + _dot(hi, rw_lo))
    lt = lg.T[:N_EXP] + rb_ref[...]
    eid = lax.broadcasted_iota(jnp.int32, lt.shape, 0).astype(F32)
    vals, idxs = [], []
    for _ in range(TOP_K):
        m = jnp.max(lt, axis=0, keepdims=True)
        sel = jnp.min(jnp.where(lt == m, eid, float(N_EXP)), axis=0, keepdims=True)
        vals.append(m)
        idxs.append(sel)
        lt = jnp.where(eid == sel, -jnp.inf, lt)
    ex = [jnp.exp(vv - vals[0]) for vv in vals]
    inv = 1.0 / (ex[0] + ex[1] + ex[2] + ex[3])
    idx_ref[...] = jnp.concatenate(idxs, axis=0).astype(jnp.int32)
    gate_ref[...] = jnp.concatenate([e * inv for e in ex], axis=0)
    run = run_ref[:, :1]
    ranks = []
    for sel in idxs:
        onehot = jnp.where(eid == sel, 1.0, 0.0)
        earlier = _dot(onehot.astype(BF16), before_ref[...])
        ranks.append(jnp.sum(onehot * (run + earlier), axis=0, keepdims=True))
        run = run + jnp.sum(onehot, axis=1, keepdims=True)
    rank_ref[...] = jnp.concatenate(ranks, axis=0).astype(jnp.int32)
    run_ref[...] = jnp.broadcast_to(run, run_ref.shape)
    cnt_ref[...] = jnp.broadcast_to(run, cnt_ref.shape)


def _post(y, gla, x, gw, gb, sg, wo, n2, rwt, rb):
    tm = 256
    width = S5_G * S5_N
    before = (jnp.arange(tm)[:, None] < jnp.arange(tm)[None, :]).astype(BF16)
    return pl.pallas_call(
        _post_kernel,
        grid=(T // tm,),
        in_specs=[
            pl.BlockSpec((tm, width), lambda i: (i, 0)),
            pl.BlockSpec((tm, GLA_H * GLA_DV), lambda i: (i, 0)),
            pl.BlockSpec((tm, D), lambda i: (i, 0)),
            _const_spec(gw.shape), _const_spec(gb.shape), _const_spec(sg.shape), _const_spec(wo.shape),
            _const_spec(n2.shape), _const_spec(rwt.shape), _const_spec(rb.shape), _const_spec(before.shape),
        ],
        out_specs=[
            pl.BlockSpec((tm, D), lambda i: (i, 0)),
            pl.BlockSpec((tm, D // 2), lambda i: (i, 0)),
            pl.BlockSpec((TOP_K, tm), lambda i: (0, i)),
            pl.BlockSpec((TOP_K, tm), lambda i: (0, i)),
            pl.BlockSpec((TOP_K, tm), lambda i: (0, i)),
            pl.BlockSpec((N_EXP, 128), lambda i: (0, 0)),
        ],
        out_shape=[
            jax.ShapeDtypeStruct((T, D), F32),
            jax.ShapeDtypeStruct((T, D // 2), jnp.int32),
            jax.ShapeDtypeStruct((TOP_K, T), jnp.int32),
            jax.ShapeDtypeStruct((TOP_K, T), F32),
            jax.ShapeDtypeStruct((TOP_K, T), jnp.int32),
            jax.ShapeDtypeStruct((N_EXP, 128), F32),
        ],
        scratch_shapes=[pltpu.VMEM((N_EXP, 128), F32)],
        compiler_params=pltpu.CompilerParams(
            dimension_semantics=("arbitrary",), vmem_limit_bytes=VMEM_LIMIT),
        name="post",
    )(y, gla, x, gw, gb, sg, wo, n2, rwt, rb, before)


def _route(top_idx, rank, counts):
    e_flat = top_idx.reshape(-1)
    rank = rank.reshape(-1)
    counts = counts[:, 0].astype(jnp.int32)
    padded = (counts + MOE_RB - 1) // MOE_RB * MOE_RB
    pad_end = jnp.cumsum(padded)
    pad_start = pad_end - padded
    dest = _dest(top_idx, rank.reshape(top_idx.shape), pad_start.astype(jnp.int32)).reshape(-1)
    nvis_e = (counts + MOE_RMAX - 1) // MOE_RMAX
    vis_end = jnp.cumsum(nvis_e)
    n_vis = vis_end[-1]
    v = jnp.arange(MOE_NV, dtype=jnp.int32)
    vc = jnp.minimum(v, n_vis - 1)
    ve = jnp.minimum(jnp.searchsorted(vis_end, vc, side='right'), N_EXP - 1).astype(jnp.int32)
    local = vc - (vis_end[ve] - nvis_e[ve])
    vcnt = jnp.where(v < n_vis, jnp.minimum(MOE_RMAX, counts[ve] - local * MOE_RMAX), 0)
    vrow = pad_start[ve] + local * MOE_RMAX
    return dict(dest=dest, fill_from=(pad_start + counts).astype(jnp.int32), fill_to=pad_end.astype(jnp.int32),
                tail_blk=(pad_end[-1:] // MOE_RB).astype(jnp.int32),
                vis_e=ve, vis_cnt=vcnt.astype(jnp.int32), vis_row=vrow.astype(jnp.int32),
                n_vis=n_vis.reshape(1).astype(jnp.int32))


def _dest_kernel(start_ref, idx_ref, rank_ref, dest_ref):
    e = idx_ref[...]
    d = rank_ref[...]
    for x in range(N_EXP):
        d = d + jnp.where(e == x, start_ref[x], 0)
    dest_ref[...] = d


def _dest(top_idx, rank, pad_start):
    full = pl.BlockSpec(top_idx.shape, lambda i, *_: (0, 0))
    return pl.pallas_call(
        _dest_kernel,
        grid_spec=pltpu.PrefetchScalarGridSpec(num_scalar_prefetch=1, grid=(1,), in_specs=[full, full], out_specs=full),
        out_shape=jax.ShapeDtypeStruct(top_idx.shape, jnp.int32),
        name="dest",
    )(pad_start, top_idx, rank)


def _row_copy(src_ref, srow, dst_ref, drow, sem):
    return pltpu.make_async_copy(src_ref.at[pl.ds(srow, 1), :], dst_ref.at[pl.ds(drow, 1), :], sem)


def _tail_fill(src_ref, dst_ref, tail_ref, sem):
    def cp(b):
        return pltpu.make_async_copy(src_ref.at[pl.ds(0, MOE_RB), :],
                                     dst_ref.at[pl.ds(pl.multiple_of(b * MOE_RB, MOE_RB), MOE_RB), :], sem)

    def start(b, c):
        cp(b).start()
        return c
    lax.fori_loop(tail_ref[0], MOE_ROWS // MOE_RB, start, 0)

    def finish(b, c):
        cp(b).wait()
        return c
    lax.fori_loop(tail_ref[0], MOE_ROWS // MOE_RB, finish, 0)


def _scatter_kernel(dest_ref, from_ref, to_ref, tail_ref, hn_ref, xs_ref, zero_ref, sem, zsem):
    i = pl.program_id(0)
    tm = hn_ref.shape[0]

    @pl.when(i == 0)
    def _():
        zero_ref[...] = jnp.zeros_like(zero_ref)
        _tail_fill(zero_ref, xs_ref, tail_ref, zsem)

        def per_expert(e, c):
            def fill(r, c2):
                _row_copy(zero_ref, 0, xs_ref, r, zsem).start()
                return c2
            lax.fori_loop(from_ref[e], to_ref[e], fill, 0)

            def drain(r, c2):
                _row_copy(zero_ref, 0, xs_ref, r, zsem).wait()
                return c2
            lax.fori_loop(from_ref[e], to_ref[e], drain, 0)
            return c
        lax.fori_loop(0, N_EXP, per_expert, 0)

    def issue(t, c):
        for k in range(TOP_K):
            _row_copy(hn_ref, t, xs_ref, dest_ref[k * T + i * tm + t], sem).start()
        return c
    lax.fori_loop(0, tm, issue, 0)

    for k in range(TOP_K):
        pltpu.make_async_copy(hn_ref, xs_ref.at[pl.ds(0, tm), :], sem).wait()


def _scatter(hn, rt):
    tm = 256
    return pl.pallas_call(
        _scatter_kernel,
        grid_spec=pltpu.PrefetchScalarGridSpec(
            num_scalar_prefetch=4,
            grid=(T // tm,),
            in_specs=[pl.BlockSpec((tm, D // 2), lambda i, *_: (i, 0))],
            out_specs=pl.BlockSpec(memory_space=pl.ANY),
            scratch_shapes=[pltpu.VMEM((MOE_RB, D // 2), jnp.int32), pltpu.SemaphoreType.DMA,
                            pltpu.SemaphoreType.DMA],
        ),
        out_shape=jax.ShapeDtypeStruct((MOE_ROWS, D // 2), jnp.int32),
        compiler_params=pltpu.CompilerParams(
            dimension_semantics=("arbitrary",), vmem_limit_bytes=VMEM_LIMIT),
        name="scatter",
    )(rt["dest"], rt["fill_from"], rt["fill_to"], rt["tail_blk"], hn)


def _experts_kernel(ve_ref, vcnt_ref, vrow_ref, nvis_ref, tail_ref, xs_ref, w1g_ref, w1l_ref, w2_ref, b1g_ref,
                    b1l_ref, b2_ref, ys_ref, xin_ref, xb_ref, acc_ref, isem, osem):
    v = pl.program_id(0)
    j = pl.program_id(1)
    RB = MOE_RB
    BIG, MID = 4 * RB, 2 * RB
    nvis = nvis_ref[0]
    active = v < nvis
    first = j == 0
    last = j == MOE_NF - 1

    def geometry(vv):
        nblk = (vcnt_ref[vv] + RB - 1) // RB
        return vrow_ref[vv], nblk // 4, (nblk // 2) % 2, nblk % 2

    row0, nbig, mid, small = geometry(v)
    mid_r0 = nbig * BIG
    small_r0 = mid_r0 + mid * MID

    def x_copy(vrow, r0, n, slot):
        return pltpu.make_async_copy(xs_ref.at[pl.ds(pl.multiple_of(vrow + r0, RB), n), :],
                                     xin_ref.at[slot, pl.ds(0, n), :], isem.at[slot])

    def y_copy(r0, n):
        return pltpu.make_async_copy(acc_ref.at[pl.ds(pl.multiple_of(r0, RB), n), :],
                                     ys_ref.at[pl.ds(pl.multiple_of(row0 + r0, RB), n), :], osem)

    def start_unit(vv, u):
        vrow, nb, md, sm = geometry(vv)

        @pl.when(u < nb)
        def _():
            x_copy(vrow, u * BIG, BIG, u % 2).start()

        @pl.when((u == nb) & (md == 1))
        def _():
            x_copy(vrow, nb * BIG, MID, u % 2).start()

        @pl.when((u == nb + md) & (sm == 1))
        def _():
            x_copy(vrow, nb * BIG + md * MID, RB, u % 2).start()

    @pl.when((v == 0) & first)
    def _():
        acc_ref[...] = jnp.zeros_like(acc_ref)
        _tail_fill(acc_ref, ys_ref, tail_ref, osem)
        start_unit(0, 0)

    def block(u, r0, n):
        rows = pl.ds(pl.multiple_of(r0, RB), n)
        slot = u % 2

        @pl.when(first)
        def _():
            x_copy(row0, r0, n, slot).wait()
            start_unit(v, u + 1)
            lo_half, hi_half = _unpack_halves(xin_ref[slot, pl.ds(0, n), :])
            xb_ref[rows, :D // 2] = lo_half
            xb_ref[rows, D // 2:] = hi_half

        x = xb_ref[rows, :]
        glu = jnp.minimum(_dot(x, w1g_ref[0].astype(BF16)) + b1g_ref[0], SWIGLU_LIMIT)
        lin = jnp.clip(_dot(x, w1l_ref[0].astype(BF16)) + b1l_ref[0], -SWIGLU_LIMIT, SWIGLU_LIMIT)
        act = glu * _sigmoid(SWIGLU_ALPHA * glu) * (lin + 1.0)
        part = _dot(act.astype(BF16), w2_ref[0].astype(BF16))

        acc_ref[rows, :] = part + jnp.where(first, b2_ref[0], acc_ref[rows, :])

        @pl.when(last)
        def _():
            y_copy(r0, n).start()

    @pl.when(active)
    def _():
        def big(i, c):
            block(i, i * BIG, BIG)
            return c
        lax.fori_loop(0, nbig, big, 0)

        @pl.when(mid == 1)
        def _():
            block(nbig, mid_r0, MID)

        @pl.when(small == 1)
        def _():
            block(nbig + mid, small_r0, RB)

    @pl.when(active & last)
    def _():
        def finish(i, c):
            y_copy(i * BIG, BIG).wait()
            return c
        lax.fori_loop(0, nbig, finish, 0)

        @pl.when(mid == 1)
        def _():
            y_copy(mid_r0, MID).wait()

        @pl.when(small == 1)
        def _():
            y_copy(small_r0, RB).wait()

        @pl.when(v + 1 < nvis)
        def _():
            start_unit(v + 1, 0)


def _experts(xs, rt, w1, b1, w2, b2):
    b1 = b1.reshape(N_EXP, 1, 2 * D_FF)
    b2 = b2.reshape(N_EXP, 1, D)

    def jj(v, j, ve, vcnt, vrow, nvis, tail):
        return jnp.where(v < nvis[0], j, MOE_NF - 1)

    return pl.pallas_call(
        _experts_kernel,
        grid_spec=pltpu.PrefetchScalarGridSpec(
            num_scalar_prefetch=5,
            grid=(MOE_NV, MOE_NF),
            in_specs=[
                pl.BlockSpec(memory_space=pl.ANY),
                pl.BlockSpec((1, D, MOE_TF), lambda v, j, ve, *s: (ve[v], 0, jj(v, j, ve, *s))),
                pl.BlockSpec((1, D, MOE_TF), lambda v, j, ve, *s: (ve[v], 0, MOE_NF + jj(v, j, ve, *s))),
                pl.BlockSpec((1, MOE_TF, D), lambda v, j, ve, *s: (ve[v], jj(v, j, ve, *s), 0)),
                pl.BlockSpec((1, 1, MOE_TF), lambda v, j, ve, *s: (ve[v], 0, jj(v, j, ve, *s))),
                pl.BlockSpec((1, 1, MOE_TF), lambda v, j, ve, *s: (ve[v], 0, MOE_NF + jj(v, j, ve, *s))),
                pl.BlockSpec((1, 1, D), lambda v, j, ve, *s: (ve[v], 0, 0)),
            ],
            out_specs=pl.BlockSpec(memory_space=pl.ANY),
            scratch_shapes=[
                pltpu.VMEM((2, 4 * MOE_RB, D // 2), jnp.int32),
                pltpu.VMEM((MOE_RMAX, D), BF16),
                pltpu.VMEM((MOE_RMAX, D), F32),
                pltpu.SemaphoreType.DMA((2,)),
                pltpu.SemaphoreType.DMA,
            ],
        ),
        out_shape=jax.ShapeDtypeStruct((MOE_ROWS, D), F32),
        compiler_params=pltpu.CompilerParams(
            dimension_semantics=("arbitrary", "arbitrary"), vmem_limit_bytes=VMEM_LIMIT),
        name="experts",
    )(rt["vis_e"], rt["vis_cnt"], rt["vis_row"], rt["n_vis"], rt["tail_blk"], xs, w1, w1, w2, b1, b1, b2)


def _combine_kernel(dest_ref, ys_ref, h_ref, gate_ref, gf_ref, o_ref, buf_ref, sem):
    i = pl.program_id(0)
    tm = h_ref.shape[0]

    def issue(t, c):
        for k in range(TOP_K):
            _row_copy(ys_ref, dest_ref[k * T + i * tm + t], buf_ref.at[k], t, sem).start()
        return c
    lax.fori_loop(0, tm, issue, 0)

    for k in range(TOP_K):
        pltpu.make_async_copy(ys_ref.at[pl.ds(0, tm), :], buf_ref.at[k], sem).wait()

    h = h_ref[...]
    for k in range(TOP_K):
        h = h + gate_ref[:, k:k + 1] * buf_ref[k]
    o_ref[...] = _rms(h, gf_ref[...])


def _combine(ys, h, gates_t, gf, rt):
    tm = 256
    return pl.pallas_call(
        _combine_kernel,
        grid_spec=pltpu.PrefetchScalarGridSpec(
            num_scalar_prefetch=1,
            grid=(T // tm,),
            in_specs=[
                pl.BlockSpec(memory_space=pl.ANY),
                pl.BlockSpec((tm, D), lambda i, *_: (i, 0)),
                pl.BlockSpec((tm, TOP_K), lambda i, *_: (i, 0)),
                pl.BlockSpec((1, D), lambda i, *_: (0, 0)),
            ],
            out_specs=pl.BlockSpec((tm, D), lambda i, *_: (i, 0)),
            scratch_shapes=[pltpu.VMEM((TOP_K, tm, D), F32), pltpu.SemaphoreType.DMA],
        ),
        out_shape=jax.ShapeDtypeStruct((T, D), F32),
        compiler_params=pltpu.CompilerParams(
            dimension_semantics=("arbitrary",), vmem_limit_bytes=VMEM_LIMIT),
        name="combine",
    )(rt["dest"], ys, h, gates_t, gf)


def kernel(x, norm1_g, w_in, gla_gate_w2, gla_gate_b, gla_norm_g, s5_a_re, s5_a_im, s5_b_re, s5_b_im, s5_c_re, s5_c_im, s5_d, s5_log_step, s5_glu_w, s5_glu_b, s5_norm_g, w_out, norm2_g, router_w, router_b, expert_w1, expert_b1, expert_w2, expert_b2, final_norm_g):
    assert x.shape == (1, T, D) and w_in.shape[0] == 1
    xt = x.reshape(T, D)
    nqkvr = 2 * GLA_H * GLA_DK + 2 * GLA_H * GLA_DV
    wi = w_in[0]
    wq = wi[:, :nqkvr].astype(BF16)
    wg = jnp.pad(wi[:, nqkvr:nqkvr + GATE_RANK], ((0, 0), (0, 128 - GATE_RANK))).astype(BF16)
    wu = wi[:, nqkvr + GATE_RANK:].astype(BF16)
    w2g = jnp.pad(gla_gate_w2[0], ((0, 128 - GATE_RANK), (0, 0)))
    qkvr, lg, u = _proj(xt, norm1_g, wq, wg, wu, w2g, gla_gate_b)
    gla = _gla(qkvr, lg, gla_norm_g)
    tables = _s5_tables(s5_a_re[0], s5_a_im[0], s5_b_re[0], s5_b_im[0], s5_c_re[0], s5_c_im[0],
                        s5_d[0], s5_log_step[0])
    y = _s5(u, tables)
    h, hn, top_idx, gates, rank, counts = _post(
        y, gla, xt, s5_glu_w[0].astype(BF16), s5_glu_b, s5_norm_g, w_out[0].astype(BF16), norm2_g,
        jnp.pad(router_w[0], ((0, 0), (0, 128 - N_EXP))), router_b.reshape(N_EXP, 1))
    rt = _route(top_idx, rank, counts)
    xs = _scatter(hn, rt)
    ys = _experts(xs, rt, expert_w1[0], expert_b1[0], expert_w2[0], expert_b2[0])
    out = _combine(ys, h, gates.T, final_norm_g.reshape(1, D), rt)
    return out.reshape(1, T, D)
```

```python
import functools
import math

import jax
import jax.numpy as jnp
from jax import lax
from jax.experimental import pallas as pl
from jax.experimental.pallas import tpu as pltpu

F32 = jnp.float32
BF16 = jnp.bfloat16
HIGHEST = lax.Precision.HIGHEST

T = 8192
D = 2048
GLA_H = 4
GLA_DK = 128
GLA_DV = 256
GLA_CHUNK = 64
GLA_SUB = 16
GATE_RANK = 16
GATE_TAU = 16.0
S5_G = 64
S5_N = 16
S5_P = 64
S5_L = 16
S5_GT = 8
N_EXP = 32
TOP_K = 4
D_FF = 2048
SWIGLU_ALPHA = 1.702
SWIGLU_LIMIT = 7.0
RMS_EPS = 1e-6

MOE_RB = 128
MOE_RMAX = 1536
MOE_TF = 512
MOE_NF = D_FF // MOE_TF
MOE_NV = -(-T * TOP_K // MOE_RMAX) + N_EXP
MOE_ROWS = T * TOP_K + N_EXP * MOE_RB
VMEM_LIMIT = 56 * 1024 * 1024

NN = (((1,), (0,)), ((), ()))
NT = (((1,), (1,)), ((), ()))
TN = (((0,), (0,)), ((), ()))


def _dot(a, b, dims=NN, precision=None):
    return lax.dot_general(a, b, dims, preferred_element_type=F32, precision=precision)


def _rms(x, g):
    return x * lax.rsqrt(jnp.mean(x * x, axis=-1, keepdims=True) + RMS_EPS) * g


def _sigmoid(x):
    return 1.0 / (1.0 + jnp.exp(-x))


def _pack_halves(lo_half, hi_half):
    a = lax.bitcast_convert_type(lo_half.astype(F32), jnp.int32)
    b = lax.bitcast_convert_type(hi_half.astype(F32), jnp.int32)
    return a | lax.shift_right_logical(b, 16)


def _unpack_halves(p):
    a = lax.bitcast_convert_type(p & jnp.int32(-65536), F32).astype(BF16)
    b = lax.bitcast_convert_type(lax.shift_left(p, 16), F32).astype(BF16)
    return a, b


def _const_spec(shape):
    nd = len(shape)
    return pl.BlockSpec(shape, lambda *_: (0,) * nd, pipeline_mode=pl.Buffered(1))


def _proj_kernel(x_ref, g1_ref, wq_ref, wg_ref, wu_ref, w2_ref, gb_ref, qkvr_ref, lg_ref, u_ref):
    hb = _rms(x_ref[...], g1_ref[...]).astype(BF16)
    p = _dot(hb, wq_ref[...])
    nq = GLA_H * GLA_DK
    qkvr_ref[:, :nq] = (p[:, :nq] * (GLA_DK ** -0.5)).astype(BF16)
    qkvr_ref[:, nq:] = p[:, nq:].astype(BF16)
    glr = _dot(hb, wg_ref[...])
    z = _dot(glr, w2_ref[...], precision=HIGHEST) + gb_ref[...]
    lg_ref[...] = (jnp.minimum(z, 0.0) - jnp.log(1.0 + jnp.exp(-jnp.abs(z)))) * (1.0 / GATE_TAU)
    u_ref[...] = _dot(hb, wu_ref[...])


def _proj(x, g1, wq, wg, wu, w2, gb):
    tm = 256
    nqkvr = wq.shape[1]
    return pl.pallas_call(
        _proj_kernel,
        grid=(T // tm,),
        in_specs=[
            pl.BlockSpec((tm, D), lambda i: (i, 0)),
            _const_spec((1, D)),
            _const_spec(wq.shape), _const_spec(wg.shape), _const_spec(wu.shape),
            _const_spec(w2.shape), _const_spec(gb.shape),
        ],
        out_specs=[
            pl.BlockSpec((tm, nqkvr), lambda i: (i, 0)),
            pl.BlockSpec((tm, GLA_H * GLA_DK), lambda i: (i, 0)),
            pl.BlockSpec((tm, S5_G * S5_N), lambda i: (i, 0)),
        ],
        out_shape=[
            jax.ShapeDtypeStruct((T, nqkvr), BF16),
            jax.ShapeDtypeStruct((T, GLA_H * GLA_DK), F32),
            jax.ShapeDtypeStruct((T, S5_G * S5_N), F32),
        ],
        compiler_params=pltpu.CompilerParams(
            dimension_semantics=("arbitrary",), vmem_limit_bytes=VMEM_LIMIT),
        name="proj",
    )(x, g1, wq, wg, wu, w2, gb)


def _gla_kernel(q_ref, k_ref, v_ref, r_ref, lg_ref, gn_ref, tril_ref, esel_ref, o_ref, state_ref):
    C, S = GLA_CHUNK, GLA_SUB
    nsub = C // S

    @pl.when(pl.program_id(0) == 0)
    def _():
        state_ref[...] = jnp.zeros_like(state_ref)

    row = lax.broadcasted_iota(jnp.int32, (C, C), 0)
    col = lax.broadcasted_iota(jnp.int32, (C, C), 1)
    below = (col // S) < (row // S)
    diag = ((col // S) == (row // S)) & (col <= row)
    tril = tril_ref[...]
    esel = esel_ref[...]
    gn = gn_ref[...]

    def head(h, rows, b):
        kcols = slice(h * GLA_DK, (h + 1) * GLA_DK)
        vcols = slice(h * GLA_DV, (h + 1) * GLA_DV)
        q = q_ref[rows, kcols].astype(F32)
        k = k_ref[rows, kcols].astype(F32)
        v = v_ref[rows, vcols]
        bl = b[C - 1:C, :]
        st = state_ref[h]
        o = _dot((q * jnp.exp2(b)).astype(BF16), st.astype(BF16), NT)
        s_rows = [jnp.zeros((S, C), F32)]
        for j in range(1, nsub):
            ref = b[j * S - 1:j * S, :]
            qj = (q[j * S:(j + 1) * S] * jnp.exp2(b[j * S:(j + 1) * S] - ref)).astype(BF16)
            kj = (k * jnp.exp2(jnp.minimum(ref - b, 0.0))).astype(BF16)
            s_rows.append(_dot(qj, kj, NT))
        s_off = jnp.concatenate(s_rows, axis=0)
        q3 = q.reshape(nsub, S, GLA_DK)
        k3 = k.reshape(nsub, S, GLA_DK)
        b3 = b.reshape(nsub, S, GLA_DK)
        xs = []
        for s in range(S):
            lo = (s // 8) * 8
            dec = jnp.exp2(jnp.minimum(b3[:, lo:] - b3[:, s:s + 1, :], 0.0))
            x = q3[:, lo:] * k3[:, s:s + 1, :] * dec
            if lo:
                x = jnp.concatenate([jnp.zeros((nsub, lo, GLA_DK), F32), x], axis=1)
            xs.append(x.reshape(C, GLA_DK).astype(BF16))
        dsc = _dot(jnp.concatenate(xs, axis=1), esel)
        a = jnp.where(below, s_off, jnp.where(diag, dsc, 0.0))
        o = o + _dot(a.astype(BF16), v)
        kout = (k * jnp.exp2(bl - b)).astype(BF16)
        state_ref[h] = st * jnp.exp2(bl) + _dot(v, kout, TN)
        y = _rms(o, gn)
        r = r_ref[rows, vcols].astype(F32)
        o_ref[rows, vcols] = (y * (r * _sigmoid(r))).astype(BF16)

    def chunk(c, carry):
        rows = pl.ds(pl.multiple_of(c * C, C), C)
        b = _dot(tril, lg_ref[rows, :], precision=HIGHEST) * math.log2(math.e)
        for h in range(GLA_H):
            head(h, rows, b[:, h * GLA_DK:(h + 1) * GLA_DK])
        return carry

    lax.fori_loop(0, q_ref.shape[0] // C, chunk, 0, unroll=4)


def _gla(qkvr, lg, gn):
    tb = 512
    C, S = GLA_CHUNK, GLA_SUB
    nk, nv = GLA_H * GLA_DK, GLA_H * GLA_DV
    tril = jnp.tril(jnp.ones((C, C), F32))
    esel = (jnp.arange(S * GLA_DK)[:, None] // GLA_DK == jnp.arange(C)[None, :] % S).astype(BF16)
    return pl.pallas_call(
        _gla_kernel,
        grid=(T // tb,),
        in_specs=[
            pl.BlockSpec((tb, nk), lambda i: (i, 0)),
            pl.BlockSpec((tb, nk), lambda i: (i, 1)),
            pl.BlockSpec((tb, nv), lambda i: (i, 1)),
            pl.BlockSpec((tb, nv), lambda i: (i, 2)),
            pl.BlockSpec((tb, nk), lambda i: (i, 0)),
            _const_spec((1, GLA_DV)), _const_spec((C, C)), _const_spec((S * GLA_DK, C)),
        ],
        out_specs=pl.BlockSpec((tb, nv), lambda i: (i, 0)),
        out_shape=jax.ShapeDtypeStruct((T, nv), BF16),
        scratch_shapes=[pltpu.VMEM((GLA_H, GLA_DV, GLA_DK), F32)],
        compiler_params=pltpu.CompilerParams(
            dimension_semantics=("arbitrary",), vmem_limit_bytes=VMEM_LIMIT),
        name="gla",
    )(qkvr, qkvr, qkvr, qkvr, lg, gn, tril, esel)


def _s5_tables(a_re, a_im, b_re, b_im, c_re, c_im, d_skip, log_step):
    L, G, P, N = S5_L, S5_G, S5_P, S5_N
    delta = jnp.exp(log_step)[:, None]
    ar, ai = a_re * delta, a_im * delta

    def apow(tau):
        tau = jnp.asarray(tau, F32)[None, :, None]
        mag = jnp.exp(ar[:, None, :] * tau)
        ph = ai[:, None, :] * tau
        return mag * jnp.cos(ph), mag * jnp.sin(ph)

    p_re, p_im = apow(jnp.arange(L + 1))
    ab_re, ab_im = p_re[:, 1], p_im[:, 1]
    den = a_re * a_re + a_im * a_im
    cf_re = ((ab_re - 1.0) * a_re + ab_im * a_im) / den
    cf_im = (ab_im * a_re - (ab_re - 1.0) * a_im) / den
    bb_re = cf_re[:, :, None] * b_re - cf_im[:, :, None] * b_im
    bb_im = cf_re[:, :, None] * b_im + cf_im[:, :, None] * b_re
    e_re = p_re[:, :L, :, None] * bb_re[:, None] - p_im[:, :L, :, None] * bb_im[:, None]
    e_im = p_re[:, :L, :, None] * bb_im[:, None] + p_im[:, :L, :, None] * bb_re[:, None]
    kk = (jnp.einsum('gnp,gtpm->gtnm', c_re, e_re, precision=HIGHEST)
          - jnp.einsum('gnp,gtpm->gtnm', c_im, e_im, precision=HIGHEST))
    kk = kk.at[:, 0].add(jnp.eye(N, dtype=F32)[None] * d_skip[:, :, None])
    r_re, r_im = apow(L - 1 - jnp.arange(L))
    bs_re = (r_re[:, :, None, :] * bb_re.transpose(0, 2, 1)[:, None] - r_im[:, :, None, :] * bb_im.transpose(0, 2, 1)[:, None])
    bs_im = (r_re[:, :, None, :] * bb_im.transpose(0, 2, 1)[:, None] + r_im[:, :, None, :] * bb_re.transpose(0, 2, 1)[:, None])
    ct_re = c_re.transpose(0, 2, 1)[:, :, None, :]
    ct_im = c_im.transpose(0, 2, 1)[:, :, None, :]
    q_re = p_re[:, 1:].transpose(0, 2, 1)[:, :, :, None]
    q_im = p_im[:, 1:].transpose(0, 2, 1)[:, :, :, None]
    cre = ct_re * q_re - ct_im * q_im
    cim = -(ct_re * q_im + ct_im * q_re)
    GT = S5_GT
    nt = G // GT

    def tile_groups(w):
        return w.reshape((nt, GT) + w.shape[1:])

    kc = tile_groups(kk).transpose(0, 1, 4, 2, 3).reshape(nt, GT * N, L * N)

    def in_table(w):
        return tile_groups(w).transpose(0, 2, 1, 3, 4).reshape(nt, L * GT * N, P)

    def out_table(w):
        return w.reshape(nt, GT * P, L * N)

    nstep = (T // L - 1).bit_length()
    sc_re, sc_im = apow(jnp.asarray([L * 2 ** i for i in range(nstep)]))

    def lanes(w):
        return tile_groups(w).transpose(0, 2, 1, 3).reshape(nt, w.shape[1], GT * P)

    r1 = jnp.arange(L * N)[:, None]
    c1 = jnp.arange(L * GT * N)[None, :]
    wide_tn = ((r1 // N == c1 // (GT * N)) & (r1 % N == c1 % N)).astype(BF16)
    wide_p = (jnp.arange(P)[:, None] == jnp.arange(GT * P)[None, :] % P).astype(BF16)
    return dict(
        kc=kc.astype(BF16),
        bs_re=in_table(bs_re).astype(BF16), bs_im=in_table(bs_im).astype(BF16),
        cre=out_table(cre).astype(BF16), cim=out_table(cim).astype(BF16),
        sc_re=lanes(sc_re), sc_im=lanes(sc_im), wide_tn=wide_tn, wide_p=wide_p,
    )


def _s5_kernel(u_ref, kc_ref, bsr_ref, bsi_ref, cre_ref, cim_ref, scr_ref, sci_ref, wtn_ref, wp_ref,
               y_ref, m_ref):
    L, N, P = S5_L, S5_N, S5_P
    W = S5_GT * N
    nch = u_ref.shape[0] // L

    @pl.when(pl.program_id(0) == 0)
    def _():
        m_ref[...] = jnp.zeros_like(m_ref)

    def widen(compact, wide, row_group, col_group):
        full = _dot(compact, wide)
        r = lax.broadcasted_iota(jnp.int32, full.shape, 0)
        c = lax.broadcasted_iota(jnp.int32, full.shape, 1)
        return jnp.where(row_group(r) == col_group(c), full, 0.0).astype(BF16)

    wtn, wp = wtn_ref[...], wp_ref[...]
    bd = widen(kc_ref[0], wtn, lambda r: r // N, lambda c: (c % W) // N)
    bsr = widen(bsr_ref[0], wp, lambda r: (r % W) // N, lambda c: c // P)
    bsi = widen(bsi_ref[0], wp, lambda r: (r % W) // N, lambda c: c // P)
    cre = widen(cre_ref[0], wtn, lambda r: r // P, lambda c: (c % W) // N)
    cim = widen(cim_ref[0], wtn, lambda r: r // P, lambda c: (c % W) // N)
    for s in range(L):
        m_ref[s * W:(s + 1) * W, s * W:] = bd[:, :(L - s) * W]
    ucat = jnp.concatenate([u_ref[pl.ds(s, nch, stride=L), :].astype(BF16) for s in range(L)], axis=1)
    hr = _dot(ucat, bsr)
    hi = _dot(ucat, bsi)
    row = lax.broadcasted_iota(jnp.int32, hr.shape, 0)

    def shift(x, d):
        if d % 8 == 0:
            return jnp.concatenate([jnp.zeros((d, x.shape[1]), F32), x[:nch - d]], axis=0)
        return jnp.where(row >= d, pltpu.roll(x, d, 0), 0.0)

    for i in range(scr_ref.shape[1]):
        mr, mi = scr_ref[0, i:i + 1, :], sci_ref[0, i:i + 1, :]
        pr, pi = shift(hr, 2 ** i), shift(hi, 2 ** i)
        hr, hi = hr + mr * pr - mi * pi, hi + mr * pi + mi * pr
    gr, gi = shift(hr, 1), shift(hi, 1)
    grb, gib = gr.astype(BF16), gi.astype(BF16)
    for tp in range(L // 2):
        cols = slice(2 * tp * W, (2 * tp + 2) * W)
        depth = (2 * tp + 2) * W
        y = _dot(ucat[:, :depth], m_ref[:depth, cols]) + _dot(grb, cre[:, cols]) + _dot(gib, cim[:, cols])
        for t in (2 * tp, 2 * tp + 1):
            y_ref[pl.ds(t, nch, stride=L), :] = y[:, (t - 2 * tp) * W:(t - 2 * tp + 1) * W]


def _s5(u, tb):
    W = S5_GT * S5_N
    nt = S5_G // S5_GT

    def tile_spec(a):
        return pl.BlockSpec((1,) + a.shape[1:], lambda g: (g,) + (0,) * (a.ndim - 1))

    ws = [tb[n] for n in ["kc", "bs_re", "bs_im", "cre", "cim", "sc_re", "sc_im"]]
    consts = [tb["wide_tn"], tb["wide_p"]]
    return pl.pallas_call(
        _s5_kernel,
        grid=(nt,),
        in_specs=([pl.BlockSpec((T, W), lambda g: (0, g))] + [tile_spec(w) for w in ws]
                  + [_const_spec(c.shape) for c in consts]),
        out_specs=pl.BlockSpec((T, W), lambda g: (0, g)),
        out_shape=jax.ShapeDtypeStruct((T, S5_G * S5_N), F32),
        scratch_shapes=[pltpu.VMEM((S5_L * W, S5_L * W), BF16)],
        compiler_params=pltpu.CompilerParams(
            dimension_semantics=("arbitrary",), vmem_limit_bytes=VMEM_LIMIT),
        name="s5",
    )(u, *ws, *consts)


def _post_kernel(y_ref, gla_ref, x_ref, gw_ref, gb_ref, sg_ref, wo_ref, n2_ref, rw_ref, rb_ref, before_ref,
                 h_ref, hn_ref, idx_ref, gate_ref, rank_ref, cnt_ref, run_ref):
    @pl.when(pl.program_id(0) == 0)
    def _():
        run_ref[...] = jnp.zeros_like(run_ref)

    y = y_ref[...]
    z = 0.5 * y * (1.0 + jnp.tanh(math.sqrt(2.0 / math.pi) * (y + 0.044715 * (y * y * y))))
    z = z * _sigmoid(_dot(z.astype(BF16), gw_ref[...]) + gb_ref[...])
    s5o = _rms(z, sg_ref[...]).astype(BF16)
    half = GLA_H * GLA_DV
    h = x_ref[...] + _dot(gla_ref[...], wo_ref[:half, :]) + _dot(s5o, wo_ref[half:, :])
    h_ref[...] = h
    hn = _rms(h, n2_ref[...])
    hi = hn.astype(BF16)
    hn_ref[...] = _pack_halves(hi[:, :D // 2], hi[:, D // 2:])
    lo = (hn - hi.astype(F32)).astype(BF16)
    rw = rw_ref[...]
    rw_hi = rw.astype(BF16)
    rw_lo = (rw - rw_hi.astype(F32)).astype(BF16)
    lg = _dot(hi, rw_hi) + (_dot(lo, rw_hi) + _dot(hi, rw_lo))
    lt = lg.T[:N_EXP] + rb_ref[...]
    eid = lax.broadcasted_iota(jnp.int32, lt.shape, 0).astype(F32)
    vals, idxs = [], []
    for _ in range(TOP_K):
        m = jnp.max(lt, axis=0, keepdims=True)
        sel = jnp.min(jnp.where(lt == m, eid, float(N_EXP)), axis=0, keepdims=True)
        vals.append(m)
        idxs.append(sel)
        lt = jnp.where(eid == sel, -jnp.inf, lt)
    ex = [jnp.exp(vv - vals[0]) for vv in vals]
    inv = 1.0 / (ex[0] + ex[1] + ex[2] + ex[3])
    idx_ref[...] = jnp.concatenate(idxs, axis=0).astype(jnp.int32)
    gate_ref[...] = jnp.concatenate([e * inv for e in ex], axis=0)
    run = run_ref[:, :1]
    ranks = []
    for sel in idxs:
        onehot = jnp.where(eid == sel, 1.0, 0.0)
        earlier = _dot(onehot.astype(BF16), before_ref[...])
        ranks.append(jnp.sum(onehot * (run + earlier), axis=0, keepdims=True))
        run = run + jnp.sum(onehot, axis=1, keepdims=True)
    rank_ref[...] = jnp.concatenate(ranks, axis=0).astype(jnp.int32)
    run_ref[...] = jnp.broadcast_to(run, run_ref.shape)
    cnt_ref[...] = jnp.broadcast_to(run, cnt_ref.shape)


def _post(y, gla, x, gw, gb, sg, wo, n2, rwt, rb):
    tm = 256
    width = S5_G * S5_N
    before = (jnp.arange(tm)[:, None] < jnp.arange(tm)[None, :]).astype(BF16)
    return pl.pallas_call(
        _post_kernel,
        grid=(T // tm,),
        in_specs=[
            pl.BlockSpec((tm, width), lambda i: (i, 0)),
            pl.BlockSpec((tm, GLA_H * GLA_DV), lambda i: (i, 0)),
            pl.BlockSpec((tm, D), lambda i: (i, 0)),
            _const_spec(gw.shape), _const_spec(gb.shape), _const_spec(sg.shape), _const_spec(wo.shape),
            _const_spec(n2.shape), _const_spec(rwt.shape), _const_spec(rb.shape), _const_spec(before.shape),
        ],
        out_specs=[
            pl.BlockSpec((tm, D), lambda i: (i, 0)),
            pl.BlockSpec((tm, D // 2), lambda i: (i, 0)),
            pl.BlockSpec((TOP_K, tm), lambda i: (0, i)),
            pl.BlockSpec((TOP_K, tm), lambda i: (0, i)),
            pl.BlockSpec((TOP_K, tm), lambda i: (0, i)),
            pl.BlockSpec((N_EXP, 128), lambda i: (0, 0)),
        ],
        out_shape=[
            jax.ShapeDtypeStruct((T, D), F32),
            jax.ShapeDtypeStruct((T, D // 2), jnp.int32),
            jax.ShapeDtypeStruct((TOP_K, T), jnp.int32),
            jax.ShapeDtypeStruct((TOP_K, T), F32),
            jax.ShapeDtypeStruct((TOP_K, T), jnp.int32),
            jax.ShapeDtypeStruct((N_EXP, 128), F32),
        ],
        scratch_shapes=[pltpu.VMEM((N_EXP, 128), F32)],
        compiler_params=pltpu.CompilerParams(
            dimension_semantics=("arbitrary",), vmem_limit_bytes=VMEM_LIMIT),
        name="post",
    )(y, gla, x, gw, gb, sg, wo, n2, rwt, rb, before)


def _route(top_idx, rank, counts):
    e_flat = top_idx.reshape(-1)
    rank = rank.reshape(-1)
    counts = counts[:, 0].astype(jnp.int32)
    padded = (counts + MOE_RB - 1) // MOE_RB * MOE_RB
    pad_end = jnp.cumsum(padded)
    pad_start = pad_end - padded
    dest = _dest(top_idx, rank.reshape(top_idx.shape), pad_start.astype(jnp.int32)).reshape(-1)
    nvis_e = (counts + MOE_RMAX - 1) // MOE_RMAX
    vis_end = jnp.cumsum(nvis_e)
    n_vis = vis_end[-1]
    v = jnp.arange(MOE_NV, dtype=jnp.int32)
    vc = jnp.minimum(v, n_vis - 1)
    ve = jnp.minimum(jnp.searchsorted(vis_end, vc, side='right'), N_EXP - 1).astype(jnp.int32)
    local = vc - (vis_end[ve] - nvis_e[ve])
    vcnt = jnp.where(v < n_vis, jnp.minimum(MOE_RMAX, counts[ve] - local * MOE_RMAX), 0)
    vrow = pad_start[ve] + local * MOE_RMAX
    return dict(dest=dest, fill_from=(pad_start + counts).astype(jnp.int32), fill_to=pad_end.astype(jnp.int32),
                tail_blk=(pad_end[-1:] // MOE_RB).astype(jnp.int32),
                vis_e=ve, vis_cnt=vcnt.astype(jnp.int32), vis_row=vrow.astype(jnp.int32),
                n_vis=n_vis.reshape(1).astype(jnp.int32))


def _dest_kernel(start_ref, idx_ref, rank_ref, dest_ref):
    e = idx_ref[...]
    d = rank_ref[...]
    for x in range(N_EXP):
        d = d + jnp.where(e == x, start_ref[x], 0)
    dest_ref[...] = d


def _dest(top_idx, rank, pad_start):
    full = pl.BlockSpec(top_idx.shape, lambda i, *_: (0, 0))
    return pl.pallas_call(
        _dest_kernel,
        grid_spec=pltpu.PrefetchScalarGridSpec(num_scalar_prefetch=1, grid=(1,), in_specs=[full, full], out_specs=full),
        out_shape=jax.ShapeDtypeStruct(top_idx.shape, jnp.int32),
        name="dest",
    )(pad_start, top_idx, rank)


def _row_copy(src_ref, srow, dst_ref, drow, sem):
    return pltpu.make_async_copy(src_ref.at[pl.ds(srow, 1), :], dst_ref.at[pl.ds(drow, 1), :], sem)


def _tail_fill(src_ref, dst_ref, tail_ref, sem):
    def cp(b):
        return pltpu.make_async_copy(src_ref.at[pl.ds(0, MOE_RB), :],
                                     dst_ref.at[pl.ds(pl.multiple_of(b * MOE_RB, MOE_RB), MOE_RB), :], sem)

    def start(b, c):
        cp(b).start()
        return c
    lax.fori_loop(tail_ref[0], MOE_ROWS // MOE_RB, start, 0)

    def finish(b, c):
        cp(b).wait()
        return c
    lax.fori_loop(tail_ref[0], MOE_ROWS // MOE_RB, finish, 0)


def _scatter_kernel(dest_ref, from_ref, to_ref, tail_ref, hn_ref, xs_ref, zero_ref, sem, zsem):
    i = pl.program_id(0)
    tm = hn_ref.shape[0]

    @pl.when(i == 0)
    def _():
        zero_ref[...] = jnp.zeros_like(zero_ref)
        _tail_fill(zero_ref, xs_ref, tail_ref, zsem)

        def per_expert(e, c):
            def fill(r, c2):
                _row_copy(zero_ref, 0, xs_ref, r, zsem).start()
                return c2
            lax.fori_loop(from_ref[e], to_ref[e], fill, 0)

            def drain(r, c2):
                _row_copy(zero_ref, 0, xs_ref, r, zsem).wait()
                return c2
            lax.fori_loop(from_ref[e], to_ref[e], drain, 0)
            return c
        lax.fori_loop(0, N_EXP, per_expert, 0)

    def issue(t, c):
        for k in range(TOP_K):
            _row_copy(hn_ref, t, xs_ref, dest_ref[k * T + i * tm + t], sem).start(priority=k % 2)
        return c
    lax.fori_loop(0, tm, issue, 0)

    for k in range(TOP_K):
        pltpu.make_async_copy(hn_ref, xs_ref.at[pl.ds(0, tm), :], sem).wait()


def _scatter(hn, rt):
    tm = 1024
    return pl.pallas_call(
        _scatter_kernel,
        grid_spec=pltpu.PrefetchScalarGridSpec(
            num_scalar_prefetch=4,
            grid=(T // tm,),
            in_specs=[pl.BlockSpec((tm, D // 2), lambda i, *_: (i, 0))],
            out_specs=pl.BlockSpec(memory_space=pl.ANY),
            scratch_shapes=[pltpu.VMEM((MOE_RB, D // 2), jnp.int32), pltpu.SemaphoreType.DMA,
                            pltpu.SemaphoreType.DMA],
        ),
        out_shape=jax.ShapeDtypeStruct((MOE_ROWS, D // 2), jnp.int32),
        compiler_params=pltpu.CompilerParams(
            dimension_semantics=("arbitrary",), vmem_limit_bytes=VMEM_LIMIT),
        name="scatter",
    )(rt["dest"], rt["fill_from"], rt["fill_to"], rt["tail_blk"], hn)


def _experts_kernel(ve_ref, vcnt_ref, vrow_ref, nvis_ref, tail_ref, xs_ref, w1g_ref, w1l_ref, w2_ref, b1g_ref,
                    b1l_ref, b2_ref, ys_ref, xin_ref, xb_ref, acc_ref, isem, osem):
    v = pl.program_id(0)
    j = pl.program_id(1)
    RB = MOE_RB
    BIG, MID = 4 * RB, 2 * RB
    nvis = nvis_ref[0]
    active = v < nvis
    first = j == 0
    last = j == MOE_NF - 1

    def geometry(vv):
        nblk = (vcnt_ref[vv] + RB - 1) // RB
        return vrow_ref[vv], nblk // 4, (nblk // 2) % 2, nblk % 2

    row0, nbig, mid, small = geometry(v)
    mid_r0 = nbig * BIG
    small_r0 = mid_r0 + mid * MID

    def x_copy(vrow, r0, n, slot):
        return pltpu.make_async_copy(xs_ref.at[pl.ds(pl.multiple_of(vrow + r0, RB), n), :],
                                     xin_ref.at[slot, pl.ds(0, n), :], isem.at[slot])

    def y_copy(r0, n):
        return pltpu.make_async_copy(acc_ref.at[pl.ds(pl.multiple_of(r0, RB), n), :],
                                     ys_ref.at[pl.ds(pl.multiple_of(row0 + r0, RB), n), :], osem)

    def start_unit(vv, u):
        vrow, nb, md, sm = geometry(vv)

        @pl.when(u < nb)
        def _():
            x_copy(vrow, u * BIG, BIG, u % 2).start()

        @pl.when((u == nb) & (md == 1))
        def _():
            x_copy(vrow, nb * BIG, MID, u % 2).start()

        @pl.when((u == nb + md) & (sm == 1))
        def _():
            x_copy(vrow, nb * BIG + md * MID, RB, u % 2).start()

    @pl.when((v == 0) & first)
    def _():
        acc_ref[...] = jnp.zeros_like(acc_ref)
        _tail_fill(acc_ref, ys_ref, tail_ref, osem)
        start_unit(0, 0)

    def block(u, r0, n):
        rows = pl.ds(pl.multiple_of(r0, RB), n)
        slot = u % 2

        @pl.when(first)
        def _():
            x_copy(row0, r0, n, slot).wait()
            start_unit(v, u + 1)
            lo_half, hi_half = _unpack_halves(xin_ref[slot, pl.ds(0, n), :])
            xb_ref[rows, :D // 2] = lo_half
            xb_ref[rows, D // 2:] = hi_half

        x = xb_ref[rows, :]
        glu = jnp.minimum(_dot(x, w1g_ref[0].astype(BF16)) + b1g_ref[0], SWIGLU_LIMIT)
        lin = jnp.clip(_dot(x, w1l_ref[0].astype(BF16)) + b1l_ref[0], -SWIGLU_LIMIT, SWIGLU_LIMIT)
        act = glu * _sigmoid(SWIGLU_ALPHA * glu) * (lin + 1.0)
        part = _dot(act.astype(BF16), w2_ref[0].astype(BF16))

        acc_ref[rows, :] = part + jnp.where(first, b2_ref[0], acc_ref[rows, :])

        @pl.when(last)
        def _():
            y_copy(r0, n).start()

    @pl.when(active)
    def _():
        def big(i, c):
            block(i, i * BIG, BIG)
            return c
        lax.fori_loop(0, nbig, big, 0)

        @pl.when(mid == 1)
        def _():
            block(nbig, mid_r0, MID)

        @pl.when(small == 1)
        def _():
            block(nbig + mid, small_r0, RB)

    @pl.when(active & last)
    def _():
        def finish(i, c):
            y_copy(i * BIG, BIG).wait()
            return c
        lax.fori_loop(0, nbig, finish, 0)

        @pl.when(mid == 1)
        def _():
            y_copy(mid_r0, MID).wait()

        @pl.when(small == 1)
        def _():
            y_copy(small_r0, RB).wait()

        @pl.when(v + 1 < nvis)
        def _():
            start_unit(v + 1, 0)


def _experts(xs, rt, w1, b1, w2, b2):
    b1 = b1.reshape(N_EXP, 1, 2 * D_FF)
    b2 = b2.reshape(N_EXP, 1, D)

    def jj(v, j, ve, vcnt, vrow, nvis, tail):
        return jnp.where(v < nvis[0], j, MOE_NF - 1)

    return pl.pallas_call(
        _experts_kernel,
        grid_spec=pltpu.PrefetchScalarGridSpec(
            num_scalar_prefetch=5,
            grid=(MOE_NV, MOE_NF),
            in_specs=[
                pl.BlockSpec(memory_space=pl.ANY),
                pl.BlockSpec((1, D, MOE_TF), lambda v, j, ve, *s: (ve[v], 0, jj(v, j, ve, *s))),
                pl.BlockSpec((1, D, MOE_TF), lambda v, j, ve, *s: (ve[v], 0, MOE_NF + jj(v, j, ve, *s))),
                pl.BlockSpec((1, MOE_TF, D), lambda v, j, ve, *s: (ve[v], jj(v, j, ve, *s), 0)),
                pl.BlockSpec((1, 1, MOE_TF), lambda v, j, ve, *s: (ve[v], 0, jj(v, j, ve, *s))),
                pl.BlockSpec((1, 1, MOE_TF), lambda v, j, ve, *s: (ve[v], 0, MOE_NF + jj(v, j, ve, *s))),
                pl.BlockSpec((1, 1, D), lambda v, j, ve, *s: (ve[v], 0, 0)),
            ],
            out_specs=pl.BlockSpec(memory_space=pl.ANY),
            scratch_shapes=[
                pltpu.VMEM((2, 4 * MOE_RB, D // 2), jnp.int32),
                pltpu.VMEM((MOE_RMAX, D), BF16),
                pltpu.VMEM((MOE_RMAX, D), F32),
                pltpu.SemaphoreType.DMA((2,)),
                pltpu.SemaphoreType.DMA,
            ],
        ),
        out_shape=jax.ShapeDtypeStruct((MOE_ROWS, D), F32),
        compiler_params=pltpu.CompilerParams(
            dimension_semantics=("arbitrary", "arbitrary"), vmem_limit_bytes=VMEM_LIMIT),
        name="experts",
    )(rt["vis_e"], rt["vis_cnt"], rt["vis_row"], rt["n_vis"], rt["tail_blk"], xs, w1, w1, w2, b1, b1, b2)


def _combine_kernel(dest_ref, ys_ref, h_ref, gate_ref, gf_ref, o_ref, buf_ref, sem):
    i = pl.program_id(0)
    tm = h_ref.shape[0]

    def gather(tile, slot):
        def issue(t, c):
            for k in range(TOP_K):
                _row_copy(ys_ref, dest_ref[k * T + tile * tm + t], buf_ref.at[slot, k], t,
                          sem.at[slot]).start(priority=k % 2)
            return c
        lax.fori_loop(0, tm, issue, 0)

    @pl.when(i == 0)
    def _():
        gather(0, 0)

    @pl.when(i + 1 < pl.num_programs(0))
    def _():
        gather(i + 1, (i + 1) % 2)

    slot = i % 2
    for k in range(TOP_K):
        pltpu.make_async_copy(ys_ref.at[pl.ds(0, tm), :], buf_ref.at[slot, k], sem.at[slot]).wait()

    h = h_ref[...]
    for k in range(TOP_K):
        h = h + gate_ref[:, k:k + 1] * buf_ref[slot, k]
    o_ref[...] = _rms(h, gf_ref[...])


def _combine(ys, h, gates_t, gf, rt):
    tm = 256
    return pl.pallas_call(
        _combine_kernel,
        grid_spec=pltpu.PrefetchScalarGridSpec(
            num_scalar_prefetch=1,
            grid=(T // tm,),
            in_specs=[
                pl.BlockSpec(memory_space=pl.ANY),
                pl.BlockSpec((tm, D), lambda i, *_: (i, 0)),
                pl.BlockSpec((tm, TOP_K), lambda i, *_: (i, 0)),
                pl.BlockSpec((1, D), lambda i, *_: (0, 0)),
            ],
            out_specs=pl.BlockSpec((tm, D), lambda i, *_: (i, 0)),
            scratch_shapes=[pltpu.VMEM((2, TOP_K, tm, D), F32), pltpu.SemaphoreType.DMA((2,))],
        ),
        out_shape=jax.ShapeDtypeStruct((T, D), F32),
        compiler_params=pltpu.CompilerParams(
            dimension_semantics=("arbitrary",), vmem_limit_bytes=VMEM_LIMIT),
        name="combine",
    )(rt["dest"], ys, h, gates_t, gf)


def kernel(x, norm1_g, w_in, gla_gate_w2, gla_gate_b, gla_norm_g, s5_a_re, s5_a_im, s5_b_re, s5_b_im, s5_c_re, s5_c_im, s5_d, s5_log_step, s5_glu_w, s5_glu_b, s5_norm_g, w_out, norm2_g, router_w, router_b, expert_w1, expert_b1, expert_w2, expert_b2, final_norm_g):
    assert x.shape == (1, T, D) and w_in.shape[0] == 1
    xt = x.reshape(T, D)
    nqkvr = 2 * GLA_H * GLA_DK + 2 * GLA_H * GLA_DV
    wi = w_in[0]
    wq = wi[:, :nqkvr].astype(BF16)
    wg = jnp.pad(wi[:, nqkvr:nqkvr + GATE_RANK], ((0, 0), (0, 128 - GATE_RANK))).astype(BF16)
    wu = wi[:, nqkvr + GATE_RANK:].astype(BF16)
    w2g = jnp.pad(gla_gate_w2[0], ((0, 128 - GATE_RANK), (0, 0)))
    qkvr, lg, u = _proj(xt, norm1_g, wq, wg, wu, w2g, gla_gate_b)
    gla = _gla(qkvr, lg, gla_norm_g)
    tables = _s5_tables(s5_a_re[0], s5_a_im[0], s5_b_re[0], s5_b_im[0], s5_c_re[0], s5_c_im[0],
                        s5_d[0], s5_log_step[0])
    y = _s5(u, tables)
    h, hn, top_idx, gates, rank, counts = _post(
        y, gla, xt, s5_glu_w[0].astype(BF16), s5_glu_b, s5_norm_g, w_out[0].astype(BF16), norm2_g,
        jnp.pad(router_w[0], ((0, 0), (0, 128 - N_EXP))), router_b.reshape(N_EXP, 1))
    rt = _route(top_idx, rank, counts)
    xs = _scatter(hn, rt)
    ys = _experts(xs, rt, expert_w1[0], expert_b1[0], expert_w2[0], expert_b2[0])
    out = _combine(ys, h, gates.T, final_norm_g.reshape(1, D), rt)
    return out.reshape(1, T, D)
```

```python
import functools
import math

import jax
import jax.numpy as jnp
from jax import lax
from jax.experimental import pallas as pl
from jax.experimental.pallas import tpu as pltpu

F32 = jnp.float32
BF16 = jnp.bfloat16
HIGHEST = lax.Precision.HIGHEST

T = 8192
D = 2048
GLA_H = 4
GLA_DK = 128
GLA_DV = 256
GLA_CHUNK = 64
GLA_SUB = 16
GATE_RANK = 16
GATE_TAU = 16.0
S5_G = 64
S5_N = 16
S5_P = 64
S5_L = 16
S5_GT = 8
N_EXP = 32
TOP_K = 4
D_FF = 2048
SWIGLU_ALPHA = 1.702
SWIGLU_LIMIT = 7.0
RMS_EPS = 1e-6

MOE_RB = 128
MOE_RMAX = 1536
MOE_TF = 512
MOE_NF = D_FF // MOE_TF
MOE_NV = -(-T * TOP_K // MOE_RMAX) + N_EXP
MOE_ROWS = T * TOP_K + N_EXP * MOE_RB
VMEM_LIMIT = 56 * 1024 * 1024

NN = (((1,), (0,)), ((), ()))
NT = (((1,), (1,)), ((), ()))
TN = (((0,), (0,)), ((), ()))


def _dot(a, b, dims=NN, precision=None):
    return lax.dot_general(a, b, dims, preferred_element_type=F32, precision=precision)


def _dot3(a, b):
    a_hi, b_hi = a.astype(BF16), b.astype(BF16)
    a_lo = (a - a_hi.astype(F32)).astype(BF16)
    b_lo = (b - b_hi.astype(F32)).astype(BF16)
    return _dot(a_hi, b_hi) + (_dot(a_lo, b_hi) + _dot(a_hi, b_lo))


def _rms(x, g):
    return x * lax.rsqrt(jnp.mean(x * x, axis=-1, keepdims=True) + RMS_EPS) * g


def _sigmoid(x):
    return 1.0 / (1.0 + jnp.exp(-x))


def _pack_halves(lo_half, hi_half):
    a = lax.bitcast_convert_type(lo_half.astype(F32), jnp.int32)
    b = lax.bitcast_convert_type(hi_half.astype(F32), jnp.int32)
    return a | lax.shift_right_logical(b, 16)


def _unpack_halves(p):
    a = lax.bitcast_convert_type(p & jnp.int32(-65536), F32).astype(BF16)
    b = lax.bitcast_convert_type(lax.shift_left(p, 16), F32).astype(BF16)
    return a, b


def _const_spec(shape):
    nd = len(shape)
    return pl.BlockSpec(shape, lambda *_: (0,) * nd, pipeline_mode=pl.Buffered(1))


def _proj_kernel(x_ref, g1_ref, wq_ref, wg_ref, wu_ref, w2_ref, gb_ref, qkvr_ref, lg_ref, u_ref):
    hb = _rms(x_ref[...], g1_ref[...]).astype(BF16)
    p = _dot(hb, wq_ref[...])
    nq = GLA_H * GLA_DK
    qkvr_ref[:, :nq] = (p[:, :nq] * (GLA_DK ** -0.5)).astype(BF16)
    qkvr_ref[:, nq:] = p[:, nq:].astype(BF16)
    glr = _dot(hb, wg_ref[...])
    z = _dot3(glr, w2_ref[...]) + gb_ref[...]
    lg_ref[...] = (jnp.minimum(z, 0.0) - jnp.log(1.0 + jnp.exp(-jnp.abs(z)))) * (1.0 / GATE_TAU)
    u_ref[...] = _dot(hb, wu_ref[...])


def _proj(x, g1, wq, wg, wu, w2, gb):
    tm = 256
    nqkvr = wq.shape[1]
    return pl.pallas_call(
        _proj_kernel,
        grid=(T // tm,),
        in_specs=[
            pl.BlockSpec((tm, D), lambda i: (i, 0)),
            _const_spec((1, D)),
            _const_spec(wq.shape), _const_spec(wg.shape), _const_spec(wu.shape),
            _const_spec(w2.shape), _const_spec(gb.shape),
        ],
        out_specs=[
            pl.BlockSpec((tm, nqkvr), lambda i: (i, 0)),
            pl.BlockSpec((tm, GLA_H * GLA_DK), lambda i: (i, 0)),
            pl.BlockSpec((tm, S5_G * S5_N), lambda i: (i, 0)),
        ],
        out_shape=[
            jax.ShapeDtypeStruct((T, nqkvr), BF16),
            jax.ShapeDtypeStruct((T, GLA_H * GLA_DK), F32),
            jax.ShapeDtypeStruct((T, S5_G * S5_N), F32),
        ],
        compiler_params=pltpu.CompilerParams(
            dimension_semantics=("arbitrary",), vmem_limit_bytes=VMEM_LIMIT),
        name="proj",
    )(x, g1, wq, wg, wu, w2, gb)


def _gla_kernel(q_ref, k_ref, v_ref, r_ref, lg_ref, gn_ref, tril_ref, esel_ref, o_ref, state_ref):
    C, S = GLA_CHUNK, GLA_SUB
    nsub = C // S

    @pl.when(pl.program_id(0) == 0)
    def _():
        state_ref[...] = jnp.zeros_like(state_ref)

    row = lax.broadcasted_iota(jnp.int32, (C, C), 0)
    col = lax.broadcasted_iota(jnp.int32, (C, C), 1)
    below = (col // S) < (row // S)
    diag = ((col // S) == (row // S)) & (col <= row)
    tril = tril_ref[...]
    esel = esel_ref[...]
    gn = gn_ref[...]

    def head(h, rows, b):
        kcols = slice(h * GLA_DK, (h + 1) * GLA_DK)
        vcols = slice(h * GLA_DV, (h + 1) * GLA_DV)
        q = q_ref[rows, kcols].astype(F32)
        k = k_ref[rows, kcols].astype(F32)
        v = v_ref[rows, vcols]
        bl = b[C - 1:C, :]
        st = state_ref[h]
        o = _dot((q * jnp.exp2(b)).astype(BF16), st.astype(BF16), NT)
        s_rows = [jnp.zeros((S, C), F32)]
        for j in range(1, nsub):
            ref = b[j * S - 1:j * S, :]
            qj = (q[j * S:(j + 1) * S] * jnp.exp2(b[j * S:(j + 1) * S] - ref)).astype(BF16)
            kj = (k * jnp.exp2(jnp.minimum(ref - b, 0.0))).astype(BF16)
            s_rows.append(_dot(qj, kj, NT))
        s_off = jnp.concatenate(s_rows, axis=0)
        q3 = q.reshape(nsub, S, GLA_DK)
        k3 = k.reshape(nsub, S, GLA_DK)
        b3 = b.reshape(nsub, S, GLA_DK)
        xs = []
        for s in range(S):
            lo = (s // 8) * 8
            dec = jnp.exp2(jnp.minimum(b3[:, lo:] - b3[:, s:s + 1, :], 0.0))
            x = q3[:, lo:] * k3[:, s:s + 1, :] * dec
            if lo:
                x = jnp.concatenate([jnp.zeros((nsub, lo, GLA_DK), F32), x], axis=1)
            xs.append(x.reshape(C, GLA_DK).astype(BF16))
        dsc = _dot(jnp.concatenate(xs, axis=1), esel)
        a = jnp.where(below, s_off, jnp.where(diag, dsc, 0.0))
        o = o + _dot(a.astype(BF16), v)
        kout = (k * jnp.exp2(bl - b)).astype(BF16)
        state_ref[h] = st * jnp.exp2(bl) + _dot(v, kout, TN)
        y = _rms(o, gn)
        r = r_ref[rows, vcols].astype(F32)
        o_ref[rows, vcols] = (y * (r * _sigmoid(r))).astype(BF16)

    def chunk(c, carry):
        rows = pl.ds(pl.multiple_of(c * C, C), C)
        g = lg_ref[rows, :]
        g_hi = g.astype(BF16)
        g_mid = (g - g_hi.astype(F32)).astype(BF16)
        g_lo = (g - g_hi.astype(F32) - g_mid.astype(F32)).astype(BF16)
        b = (_dot(tril, g_hi) + (_dot(tril, g_mid) + _dot(tril, g_lo))) * math.log2(math.e)
        for h in range(GLA_H):
            head(h, rows, b[:, h * GLA_DK:(h + 1) * GLA_DK])
        return carry

    lax.fori_loop(0, q_ref.shape[0] // C, chunk, 0, unroll=8)


def _gla(qkvr, lg, gn):
    tb = 512
    C, S = GLA_CHUNK, GLA_SUB
    nk, nv = GLA_H * GLA_DK, GLA_H * GLA_DV
    tril = jnp.tril(jnp.ones((C, C), BF16))
    esel = (jnp.arange(S * GLA_DK)[:, None] // GLA_DK == jnp.arange(C)[None, :] % S).astype(BF16)
    return pl.pallas_call(
        _gla_kernel,
        grid=(T // tb,),
        in_specs=[
            pl.BlockSpec((tb, nk), lambda i: (i, 0)),
            pl.BlockSpec((tb, nk), lambda i: (i, 1)),
            pl.BlockSpec((tb, nv), lambda i: (i, 1)),
            pl.BlockSpec((tb, nv), lambda i: (i, 2)),
            pl.BlockSpec((tb, nk), lambda i: (i, 0)),
            _const_spec((1, GLA_DV)), _const_spec((C, C)), _const_spec((S * GLA_DK, C)),
        ],
        out_specs=pl.BlockSpec((tb, nv), lambda i: (i, 0)),
        out_shape=jax.ShapeDtypeStruct((T, nv), BF16),
        scratch_shapes=[pltpu.VMEM((GLA_H, GLA_DV, GLA_DK), F32)],
        compiler_params=pltpu.CompilerParams(
            dimension_semantics=("arbitrary",), vmem_limit_bytes=VMEM_LIMIT),
        name="gla",
    )(qkvr, qkvr, qkvr, qkvr, lg, gn, tril, esel)


def _s5_tables(a_re, a_im, b_re, b_im, c_re, c_im, d_skip, log_step):
    L, G, P, N = S5_L, S5_G, S5_P, S5_N
    delta = jnp.exp(log_step)[:, None]
    ar, ai = a_re * delta, a_im * delta

    def apow(tau):
        tau = jnp.asarray(tau, F32)[None, :, None]
        mag = jnp.exp(ar[:, None, :] * tau)
        ph = ai[:, None, :] * tau
        return mag * jnp.cos(ph), mag * jnp.sin(ph)

    p_re, p_im = apow(jnp.arange(L + 1))
    ab_re, ab_im = p_re[:, 1], p_im[:, 1]
    den = a_re * a_re + a_im * a_im
    cf_re = ((ab_re - 1.0) * a_re + ab_im * a_im) / den
    cf_im = (ab_im * a_re - (ab_re - 1.0) * a_im) / den
    bb_re = cf_re[:, :, None] * b_re - cf_im[:, :, None] * b_im
    bb_im = cf_re[:, :, None] * b_im + cf_im[:, :, None] * b_re
    e_re = p_re[:, :L, :, None] * bb_re[:, None] - p_im[:, :L, :, None] * bb_im[:, None]
    e_im = p_re[:, :L, :, None] * bb_im[:, None] + p_im[:, :L, :, None] * bb_re[:, None]
    kk = (jnp.einsum('gnp,gtpm->gtnm', c_re, e_re, precision=HIGHEST)
          - jnp.einsum('gnp,gtpm->gtnm', c_im, e_im, precision=HIGHEST))
    kk = kk.at[:, 0].add(jnp.eye(N, dtype=F32)[None] * d_skip[:, :, None])
    r_re, r_im = apow(L - 1 - jnp.arange(L))
    bs_re = (r_re[:, :, None, :] * bb_re.transpose(0, 2, 1)[:, None] - r_im[:, :, None, :] * bb_im.transpose(0, 2, 1)[:, None])
    bs_im = (r_re[:, :, None, :] * bb_im.transpose(0, 2, 1)[:, None] + r_im[:, :, None, :] * bb_re.transpose(0, 2, 1)[:, None])
    ct_re = c_re.transpose(0, 2, 1)[:, :, None, :]
    ct_im = c_im.transpose(0, 2, 1)[:, :, None, :]
    q_re = p_re[:, 1:].transpose(0, 2, 1)[:, :, :, None]
    q_im = p_im[:, 1:].transpose(0, 2, 1)[:, :, :, None]
    cre = ct_re * q_re - ct_im * q_im
    cim = -(ct_re * q_im + ct_im * q_re)
    GT = S5_GT
    nt = G // GT

    def tile_groups(w):
        return w.reshape((nt, GT) + w.shape[1:])

    kc = tile_groups(kk).transpose(0, 1, 4, 2, 3).reshape(nt, GT * N, L * N)

    def in_table(w):
        return tile_groups(w).transpose(0, 2, 1, 3, 4).reshape(nt, L * GT * N, P)

    def out_table(w):
        return w.reshape(nt, GT * P, L * N)

    nstep = (T // L - 1).bit_length()
    sc_re, sc_im = apow(jnp.asarray([L * 2 ** i for i in range(nstep)]))

    def lanes(w):
        return tile_groups(w).transpose(0, 2, 1, 3).reshape(nt, w.shape[1], GT * P)

    r1 = jnp.arange(L * N)[:, None]
    c1 = jnp.arange(L * GT * N)[None, :]
    wide_tn = ((r1 // N == c1 // (GT * N)) & (r1 % N == c1 % N)).astype(BF16)
    wide_p = (jnp.arange(P)[:, None] == jnp.arange(GT * P)[None, :] % P).astype(BF16)
    return dict(
        kc=kc.astype(BF16),
        bs_re=in_table(bs_re).astype(BF16), bs_im=in_table(bs_im).astype(BF16),
        cre=out_table(cre).astype(BF16), cim=out_table(cim).astype(BF16),
        sc_re=lanes(sc_re), sc_im=lanes(sc_im), wide_tn=wide_tn, wide_p=wide_p,
    )


def _s5_kernel(u_ref, kc_ref, bsr_ref, bsi_ref, cre_ref, cim_ref, scr_ref, sci_ref, wtn_ref, wp_ref,
               y_ref, m_ref):
    L, N, P = S5_L, S5_N, S5_P
    W = S5_GT * N
    nch = u_ref.shape[0] // L

    @pl.when(pl.program_id(0) == 0)
    def _():
        m_ref[...] = jnp.zeros_like(m_ref)

    def widen(compact, wide, row_group, col_group):
        full = _dot(compact, wide)
        r = lax.broadcasted_iota(jnp.int32, full.shape, 0)
        c = lax.broadcasted_iota(jnp.int32, full.shape, 1)
        return jnp.where(row_group(r) == col_group(c), full, 0.0).astype(BF16)

    wtn, wp = wtn_ref[...], wp_ref[...]
    bd = widen(kc_ref[0], wtn, lambda r: r // N, lambda c: (c % W) // N)
    bsr = widen(bsr_ref[0], wp, lambda r: (r % W) // N, lambda c: c // P)
    bsi = widen(bsi_ref[0], wp, lambda r: (r % W) // N, lambda c: c // P)
    cre = widen(cre_ref[0], wtn, lambda r: r // P, lambda c: (c % W) // N)
    cim = widen(cim_ref[0], wtn, lambda r: r // P, lambda c: (c % W) // N)
    for s in range(L):
        m_ref[s * W:(s + 1) * W, s * W:] = bd[:, :(L - s) * W]
    ucat = jnp.concatenate([u_ref[pl.ds(s, nch, stride=L), :].astype(BF16) for s in range(L)], axis=1)
    hr = _dot(ucat, bsr)
    hi = _dot(ucat, bsi)
    row = lax.broadcasted_iota(jnp.int32, hr.shape, 0)

    def shift(x, d):
        if d % 8 == 0:
            return jnp.concatenate([jnp.zeros((d, x.shape[1]), F32), x[:nch - d]], axis=0)
        return jnp.where(row >= d, pltpu.roll(x, d, 0), 0.0)

    for i in range(scr_ref.shape[1]):
        mr, mi = scr_ref[0, i:i + 1, :], sci_ref[0, i:i + 1, :]
        pr, pi = shift(hr, 2 ** i), shift(hi, 2 ** i)
        hr, hi = hr + mr * pr - mi * pi, hi + mr * pi + mi * pr
    gr, gi = shift(hr, 1), shift(hi, 1)
    grb, gib = gr.astype(BF16), gi.astype(BF16)
    for tp in range(L // 2):
        cols = slice(2 * tp * W, (2 * tp + 2) * W)
        depth = (2 * tp + 2) * W
        y = _dot(ucat[:, :depth], m_ref[:depth, cols]) + _dot(grb, cre[:, cols]) + _dot(gib, cim[:, cols])
        for t in (2 * tp, 2 * tp + 1):
            y_ref[pl.ds(t, nch, stride=L), :] = y[:, (t - 2 * tp) * W:(t - 2 * tp + 1) * W]


def _s5(u, tb):
    W = S5_GT * S5_N
    nt = S5_G // S5_GT

    def tile_spec(a):
        return pl.BlockSpec((1,) + a.shape[1:], lambda g: (g,) + (0,) * (a.ndim - 1))

    ws = [tb[n] for n in ["kc", "bs_re", "bs_im", "cre", "cim", "sc_re", "sc_im"]]
    consts = [tb["wide_tn"], tb["wide_p"]]
    return pl.pallas_call(
        _s5_kernel,
        grid=(nt,),
        in_specs=([pl.BlockSpec((T, W), lambda g: (0, g))] + [tile_spec(w) for w in ws]
                  + [_const_spec(c.shape) for c in consts]),
        out_specs=pl.BlockSpec((T, W), lambda g: (0, g)),
        out_shape=jax.ShapeDtypeStruct((T, S5_G * S5_N), F32),
        scratch_shapes=[pltpu.VMEM((S5_L * W, S5_L * W), BF16)],
        compiler_params=pltpu.CompilerParams(
            dimension_semantics=("arbitrary",), vmem_limit_bytes=VMEM_LIMIT),
        name="s5",
    )(u, *ws, *consts)


def _post_kernel(y_ref, gla_ref, x_ref, gw_ref, gb_ref, sg_ref, wo_ref, n2_ref, rw_ref, rb_ref, before_ref,
                 h_ref, hn_ref, idx_ref, gate_ref, rank_ref, cnt_ref, run_ref):
    @pl.when(pl.program_id(0) == 0)
    def _():
        run_ref[...] = jnp.zeros_like(run_ref)

    y = y_ref[...]
    z = 0.5 * y * (1.0 + jnp.tanh(math.sqrt(2.0 / math.pi) * (y + 0.044715 * (y * y * y))))
    z = z * _sigmoid(_dot(z.astype(BF16), gw_ref[...]) + gb_ref[...])
    s5o = _rms(z, sg_ref[...]).astype(BF16)
    half = GLA_H * GLA_DV
    h = x_ref[...] + _dot(gla_ref[...], wo_ref[:half, :]) + _dot(s5o, wo_ref[half:, :])
    h_ref[...] = h
    hn = _rms(h, n2_ref[...])
    hi = hn.astype(BF16)
    hn_ref[...] = _pack_halves(hi[:, :D // 2], hi[:, D // 2:])
    lo = (hn - hi.astype(F32)).astype(BF16)
    rw = rw_ref[...]
    rw_hi = rw.astype(BF16)
    rw_lo = (rw - rw_hi.astype(F32)).astype(BF16)
    lg = _dot(hi, rw_hi) + (_dot(lo, rw_hi) + _dot(hi, rw_lo))
    lt = lg.T[:N_EXP] + rb_ref[...]
    eid = lax.broadcasted_iota(jnp.int32, lt.shape, 0).astype(F32)
    vals, idxs = [], []
    for _ in range(TOP_K):
        m = jnp.max(lt, axis=0, keepdims=True)
        sel = jnp.min(jnp.where(lt == m, eid, float(N_EXP)), axis=0, keepdims=True)
        vals.append(m)
        idxs.append(sel)
        lt = jnp.where(eid == sel, -jnp.inf, lt)
    ex = [jnp.exp(vv - vals[0]) for vv in vals]
    inv = 1.0 / (ex[0] + ex[1] + ex[2] + ex[3])
    idx_ref[...] = jnp.concatenate(idxs, axis=0).astype(jnp.int32)
    gate_ref[...] = jnp.concatenate([e * inv for e in ex], axis=0)
    run = run_ref[:, :1]
    ranks = []
    for sel in idxs:
        onehot = jnp.where(eid == sel, 1.0, 0.0)
        earlier = _dot(onehot.astype(BF16), before_ref[...])
        ranks.append(jnp.sum(onehot * (run + earlier), axis=0, keepdims=True))
        run = run + jnp.sum(onehot, axis=1, keepdims=True)
    rank_ref[...] = jnp.concatenate(ranks, axis=0).astype(jnp.int32)
    run_ref[...] = jnp.broadcast_to(run, run_ref.shape)
    cnt_ref[...] = jnp.broadcast_to(run, cnt_ref.shape)


def _post(y, gla, x, gw, gb, sg, wo, n2, rwt, rb):
    tm = 256
    width = S5_G * S5_N
    before = (jnp.arange(tm)[:, None] < jnp.arange(tm)[None, :]).astype(BF16)
    return pl.pallas_call(
        _post_kernel,
        grid=(T // tm,),
        in_specs=[
            pl.BlockSpec((tm, width), lambda i: (i, 0)),
            pl.BlockSpec((tm, GLA_H * GLA_DV), lambda i: (i, 0)),
            pl.BlockSpec((tm, D), lambda i: (i, 0)),
            _const_spec(gw.shape), _const_spec(gb.shape), _const_spec(sg.shape), _const_spec(wo.shape),
            _const_spec(n2.shape), _const_spec(rwt.shape), _const_spec(rb.shape), _const_spec(before.shape),
        ],
        out_specs=[
            pl.BlockSpec((tm, D), lambda i: (i, 0)),
            pl.BlockSpec((tm, D // 2), lambda i: (i, 0)),
            pl.BlockSpec((TOP_K, tm), lambda i: (0, i)),
            pl.BlockSpec((TOP_K, tm), lambda i: (0, i)),
            pl.BlockSpec((TOP_K, tm), lambda i: (0, i)),
            pl.BlockSpec((N_EXP, 128), lambda i: (0, 0)),
        ],
        out_shape=[
            jax.ShapeDtypeStruct((T, D), F32),
            jax.ShapeDtypeStruct((T, D // 2), jnp.int32),
            jax.ShapeDtypeStruct((TOP_K, T), jnp.int32),
            jax.ShapeDtypeStruct((TOP_K, T), F32),
            jax.ShapeDtypeStruct((TOP_K, T), jnp.int32),
            jax.ShapeDtypeStruct((N_EXP, 128), F32),
        ],
        scratch_shapes=[pltpu.VMEM((N_EXP, 128), F32)],
        compiler_params=pltpu.CompilerParams(
            dimension_semantics=("arbitrary",), vmem_limit_bytes=VMEM_LIMIT),
        name="post",
    )(y, gla, x, gw, gb, sg, wo, n2, rwt, rb, before)


def _route(top_idx, rank, counts):
    e_flat = top_idx.reshape(-1)
    rank = rank.reshape(-1)
    counts = counts[:, 0].astype(jnp.int32)
    padded = (counts + MOE_RB - 1) // MOE_RB * MOE_RB
    pad_end = jnp.cumsum(padded)
    pad_start = pad_end - padded
    dest = _dest(top_idx, rank.reshape(top_idx.shape), pad_start.astype(jnp.int32)).reshape(-1)
    nvis_e = (counts + MOE_RMAX - 1) // MOE_RMAX
    vis_end = jnp.cumsum(nvis_e)
    n_vis = vis_end[-1]
    v = jnp.arange(MOE_NV, dtype=jnp.int32)
    vc = jnp.minimum(v, n_vis - 1)
    ve = jnp.minimum(jnp.searchsorted(vis_end, vc, side='right'), N_EXP - 1).astype(jnp.int32)
    local = vc - (vis_end[ve] - nvis_e[ve])
    vcnt = jnp.where(v < n_vis, jnp.minimum(MOE_RMAX, counts[ve] - local * MOE_RMAX), 0)
    vrow = pad_start[ve] + local * MOE_RMAX
    return dict(dest=dest, fill_from=(pad_start + counts).astype(jnp.int32), fill_to=pad_end.astype(jnp.int32),
                tail_blk=(pad_end[-1:] // MOE_RB).astype(jnp.int32),
                vis_e=ve, vis_cnt=vcnt.astype(jnp.int32), vis_row=vrow.astype(jnp.int32),
                n_vis=n_vis.reshape(1).astype(jnp.int32))


def _dest_kernel(start_ref, idx_ref, rank_ref, dest_ref):
    e = idx_ref[...]
    d = rank_ref[...]
    for x in range(N_EXP):
        d = d + jnp.where(e == x, start_ref[x], 0)
    dest_ref[...] = d


def _dest(top_idx, rank, pad_start):
    full = pl.BlockSpec(top_idx.shape, lambda i, *_: (0, 0))
    return pl.pallas_call(
        _dest_kernel,
        grid_spec=pltpu.PrefetchScalarGridSpec(num_scalar_prefetch=1, grid=(1,), in_specs=[full, full], out_specs=full),
        out_shape=jax.ShapeDtypeStruct(top_idx.shape, jnp.int32),
        name="dest",
    )(pad_start, top_idx, rank)


def _row_copy(src_ref, srow, dst_ref, drow, sem):
    return pltpu.make_async_copy(src_ref.at[pl.ds(srow, 1), :], dst_ref.at[pl.ds(drow, 1), :], sem)


def _tail_fill(src_ref, dst_ref, tail_ref, sem):
    def cp(b):
        return pltpu.make_async_copy(src_ref.at[pl.ds(0, MOE_RB), :],
                                     dst_ref.at[pl.ds(pl.multiple_of(b * MOE_RB, MOE_RB), MOE_RB), :], sem)

    def start(b, c):
        cp(b).start()
        return c
    lax.fori_loop(tail_ref[0], MOE_ROWS // MOE_RB, start, 0)

    def finish(b, c):
        cp(b).wait()
        return c
    lax.fori_loop(tail_ref[0], MOE_ROWS // MOE_RB, finish, 0)


def _scatter_kernel(dest_ref, from_ref, to_ref, tail_ref, hn_ref, xs_ref, zero_ref, sem, zsem):
    i = pl.program_id(0)
    tm = hn_ref.shape[0]

    @pl.when(i == 0)
    def _():
        zero_ref[...] = jnp.zeros_like(zero_ref)
        _tail_fill(zero_ref, xs_ref, tail_ref, zsem)

        def per_expert(e, c):
            def fill(r, c2):
                _row_copy(zero_ref, 0, xs_ref, r, zsem).start()
                return c2
            lax.fori_loop(from_ref[e], to_ref[e], fill, 0)

            def drain(r, c2):
                _row_copy(zero_ref, 0, xs_ref, r, zsem).wait()
                return c2
            lax.fori_loop(from_ref[e], to_ref[e], drain, 0)
            return c
        lax.fori_loop(0, N_EXP, per_expert, 0)

    def issue(t, c):
        for k in range(TOP_K):
            _row_copy(hn_ref, t, xs_ref, dest_ref[k * T + i * tm + t], sem).start(priority=k % 2)
        return c
    lax.fori_loop(0, tm, issue, 0)

    for k in range(TOP_K):
        pltpu.make_async_copy(hn_ref, xs_ref.at[pl.ds(0, tm), :], sem).wait()


def _scatter(hn, rt):
    tm = 1024
    return pl.pallas_call(
        _scatter_kernel,
        grid_spec=pltpu.PrefetchScalarGridSpec(
            num_scalar_prefetch=4,
            grid=(T // tm,),
            in_specs=[pl.BlockSpec((tm, D // 2), lambda i, *_: (i, 0))],
            out_specs=pl.BlockSpec(memory_space=pl.ANY),
            scratch_shapes=[pltpu.VMEM((MOE_RB, D // 2), jnp.int32), pltpu.SemaphoreType.DMA,
                            pltpu.SemaphoreType.DMA],
        ),
        out_shape=jax.ShapeDtypeStruct((MOE_ROWS, D // 2), jnp.int32),
        compiler_params=pltpu.CompilerParams(
            dimension_semantics=("arbitrary",), vmem_limit_bytes=VMEM_LIMIT),
        name="scatter",
    )(rt["dest"], rt["fill_from"], rt["fill_to"], rt["tail_blk"], hn)


def _experts_kernel(ve_ref, vcnt_ref, vrow_ref, nvis_ref, tail_ref, xs_ref, w1g_ref, w1l_ref, w2_ref, b1g_ref,
                    b1l_ref, b2_ref, ys_ref, xin_ref, xb_ref, acc_ref, isem, osem):
    v = pl.program_id(0)
    j = pl.program_id(1)
    RB = MOE_RB
    BIG, MID = 4 * RB, 2 * RB
    nvis = nvis_ref[0]
    active = v < nvis
    first = j == 0
    last = j == MOE_NF - 1

    def geometry(vv):
        nblk = (vcnt_ref[vv] + RB - 1) // RB
        return vrow_ref[vv], nblk // 4, (nblk // 2) % 2, nblk % 2

    row0, nbig, mid, small = geometry(v)
    mid_r0 = nbig * BIG
    small_r0 = mid_r0 + mid * MID

    def x_copy(vrow, r0, n, slot):
        return pltpu.make_async_copy(xs_ref.at[pl.ds(pl.multiple_of(vrow + r0, RB), n), :],
                                     xin_ref.at[slot, pl.ds(0, n), :], isem.at[slot])

    def y_copy(r0, n):
        return pltpu.make_async_copy(acc_ref.at[pl.ds(pl.multiple_of(r0, RB), n), :],
                                     ys_ref.at[pl.ds(pl.multiple_of(row0 + r0, RB), n), :], osem)

    def start_unit(vv, u):
        vrow, nb, md, sm = geometry(vv)

        @pl.when(u < nb)
        def _():
            x_copy(vrow, u * BIG, BIG, u % 2).start()

        @pl.when((u == nb) & (md == 1))
        def _():
            x_copy(vrow, nb * BIG, MID, u % 2).start()

        @pl.when((u == nb + md) & (sm == 1))
        def _():
            x_copy(vrow, nb * BIG + md * MID, RB, u % 2).start()

    @pl.when((v == 0) & first)
    def _():
        acc_ref[...] = jnp.zeros_like(acc_ref)
        _tail_fill(acc_ref, ys_ref, tail_ref, osem)
        start_unit(0, 0)

    def block(u, r0, n):
        rows = pl.ds(pl.multiple_of(r0, RB), n)
        slot = u % 2

        @pl.when(first)
        def _():
            x_copy(row0, r0, n, slot).wait()
            start_unit(v, u + 1)
            lo_half, hi_half = _unpack_halves(xin_ref[slot, pl.ds(0, n), :])
            xb_ref[rows, :D // 2] = lo_half
            xb_ref[rows, D // 2:] = hi_half

        x = xb_ref[rows, :]
        glu = jnp.minimum(_dot(x, w1g_ref[0].astype(BF16)) + b1g_ref[0], SWIGLU_LIMIT)
        lin = jnp.clip(_dot(x, w1l_ref[0].astype(BF16)) + b1l_ref[0], -SWIGLU_LIMIT, SWIGLU_LIMIT)
        act = glu * _sigmoid(SWIGLU_ALPHA * glu) * (lin + 1.0)
        part = _dot(act.astype(BF16), w2_ref[0].astype(BF16))

        acc_ref[rows, :] = part + jnp.where(first, b2_ref[0], acc_ref[rows, :])

        @pl.when(last)
        def _():
            y_copy(r0, n).start()

    @pl.when(active)
    def _():
        def big(i, c):
            block(i, i * BIG, BIG)
            return c
        lax.fori_loop(0, nbig, big, 0)

        @pl.when(mid == 1)
        def _():
            block(nbig, mid_r0, MID)

        @pl.when(small == 1)
        def _():
            block(nbig + mid, small_r0, RB)

    @pl.when(active & last)
    def _():
        def finish(i, c):
            y_copy(i * BIG, BIG).wait()
            return c
        lax.fori_loop(0, nbig, finish, 0)

        @pl.when(mid == 1)
        def _():
            y_copy(mid_r0, MID).wait()

        @pl.when(small == 1)
        def _():
            y_copy(small_r0, RB).wait()

        @pl.when(v + 1 < nvis)
        def _():
            start_unit(v + 1, 0)


def _experts(xs, rt, w1, b1, w2, b2):
    b1 = b1.reshape(N_EXP, 1, 2 * D_FF)
    b2 = b2.reshape(N_EXP, 1, D)

    def jj(v, j, ve, vcnt, vrow, nvis, tail):
        return jnp.where(v < nvis[0], j, MOE_NF - 1)

    return pl.pallas_call(
        _experts_kernel,
        grid_spec=pltpu.PrefetchScalarGridSpec(
            num_scalar_prefetch=5,
            grid=(MOE_NV, MOE_NF),
            in_specs=[
                pl.BlockSpec(memory_space=pl.ANY),
                pl.BlockSpec((1, D, MOE_TF), lambda v, j, ve, *s: (ve[v], 0, jj(v, j, ve, *s))),
                pl.BlockSpec((1, D, MOE_TF), lambda v, j, ve, *s: (ve[v], 0, MOE_NF + jj(v, j, ve, *s))),
                pl.BlockSpec((1, MOE_TF, D), lambda v, j, ve, *s: (ve[v], jj(v, j, ve, *s), 0)),
                pl.BlockSpec((1, 1, MOE_TF), lambda v, j, ve, *s: (ve[v], 0, jj(v, j, ve, *s))),
                pl.BlockSpec((1, 1, MOE_TF), lambda v, j, ve, *s: (ve[v], 0, MOE_NF + jj(v, j, ve, *s))),
                pl.BlockSpec((1, 1, D), lambda v, j, ve, *s: (ve[v], 0, 0)),
            ],
            out_specs=pl.BlockSpec(memory_space=pl.ANY),
            scratch_shapes=[
                pltpu.VMEM((2, 4 * MOE_RB, D // 2), jnp.int32),
                pltpu.VMEM((MOE_RMAX, D), BF16),
                pltpu.VMEM((MOE_RMAX, D), F32),
                pltpu.SemaphoreType.DMA((2,)),
                pltpu.SemaphoreType.DMA,
            ],
        ),
        out_shape=jax.ShapeDtypeStruct((MOE_ROWS, D), F32),
        compiler_params=pltpu.CompilerParams(
            dimension_semantics=("arbitrary", "arbitrary"), vmem_limit_bytes=VMEM_LIMIT),
        name="experts",
    )(rt["vis_e"], rt["vis_cnt"], rt["vis_row"], rt["n_vis"], rt["tail_blk"], xs, w1, w1, w2, b1, b1, b2)


def _combine_kernel(dest_ref, ys_ref, h_ref, gate_ref, gf_ref, o_ref, buf_ref, sem):
    i = pl.program_id(0)
    tm = h_ref.shape[0]

    def gather(tile, slot):
        def issue(t, c):
            for k in range(TOP_K):
                _row_copy(ys_ref, dest_ref[k * T + tile * tm + t], buf_ref.at[slot, k], t,
                          sem.at[slot]).start(priority=k % 2)
            return c
        lax.fori_loop(0, tm, issue, 0)

    @pl.when(i == 0)
    def _():
        gather(0, 0)

    @pl.when(i + 1 < pl.num_programs(0))
    def _():
        gather(i + 1, (i + 1) % 2)

    slot = i % 2
    for k in range(TOP_K):
        pltpu.make_async_copy(ys_ref.at[pl.ds(0, tm), :], buf_ref.at[slot, k], sem.at[slot]).wait()

    h = h_ref[...]
    for k in range(TOP_K):
        h = h + gate_ref[:, k:k + 1] * buf_ref[slot, k]
    o_ref[...] = _rms(h, gf_ref[...])


def _combine(ys, h, gates_t, gf, rt):
    tm = 256
    return pl.pallas_call(
        _combine_kernel,
        grid_spec=pltpu.PrefetchScalarGridSpec(
            num_scalar_prefetch=1,
            grid=(T // tm,),
            in_specs=[
                pl.BlockSpec(memory_space=pl.ANY),
                pl.BlockSpec((tm, D), lambda i, *_: (i, 0)),
                pl.BlockSpec((tm, TOP_K), lambda i, *_: (i, 0)),
                pl.BlockSpec((1, D), lambda i, *_: (0, 0)),
            ],
            out_specs=pl.BlockSpec((tm, D), lambda i, *_: (i, 0)),
            scratch_shapes=[pltpu.VMEM((2, TOP_K, tm, D), F32), pltpu.SemaphoreType.DMA((2,))],
        ),
        out_shape=jax.ShapeDtypeStruct((T, D), F32),
        compiler_params=pltpu.CompilerParams(
            dimension_semantics=("arbitrary",), vmem_limit_bytes=VMEM_LIMIT),
        name="combine",
    )(rt["dest"], ys, h, gates_t, gf)


def kernel(x, norm1_g, w_in, gla_gate_w2, gla_gate_b, gla_norm_g, s5_a_re, s5_a_im, s5_b_re, s5_b_im, s5_c_re, s5_c_im, s5_d, s5_log_step, s5_glu_w, s5_glu_b, s5_norm_g, w_out, norm2_g, router_w, router_b, expert_w1, expert_b1, expert_w2, expert_b2, final_norm_g):
    assert x.shape == (1, T, D) and w_in.shape[0] == 1
    xt = x.reshape(T, D)
    nqkvr = 2 * GLA_H * GLA_DK + 2 * GLA_H * GLA_DV
    wi = w_in[0]
    wq = wi[:, :nqkvr].astype(BF16)
    wg = jnp.pad(wi[:, nqkvr:nqkvr + GATE_RANK], ((0, 0), (0, 128 - GATE_RANK))).astype(BF16)
    wu = wi[:, nqkvr + GATE_RANK:].astype(BF16)
    w2g = jnp.pad(gla_gate_w2[0], ((0, 128 - GATE_RANK), (0, 0)))
    qkvr, lg, u = _proj(xt, norm1_g, wq, wg, wu, w2g, gla_gate_b)
    gla = _gla(qkvr, lg, gla_norm_g)
    tables = _s5_tables(s5_a_re[0], s5_a_im[0], s5_b_re[0], s5_b_im[0], s5_c_re[0], s5_c_im[0],
                        s5_d[0], s5_log_step[0])
    y = _s5(u, tables)
    h, hn, top_idx, gates, rank, counts = _post(
        y, gla, xt, s5_glu_w[0].astype(BF16), s5_glu_b, s5_norm_g, w_out[0].astype(BF16), norm2_g,
        jnp.pad(router_w[0], ((0, 0), (0, 128 - N_EXP))), router_b.reshape(N_EXP, 1))
    rt = _route(top_idx, rank, counts)
    xs = _scatter(hn, rt)
    ys = _experts(xs, rt, expert_w1[0], expert_b1[0], expert_w2[0], expert_b2[0])
    out = _combine(ys, h, gates.T, final_norm_g.reshape(1, D), rt)
    return out.reshape(1, T, D)
```

```python
import functools
import math

import jax
import jax.numpy as jnp
from jax import lax
from jax.experimental import pallas as pl
from jax.experimental.pallas import tpu as pltpu

F32 = jnp.float32
BF16 = jnp.bfloat16
HIGHEST = lax.Precision.HIGHEST

T = 8192
D = 2048
GLA_H = 4
GLA_DK = 128
GLA_DV = 256
GLA_CHUNK = 64
GLA_SUB = 16
GATE_RANK = 16
GATE_TAU = 16.0
S5_G = 64
S5_N = 16
S5_P = 64
S5_L = 16
S5_GT = 8
N_EXP = 32
TOP_K = 4
D_FF = 2048
SWIGLU_ALPHA = 1.702
SWIGLU_LIMIT = 7.0
RMS_EPS = 1e-6

MOE_RB = 128
MOE_RMAX = 1536
MOE_TF = 512
MOE_NF = D_FF // MOE_TF
MOE_NQ = MOE_RB
MOE_NV = -(-T * TOP_K // MOE_RMAX) + N_EXP
MOE_ROWS = T * TOP_K + N_EXP * MOE_RB
VMEM_LIMIT = 56 * 1024 * 1024

NN = (((1,), (0,)), ((), ()))
NT = (((1,), (1,)), ((), ()))
TN = (((0,), (0,)), ((), ()))


def _dot(a, b, dims=NN, precision=None):
    return lax.dot_general(a, b, dims, preferred_element_type=F32, precision=precision)


def _dot3(a, b):
    a_hi, b_hi = a.astype(BF16), b.astype(BF16)
    a_lo = (a - a_hi.astype(F32)).astype(BF16)
    b_lo = (b - b_hi.astype(F32)).astype(BF16)
    return _dot(a_hi, b_hi) + (_dot(a_lo, b_hi) + _dot(a_hi, b_lo))


def _rms(x, g):
    return x * lax.rsqrt(jnp.mean(x * x, axis=-1, keepdims=True) + RMS_EPS) * g


def _sigmoid(x):
    return 1.0 / (1.0 + jnp.exp(-x))


def _pack_halves(lo_half, hi_half):
    a = lax.bitcast_convert_type(lo_half.astype(F32), jnp.int32)
    b = lax.bitcast_convert_type(hi_half.astype(F32), jnp.int32)
    return a | lax.shift_right_logical(b, 16)


def _unpack_halves(p):
    a = lax.bitcast_convert_type(p & jnp.int32(-65536), F32).astype(BF16)
    b = lax.bitcast_convert_type(lax.shift_left(p, 16), F32).astype(BF16)
    return a, b


def _const_spec(shape):
    nd = len(shape)
    return pl.BlockSpec(shape, lambda *_: (0,) * nd, pipeline_mode=pl.Buffered(1))


def _proj_kernel(x_ref, g1_ref, wq_ref, wg_ref, wu_ref, w2_ref, gb_ref, qkvr_ref, lg_ref, u_ref):
    hb = _rms(x_ref[...], g1_ref[...]).astype(BF16)
    p = _dot(hb, wq_ref[...])
    nq = GLA_H * GLA_DK
    qkvr_ref[:, :nq] = (p[:, :nq] * (GLA_DK ** -0.5)).astype(BF16)
    qkvr_ref[:, nq:] = p[:, nq:].astype(BF16)
    glr = _dot(hb, wg_ref[...])
    z = _dot3(glr, w2_ref[...]) + gb_ref[...]
    lg_ref[...] = (jnp.minimum(z, 0.0) - jnp.log(1.0 + jnp.exp(-jnp.abs(z)))) * (1.0 / GATE_TAU)
    u_ref[...] = _dot(hb, wu_ref[...])


def _proj(x, g1, wq, wg, wu, w2, gb):
    tm = 256
    nqkvr = wq.shape[1]
    return pl.pallas_call(
        _proj_kernel,
        grid=(T // tm,),
        in_specs=[
            pl.BlockSpec((tm, D), lambda i: (i, 0)),
            _const_spec((1, D)),
            _const_spec(wq.shape), _const_spec(wg.shape), _const_spec(wu.shape),
            _const_spec(w2.shape), _const_spec(gb.shape),
        ],
        out_specs=[
            pl.BlockSpec((tm, nqkvr), lambda i: (i, 0)),
            pl.BlockSpec((tm, GLA_H * GLA_DK), lambda i: (i, 0)),
            pl.BlockSpec((tm, S5_G * S5_N), lambda i: (i, 0)),
        ],
        out_shape=[
            jax.ShapeDtypeStruct((T, nqkvr), BF16),
            jax.ShapeDtypeStruct((T, GLA_H * GLA_DK), F32),
            jax.ShapeDtypeStruct((T, S5_G * S5_N), F32),
        ],
        compiler_params=pltpu.CompilerParams(
            dimension_semantics=("arbitrary",), vmem_limit_bytes=VMEM_LIMIT),
        name="proj",
    )(x, g1, wq, wg, wu, w2, gb)


def _gla_kernel(q_ref, k_ref, v_ref, r_ref, lg_ref, gn_ref, tril_ref, esel_ref, o_ref, state_ref):
    C, S = GLA_CHUNK, GLA_SUB
    nsub = C // S

    @pl.when(pl.program_id(0) == 0)
    def _():
        state_ref[...] = jnp.zeros_like(state_ref)

    row = lax.broadcasted_iota(jnp.int32, (C, C), 0)
    col = lax.broadcasted_iota(jnp.int32, (C, C), 1)
    below = (col // S) < (row // S)
    diag = ((col // S) == (row // S)) & (col <= row)
    tril = tril_ref[...]
    esel = esel_ref[...]
    gn = gn_ref[...]

    def head(h, rows, b):
        kcols = slice(h * GLA_DK, (h + 1) * GLA_DK)
        vcols = slice(h * GLA_DV, (h + 1) * GLA_DV)
        q = q_ref[rows, kcols].astype(F32)
        k = k_ref[rows, kcols].astype(F32)
        v = v_ref[rows, vcols]
        bl = b[C - 1:C, :]
        st = state_ref[h]
        o = _dot((q * jnp.exp2(b)).astype(BF16), st.astype(BF16), NT)
        s_rows = [jnp.zeros((S, C), F32)]
        for j in range(1, nsub):
            ref = b[j * S - 1:j * S, :]
            qj = (q[j * S:(j + 1) * S] * jnp.exp2(b[j * S:(j + 1) * S] - ref)).astype(BF16)
            kj = (k * jnp.exp2(jnp.minimum(ref - b, 0.0))).astype(BF16)
            s_rows.append(_dot(qj, kj, NT))
        s_off = jnp.concatenate(s_rows, axis=0)
        q3 = q.reshape(nsub, S, GLA_DK)
        k3 = k.reshape(nsub, S, GLA_DK)
        b3 = b.reshape(nsub, S, GLA_DK)
        xs = []
        for s in range(S):
            lo = (s // 8) * 8
            dec = jnp.exp2(jnp.minimum(b3[:, lo:] - b3[:, s:s + 1, :], 0.0))
            x = q3[:, lo:] * k3[:, s:s + 1, :] * dec
            if lo:
                x = jnp.concatenate([jnp.zeros((nsub, lo, GLA_DK), F32), x], axis=1)
            xs.append(x.reshape(C, GLA_DK).astype(BF16))
        dsc = _dot(jnp.concatenate(xs, axis=1), esel)
        a = jnp.where(below, s_off, jnp.where(diag, dsc, 0.0))
        o = o + _dot(a.astype(BF16), v)
        kout = (k * jnp.exp2(bl - b)).astype(BF16)
        state_ref[h] = st * jnp.exp2(bl) + _dot(v, kout, TN)
        y = _rms(o, gn)
        r = r_ref[rows, vcols].astype(F32)
        o_ref[rows, vcols] = (y * (r * _sigmoid(r))).astype(BF16)

    def chunk(c, carry):
        rows = pl.ds(pl.multiple_of(c * C, C), C)
        g = lg_ref[rows, :]
        g_hi = g.astype(BF16)
        g_mid = (g - g_hi.astype(F32)).astype(BF16)
        g_lo = (g - g_hi.astype(F32) - g_mid.astype(F32)).astype(BF16)
        b = (_dot(tril, g_hi) + (_dot(tril, g_mid) + _dot(tril, g_lo))) * math.log2(math.e)
        for h in range(GLA_H):
            head(h, rows, b[:, h * GLA_DK:(h + 1) * GLA_DK])
        return carry

    lax.fori_loop(0, q_ref.shape[0] // C, chunk, 0, unroll=8)


def _gla(qkvr, lg, gn):
    tb = 512
    C, S = GLA_CHUNK, GLA_SUB
    nk, nv = GLA_H * GLA_DK, GLA_H * GLA_DV
    tril = jnp.tril(jnp.ones((C, C), BF16))
    esel = (jnp.arange(S * GLA_DK)[:, None] // GLA_DK == jnp.arange(C)[None, :] % S).astype(BF16)
    return pl.pallas_call(
        _gla_kernel,
        grid=(T // tb,),
        in_specs=[
            pl.BlockSpec((tb, nk), lambda i: (i, 0)),
            pl.BlockSpec((tb, nk), lambda i: (i, 1)),
            pl.BlockSpec((tb, nv), lambda i: (i, 1)),
            pl.BlockSpec((tb, nv), lambda i: (i, 2)),
            pl.BlockSpec((tb, nk), lambda i: (i, 0)),
            _const_spec((1, GLA_DV)), _const_spec((C, C)), _const_spec((S * GLA_DK, C)),
        ],
        out_specs=pl.BlockSpec((tb, nv), lambda i: (i, 0)),
        out_shape=jax.ShapeDtypeStruct((T, nv), BF16),
        scratch_shapes=[pltpu.VMEM((GLA_H, GLA_DV, GLA_DK), F32)],
        compiler_params=pltpu.CompilerParams(
            dimension_semantics=("arbitrary",), vmem_limit_bytes=VMEM_LIMIT),
        name="gla",
    )(qkvr, qkvr, qkvr, qkvr, lg, gn, tril, esel)


def _s5_tables(a_re, a_im, b_re, b_im, c_re, c_im, d_skip, log_step):
    L, G, P, N = S5_L, S5_G, S5_P, S5_N
    delta = jnp.exp(log_step)[:, None]
    ar, ai = a_re * delta, a_im * delta

    def apow(tau):
        tau = jnp.asarray(tau, F32)[None, :, None]
        mag = jnp.exp(ar[:, None, :] * tau)
        ph = ai[:, None, :] * tau
        return mag * jnp.cos(ph), mag * jnp.sin(ph)

    p_re, p_im = apow(jnp.arange(L + 1))
    ab_re, ab_im = p_re[:, 1], p_im[:, 1]
    den = a_re * a_re + a_im * a_im
    cf_re = ((ab_re - 1.0) * a_re + ab_im * a_im) / den
    cf_im = (ab_im * a_re - (ab_re - 1.0) * a_im) / den
    bb_re = cf_re[:, :, None] * b_re - cf_im[:, :, None] * b_im
    bb_im = cf_re[:, :, None] * b_im + cf_im[:, :, None] * b_re
    e_re = p_re[:, :L, :, None] * bb_re[:, None] - p_im[:, :L, :, None] * bb_im[:, None]
    e_im = p_re[:, :L, :, None] * bb_im[:, None] + p_im[:, :L, :, None] * bb_re[:, None]
    kk = (jnp.einsum('gnp,gtpm->gtnm', c_re, e_re, precision=HIGHEST)
          - jnp.einsum('gnp,gtpm->gtnm', c_im, e_im, precision=HIGHEST))
    kk = kk.at[:, 0].add(jnp.eye(N, dtype=F32)[None] * d_skip[:, :, None])
    r_re, r_im = apow(L - 1 - jnp.arange(L))
    bs_re = (r_re[:, :, None, :] * bb_re.transpose(0, 2, 1)[:, None] - r_im[:, :, None, :] * bb_im.transpose(0, 2, 1)[:, None])
    bs_im = (r_re[:, :, None, :] * bb_im.transpose(0, 2, 1)[:, None] + r_im[:, :, None, :] * bb_re.transpose(0, 2, 1)[:, None])
    ct_re = c_re.transpose(0, 2, 1)[:, :, None, :]
    ct_im = c_im.transpose(0, 2, 1)[:, :, None, :]
    q_re = p_re[:, 1:].transpose(0, 2, 1)[:, :, :, None]
    q_im = p_im[:, 1:].transpose(0, 2, 1)[:, :, :, None]
    cre = ct_re * q_re - ct_im * q_im
    cim = -(ct_re * q_im + ct_im * q_re)
    GT = S5_GT
    nt = G // GT

    def tile_groups(w):
        return w.reshape((nt, GT) + w.shape[1:])

    kc = tile_groups(kk).transpose(0, 1, 4, 2, 3).reshape(nt, GT * N, L * N)

    def in_table(w):
        return tile_groups(w).transpose(0, 2, 1, 3, 4).reshape(nt, L * GT * N, P)

    def out_table(w):
        return w.reshape(nt, GT * P, L * N)

    nstep = (T // L - 1).bit_length()
    sc_re, sc_im = apow(jnp.asarray([L * 2 ** i for i in range(nstep)]))

    def lanes(w):
        return tile_groups(w).transpose(0, 2, 1, 3).reshape(nt, w.shape[1], GT * P)

    r1 = jnp.arange(L * N)[:, None]
    c1 = jnp.arange(L * GT * N)[None, :]
    wide_tn = ((r1 // N == c1 // (GT * N)) & (r1 % N == c1 % N)).astype(BF16)
    wide_p = (jnp.arange(P)[:, None] == jnp.arange(GT * P)[None, :] % P).astype(BF16)
    return dict(
        kc=kc.astype(BF16),
        bs_re=in_table(bs_re).astype(BF16), bs_im=in_table(bs_im).astype(BF16),
        cre=out_table(cre).astype(BF16), cim=out_table(cim).astype(BF16),
        sc_re=lanes(sc_re), sc_im=lanes(sc_im), wide_tn=wide_tn, wide_p=wide_p,
    )


def _s5_kernel(u_ref, kc_ref, bsr_ref, bsi_ref, cre_ref, cim_ref, scr_ref, sci_ref, wtn_ref, wp_ref,
               y_ref, m_ref):
    L, N, P = S5_L, S5_N, S5_P
    W = S5_GT * N
    nch = u_ref.shape[0] // L

    @pl.when(pl.program_id(0) == 0)
    def _():
        m_ref[...] = jnp.zeros_like(m_ref)

    def widen(compact, wide, row_group, col_group):
        full = _dot(compact, wide)
        r = lax.broadcasted_iota(jnp.int32, full.shape, 0)
        c = lax.broadcasted_iota(jnp.int32, full.shape, 1)
        return jnp.where(row_group(r) == col_group(c), full, 0.0).astype(BF16)

    wtn, wp = wtn_ref[...], wp_ref[...]
    bd = widen(kc_ref[0], wtn, lambda r: r // N, lambda c: (c % W) // N)
    bsr = widen(bsr_ref[0], wp, lambda r: (r % W) // N, lambda c: c // P)
    bsi = widen(bsi_ref[0], wp, lambda r: (r % W) // N, lambda c: c // P)
    cre = widen(cre_ref[0], wtn, lambda r: r // P, lambda c: (c % W) // N)
    cim = widen(cim_ref[0], wtn, lambda r: r // P, lambda c: (c % W) // N)
    for s in range(L):
        m_ref[s * W:(s + 1) * W, s * W:] = bd[:, :(L - s) * W]
    ucat = jnp.concatenate([u_ref[pl.ds(s, nch, stride=L), :].astype(BF16) for s in range(L)], axis=1)
    hr = _dot(ucat, bsr)
    hi = _dot(ucat, bsi)
    row = lax.broadcasted_iota(jnp.int32, hr.shape, 0)

    def shift(x, d):
        if d % 8 == 0:
            return jnp.concatenate([jnp.zeros((d, x.shape[1]), F32), x[:nch - d]], axis=0)
        return jnp.where(row >= d, pltpu.roll(x, d, 0), 0.0)

    for i in range(scr_ref.shape[1]):
        mr, mi = scr_ref[0, i:i + 1, :], sci_ref[0, i:i + 1, :]
        pr, pi = shift(hr, 2 ** i), shift(hi, 2 ** i)
        hr, hi = hr + mr * pr - mi * pi, hi + mr * pi + mi * pr
    gr, gi = shift(hr, 1), shift(hi, 1)
    grb, gib = gr.astype(BF16), gi.astype(BF16)
    for tp in range(L // 2):
        cols = slice(2 * tp * W, (2 * tp + 2) * W)
        depth = (2 * tp + 2) * W
        y = _dot(ucat[:, :depth], m_ref[:depth, cols]) + _dot(grb, cre[:, cols]) + _dot(gib, cim[:, cols])
        for t in (2 * tp, 2 * tp + 1):
            y_ref[pl.ds(t, nch, stride=L), :] = y[:, (t - 2 * tp) * W:(t - 2 * tp + 1) * W]


def _s5(u, tb):
    W = S5_GT * S5_N
    nt = S5_G // S5_GT

    def tile_spec(a):
        return pl.BlockSpec((1,) + a.shape[1:], lambda g: (g,) + (0,) * (a.ndim - 1))

    ws = [tb[n] for n in ["kc", "bs_re", "bs_im", "cre", "cim", "sc_re", "sc_im"]]
    consts = [tb["wide_tn"], tb["wide_p"]]
    return pl.pallas_call(
        _s5_kernel,
        grid=(nt,),
        in_specs=([pl.BlockSpec((T, W), lambda g: (0, g))] + [tile_spec(w) for w in ws]
                  + [_const_spec(c.shape) for c in consts]),
        out_specs=pl.BlockSpec((T, W), lambda g: (0, g)),
        out_shape=jax.ShapeDtypeStruct((T, S5_G * S5_N), F32),
        scratch_shapes=[pltpu.VMEM((S5_L * W, S5_L * W), BF16)],
        compiler_params=pltpu.CompilerParams(
            dimension_semantics=("arbitrary",), vmem_limit_bytes=VMEM_LIMIT),
        name="s5",
    )(u, *ws, *consts)


def _post_kernel(y_ref, gla_ref, x_ref, gw_ref, gb_ref, sg_ref, wo_ref, n2_ref, rw_ref, rb_ref, before_ref,
                 h_ref, hn_ref, idx_ref, gate_ref, rank_ref, cnt_ref, run_ref):
    @pl.when(pl.program_id(0) == 0)
    def _():
        run_ref[...] = jnp.zeros_like(run_ref)

    y = y_ref[...]
    z = 0.5 * y * (1.0 + jnp.tanh(math.sqrt(2.0 / math.pi) * (y + 0.044715 * (y * y * y))))
    z = z * _sigmoid(_dot(z.astype(BF16), gw_ref[...]) + gb_ref[...])
    s5o = _rms(z, sg_ref[...]).astype(BF16)
    half = GLA_H * GLA_DV
    h = x_ref[...] + _dot(gla_ref[...], wo_ref[:half, :]) + _dot(s5o, wo_ref[half:, :])
    h_ref[...] = h
    hn = _rms(h, n2_ref[...])
    hi = hn.astype(BF16)
    hn_ref[...] = _pack_halves(hi[:, :D // 2], hi[:, D // 2:])
    lo = (hn - hi.astype(F32)).astype(BF16)
    rw = rw_ref[...]
    rw_hi = rw.astype(BF16)
    rw_lo = (rw - rw_hi.astype(F32)).astype(BF16)
    lg = _dot(hi, rw_hi) + (_dot(lo, rw_hi) + _dot(hi, rw_lo))
    lt = lg.T[:N_EXP] + rb_ref[...]
    eid = lax.broadcasted_iota(jnp.int32, lt.shape, 0).astype(F32)
    vals, idxs = [], []
    for _ in range(TOP_K):
        m = jnp.max(lt, axis=0, keepdims=True)
        sel = jnp.min(jnp.where(lt == m, eid, float(N_EXP)), axis=0, keepdims=True)
        vals.append(m)
        idxs.append(sel)
        lt = jnp.where(eid == sel, -jnp.inf, lt)
    ex = [jnp.exp(vv - vals[0]) for vv in vals]
    inv = 1.0 / (ex[0] + ex[1] + ex[2] + ex[3])
    idx_ref[...] = jnp.concatenate(idxs, axis=0).astype(jnp.int32)
    gate_ref[...] = jnp.concatenate([e * inv for e in ex], axis=0)
    run = run_ref[:, :1]
    ranks = []
    for sel in idxs:
        onehot = jnp.where(eid == sel, 1.0, 0.0)
        earlier = _dot(onehot.astype(BF16), before_ref[...])
        ranks.append(jnp.sum(onehot * (run + earlier), axis=0, keepdims=True))
        run = run + jnp.sum(onehot, axis=1, keepdims=True)
    rank_ref[...] = jnp.concatenate(ranks, axis=0).astype(jnp.int32)
    run_ref[...] = jnp.broadcast_to(run, run_ref.shape)
    cnt_ref[...] = jnp.broadcast_to(run, cnt_ref.shape)


def _post(y, gla, x, gw, gb, sg, wo, n2, rwt, rb):
    tm = 256
    width = S5_G * S5_N
    before = (jnp.arange(tm)[:, None] < jnp.arange(tm)[None, :]).astype(BF16)
    return pl.pallas_call(
        _post_kernel,
        grid=(T // tm,),
        in_specs=[
            pl.BlockSpec((tm, width), lambda i: (i, 0)),
            pl.BlockSpec((tm, GLA_H * GLA_DV), lambda i: (i, 0)),
            pl.BlockSpec((tm, D), lambda i: (i, 0)),
            _const_spec(gw.shape), _const_spec(gb.shape), _const_spec(sg.shape), _const_spec(wo.shape),
            _const_spec(n2.shape), _const_spec(rwt.shape), _const_spec(rb.shape), _const_spec(before.shape),
        ],
        out_specs=[
            pl.BlockSpec((tm, D), lambda i: (i, 0)),
            pl.BlockSpec((tm, D // 2), lambda i: (i, 0)),
            pl.BlockSpec((TOP_K, tm), lambda i: (0, i)),
            pl.BlockSpec((TOP_K, tm), lambda i: (0, i)),
            pl.BlockSpec((TOP_K, tm), lambda i: (0, i)),
            pl.BlockSpec((N_EXP, 128), lambda i: (0, 0)),
        ],
        out_shape=[
            jax.ShapeDtypeStruct((T, D), F32),
            jax.ShapeDtypeStruct((T, D // 2), jnp.int32),
            jax.ShapeDtypeStruct((TOP_K, T), jnp.int32),
            jax.ShapeDtypeStruct((TOP_K, T), F32),
            jax.ShapeDtypeStruct((TOP_K, T), jnp.int32),
            jax.ShapeDtypeStruct((N_EXP, 128), F32),
        ],
        scratch_shapes=[pltpu.VMEM((N_EXP, 128), F32)],
        compiler_params=pltpu.CompilerParams(
            dimension_semantics=("arbitrary",), vmem_limit_bytes=VMEM_LIMIT),
        name="post",
    )(y, gla, x, gw, gb, sg, wo, n2, rwt, rb, before)


def _route(top_idx, rank, counts):
    e_flat = top_idx.reshape(-1)
    rank = rank.reshape(-1)
    counts = counts[:, 0].astype(jnp.int32)
    padded = (counts + MOE_RB - 1) // MOE_RB * MOE_RB
    pad_end = jnp.cumsum(padded)
    pad_start = pad_end - padded
    dest = _dest(top_idx, rank.reshape(top_idx.shape), pad_start.astype(jnp.int32)).reshape(-1)
    nvis_e = (counts + MOE_RMAX - 1) // MOE_RMAX
    vis_end = jnp.cumsum(nvis_e)
    n_vis = vis_end[-1]
    v = jnp.arange(MOE_NV, dtype=jnp.int32)
    vc = jnp.minimum(v, n_vis - 1)
    ve = jnp.minimum(jnp.searchsorted(vis_end, vc, side='right'), N_EXP - 1).astype(jnp.int32)
    local = vc - (vis_end[ve] - nvis_e[ve])
    vcnt = jnp.where(v < n_vis, jnp.minimum(MOE_RMAX, counts[ve] - local * MOE_RMAX), 0)
    vrow = pad_start[ve] + local * MOE_RMAX
    tok = (jnp.argsort(dest) % T).astype(jnp.int32)
    vtok = (jnp.cumsum(counts) - counts)[ve] + local * MOE_RMAX
    return dict(dest=dest, tok=tok, tail_blk=(pad_end[-1:] // MOE_RB).astype(jnp.int32),
                vis_e=ve, vis_cnt=vcnt.astype(jnp.int32), vis_row=vrow.astype(jnp.int32),
                vis_tok=vtok.astype(jnp.int32), n_vis=n_vis.reshape(1).astype(jnp.int32))


def _dest_kernel(start_ref, idx_ref, rank_ref, dest_ref):
    e = idx_ref[...]
    d = rank_ref[...]
    for x in range(N_EXP):
        d = d + jnp.where(e == x, start_ref[x], 0)
    dest_ref[...] = d


def _dest(top_idx, rank, pad_start):
    full = pl.BlockSpec(top_idx.shape, lambda i, *_: (0, 0))
    return pl.pallas_call(
        _dest_kernel,
        grid_spec=pltpu.PrefetchScalarGridSpec(num_scalar_prefetch=1, grid=(1,), in_specs=[full, full], out_specs=full),
        out_shape=jax.ShapeDtypeStruct(top_idx.shape, jnp.int32),
        name="dest",
    )(pad_start, top_idx, rank)


def _row_copy(src_ref, srow, dst_ref, drow, sem):
    return pltpu.make_async_copy(src_ref.at[pl.ds(srow, 1), :], dst_ref.at[pl.ds(drow, 1), :], sem)


def _tail_fill(src_ref, dst_ref, tail_ref, sem):
    def cp(b):
        return pltpu.make_async_copy(src_ref.at[pl.ds(0, MOE_RB), :],
                                     dst_ref.at[pl.ds(pl.multiple_of(b * MOE_RB, MOE_RB), MOE_RB), :], sem)

    def start(b, c):
        cp(b).start()
        return c
    lax.fori_loop(tail_ref[0], MOE_ROWS // MOE_RB, start, 0)

    def finish(b, c):
        cp(b).wait()
        return c
    lax.fori_loop(tail_ref[0], MOE_ROWS // MOE_RB, finish, 0)


def _experts_kernel(ve_ref, vcnt_ref, vrow_ref, vtok_ref, nvis_ref, tail_ref, tok_ref, hn_ref, w1g_ref, w1l_ref,
                    w2_ref, b1g_ref, b1l_ref, b2_ref, ys_ref, xst_ref, xb_ref, acc_ref, gsem, osem):
    v = pl.program_id(0)
    j = pl.program_id(1)
    RB = MOE_RB
    BIG, MID = 4 * RB, 2 * RB
    NQ = MOE_NQ
    nvis = nvis_ref[0]
    active = v < nvis
    first = j == 0
    last = j == MOE_NF - 1

    def geometry(vv):
        nblk = (vcnt_ref[vv] + RB - 1) // RB
        return nblk // 4, (nblk // 2) % 2, nblk % 2

    nbig, mid, small = geometry(v)
    units = nbig + mid + small
    row0 = vrow_ref[v]
    cnt = vcnt_ref[v]
    mid_r0 = nbig * BIG
    small_r0 = mid_r0 + mid * MID
    has_next = v + 1 < nvis
    vn = jnp.minimum(v + 1, MOE_NV - 1)
    cnt_next = jnp.where(has_next, vcnt_ref[vn], 0)
    tok0_next = vtok_ref[vn]

    def gather_row(tok0, c, q):
        return pltpu.make_async_copy(hn_ref.at[pl.ds(tok_ref[tok0 + q], 1), :], xst_ref.at[c, pl.ds(q, 1), :], gsem)

    def y_copy(r0, n):
        return pltpu.make_async_copy(acc_ref.at[pl.ds(pl.multiple_of(r0, RB), n), :],
                                     ys_ref.at[pl.ds(pl.multiple_of(row0 + r0, RB), n), :], osem)

    @pl.when((v == 0) & first)
    def _():
        acc_ref[...] = jnp.zeros_like(acc_ref)
        xst_ref[...] = jnp.zeros_like(xst_ref)
        _tail_fill(acc_ref, ys_ref, tail_ref, osem)

    @pl.when(active & first)
    def _():
        pu = jnp.maximum(v - 1, 0)
        pb, pm, ps = geometry(pu)
        done = jnp.where(v > 0, jnp.minimum(cnt, MOE_NF * (pb + pm + ps) * NQ), 0)

        def request(i, c):
            blk = i // NQ
            gather_row(vtok_ref[v] + blk * NQ, blk, i - blk * NQ).start()
            return c
        lax.fori_loop(done, cnt, request, 0)

        def arrived_block(b, c):
            pltpu.make_async_copy(hn_ref.at[pl.ds(0, NQ), :], xst_ref.at[0], gsem).wait()
            return c
        lax.fori_loop(0, cnt // NQ, arrived_block, 0)

        def arrived_row(i, c):
            gather_row(vtok_ref[v], 0, 0).wait()
            return c
        lax.fori_loop(0, cnt % NQ, arrived_row, 0)

        def unpack(b, c):
            rows = pl.ds(pl.multiple_of(b * RB, RB), RB)
            valid = (b * RB + lax.broadcasted_iota(jnp.int32, (RB, D // 2), 0)) < cnt
            lo_half, hi_half = _unpack_halves(jnp.where(valid, xst_ref[b], 0))
            xb_ref[rows, :D // 2] = lo_half
            xb_ref[rows, D // 2:] = hi_half
            return c
        lax.fori_loop(0, (cnt + RB - 1) // RB, unpack, 0)

    def block(u, r0, n):
        rows = pl.ds(pl.multiple_of(r0, RB), n)
        piece = j * units + u
        left = cnt_next - piece * NQ
        tok0 = tok0_next + piece * NQ
        for q in range(NQ):
            @pl.when(q < left)
            def _():
                gather_row(tok0, piece, q).start()

        x = xb_ref[rows, :]
        glu = jnp.minimum(_dot(x, w1g_ref[0].astype(BF16)) + b1g_ref[0], SWIGLU_LIMIT)
        lin = jnp.clip(_dot(x, w1l_ref[0].astype(BF16)) + b1l_ref[0], -SWIGLU_LIMIT, SWIGLU_LIMIT)
        act = glu * _sigmoid(SWIGLU_ALPHA * glu) * (lin + 1.0)
        part = _dot(act.astype(BF16), w2_ref[0].astype(BF16))

        acc_ref[rows, :] = part + jnp.where(first, b2_ref[0], acc_ref[rows, :])

        @pl.when(last)
        def _():
            y_copy(r0, n).start()

    @pl.when(active)
    def _():
        def big(i, c):
            block(i, i * BIG, BIG)
            return c
        lax.fori_loop(0, nbig, big, 0)

        @pl.when(mid == 1)
        def _():
            block(nbig, mid_r0, MID)

        @pl.when(small == 1)
        def _():
            block(nbig + mid, small_r0, RB)

    @pl.when(active & last)
    def _():
        def finish(i, c):
            y_copy(i * BIG, BIG).wait()
            return c
        lax.fori_loop(0, nbig, finish, 0)

        @pl.when(mid == 1)
        def _():
            y_copy(mid_r0, MID).wait()

        @pl.when(small == 1)
        def _():
            y_copy(small_r0, RB).wait()


def _experts(hn, rt, w1, b1, w2, b2):
    b1 = b1.reshape(N_EXP, 1, 2 * D_FF)
    b2 = b2.reshape(N_EXP, 1, D)

    def jj(v, j, ve, vcnt, vrow, vtok, nvis, tail, tok):
        return jnp.where(v < nvis[0], j, MOE_NF - 1)

    return pl.pallas_call(
        _experts_kernel,
        grid_spec=pltpu.PrefetchScalarGridSpec(
            num_scalar_prefetch=7,
            grid=(MOE_NV, MOE_NF),
            in_specs=[
                pl.BlockSpec(memory_space=pl.ANY),
                pl.BlockSpec((1, D, MOE_TF), lambda v, j, ve, *s: (ve[v], 0, jj(v, j, ve, *s))),
                pl.BlockSpec((1, D, MOE_TF), lambda v, j, ve, *s: (ve[v], 0, MOE_NF + jj(v, j, ve, *s))),
                pl.BlockSpec((1, MOE_TF, D), lambda v, j, ve, *s: (ve[v], jj(v, j, ve, *s), 0)),
                pl.BlockSpec((1, 1, MOE_TF), lambda v, j, ve, *s: (ve[v], 0, jj(v, j, ve, *s))),
                pl.BlockSpec((1, 1, MOE_TF), lambda v, j, ve, *s: (ve[v], 0, MOE_NF + jj(v, j, ve, *s))),
                pl.BlockSpec((1, 1, D), lambda v, j, ve, *s: (ve[v], 0, 0)),
            ],
            out_specs=pl.BlockSpec(memory_space=pl.ANY),
            scratch_shapes=[
                pltpu.VMEM((MOE_RMAX // MOE_NQ, MOE_NQ, D // 2), jnp.int32),
                pltpu.VMEM((MOE_RMAX, D), BF16),
                pltpu.VMEM((MOE_RMAX, D), F32),
                pltpu.SemaphoreType.DMA,
                pltpu.SemaphoreType.DMA,
            ],
        ),
        out_shape=jax.ShapeDtypeStruct((MOE_ROWS, D), F32),
        compiler_params=pltpu.CompilerParams(
            dimension_semantics=("arbitrary", "arbitrary"), vmem_limit_bytes=VMEM_LIMIT),
        name="experts",
    )(rt["vis_e"], rt["vis_cnt"], rt["vis_row"], rt["vis_tok"], rt["n_vis"], rt["tail_blk"], rt["tok"],
      hn, w1, w1, w2, b1, b1, b2)


def _combine_kernel(dest_ref, ys_ref, h_ref, gate_ref, gf_ref, o_ref, buf_ref, sem):
    i = pl.program_id(0)
    tm = h_ref.shape[0]

    def gather(tile, slot):
        def issue(t, c):
            for k in range(TOP_K):
                _row_copy(ys_ref, dest_ref[k * T + tile * tm + t], buf_ref.at[slot, k], t,
                          sem.at[slot]).start(priority=k % 2)
            return c
        lax.fori_loop(0, tm, issue, 0)

    @pl.when(i == 0)
    def _():
        gather(0, 0)

    @pl.when(i + 1 < pl.num_programs(0))
    def _():
        gather(i + 1, (i + 1) % 2)

    slot = i % 2
    for k in range(TOP_K):
        pltpu.make_async_copy(ys_ref.at[pl.ds(0, tm), :], buf_ref.at[slot, k], sem.at[slot]).wait()

    h = h_ref[...]
    for k in range(TOP_K):
        h = h + gate_ref[:, k:k + 1] * buf_ref[slot, k]
    o_ref[...] = _rms(h, gf_ref[...])


def _combine(ys, h, gates_t, gf, rt):
    tm = 256
    return pl.pallas_call(
        _combine_kernel,
        grid_spec=pltpu.PrefetchScalarGridSpec(
            num_scalar_prefetch=1,
            grid=(T // tm,),
            in_specs=[
                pl.BlockSpec(memory_space=pl.ANY),
                pl.BlockSpec((tm, D), lambda i, *_: (i, 0)),
                pl.BlockSpec((tm, TOP_K), lambda i, *_: (i, 0)),
                pl.BlockSpec((1, D), lambda i, *_: (0, 0)),
            ],
            out_specs=pl.BlockSpec((tm, D), lambda i, *_: (i, 0)),
            scratch_shapes=[pltpu.VMEM((2, TOP_K, tm, D), F32), pltpu.SemaphoreType.DMA((2,))],
        ),
        out_shape=jax.ShapeDtypeStruct((T, D), F32),
        compiler_params=pltpu.CompilerParams(
            dimension_semantics=("arbitrary",), vmem_limit_bytes=VMEM_LIMIT),
        name="combine",
    )(rt["dest"], ys, h, gates_t, gf)


def kernel(x, norm1_g, w_in, gla_gate_w2, gla_gate_b, gla_norm_g, s5_a_re, s5_a_im, s5_b_re, s5_b_im, s5_c_re, s5_c_im, s5_d, s5_log_step, s5_glu_w, s5_glu_b, s5_norm_g, w_out, norm2_g, router_w, router_b, expert_w1, expert_b1, expert_w2, expert_b2, final_norm_g):
    assert x.shape == (1, T, D) and w_in.shape[0] == 1
    xt = x.reshape(T, D)
    nqkvr = 2 * GLA_H * GLA_DK + 2 * GLA_H * GLA_DV
    wi = w_in[0]
    wq = wi[:, :nqkvr].astype(BF16)
    wg = jnp.pad(wi[:, nqkvr:nqkvr + GATE_RANK], ((0, 0), (0, 128 - GATE_RANK))).astype(BF16)
    wu = wi[:, nqkvr + GATE_RANK:].astype(BF16)
    w2g = jnp.pad(gla_gate_w2[0], ((0, 128 - GATE_RANK), (0, 0)))
    qkvr, lg, u = _proj(xt, norm1_g, wq, wg, wu, w2g, gla_gate_b)
    gla = _gla(qkvr, lg, gla_norm_g)
    tables = _s5_tables(s5_a_re[0], s5_a_im[0], s5_b_re[0], s5_b_im[0], s5_c_re[0], s5_c_im[0],
                        s5_d[0], s5_log_step[0])
    y = _s5(u, tables)
    h, hn, top_idx, gates, rank, counts = _post(
        y, gla, xt, s5_glu_w[0].astype(BF16), s5_glu_b, s5_norm_g, w_out[0].astype(BF16), norm2_g,
        jnp.pad(router_w[0], ((0, 0), (0, 128 - N_EXP))), router_b.reshape(N_EXP, 1))
    rt = _route(top_idx, rank, counts)
    ys = _experts(hn, rt, expert_w1[0], expert_b1[0], expert_w2[0], expert_b2[0])
    out = _combine(ys, h, gates.T, final_norm_g.reshape(1, D), rt)
    return out.reshape(1, T, D)
```

```python
import functools
import math

import jax
import jax.numpy as jnp
from jax import lax
from jax.experimental import pallas as pl
from jax.experimental.pallas import tpu as pltpu

F32 = jnp.float32
BF16 = jnp.bfloat16
HIGHEST = lax.Precision.HIGHEST

T = 8192
D = 2048
GLA_H = 4
GLA_DK = 128
GLA_DV = 256
GLA_CHUNK = 64
GLA_SUB = 16
GATE_RANK = 16
GATE_TAU = 16.0
S5_G = 64
S5_N = 16
S5_P = 64
S5_L = 16
S5_GT = 8
N_EXP = 32
TOP_K = 4
D_FF = 2048
SWIGLU_ALPHA = 1.702
SWIGLU_LIMIT = 7.0
RMS_EPS = 1e-6

MOE_RB = 128
MOE_RMAX = 1536
MOE_TF = 512
MOE_NF = D_FF // MOE_TF
MOE_NV = -(-T * TOP_K // MOE_RMAX) + N_EXP
MOE_ROWS = T * TOP_K + N_EXP * MOE_RB
VMEM_LIMIT = 56 * 1024 * 1024

NN = (((1,), (0,)), ((), ()))
NT = (((1,), (1,)), ((), ()))
TN = (((0,), (0,)), ((), ()))


def _dot(a, b, dims=NN, precision=None):
    return lax.dot_general(a, b, dims, preferred_element_type=F32, precision=precision)


def _dot3(a, b):
    a_hi, b_hi = a.astype(BF16), b.astype(BF16)
    a_lo = (a - a_hi.astype(F32)).astype(BF16)
    b_lo = (b - b_hi.astype(F32)).astype(BF16)
    return _dot(a_hi, b_hi) + (_dot(a_lo, b_hi) + _dot(a_hi, b_lo))


def _rms(x, g):
    return x * lax.rsqrt(jnp.mean(x * x, axis=-1, keepdims=True) + RMS_EPS) * g


def _sigmoid(x):
    return 1.0 / (1.0 + jnp.exp(-x))


def _pack_halves(lo_half, hi_half):
    a = lax.bitcast_convert_type(lo_half.astype(F32), jnp.int32)
    b = lax.bitcast_convert_type(hi_half.astype(F32), jnp.int32)
    return a | lax.shift_right_logical(b, 16)


def _unpack_halves(p):
    a = lax.bitcast_convert_type(p & jnp.int32(-65536), F32).astype(BF16)
    b = lax.bitcast_convert_type(lax.shift_left(p, 16), F32).astype(BF16)
    return a, b


def _const_spec(shape):
    nd = len(shape)
    return pl.BlockSpec(shape, lambda *_: (0,) * nd, pipeline_mode=pl.Buffered(1))


PROJ_QKVR = 2 * GLA_H * GLA_DK + 2 * GLA_H * GLA_DV
PROJ_TAIL = GATE_RANK + S5_G * S5_N
PROJ_TAIL_PAD = -(-PROJ_TAIL // 128) * 128
PROJ_CHUNK = 512


def _proj_kernel(x_ref, g1_ref, win_ref, w2_ref, gb_ref, qkvr_ref, lg_ref, u_ref,
                 wq_ref, wt_ref, stage_ref, tail_ref, sem, tsem):
    @pl.when(pl.program_id(0) == 0)
    def _():
        nchunk = PROJ_QKVR // PROJ_CHUNK

        def piece(c):
            cols = pl.ds(c * PROJ_CHUNK, PROJ_CHUNK)
            return pltpu.make_async_copy(win_ref.at[0, :, cols], stage_ref.at[c % 2], sem.at[c % 2])

        tail = pltpu.make_async_copy(win_ref.at[0, :, pl.ds(PROJ_QKVR, PROJ_TAIL)], tail_ref, tsem)
        tail.start()
        piece(0).start()
        for c in range(nchunk):
            piece(c).wait()
            if c + 1 < nchunk:
                piece(c + 1).start()
            wq_ref[:, c * PROJ_CHUNK:(c + 1) * PROJ_CHUNK] = stage_ref[c % 2].astype(BF16)
        tail.wait()
        wt_ref[...] = jnp.zeros_like(wt_ref)
        wt_ref[:, :PROJ_TAIL] = tail_ref[...].astype(BF16)

    hb = _rms(x_ref[...], g1_ref[...]).astype(BF16)
    p = _dot(hb, wq_ref[...])
    nq = GLA_H * GLA_DK
    qkvr_ref[:, :nq] = (p[:, :nq] * (GLA_DK ** -0.5)).astype(BF16)
    qkvr_ref[:, nq:] = p[:, nq:].astype(BF16)
    gu = _dot(hb, wt_ref[...])
    z = _dot3(gu[:, :128], w2_ref[...]) + gb_ref[...]
    lg_ref[...] = (jnp.minimum(z, 0.0) - jnp.log(1.0 + jnp.exp(-jnp.abs(z)))) * (1.0 / GATE_TAU)
    u_ref[...] = gu[:, GATE_RANK:PROJ_TAIL]


def _proj(x, g1, w_in, w2, gb):
    tm = 256
    return pl.pallas_call(
        _proj_kernel,
        grid=(T // tm,),
        in_specs=[
            pl.BlockSpec((tm, D), lambda i: (i, 0)),
            _const_spec((1, D)),
            pl.BlockSpec(memory_space=pl.ANY),
            _const_spec(w2.shape), _const_spec(gb.shape),
        ],
        out_specs=[
            pl.BlockSpec((tm, PROJ_QKVR), lambda i: (i, 0)),
            pl.BlockSpec((tm, GLA_H * GLA_DK), lambda i: (i, 0)),
            pl.BlockSpec((tm, S5_G * S5_N), lambda i: (i, 0)),
        ],
        out_shape=[
            jax.ShapeDtypeStruct((T, PROJ_QKVR), BF16),
            jax.ShapeDtypeStruct((T, GLA_H * GLA_DK), F32),
            jax.ShapeDtypeStruct((T, S5_G * S5_N), F32),
        ],
        scratch_shapes=[
            pltpu.VMEM((D, PROJ_QKVR), BF16),
            pltpu.VMEM((D, PROJ_TAIL_PAD), BF16),
            pltpu.VMEM((2, D, PROJ_CHUNK), F32),
            pltpu.VMEM((D, PROJ_TAIL), F32),
            pltpu.SemaphoreType.DMA((2,)),
            pltpu.SemaphoreType.DMA,
        ],
        compiler_params=pltpu.CompilerParams(
            dimension_semantics=("arbitrary",), vmem_limit_bytes=VMEM_LIMIT),
        name="proj",
    )(x, g1, w_in, w2, gb)


def _gla_kernel(q_ref, k_ref, v_ref, r_ref, lg_ref, gn_ref, tril_ref, esel_ref, o_ref, state_ref):
    C, S = GLA_CHUNK, GLA_SUB
    nsub = C // S

    @pl.when(pl.program_id(0) == 0)
    def _():
        state_ref[...] = jnp.zeros_like(state_ref)

    row = lax.broadcasted_iota(jnp.int32, (C, C), 0)
    col = lax.broadcasted_iota(jnp.int32, (C, C), 1)
    below = (col // S) < (row // S)
    diag = ((col // S) == (row // S)) & (col <= row)
    tril = tril_ref[...]
    esel = esel_ref[...]
    gn = gn_ref[...]

    def head(h, rows, b):
        kcols = slice(h * GLA_DK, (h + 1) * GLA_DK)
        vcols = slice(h * GLA_DV, (h + 1) * GLA_DV)
        q = q_ref[rows, kcols].astype(F32)
        k = k_ref[rows, kcols].astype(F32)
        v = v_ref[rows, vcols]
        bl = b[C - 1:C, :]
        st = state_ref[h]
        o = _dot((q * jnp.exp2(b)).astype(BF16), st.astype(BF16), NT)
        s_rows = [jnp.zeros((S, C), F32)]
        for j in range(1, nsub):
            ref = b[j * S - 1:j * S, :]
            qj = (q[j * S:(j + 1) * S] * jnp.exp2(b[j * S:(j + 1) * S] - ref)).astype(BF16)
            kj = (k * jnp.exp2(jnp.minimum(ref - b, 0.0))).astype(BF16)
            s_rows.append(_dot(qj, kj, NT))
        s_off = jnp.concatenate(s_rows, axis=0)
        q3 = q.reshape(nsub, S, GLA_DK)
        k3 = k.reshape(nsub, S, GLA_DK)
        b3 = b.reshape(nsub, S, GLA_DK)
        xs = []
        for s in range(S):
            lo = (s // 8) * 8
            dec = jnp.exp2(jnp.minimum(b3[:, lo:] - b3[:, s:s + 1, :], 0.0))
            x = q3[:, lo:] * k3[:, s:s + 1, :] * dec
            if lo:
                x = jnp.concatenate([jnp.zeros((nsub, lo, GLA_DK), F32), x], axis=1)
            xs.append(x.reshape(C, GLA_DK).astype(BF16))
        dsc = _dot(jnp.concatenate(xs, axis=1), esel)
        a = jnp.where(below, s_off, jnp.where(diag, dsc, 0.0))
        o = o + _dot(a.astype(BF16), v)
        kout = (k * jnp.exp2(bl - b)).astype(BF16)
        state_ref[h] = st * jnp.exp2(bl) + _dot(v, kout, TN)
        y = _rms(o, gn)
        r = r_ref[rows, vcols].astype(F32)
        o_ref[rows, vcols] = (y * (r * _sigmoid(r))).astype(BF16)

    def chunk(c, carry):
        rows = pl.ds(pl.multiple_of(c * C, C), C)
        g = lg_ref[rows, :]
        g_hi = g.astype(BF16)
        g_mid = (g - g_hi.astype(F32)).astype(BF16)
        g_lo = (g - g_hi.astype(F32) - g_mid.astype(F32)).astype(BF16)
        b = (_dot(tril, g_hi) + (_dot(tril, g_mid) + _dot(tril, g_lo))) * math.log2(math.e)
        for h in range(GLA_H):
            head(h, rows, b[:, h * GLA_DK:(h + 1) * GLA_DK])
        return carry

    lax.fori_loop(0, q_ref.shape[0] // C, chunk, 0, unroll=8)


def _gla(qkvr, lg, gn):
    tb = 512
    C, S = GLA_CHUNK, GLA_SUB
    nk, nv = GLA_H * GLA_DK, GLA_H * GLA_DV
    tril = jnp.tril(jnp.ones((C, C), BF16))
    esel = (jnp.arange(S * GLA_DK)[:, None] // GLA_DK == jnp.arange(C)[None, :] % S).astype(BF16)
    return pl.pallas_call(
        _gla_kernel,
        grid=(T // tb,),
        in_specs=[
            pl.BlockSpec((tb, nk), lambda i: (i, 0)),
            pl.BlockSpec((tb, nk), lambda i: (i, 1)),
            pl.BlockSpec((tb, nv), lambda i: (i, 1)),
            pl.BlockSpec((tb, nv), lambda i: (i, 2)),
            pl.BlockSpec((tb, nk), lambda i: (i, 0)),
            _const_spec((1, GLA_DV)), _const_spec((C, C)), _const_spec((S * GLA_DK, C)),
        ],
        out_specs=pl.BlockSpec((tb, nv), lambda i: (i, 0)),
        out_shape=jax.ShapeDtypeStruct((T, nv), BF16),
        scratch_shapes=[pltpu.VMEM((GLA_H, GLA_DV, GLA_DK), F32)],
        compiler_params=pltpu.CompilerParams(
            dimension_semantics=("arbitrary",), vmem_limit_bytes=VMEM_LIMIT),
        name="gla",
    )(qkvr, qkvr, qkvr, qkvr, lg, gn, tril, esel)


def _s5_tables(a_re, a_im, b_re, b_im, c_re, c_im, d_skip, log_step):
    L, G, P, N = S5_L, S5_G, S5_P, S5_N
    delta = jnp.exp(log_step)[:, None]
    ar, ai = a_re * delta, a_im * delta

    def apow(tau):
        tau = jnp.asarray(tau, F32)[None, :, None]
        mag = jnp.exp(ar[:, None, :] * tau)
        ph = ai[:, None, :] * tau
        return mag * jnp.cos(ph), mag * jnp.sin(ph)

    p_re, p_im = apow(jnp.arange(L + 1))
    ab_re, ab_im = p_re[:, 1], p_im[:, 1]
    den = a_re * a_re + a_im * a_im
    cf_re = ((ab_re - 1.0) * a_re + ab_im * a_im) / den
    cf_im = (ab_im * a_re - (ab_re - 1.0) * a_im) / den
    bb_re = cf_re[:, :, None] * b_re - cf_im[:, :, None] * b_im
    bb_im = cf_re[:, :, None] * b_im + cf_im[:, :, None] * b_re
    e_re = p_re[:, :L, :, None] * bb_re[:, None] - p_im[:, :L, :, None] * bb_im[:, None]
    e_im = p_re[:, :L, :, None] * bb_im[:, None] + p_im[:, :L, :, None] * bb_re[:, None]
    kk = (jnp.einsum('gnp,gtpm->gtnm', c_re, e_re, precision=HIGHEST)
          - jnp.einsum('gnp,gtpm->gtnm', c_im, e_im, precision=HIGHEST))
    kk = kk.at[:, 0].add(jnp.eye(N, dtype=F32)[None] * d_skip[:, :, None])
    r_re, r_im = apow(L - 1 - jnp.arange(L))
    bs_re = (r_re[:, :, None, :] * bb_re.transpose(0, 2, 1)[:, None] - r_im[:, :, None, :] * bb_im.transpose(0, 2, 1)[:, None])
    bs_im = (r_re[:, :, None, :] * bb_im.transpose(0, 2, 1)[:, None] + r_im[:, :, None, :] * bb_re.transpose(0, 2, 1)[:, None])
    ct_re = c_re.transpose(0, 2, 1)[:, :, None, :]
    ct_im = c_im.transpose(0, 2, 1)[:, :, None, :]
    q_re = p_re[:, 1:].transpose(0, 2, 1)[:, :, :, None]
    q_im = p_im[:, 1:].transpose(0, 2, 1)[:, :, :, None]
    cre = ct_re * q_re - ct_im * q_im
    cim = -(ct_re * q_im + ct_im * q_re)
    GT = S5_GT
    nt = G // GT

    def tile_groups(w):
        return w.reshape((nt, GT) + w.shape[1:])

    kc = tile_groups(kk).transpose(0, 1, 4, 2, 3).reshape(nt, GT * N, L * N)

    def in_table(w):
        return tile_groups(w).transpose(0, 2, 1, 3, 4).reshape(nt, L * GT * N, P)

    def out_table(w):
        return w.reshape(nt, GT * P, L * N)

    nstep = (T // L - 1).bit_length()
    sc_re, sc_im = apow(jnp.asarray([L * 2 ** i for i in range(nstep)]))

    def lanes(w):
        return tile_groups(w).transpose(0, 2, 1, 3).reshape(nt, w.shape[1], GT * P)

    r1 = jnp.arange(L * N)[:, None]
    c1 = jnp.arange(L * GT * N)[None, :]
    wide_tn = ((r1 // N == c1 // (GT * N)) & (r1 % N == c1 % N)).astype(BF16)
    wide_p = (jnp.arange(P)[:, None] == jnp.arange(GT * P)[None, :] % P).astype(BF16)
    return dict(
        kc=kc.astype(BF16),
        bs_re=in_table(bs_re).astype(BF16), bs_im=in_table(bs_im).astype(BF16),
        cre=out_table(cre).astype(BF16), cim=out_table(cim).astype(BF16),
        sc_re=lanes(sc_re), sc_im=lanes(sc_im), wide_tn=wide_tn, wide_p=wide_p,
    )


def _s5_kernel(u_ref, kc_ref, bsr_ref, bsi_ref, cre_ref, cim_ref, scr_ref, sci_ref, wtn_ref, wp_ref,
               y_ref, m_ref):
    L, N, P = S5_L, S5_N, S5_P
    W = S5_GT * N
    nch = u_ref.shape[0] // L

    @pl.when(pl.program_id(0) == 0)
    def _():
        m_ref[...] = jnp.zeros_like(m_ref)

    def widen(compact, wide, row_group, col_group):
        full = _dot(compact, wide)
        r = lax.broadcasted_iota(jnp.int32, full.shape, 0)
        c = lax.broadcasted_iota(jnp.int32, full.shape, 1)
        return jnp.where(row_group(r) == col_group(c), full, 0.0).astype(BF16)

    wtn, wp = wtn_ref[...], wp_ref[...]
    bd = widen(kc_ref[0], wtn, lambda r: r // N, lambda c: (c % W) // N)
    bsr = widen(bsr_ref[0], wp, lambda r: (r % W) // N, lambda c: c // P)
    bsi = widen(bsi_ref[0], wp, lambda r: (r % W) // N, lambda c: c // P)
    cre = widen(cre_ref[0], wtn, lambda r: r // P, lambda c: (c % W) // N)
    cim = widen(cim_ref[0], wtn, lambda r: r // P, lambda c: (c % W) // N)
    for s in range(L):
        m_ref[s * W:(s + 1) * W, s * W:] = bd[:, :(L - s) * W]
    ucat = jnp.concatenate([u_ref[pl.ds(s, nch, stride=L), :].astype(BF16) for s in range(L)], axis=1)
    hr = _dot(ucat, bsr)
    hi = _dot(ucat, bsi)
    row = lax.broadcasted_iota(jnp.int32, hr.shape, 0)

    def shift(x, d):
        if d % 8 == 0:
            return jnp.concatenate([jnp.zeros((d, x.shape[1]), F32), x[:nch - d]], axis=0)
        return jnp.where(row >= d, pltpu.roll(x, d, 0), 0.0)

    for i in range(scr_ref.shape[1]):
        mr, mi = scr_ref[0, i:i + 1, :], sci_ref[0, i:i + 1, :]
        pr, pi = shift(hr, 2 ** i), shift(hi, 2 ** i)
        hr, hi = hr + mr * pr - mi * pi, hi + mr * pi + mi * pr
    gr, gi = shift(hr, 1), shift(hi, 1)
    grb, gib = gr.astype(BF16), gi.astype(BF16)
    for tp in range(L // 2):
        cols = slice(2 * tp * W, (2 * tp + 2) * W)
        depth = (2 * tp + 2) * W
        y = _dot(ucat[:, :depth], m_ref[:depth, cols]) + _dot(grb, cre[:, cols]) + _dot(gib, cim[:, cols])
        for t in (2 * tp, 2 * tp + 1):
            y_ref[pl.ds(t, nch, stride=L), :] = y[:, (t - 2 * tp) * W:(t - 2 * tp + 1) * W]


def _s5(u, tb):
    W = S5_GT * S5_N
    nt = S5_G // S5_GT

    def tile_spec(a):
        return pl.BlockSpec((1,) + a.shape[1:], lambda g: (g,) + (0,) * (a.ndim - 1))

    ws = [tb[n] for n in ["kc", "bs_re", "bs_im", "cre", "cim", "sc_re", "sc_im"]]
    consts = [tb["wide_tn"], tb["wide_p"]]
    return pl.pallas_call(
        _s5_kernel,
        grid=(nt,),
        in_specs=([pl.BlockSpec((T, W), lambda g: (0, g))] + [tile_spec(w) for w in ws]
                  + [_const_spec(c.shape) for c in consts]),
        out_specs=pl.BlockSpec((T, W), lambda g: (0, g)),
        out_shape=jax.ShapeDtypeStruct((T, S5_G * S5_N), F32),
        scratch_shapes=[pltpu.VMEM((S5_L * W, S5_L * W), BF16)],
        compiler_params=pltpu.CompilerParams(
            dimension_semantics=("arbitrary",), vmem_limit_bytes=VMEM_LIMIT),
        name="s5",
    )(u, *ws, *consts)


def _post_kernel(y_ref, gla_ref, x_ref, gw_ref, gb_ref, sg_ref, wo_ref, n2_ref, rw_ref, rb_ref, before_ref,
                 h_ref, hn_ref, idx_ref, gate_ref, rank_ref, cnt_ref, run_ref):
    @pl.when(pl.program_id(0) == 0)
    def _():
        run_ref[...] = jnp.zeros_like(run_ref)

    y = y_ref[...]
    z = 0.5 * y * (1.0 + jnp.tanh(math.sqrt(2.0 / math.pi) * (y + 0.044715 * (y * y * y))))
    z = z * _sigmoid(_dot(z.astype(BF16), gw_ref[...]) + gb_ref[...])
    s5o = _rms(z, sg_ref[...]).astype(BF16)
    half = GLA_H * GLA_DV
    h = x_ref[...] + _dot(gla_ref[...], wo_ref[:half, :]) + _dot(s5o, wo_ref[half:, :])
    h_ref[...] = h
    hn = _rms(h, n2_ref[...])
    hi = hn.astype(BF16)
    hn_ref[...] = _pack_halves(hi[:, :D // 2], hi[:, D // 2:])
    lo = (hn - hi.astype(F32)).astype(BF16)
    rw = rw_ref[...]
    rw_hi = rw.astype(BF16)
    rw_lo = (rw - rw_hi.astype(F32)).astype(BF16)
    lg = _dot(hi, rw_hi) + (_dot(lo, rw_hi) + _dot(hi, rw_lo))
    lt = lg.T[:N_EXP] + rb_ref[...]
    eid = lax.broadcasted_iota(jnp.int32, lt.shape, 0).astype(F32)
    vals, idxs = [], []
    for _ in range(TOP_K):
        m = jnp.max(lt, axis=0, keepdims=True)
        sel = jnp.min(jnp.where(lt == m, eid, float(N_EXP)), axis=0, keepdims=True)
        vals.append(m)
        idxs.append(sel)
        lt = jnp.where(eid == sel, -jnp.inf, lt)
    ex = [jnp.exp(vv - vals[0]) for vv in vals]
    inv = 1.0 / (ex[0] + ex[1] + ex[2] + ex[3])
    idx_ref[...] = jnp.concatenate(idxs, axis=0).astype(jnp.int32)
    gate_ref[...] = jnp.concatenate([e * inv for e in ex], axis=0)
    run = run_ref[:, :1]
    ranks = []
    for sel in idxs:
        onehot = jnp.where(eid == sel, 1.0, 0.0)
        earlier = _dot(onehot.astype(BF16), before_ref[...])
        ranks.append(jnp.sum(onehot * (run + earlier), axis=0, keepdims=True))
        run = run + jnp.sum(onehot, axis=1, keepdims=True)
    rank_ref[...] = jnp.concatenate(ranks, axis=0).astype(jnp.int32)
    run_ref[...] = jnp.broadcast_to(run, run_ref.shape)
    cnt_ref[...] = jnp.broadcast_to(run, cnt_ref.shape)


def _post(y, gla, x, gw, gb, sg, wo, n2, rwt, rb):
    tm = 256
    width = S5_G * S5_N
    before = (jnp.arange(tm)[:, None] < jnp.arange(tm)[None, :]).astype(BF16)
    return pl.pallas_call(
        _post_kernel,
        grid=(T // tm,),
        in_specs=[
            pl.BlockSpec((tm, width), lambda i: (i, 0)),
            pl.BlockSpec((tm, GLA_H * GLA_DV), lambda i: (i, 0)),
            pl.BlockSpec((tm, D), lambda i: (i, 0)),
            _const_spec(gw.shape), _const_spec(gb.shape), _const_spec(sg.shape), _const_spec(wo.shape),
            _const_spec(n2.shape), _const_spec(rwt.shape), _const_spec(rb.shape), _const_spec(before.shape),
        ],
        out_specs=[
            pl.BlockSpec((tm, D), lambda i: (i, 0)),
            pl.BlockSpec((tm, D // 2), lambda i: (i, 0)),
            pl.BlockSpec((TOP_K, tm), lambda i: (0, i)),
            pl.BlockSpec((TOP_K, tm), lambda i: (0, i)),
            pl.BlockSpec((TOP_K, tm), lambda i: (0, i)),
            pl.BlockSpec((N_EXP, 128), lambda i: (0, 0)),
        ],
        out_shape=[
            jax.ShapeDtypeStruct((T, D), F32),
            jax.ShapeDtypeStruct((T, D // 2), jnp.int32),
            jax.ShapeDtypeStruct((TOP_K, T), jnp.int32),
            jax.ShapeDtypeStruct((TOP_K, T), F32),
            jax.ShapeDtypeStruct((TOP_K, T), jnp.int32),
            jax.ShapeDtypeStruct((N_EXP, 128), F32),
        ],
        scratch_shapes=[pltpu.VMEM((N_EXP, 128), F32)],
        compiler_params=pltpu.CompilerParams(
            dimension_semantics=("arbitrary",), vmem_limit_bytes=VMEM_LIMIT),
        name="post",
    )(y, gla, x, gw, gb, sg, wo, n2, rwt, rb, before)


def _route(top_idx, rank, counts):
    e_flat = top_idx.reshape(-1)
    rank = rank.reshape(-1)
    counts = counts[:, 0].astype(jnp.int32)
    padded = (counts + MOE_RB - 1) // MOE_RB * MOE_RB
    pad_end = jnp.cumsum(padded)
    pad_start = pad_end - padded
    dest = _dest(top_idx, rank.reshape(top_idx.shape), pad_start.astype(jnp.int32)).reshape(-1)
    nvis_e = (counts + MOE_RMAX - 1) // MOE_RMAX
    vis_end = jnp.cumsum(nvis_e)
    n_vis = vis_end[-1]
    v = jnp.arange(MOE_NV, dtype=jnp.int32)
    vc = jnp.minimum(v, n_vis - 1)
    ve = jnp.minimum(jnp.searchsorted(vis_end, vc, side='right'), N_EXP - 1).astype(jnp.int32)
    local = vc - (vis_end[ve] - nvis_e[ve])
    vcnt = jnp.where(v < n_vis, jnp.minimum(MOE_RMAX, counts[ve] - local * MOE_RMAX), 0)
    vrow = pad_start[ve] + local * MOE_RMAX
    return dict(dest=dest, fill_from=(pad_start + counts).astype(jnp.int32), fill_to=pad_end.astype(jnp.int32),
                tail_blk=(pad_end[-1:] // MOE_RB).astype(jnp.int32),
                vis_e=ve, vis_cnt=vcnt.astype(jnp.int32), vis_row=vrow.astype(jnp.int32),
                n_vis=n_vis.reshape(1).astype(jnp.int32))


def _dest_kernel(start_ref, idx_ref, rank_ref, dest_ref):
    e = idx_ref[...]
    d = rank_ref[...]
    for x in range(N_EXP):
        d = d + jnp.where(e == x, start_ref[x], 0)
    dest_ref[...] = d


def _dest(top_idx, rank, pad_start):
    full = pl.BlockSpec(top_idx.shape, lambda i, *_: (0, 0))
    return pl.pallas_call(
        _dest_kernel,
        grid_spec=pltpu.PrefetchScalarGridSpec(num_scalar_prefetch=1, grid=(1,), in_specs=[full, full], out_specs=full),
        out_shape=jax.ShapeDtypeStruct(top_idx.shape, jnp.int32),
        name="dest",
    )(pad_start, top_idx, rank)


def _row_copy(src_ref, srow, dst_ref, drow, sem):
    return pltpu.make_async_copy(src_ref.at[pl.ds(srow, 1), :], dst_ref.at[pl.ds(drow, 1), :], sem)


def _tail_fill(src_ref, dst_ref, tail_ref, sem):
    def cp(b):
        return pltpu.make_async_copy(src_ref.at[pl.ds(0, MOE_RB), :],
                                     dst_ref.at[pl.ds(pl.multiple_of(b * MOE_RB, MOE_RB), MOE_RB), :], sem)

    def start(b, c):
        cp(b).start()
        return c
    lax.fori_loop(tail_ref[0], MOE_ROWS // MOE_RB, start, 0)

    def finish(b, c):
        cp(b).wait()
        return c
    lax.fori_loop(tail_ref[0], MOE_ROWS // MOE_RB, finish, 0)


def _scatter_kernel(dest_ref, from_ref, to_ref, tail_ref, hn_ref, xs_ref, zero_ref, sem, zsem):
    i = pl.program_id(0)
    tm = hn_ref.shape[0]

    @pl.when(i == 0)
    def _():
        zero_ref[...] = jnp.zeros_like(zero_ref)
        _tail_fill(zero_ref, xs_ref, tail_ref, zsem)

        def per_expert(e, c):
            def fill(r, c2):
                _row_copy(zero_ref, 0, xs_ref, r, zsem).start()
                return c2
            lax.fori_loop(from_ref[e], to_ref[e], fill, 0)

            def drain(r, c2):
                _row_copy(zero_ref, 0, xs_ref, r, zsem).wait()
                return c2
            lax.fori_loop(from_ref[e], to_ref[e], drain, 0)
            return c
        lax.fori_loop(0, N_EXP, per_expert, 0)

    def issue(t, c):
        for k in range(TOP_K):
            _row_copy(hn_ref, t, xs_ref, dest_ref[k * T + i * tm + t], sem).start(priority=k % 2)
        return c
    lax.fori_loop(0, tm, issue, 0)

    for k in range(TOP_K):
        pltpu.make_async_copy(hn_ref, xs_ref.at[pl.ds(0, tm), :], sem).wait()


def _scatter(hn, rt):
    tm = 1024
    return pl.pallas_call(
        _scatter_kernel,
        grid_spec=pltpu.PrefetchScalarGridSpec(
            num_scalar_prefetch=4,
            grid=(T // tm,),
            in_specs=[pl.BlockSpec((tm, D // 2), lambda i, *_: (i, 0))],
            out_specs=pl.BlockSpec(memory_space=pl.ANY),
            scratch_shapes=[pltpu.VMEM((MOE_RB, D // 2), jnp.int32), pltpu.SemaphoreType.DMA,
                            pltpu.SemaphoreType.DMA],
        ),
        out_shape=jax.ShapeDtypeStruct((MOE_ROWS, D // 2), jnp.int32),
        compiler_params=pltpu.CompilerParams(
            dimension_semantics=("arbitrary",), vmem_limit_bytes=VMEM_LIMIT),
        name="scatter",
    )(rt["dest"], rt["fill_from"], rt["fill_to"], rt["tail_blk"], hn)


def _experts_kernel(ve_ref, vcnt_ref, vrow_ref, nvis_ref, tail_ref, xs_ref, w1g_ref, w1l_ref, w2_ref, b1g_ref,
                    b1l_ref, b2_ref, ys_ref, xin_ref, xb_ref, acc_ref, isem, osem):
    v = pl.program_id(0)
    j = pl.program_id(1)
    RB = MOE_RB
    BIG, MID = 4 * RB, 2 * RB
    nvis = nvis_ref[0]
    active = v < nvis
    first = j == 0
    last = j == MOE_NF - 1

    def geometry(vv):
        nblk = (vcnt_ref[vv] + RB - 1) // RB
        return vrow_ref[vv], nblk // 4, (nblk // 2) % 2, nblk % 2

    row0, nbig, mid, small = geometry(v)
    mid_r0 = nbig * BIG
    small_r0 = mid_r0 + mid * MID

    def x_copy(vrow, r0, n, slot):
        return pltpu.make_async_copy(xs_ref.at[pl.ds(pl.multiple_of(vrow + r0, RB), n), :],
                                     xin_ref.at[slot, pl.ds(0, n), :], isem.at[slot])

    def y_copy(r0, n):
        return pltpu.make_async_copy(acc_ref.at[pl.ds(pl.multiple_of(r0, RB), n), :],
                                     ys_ref.at[pl.ds(pl.multiple_of(row0 + r0, RB), n), :], osem)

    def start_unit(vv, u):
        vrow, nb, md, sm = geometry(vv)

        @pl.when(u < nb)
        def _():
            x_copy(vrow, u * BIG, BIG, u % 2).start()

        @pl.when((u == nb) & (md == 1))
        def _():
            x_copy(vrow, nb * BIG, MID, u % 2).start()

        @pl.when((u == nb + md) & (sm == 1))
        def _():
            x_copy(vrow, nb * BIG + md * MID, RB, u % 2).start()

    @pl.when((v == 0) & first)
    def _():
        acc_ref[...] = jnp.zeros_like(acc_ref)
        _tail_fill(acc_ref, ys_ref, tail_ref, osem)
        start_unit(0, 0)

    def block(u, r0, n):
        rows = pl.ds(pl.multiple_of(r0, RB), n)
        slot = u % 2

        @pl.when(first)
        def _():
            x_copy(row0, r0, n, slot).wait()
            start_unit(v, u + 1)
            lo_half, hi_half = _unpack_halves(xin_ref[slot, pl.ds(0, n), :])
            xb_ref[rows, :D // 2] = lo_half
            xb_ref[rows, D // 2:] = hi_half

        x = xb_ref[rows, :]
        glu = jnp.minimum(_dot(x, w1g_ref[0].astype(BF16)) + b1g_ref[0], SWIGLU_LIMIT)
        lin = jnp.clip(_dot(x, w1l_ref[0].astype(BF16)) + b1l_ref[0], -SWIGLU_LIMIT, SWIGLU_LIMIT)
        act = glu * _sigmoid(SWIGLU_ALPHA * glu) * (lin + 1.0)
        part = _dot(act.astype(BF16), w2_ref[0].astype(BF16))

        acc_ref[rows, :] = part + jnp.where(first, b2_ref[0], acc_ref[rows, :])

        @pl.when(last)
        def _():
            y_copy(r0, n).start()

    @pl.when(active)
    def _():
        def big(i, c):
            block(i, i * BIG, BIG)
            return c
        lax.fori_loop(0, nbig, big, 0)

        @pl.when(mid == 1)
        def _():
            block(nbig, mid_r0, MID)

        @pl.when(small == 1)
        def _():
            block(nbig + mid, small_r0, RB)

    @pl.when(active & last)
    def _():
        def finish(i, c):
            y_copy(i * BIG, BIG).wait()
            return c
        lax.fori_loop(0, nbig, finish, 0)

        @pl.when(mid == 1)
        def _():
            y_copy(mid_r0, MID).wait()

        @pl.when(small == 1)
        def _():
            y_copy(small_r0, RB).wait()

        @pl.when(v + 1 < nvis)
        def _():
            start_unit(v + 1, 0)


def _experts(xs, rt, w1, b1, w2, b2):
    b1 = b1.reshape(N_EXP, 1, 2 * D_FF)
    b2 = b2.reshape(N_EXP, 1, D)

    def jj(v, j, ve, vcnt, vrow, nvis, tail):
        return jnp.where(v < nvis[0], j, MOE_NF - 1)

    return pl.pallas_call(
        _experts_kernel,
        grid_spec=pltpu.PrefetchScalarGridSpec(
            num_scalar_prefetch=5,
            grid=(MOE_NV, MOE_NF),
            in_specs=[
                pl.BlockSpec(memory_space=pl.ANY),
                pl.BlockSpec((1, D, MOE_TF), lambda v, j, ve, *s: (ve[v], 0, jj(v, j, ve, *s))),
                pl.BlockSpec((1, D, MOE_TF), lambda v, j, ve, *s: (ve[v], 0, MOE_NF + jj(v, j, ve, *s))),
                pl.BlockSpec((1, MOE_TF, D), lambda v, j, ve, *s: (ve[v], jj(v, j, ve, *s), 0)),
                pl.BlockSpec((1, 1, MOE_TF), lambda v, j, ve, *s: (ve[v], 0, jj(v, j, ve, *s))),
                pl.BlockSpec((1, 1, MOE_TF), lambda v, j, ve, *s: (ve[v], 0, MOE_NF + jj(v, j, ve, *s))),
                pl.BlockSpec((1, 1, D), lambda v, j, ve, *s: (ve[v], 0, 0)),
            ],
            out_specs=pl.BlockSpec(memory_space=pl.ANY),
            scratch_shapes=[
                pltpu.VMEM((2, 4 * MOE_RB, D // 2), jnp.int32),
                pltpu.VMEM((MOE_RMAX, D), BF16),
                pltpu.VMEM((MOE_RMAX, D), F32),
                pltpu.SemaphoreType.DMA((2,)),
                pltpu.SemaphoreType.DMA,
            ],
        ),
        out_shape=jax.ShapeDtypeStruct((MOE_ROWS, D), F32),
        compiler_params=pltpu.CompilerParams(
            dimension_semantics=("arbitrary", "arbitrary"), vmem_limit_bytes=VMEM_LIMIT),
        name="experts",
    )(rt["vis_e"], rt["vis_cnt"], rt["vis_row"], rt["n_vis"], rt["tail_blk"], xs, w1, w1, w2, b1, b1, b2)


def _combine_kernel(dest_ref, ys_ref, h_ref, gate_ref, gf_ref, o_ref, buf_ref, sem):
    i = pl.program_id(0)
    tm = h_ref.shape[0]

    def gather(tile, slot):
        def issue(t, c):
            for k in range(TOP_K):
                _row_copy(ys_ref, dest_ref[k * T + tile * tm + t], buf_ref.at[slot, k], t,
                          sem.at[slot]).start(priority=k % 2)
            return c
        lax.fori_loop(0, tm, issue, 0)

    @pl.when(i == 0)
    def _():
        gather(0, 0)

    @pl.when(i + 1 < pl.num_programs(0))
    def _():
        gather(i + 1, (i + 1) % 2)

    slot = i % 2
    for k in range(TOP_K):
        pltpu.make_async_copy(ys_ref.at[pl.ds(0, tm), :], buf_ref.at[slot, k], sem.at[slot]).wait()

    h = h_ref[...]
    for k in range(TOP_K):
        h = h + gate_ref[:, k:k + 1] * buf_ref[slot, k]
    o_ref[...] = _rms(h, gf_ref[...])


def _combine(ys, h, gates_t, gf, rt):
    tm = 256
    return pl.pallas_call(
        _combine_kernel,
        grid_spec=pltpu.PrefetchScalarGridSpec(
            num_scalar_prefetch=1,
            grid=(T // tm,),
            in_specs=[
                pl.BlockSpec(memory_space=pl.ANY),
                pl.BlockSpec((tm, D), lambda i, *_: (i, 0)),
                pl.BlockSpec((tm, TOP_K), lambda i, *_: (i, 0)),
                pl.BlockSpec((1, D), lambda i, *_: (0, 0)),
            ],
            out_specs=pl.BlockSpec((tm, D), lambda i, *_: (i, 0)),
            scratch_shapes=[pltpu.VMEM((2, TOP_K, tm, D), F32), pltpu.SemaphoreType.DMA((2,))],
        ),
        out_shape=jax.ShapeDtypeStruct((T, D), F32),
        compiler_params=pltpu.CompilerParams(
            dimension_semantics=("arbitrary",), vmem_limit_bytes=VMEM_LIMIT),
        name="combine",
    )(rt["dest"], ys, h, gates_t, gf)


def kernel(x, norm1_g, w_in, gla_gate_w2, gla_gate_b, gla_norm_g, s5_a_re, s5_a_im, s5_b_re, s5_b_im, s5_c_re, s5_c_im, s5_d, s5_log_step, s5_glu_w, s5_glu_b, s5_norm_g, w_out, norm2_g, router_w, router_b, expert_w1, expert_b1, expert_w2, expert_b2, final_norm_g):
    assert x.shape == (1, T, D) and w_in.shape[0] == 1
    xt = x.reshape(T, D)
    w2g = jnp.pad(gla_gate_w2[0], ((0, 128 - GATE_RANK), (0, 0)))
    qkvr, lg, u = _proj(xt, norm1_g, w_in, w2g, gla_gate_b)
    gla = _gla(qkvr, lg, gla_norm_g)
    tables = _s5_tables(s5_a_re[0], s5_a_im[0], s5_b_re[0], s5_b_im[0], s5_c_re[0], s5_c_im[0],
                        s5_d[0], s5_log_step[0])
    y = _s5(u, tables)
    h, hn, top_idx, gates, rank, counts = _post(
        y, gla, xt, s5_glu_w[0].astype(BF16), s5_glu_b, s5_norm_g, w_out[0].astype(BF16), norm2_g,
        jnp.pad(router_w[0], ((0, 0), (0, 128 - N_EXP))), router_b.reshape(N_EXP, 1))
    rt = _route(top_idx, rank, counts)
    xs = _scatter(hn, rt)
    ys = _experts(xs, rt, expert_w1[0], expert_b1[0], expert_w2[0], expert_b2[0])
    out = _combine(ys, h, gates.T, final_norm_g.reshape(1, D), rt)
    return out.reshape(1, T, D)
```

```python
import functools
import math

import jax
import jax.numpy as jnp
from jax import lax
from jax.experimental import pallas as pl
from jax.experimental.pallas import tpu as pltpu

F32 = jnp.float32
BF16 = jnp.bfloat16
HIGHEST = lax.Precision.HIGHEST

T = 8192
D = 2048
GLA_H = 4
GLA_DK = 128
GLA_DV = 256
GLA_CHUNK = 64
GLA_SUB = 16
GATE_RANK = 16
GATE_TAU = 16.0
S5_G = 64
S5_N = 16
S5_P = 64
S5_L = 16
S5_GT = 8
N_EXP = 32
TOP_K = 4
D_FF = 2048
SWIGLU_ALPHA = 1.702
SWIGLU_LIMIT = 7.0
RMS_EPS = 1e-6

MOE_RB = 128
MOE_RMAX = 1536
MOE_TF = 512
MOE_NF = D_FF // MOE_TF
MOE_NV = -(-T * TOP_K // MOE_RMAX) + N_EXP
MOE_ROWS = T * TOP_K + N_EXP * MOE_RB
VMEM_LIMIT = 56 * 1024 * 1024

NN = (((1,), (0,)), ((), ()))
NT = (((1,), (1,)), ((), ()))
TN = (((0,), (0,)), ((), ()))


def _dot(a, b, dims=NN, precision=None):
    return lax.dot_general(a, b, dims, preferred_element_type=F32, precision=precision)


def _dot3(a, b):
    a_hi, b_hi = a.astype(BF16), b.astype(BF16)
    a_lo = (a - a_hi.astype(F32)).astype(BF16)
    b_lo = (b - b_hi.astype(F32)).astype(BF16)
    return _dot(a_hi, b_hi) + (_dot(a_lo, b_hi) + _dot(a_hi, b_lo))


def _rms(x, g):
    return x * lax.rsqrt(jnp.mean(x * x, axis=-1, keepdims=True) + RMS_EPS) * g


def _sigmoid(x):
    return 1.0 / (1.0 + jnp.exp(-x))


def _pack_halves(lo_half, hi_half):
    a = lax.bitcast_convert_type(lo_half.astype(F32), jnp.int32)
    b = lax.bitcast_convert_type(hi_half.astype(F32), jnp.int32)
    return a | lax.shift_right_logical(b, 16)


def _unpack_halves(p):
    a = lax.bitcast_convert_type(p & jnp.int32(-65536), F32).astype(BF16)
    b = lax.bitcast_convert_type(lax.shift_left(p, 16), F32).astype(BF16)
    return a, b


def _const_spec(shape):
    nd = len(shape)
    return pl.BlockSpec(shape, lambda *_: (0,) * nd, pipeline_mode=pl.Buffered(1))


PROJ_QKVR = 2 * GLA_H * GLA_DK + 2 * GLA_H * GLA_DV
PROJ_TAIL = GATE_RANK + S5_G * S5_N
PROJ_TAIL_PAD = -(-PROJ_TAIL // 128) * 128
PROJ_CHUNK = 512


def _proj_kernel(x_ref, g1_ref, win_ref, w2_ref, gb_ref, qkvr_ref, lg_ref, u_ref,
                 wq_ref, wt_ref, stage_ref, tail_ref, sem, tsem):
    @pl.when(pl.program_id(0) == 0)
    def _():
        nchunk = PROJ_QKVR // PROJ_CHUNK

        def piece(c):
            cols = pl.ds(c * PROJ_CHUNK, PROJ_CHUNK)
            return pltpu.make_async_copy(win_ref.at[0, :, cols], stage_ref.at[c % 2], sem.at[c % 2])

        tail = pltpu.make_async_copy(win_ref.at[0, :, pl.ds(PROJ_QKVR, PROJ_TAIL)], tail_ref, tsem)
        tail.start()
        piece(0).start()
        for c in range(nchunk):
            piece(c).wait()
            if c + 1 < nchunk:
                piece(c + 1).start()
            wq_ref[:, c * PROJ_CHUNK:(c + 1) * PROJ_CHUNK] = stage_ref[c % 2].astype(BF16)
        tail.wait()
        wt_ref[...] = jnp.zeros_like(wt_ref)
        wt_ref[:, :PROJ_TAIL] = tail_ref[...].astype(BF16)

    hb = _rms(x_ref[...], g1_ref[...]).astype(BF16)
    p = _dot(hb, wq_ref[...])
    nq = GLA_H * GLA_DK
    qkvr_ref[:, :nq] = (p[:, :nq] * (GLA_DK ** -0.5)).astype(BF16)
    qkvr_ref[:, nq:] = p[:, nq:].astype(BF16)
    gu = _dot(hb, wt_ref[...])
    z = _dot3(gu[:, :128], w2_ref[...]) + gb_ref[...]
    lg_ref[...] = (jnp.minimum(z, 0.0) - jnp.log(1.0 + jnp.exp(-jnp.abs(z)))) * (1.0 / GATE_TAU)
    u_ref[...] = gu[:, GATE_RANK:PROJ_TAIL]


def _proj(x, g1, w_in, w2, gb):
    tm = 256
    return pl.pallas_call(
        _proj_kernel,
        grid=(T // tm,),
        in_specs=[
            pl.BlockSpec((tm, D), lambda i: (i, 0)),
            _const_spec((1, D)),
            pl.BlockSpec(memory_space=pl.ANY),
            _const_spec(w2.shape), _const_spec(gb.shape),
        ],
        out_specs=[
            pl.BlockSpec((tm, PROJ_QKVR), lambda i: (i, 0)),
            pl.BlockSpec((tm, GLA_H * GLA_DK), lambda i: (i, 0)),
            pl.BlockSpec((tm, S5_G * S5_N), lambda i: (i, 0)),
        ],
        out_shape=[
            jax.ShapeDtypeStruct((T, PROJ_QKVR), BF16),
            jax.ShapeDtypeStruct((T, GLA_H * GLA_DK), F32),
            jax.ShapeDtypeStruct((T, S5_G * S5_N), F32),
        ],
        scratch_shapes=[
            pltpu.VMEM((D, PROJ_QKVR), BF16),
            pltpu.VMEM((D, PROJ_TAIL_PAD), BF16),
            pltpu.VMEM((2, D, PROJ_CHUNK), F32),
            pltpu.VMEM((D, PROJ_TAIL), F32),
            pltpu.SemaphoreType.DMA((2,)),
            pltpu.SemaphoreType.DMA,
        ],
        compiler_params=pltpu.CompilerParams(
            dimension_semantics=("arbitrary",), vmem_limit_bytes=VMEM_LIMIT),
        name="proj",
    )(x, g1, w_in, w2, gb)


def _gla_kernel(q_ref, k_ref, v_ref, r_ref, lg_ref, gn_ref, tril_ref, esel_ref, o_ref, state_ref):
    C, S = GLA_CHUNK, GLA_SUB
    nsub = C // S

    @pl.when(pl.program_id(0) == 0)
    def _():
        state_ref[...] = jnp.zeros_like(state_ref)

    row = lax.broadcasted_iota(jnp.int32, (C, C), 0)
    col = lax.broadcasted_iota(jnp.int32, (C, C), 1)
    below = (col // S) < (row // S)
    diag = ((col // S) == (row // S)) & (col <= row)
    tril = tril_ref[...]
    esel = esel_ref[...]
    gn = gn_ref[...]

    def head(h, rows, b):
        kcols = slice(h * GLA_DK, (h + 1) * GLA_DK)
        vcols = slice(h * GLA_DV, (h + 1) * GLA_DV)
        q = q_ref[rows, kcols].astype(F32)
        k = k_ref[rows, kcols].astype(F32)
        v = v_ref[rows, vcols]
        bl = b[C - 1:C, :]
        st = state_ref[h]
        o = _dot((q * jnp.exp2(b)).astype(BF16), st.astype(BF16), NT)
        s_rows = [jnp.zeros((S, C), F32)]
        for j in range(1, nsub):
            ref = b[j * S - 1:j * S, :]
            qj = (q[j * S:(j + 1) * S] * jnp.exp2(b[j * S:(j + 1) * S] - ref)).astype(BF16)
            kj = (k * jnp.exp2(jnp.minimum(ref - b, 0.0))).astype(BF16)
            s_rows.append(_dot(qj, kj, NT))
        s_off = jnp.concatenate(s_rows, axis=0)
        q3 = q.reshape(nsub, S, GLA_DK)
        k3 = k.reshape(nsub, S, GLA_DK)
        b3 = b.reshape(nsub, S, GLA_DK)
        xs = []
        for s in range(S):
            lo = (s // 8) * 8
            dec = jnp.exp2(jnp.minimum(b3[:, lo:] - b3[:, s:s + 1, :], 0.0))
            x = q3[:, lo:] * k3[:, s:s + 1, :] * dec
            if lo:
                x = jnp.concatenate([jnp.zeros((nsub, lo, GLA_DK), F32), x], axis=1)
            xs.append(x.reshape(C, GLA_DK).astype(BF16))
        dsc = _dot(jnp.concatenate(xs, axis=1), esel)
        a = jnp.where(below, s_off, jnp.where(diag, dsc, 0.0))
        o = o + _dot(a.astype(BF16), v)
        kout = (k * jnp.exp2(bl - b)).astype(BF16)
        state_ref[h] = st * jnp.exp2(bl) + _dot(v, kout, TN)
        y = _rms(o, gn)
        r = r_ref[rows, vcols].astype(F32)
        o_ref[rows, vcols] = (y * (r * _sigmoid(r))).astype(BF16)

    def chunk(c, carry):
        rows = pl.ds(pl.multiple_of(c * C, C), C)
        g = lg_ref[rows, :]
        g_hi = g.astype(BF16)
        g_mid = (g - g_hi.astype(F32)).astype(BF16)
        g_lo = (g - g_hi.astype(F32) - g_mid.astype(F32)).astype(BF16)
        b = (_dot(tril, g_hi) + (_dot(tril, g_mid) + _dot(tril, g_lo))) * math.log2(math.e)
        for h in range(GLA_H):
            head(h, rows, b[:, h * GLA_DK:(h + 1) * GLA_DK])
        return carry

    lax.fori_loop(0, q_ref.shape[0] // C, chunk, 0, unroll=8)


def _gla(qkvr, lg, gn):
    tb = 512
    C, S = GLA_CHUNK, GLA_SUB
    nk, nv = GLA_H * GLA_DK, GLA_H * GLA_DV
    tril = jnp.tril(jnp.ones((C, C), BF16))
    esel = (jnp.arange(S * GLA_DK)[:, None] // GLA_DK == jnp.arange(C)[None, :] % S).astype(BF16)
    return pl.pallas_call(
        _gla_kernel,
        grid=(T // tb,),
        in_specs=[
            pl.BlockSpec((tb, nk), lambda i: (i, 0)),
            pl.BlockSpec((tb, nk), lambda i: (i, 1)),
            pl.BlockSpec((tb, nv), lambda i: (i, 1)),
            pl.BlockSpec((tb, nv), lambda i: (i, 2)),
            pl.BlockSpec((tb, nk), lambda i: (i, 0)),
            _const_spec((1, GLA_DV)), _const_spec((C, C)), _const_spec((S * GLA_DK, C)),
        ],
        out_specs=pl.BlockSpec((tb, nv), lambda i: (i, 0)),
        out_shape=jax.ShapeDtypeStruct((T, nv), BF16),
        scratch_shapes=[pltpu.VMEM((GLA_H, GLA_DV, GLA_DK), F32)],
        compiler_params=pltpu.CompilerParams(
            dimension_semantics=("arbitrary",), vmem_limit_bytes=VMEM_LIMIT),
        name="gla",
    )(qkvr, qkvr, qkvr, qkvr, lg, gn, tril, esel)


def _s5_tables(a_re, a_im, b_re, b_im, c_re, c_im, d_skip, log_step):
    L, G, P, N = S5_L, S5_G, S5_P, S5_N
    delta = jnp.exp(log_step)[:, None]
    ar, ai = a_re * delta, a_im * delta

    def apow(tau):
        tau = jnp.asarray(tau, F32)[None, :, None]
        mag = jnp.exp(ar[:, None, :] * tau)
        ph = ai[:, None, :] * tau
        return mag * jnp.cos(ph), mag * jnp.sin(ph)

    p_re, p_im = apow(jnp.arange(L + 1))
    ab_re, ab_im = p_re[:, 1], p_im[:, 1]
    den = a_re * a_re + a_im * a_im
    cf_re = ((ab_re - 1.0) * a_re + ab_im * a_im) / den
    cf_im = (ab_im * a_re - (ab_re - 1.0) * a_im) / den
    bb_re = cf_re[:, :, None] * b_re - cf_im[:, :, None] * b_im
    bb_im = cf_re[:, :, None] * b_im + cf_im[:, :, None] * b_re
    e_re = p_re[:, :L, :, None] * bb_re[:, None] - p_im[:, :L, :, None] * bb_im[:, None]
    e_im = p_re[:, :L, :, None] * bb_im[:, None] + p_im[:, :L, :, None] * bb_re[:, None]
    kk = (jnp.einsum('gnp,gtpm->gtnm', c_re, e_re, precision=HIGHEST)
          - jnp.einsum('gnp,gtpm->gtnm', c_im, e_im, precision=HIGHEST))
    kk = kk.at[:, 0].add(jnp.eye(N, dtype=F32)[None] * d_skip[:, :, None])
    r_re, r_im = apow(L - 1 - jnp.arange(L))
    bs_re = (r_re[:, :, None, :] * bb_re.transpose(0, 2, 1)[:, None] - r_im[:, :, None, :] * bb_im.transpose(0, 2, 1)[:, None])
    bs_im = (r_re[:, :, None, :] * bb_im.transpose(0, 2, 1)[:, None] + r_im[:, :, None, :] * bb_re.transpose(0, 2, 1)[:, None])
    ct_re = c_re.transpose(0, 2, 1)[:, :, None, :]
    ct_im = c_im.transpose(0, 2, 1)[:, :, None, :]
    q_re = p_re[:, 1:].transpose(0, 2, 1)[:, :, :, None]
    q_im = p_im[:, 1:].transpose(0, 2, 1)[:, :, :, None]
    cre = ct_re * q_re - ct_im * q_im
    cim = -(ct_re * q_im + ct_im * q_re)
    GT = S5_GT
    nt = G // GT

    def tile_groups(w):
        return w.reshape((nt, GT) + w.shape[1:])

    kc = tile_groups(kk).transpose(0, 1, 4, 2, 3).reshape(nt, GT * N, L * N)

    def in_table(w):
        return tile_groups(w).transpose(0, 2, 1, 3, 4).reshape(nt, L * GT * N, P)

    def out_table(w):
        return w.reshape(nt, GT * P, L * N)

    nstep = (T // L - 1).bit_length()
    sc_re, sc_im = apow(jnp.asarray([L * 2 ** i for i in range(nstep)]))

    def lanes(w):
        return tile_groups(w).transpose(0, 2, 1, 3).reshape(nt, w.shape[1], GT * P)

    r1 = jnp.arange(L * N)[:, None]
    c1 = jnp.arange(L * GT * N)[None, :]
    wide_tn = ((r1 // N == c1 // (GT * N)) & (r1 % N == c1 % N)).astype(BF16)
    wide_p = (jnp.arange(P)[:, None] == jnp.arange(GT * P)[None, :] % P).astype(BF16)
    return dict(
        kc=kc.astype(BF16),
        bs_re=in_table(bs_re).astype(BF16), bs_im=in_table(bs_im).astype(BF16),
        cre=out_table(cre).astype(BF16), cim=out_table(cim).astype(BF16),
        sc_re=lanes(sc_re), sc_im=lanes(sc_im), wide_tn=wide_tn, wide_p=wide_p,
    )


def _s5_kernel(u_ref, kc_ref, bsr_ref, bsi_ref, cre_ref, cim_ref, scr_ref, sci_ref, wtn_ref, wp_ref,
               y_ref, m_ref):
    L, N, P = S5_L, S5_N, S5_P
    W = S5_GT * N
    nch = u_ref.shape[0] // L

    @pl.when(pl.program_id(0) == 0)
    def _():
        m_ref[...] = jnp.zeros_like(m_ref)

    def widen(compact, wide, row_group, col_group):
        full = _dot(compact, wide)
        r = lax.broadcasted_iota(jnp.int32, full.shape, 0)
        c = lax.broadcasted_iota(jnp.int32, full.shape, 1)
        return jnp.where(row_group(r) == col_group(c), full, 0.0).astype(BF16)

    wtn, wp = wtn_ref[...], wp_ref[...]
    bd = widen(kc_ref[0], wtn, lambda r: r // N, lambda c: (c % W) // N)
    bsr = widen(bsr_ref[0], wp, lambda r: (r % W) // N, lambda c: c // P)
    bsi = widen(bsi_ref[0], wp, lambda r: (r % W) // N, lambda c: c // P)
    cre = widen(cre_ref[0], wtn, lambda r: r // P, lambda c: (c % W) // N)
    cim = widen(cim_ref[0], wtn, lambda r: r // P, lambda c: (c % W) // N)
    for s in range(L):
        m_ref[s * W:(s + 1) * W, s * W:] = bd[:, :(L - s) * W]
    ucat = jnp.concatenate([u_ref[pl.ds(s, nch, stride=L), :].astype(BF16) for s in range(L)], axis=1)
    hr = _dot(ucat, bsr)
    hi = _dot(ucat, bsi)
    row = lax.broadcasted_iota(jnp.int32, hr.shape, 0)

    def shift(x, d):
        if d % 8 == 0:
            return jnp.concatenate([jnp.zeros((d, x.shape[1]), F32), x[:nch - d]], axis=0)
        return jnp.where(row >= d, pltpu.roll(x, d, 0), 0.0)

    for i in range(scr_ref.shape[1]):
        mr, mi = scr_ref[0, i:i + 1, :], sci_ref[0, i:i + 1, :]
        pr, pi = shift(hr, 2 ** i), shift(hi, 2 ** i)
        hr, hi = hr + mr * pr - mi * pi, hi + mr * pi + mi * pr
    gr, gi = shift(hr, 1), shift(hi, 1)
    grb, gib = gr.astype(BF16), gi.astype(BF16)
    for tp in range(L // 2):
        cols = slice(2 * tp * W, (2 * tp + 2) * W)
        depth = (2 * tp + 2) * W
        y = _dot(ucat[:, :depth], m_ref[:depth, cols]) + _dot(grb, cre[:, cols]) + _dot(gib, cim[:, cols])
        for t in (2 * tp, 2 * tp + 1):
            y_ref[pl.ds(t, nch, stride=L), :] = y[:, (t - 2 * tp) * W:(t - 2 * tp + 1) * W]


def _s5(u, tb):
    W = S5_GT * S5_N
    nt = S5_G // S5_GT

    def tile_spec(a):
        return pl.BlockSpec((1,) + a.shape[1:], lambda g: (g,) + (0,) * (a.ndim - 1))

    ws = [tb[n] for n in ["kc", "bs_re", "bs_im", "cre", "cim", "sc_re", "sc_im"]]
    consts = [tb["wide_tn"], tb["wide_p"]]
    return pl.pallas_call(
        _s5_kernel,
        grid=(nt,),
        in_specs=([pl.BlockSpec((T, W), lambda g: (0, g))] + [tile_spec(w) for w in ws]
                  + [_const_spec(c.shape) for c in consts]),
        out_specs=pl.BlockSpec((T, W), lambda g: (0, g)),
        out_shape=jax.ShapeDtypeStruct((T, S5_G * S5_N), F32),
        scratch_shapes=[pltpu.VMEM((S5_L * W, S5_L * W), BF16)],
        compiler_params=pltpu.CompilerParams(
            dimension_semantics=("arbitrary",), vmem_limit_bytes=VMEM_LIMIT),
        name="s5",
    )(u, *ws, *consts)


def _post_kernel(y_ref, gla_ref, x_ref, gw_ref, gb_ref, sg_ref, wo_ref, n2_ref, rw_ref, rb_ref, before_ref,
                 h_ref, hn_ref, idx_ref, gate_ref, rank_ref, cnt_ref, run_ref):
    @pl.when(pl.program_id(0) == 0)
    def _():
        run_ref[...] = jnp.zeros_like(run_ref)

    y = y_ref[...]
    z = 0.5 * y * (1.0 + jnp.tanh(math.sqrt(2.0 / math.pi) * (y + 0.044715 * (y * y * y))))
    z = z * _sigmoid(_dot(z.astype(BF16), gw_ref[...]) + gb_ref[...])
    s5o = _rms(z, sg_ref[...]).astype(BF16)
    half = GLA_H * GLA_DV
    h = x_ref[...] + _dot(gla_ref[...], wo_ref[:half, :]) + _dot(s5o, wo_ref[half:, :])
    h_ref[...] = h
    hn = _rms(h, n2_ref[...])
    hi = hn.astype(BF16)
    hn_ref[...] = _pack_halves(hi[:, :D // 2], hi[:, D // 2:])
    lo = (hn - hi.astype(F32)).astype(BF16)
    rw = rw_ref[...]
    rw_hi = rw.astype(BF16)
    rw_lo = (rw - rw_hi.astype(F32)).astype(BF16)
    lg = _dot(hi, rw_hi) + (_dot(lo, rw_hi) + _dot(hi, rw_lo))
    lt = lg.T[:N_EXP] + rb_ref[...]
    eid = lax.broadcasted_iota(jnp.int32, lt.shape, 0).astype(F32)
    vals, idxs = [], []
    for _ in range(TOP_K):
        m = jnp.max(lt, axis=0, keepdims=True)
        sel = jnp.min(jnp.where(lt == m, eid, float(N_EXP)), axis=0, keepdims=True)
        vals.append(m)
        idxs.append(sel)
        lt = jnp.where(eid == sel, -jnp.inf, lt)
    ex = [jnp.exp(vv - vals[0]) for vv in vals]
    inv = 1.0 / (ex[0] + ex[1] + ex[2] + ex[3])
    idx_ref[...] = jnp.concatenate(idxs, axis=0).astype(jnp.int32)
    gate_ref[...] = jnp.concatenate([e * inv for e in ex], axis=0)
    run = run_ref[:, :1]
    ranks = []
    for sel in idxs:
        onehot = jnp.where(eid == sel, 1.0, 0.0)
        earlier = _dot(onehot.astype(BF16), before_ref[...])
        ranks.append(jnp.sum(onehot * (run + earlier), axis=0, keepdims=True))
        run = run + jnp.sum(onehot, axis=1, keepdims=True)
    rank_ref[...] = jnp.concatenate(ranks, axis=0).astype(jnp.int32)
    run_ref[...] = jnp.broadcast_to(run, run_ref.shape)
    cnt_ref[...] = jnp.broadcast_to(run, cnt_ref.shape)


def _post(y, gla, x, gw, gb, sg, wo, n2, rwt, rb):
    tm = 256
    width = S5_G * S5_N
    before = (jnp.arange(tm)[:, None] < jnp.arange(tm)[None, :]).astype(BF16)
    return pl.pallas_call(
        _post_kernel,
        grid=(T // tm,),
        in_specs=[
            pl.BlockSpec((tm, width), lambda i: (i, 0)),
            pl.BlockSpec((tm, GLA_H * GLA_DV), lambda i: (i, 0)),
            pl.BlockSpec((tm, D), lambda i: (i, 0)),
            _const_spec(gw.shape), _const_spec(gb.shape), _const_spec(sg.shape), _const_spec(wo.shape),
            _const_spec(n2.shape), _const_spec(rwt.shape), _const_spec(rb.shape), _const_spec(before.shape),
        ],
        out_specs=[
            pl.BlockSpec((tm, D), lambda i: (i, 0)),
            pl.BlockSpec((tm, D // 2), lambda i: (i, 0)),
            pl.BlockSpec((TOP_K, tm), lambda i: (0, i)),
            pl.BlockSpec((TOP_K, tm), lambda i: (0, i)),
            pl.BlockSpec((TOP_K, tm), lambda i: (0, i)),
            pl.BlockSpec((N_EXP, 128), lambda i: (0, 0)),
        ],
        out_shape=[
            jax.ShapeDtypeStruct((T, D), F32),
            jax.ShapeDtypeStruct((T, D // 2), jnp.int32),
            jax.ShapeDtypeStruct((TOP_K, T), jnp.int32),
            jax.ShapeDtypeStruct((TOP_K, T), F32),
            jax.ShapeDtypeStruct((TOP_K, T), jnp.int32),
            jax.ShapeDtypeStruct((N_EXP, 128), F32),
        ],
        scratch_shapes=[pltpu.VMEM((N_EXP, 128), F32)],
        compiler_params=pltpu.CompilerParams(
            dimension_semantics=("arbitrary",), vmem_limit_bytes=VMEM_LIMIT),
        name="post",
    )(y, gla, x, gw, gb, sg, wo, n2, rwt, rb, before)


def _route(top_idx, rank, counts):
    e_flat = top_idx.reshape(-1)
    rank = rank.reshape(-1)
    counts = counts[:, 0].astype(jnp.int32)
    padded = (counts + MOE_RB - 1) // MOE_RB * MOE_RB
    pad_end = jnp.cumsum(padded)
    pad_start = pad_end - padded
    dest = _dest(top_idx, rank.reshape(top_idx.shape), pad_start.astype(jnp.int32)).reshape(-1)
    nvis_e = (counts + MOE_RMAX - 1) // MOE_RMAX
    vis_end = jnp.cumsum(nvis_e)
    n_vis = vis_end[-1]
    v = jnp.arange(MOE_NV, dtype=jnp.int32)
    vc = jnp.minimum(v, n_vis - 1)
    ve = jnp.minimum(jnp.searchsorted(vis_end, vc, side='right'), N_EXP - 1).astype(jnp.int32)
    local = vc - (vis_end[ve] - nvis_e[ve])
    vcnt = jnp.where(v < n_vis, jnp.minimum(MOE_RMAX, counts[ve] - local * MOE_RMAX), 0)
    vrow = pad_start[ve] + local * MOE_RMAX
    return dict(dest=dest, fill_from=(pad_start + counts).astype(jnp.int32), fill_to=pad_end.astype(jnp.int32),
                tail_blk=(pad_end[-1:] // MOE_RB).astype(jnp.int32),
                vis_e=ve, vis_cnt=vcnt.astype(jnp.int32), vis_row=vrow.astype(jnp.int32),
                n_vis=n_vis.reshape(1).astype(jnp.int32))


def _dest_kernel(start_ref, idx_ref, rank_ref, dest_ref):
    e = idx_ref[...]
    d = rank_ref[...]
    for x in range(N_EXP):
        d = d + jnp.where(e == x, start_ref[x], 0)
    dest_ref[...] = d


def _dest(top_idx, rank, pad_start):
    full = pl.BlockSpec(top_idx.shape, lambda i, *_: (0, 0))
    return pl.pallas_call(
        _dest_kernel,
        grid_spec=pltpu.PrefetchScalarGridSpec(num_scalar_prefetch=1, grid=(1,), in_specs=[full, full], out_specs=full),
        out_shape=jax.ShapeDtypeStruct(top_idx.shape, jnp.int32),
        name="dest",
    )(pad_start, top_idx, rank)


def _row_copy(src_ref, srow, dst_ref, drow, sem):
    return pltpu.make_async_copy(src_ref.at[pl.ds(srow, 1), :], dst_ref.at[pl.ds(drow, 1), :], sem)


def _tail_fill(src_ref, dst_ref, tail_ref, sem):
    def cp(b):
        return pltpu.make_async_copy(src_ref.at[pl.ds(0, MOE_RB), :],
                                     dst_ref.at[pl.ds(pl.multiple_of(b * MOE_RB, MOE_RB), MOE_RB), :], sem)

    def start(b, c):
        cp(b).start()
        return c
    lax.fori_loop(tail_ref[0], MOE_ROWS // MOE_RB, start, 0)

    def finish(b, c):
        cp(b).wait()
        return c
    lax.fori_loop(tail_ref[0], MOE_ROWS // MOE_RB, finish, 0)


def _scatter_kernel(dest_ref, from_ref, to_ref, tail_ref, hn_ref, xs_ref, zero_ref, sem, zsem):
    i = pl.program_id(0)
    tm = hn_ref.shape[0]

    @pl.when(i == 0)
    def _():
        zero_ref[...] = jnp.zeros_like(zero_ref)
        _tail_fill(zero_ref, xs_ref, tail_ref, zsem)

        def per_expert(e, c):
            def fill(r, c2):
                _row_copy(zero_ref, 0, xs_ref, r, zsem).start()
                return c2
            lax.fori_loop(from_ref[e], to_ref[e], fill, 0)

            def drain(r, c2):
                _row_copy(zero_ref, 0, xs_ref, r, zsem).wait()
                return c2
            lax.fori_loop(from_ref[e], to_ref[e], drain, 0)
            return c
        lax.fori_loop(0, N_EXP, per_expert, 0)

    def issue(t, c):
        for k in range(TOP_K):
            _row_copy(hn_ref, t, xs_ref, dest_ref[k * T + i * tm + t], sem).start(priority=k % 2)
        return c
    lax.fori_loop(0, tm, issue, 0)

    for k in range(TOP_K):
        pltpu.make_async_copy(hn_ref, xs_ref.at[pl.ds(0, tm), :], sem).wait()


def _scatter(hn, rt):
    tm = 1024
    return pl.pallas_call(
        _scatter_kernel,
        grid_spec=pltpu.PrefetchScalarGridSpec(
            num_scalar_prefetch=4,
            grid=(T // tm,),
            in_specs=[pl.BlockSpec((tm, D // 2), lambda i, *_: (i, 0))],
            out_specs=pl.BlockSpec(memory_space=pl.ANY),
            scratch_shapes=[pltpu.VMEM((MOE_RB, D // 2), jnp.int32), pltpu.SemaphoreType.DMA,
                            pltpu.SemaphoreType.DMA],
        ),
        out_shape=jax.ShapeDtypeStruct((MOE_ROWS, D // 2), jnp.int32),
        compiler_params=pltpu.CompilerParams(
            dimension_semantics=("arbitrary",), vmem_limit_bytes=VMEM_LIMIT),
        name="scatter",
    )(rt["dest"], rt["fill_from"], rt["fill_to"], rt["tail_blk"], hn)


def _experts_kernel(ve_ref, vcnt_ref, vrow_ref, nvis_ref, tail_ref, xs_ref, w1_ref, w2_ref, b1_ref, b2_ref, ys_ref,
                    xin_ref, xb_ref, acc_ref, wg_ref, wl_ref, wd_ref, isem, osem, wsem):
    v = pl.program_id(0)
    RB = MOE_RB
    BIG, MID = 4 * RB, 2 * RB
    TF, NF = MOE_TF, MOE_NF
    nvis = nvis_ref[0]
    active = v < nvis

    def geometry(vv):
        nblk = (vcnt_ref[vv] + RB - 1) // RB
        return vrow_ref[vv], nblk // 4, (nblk // 2) % 2, nblk % 2

    row0, nbig, mid, small = geometry(v)
    mid_r0 = nbig * BIG
    small_r0 = mid_r0 + mid * MID

    def x_copy(vrow, r0, n, slot):
        return pltpu.make_async_copy(xs_ref.at[pl.ds(pl.multiple_of(vrow + r0, RB), n), :],
                                     xin_ref.at[slot, pl.ds(0, n), :], isem.at[slot])

    def y_copy(r0, n):
        return pltpu.make_async_copy(acc_ref.at[pl.ds(pl.multiple_of(r0, RB), n), :],
                                     ys_ref.at[pl.ds(pl.multiple_of(row0 + r0, RB), n), :], osem)

    def tile_copies(e, t, slot):
        cols = pl.ds(pl.multiple_of(t * TF, TF), TF)
        lin_cols = pl.ds(pl.multiple_of(D_FF + t * TF, TF), TF)
        return (pltpu.make_async_copy(w1_ref.at[e, :, cols], wg_ref.at[slot], wsem.at[slot]),
                pltpu.make_async_copy(w1_ref.at[e, :, lin_cols], wl_ref.at[slot], wsem.at[slot]),
                pltpu.make_async_copy(w2_ref.at[e, cols, :], wd_ref.at[slot], wsem.at[slot]))

    def start_unit(vv, u):
        vrow, nb, md, sm = geometry(vv)

        @pl.when(u < nb)
        def _():
            x_copy(vrow, u * BIG, BIG, u % 2).start()

        @pl.when((u == nb) & (md == 1))
        def _():
            x_copy(vrow, nb * BIG, MID, u % 2).start()

        @pl.when((u == nb + md) & (sm == 1))
        def _():
            x_copy(vrow, nb * BIG + md * MID, RB, u % 2).start()

    @pl.when(v == 0)
    def _():
        acc_ref[...] = jnp.zeros_like(acc_ref)
        _tail_fill(acc_ref, ys_ref, tail_ref, osem)
        start_unit(0, 0)
        for cp in tile_copies(ve_ref[0], 0, 0):
            cp.start()

    def step(j, carry):
        slot = j % 2
        first = j == 0
        last = j == NF - 1
        for cp in tile_copies(ve_ref[v], j, slot):
            cp.wait()

        @pl.when(j + 1 < NF)
        def _():
            for cp in tile_copies(ve_ref[v], j + 1, 1 - slot):
                cp.start()

        @pl.when(last & (v + 1 < nvis))
        def _():
            for cp in tile_copies(ve_ref[jnp.minimum(v + 1, MOE_NV - 1)], 0, 1 - slot):
                cp.start()

        b1g = b1_ref[0, :, pl.ds(pl.multiple_of(j * TF, TF), TF)]
        b1l = b1_ref[0, :, pl.ds(pl.multiple_of(D_FF + j * TF, TF), TF)]

        def block(u, r0, n):
            rows = pl.ds(pl.multiple_of(r0, RB), n)
            xslot = u % 2

            @pl.when(first)
            def _():
                x_copy(row0, r0, n, xslot).wait()
                start_unit(v, u + 1)
                lo_half, hi_half = _unpack_halves(xin_ref[xslot, pl.ds(0, n), :])
                xb_ref[rows, :D // 2] = lo_half
                xb_ref[rows, D // 2:] = hi_half

            x = xb_ref[rows, :]
            glu = jnp.minimum(_dot(x, wg_ref[slot].astype(BF16)) + b1g, SWIGLU_LIMIT)
            lin = jnp.clip(_dot(x, wl_ref[slot].astype(BF16)) + b1l, -SWIGLU_LIMIT, SWIGLU_LIMIT)
            act = glu * _sigmoid(SWIGLU_ALPHA * glu) * (lin + 1.0)
            part = _dot(act.astype(BF16), wd_ref[slot].astype(BF16))

            acc_ref[rows, :] = part + jnp.where(first, b2_ref[0], acc_ref[rows, :])

            @pl.when(last)
            def _():
                y_copy(r0, n).start()

        def big(i, c):
            block(i, i * BIG, BIG)
            return c
        lax.fori_loop(0, nbig, big, 0)

        @pl.when(mid == 1)
        def _():
            block(nbig, mid_r0, MID)

        @pl.when(small == 1)
        def _():
            block(nbig + mid, small_r0, RB)

        @pl.when(last)
        def _():
            def finish(i, c):
                y_copy(i * BIG, BIG).wait()
                return c
            lax.fori_loop(0, nbig, finish, 0)

            @pl.when(mid == 1)
            def _():
                y_copy(mid_r0, MID).wait()

            @pl.when(small == 1)
            def _():
                y_copy(small_r0, RB).wait()

            @pl.when(v + 1 < nvis)
            def _():
                start_unit(v + 1, 0)
        return carry

    @pl.when(active)
    def _():
        lax.fori_loop(0, NF, step, 0)


def _experts(xs, rt, w1, b1, w2, b2):
    b1 = b1.reshape(N_EXP, 1, 2 * D_FF)
    b2 = b2.reshape(N_EXP, 1, D)
    any_spec = pl.BlockSpec(memory_space=pl.ANY)
    return pl.pallas_call(
        _experts_kernel,
        grid_spec=pltpu.PrefetchScalarGridSpec(
            num_scalar_prefetch=5,
            grid=(MOE_NV,),
            in_specs=[
                any_spec, any_spec, any_spec,
                pl.BlockSpec((1, 1, 2 * D_FF), lambda v, ve, *s: (ve[v], 0, 0)),
                pl.BlockSpec((1, 1, D), lambda v, ve, *s: (ve[v], 0, 0)),
            ],
            out_specs=any_spec,
            scratch_shapes=[
                pltpu.VMEM((2, 4 * MOE_RB, D // 2), jnp.int32),
                pltpu.VMEM((MOE_RMAX, D), BF16),
                pltpu.VMEM((MOE_RMAX, D), F32),
                pltpu.VMEM((2, D, MOE_TF), F32),
                pltpu.VMEM((2, D, MOE_TF), F32),
                pltpu.VMEM((2, MOE_TF, D), F32),
                pltpu.SemaphoreType.DMA((2,)),
                pltpu.SemaphoreType.DMA,
                pltpu.SemaphoreType.DMA((2,)),
            ],
        ),
        out_shape=jax.ShapeDtypeStruct((MOE_ROWS, D), F32),
        compiler_params=pltpu.CompilerParams(
            dimension_semantics=("arbitrary",), vmem_limit_bytes=VMEM_LIMIT),
        name="experts",
    )(rt["vis_e"], rt["vis_cnt"], rt["vis_row"], rt["n_vis"], rt["tail_blk"], xs, w1, w2, b1, b2)


def _combine_kernel(dest_ref, ys_ref, h_ref, gate_ref, gf_ref, o_ref, buf_ref, sem):
    i = pl.program_id(0)
    tm = h_ref.shape[0]

    def gather(tile, slot):
        def issue(t, c):
            for k in range(TOP_K):
                _row_copy(ys_ref, dest_ref[k * T + tile * tm + t], buf_ref.at[slot, k], t,
                          sem.at[slot]).start(priority=k % 2)
            return c
        lax.fori_loop(0, tm, issue, 0)

    @pl.when(i == 0)
    def _():
        gather(0, 0)

    @pl.when(i + 1 < pl.num_programs(0))
    def _():
        gather(i + 1, (i + 1) % 2)

    slot = i % 2
    for k in range(TOP_K):
        pltpu.make_async_copy(ys_ref.at[pl.ds(0, tm), :], buf_ref.at[slot, k], sem.at[slot]).wait()

    h = h_ref[...]
    for k in range(TOP_K):
        h = h + gate_ref[:, k:k + 1] * buf_ref[slot, k]
    o_ref[...] = _rms(h, gf_ref[...])


def _combine(ys, h, gates_t, gf, rt):
    tm = 256
    return pl.pallas_call(
        _combine_kernel,
        grid_spec=pltpu.PrefetchScalarGridSpec(
            num_scalar_prefetch=1,
            grid=(T // tm,),
            in_specs=[
                pl.BlockSpec(memory_space=pl.ANY),
                pl.BlockSpec((tm, D), lambda i, *_: (i, 0)),
                pl.BlockSpec((tm, TOP_K), lambda i, *_: (i, 0)),
                pl.BlockSpec((1, D), lambda i, *_: (0, 0)),
            ],
            out_specs=pl.BlockSpec((tm, D), lambda i, *_: (i, 0)),
            scratch_shapes=[pltpu.VMEM((2, TOP_K, tm, D), F32), pltpu.SemaphoreType.DMA((2,))],
        ),
        out_shape=jax.ShapeDtypeStruct((T, D), F32),
        compiler_params=pltpu.CompilerParams(
            dimension_semantics=("arbitrary",), vmem_limit_bytes=VMEM_LIMIT),
        name="combine",
    )(rt["dest"], ys, h, gates_t, gf)


def kernel(x, norm1_g, w_in, gla_gate_w2, gla_gate_b, gla_norm_g, s5_a_re, s5_a_im, s5_b_re, s5_b_im, s5_c_re, s5_c_im, s5_d, s5_log_step, s5_glu_w, s5_glu_b, s5_norm_g, w_out, norm2_g, router_w, router_b, expert_w1, expert_b1, expert_w2, expert_b2, final_norm_g):
    assert x.shape == (1, T, D) and w_in.shape[0] == 1
    xt = x.reshape(T, D)
    w2g = jnp.pad(gla_gate_w2[0], ((0, 128 - GATE_RANK), (0, 0)))
    qkvr, lg, u = _proj(xt, norm1_g, w_in, w2g, gla_gate_b)
    gla = _gla(qkvr, lg, gla_norm_g)
    tables = _s5_tables(s5_a_re[0], s5_a_im[0], s5_b_re[0], s5_b_im[0], s5_c_re[0], s5_c_im[0],
                        s5_d[0], s5_log_step[0])
    y = _s5(u, tables)
    h, hn, top_idx, gates, rank, counts = _post(
        y, gla, xt, s5_glu_w[0].astype(BF16), s5_glu_b, s5_norm_g, w_out[0].astype(BF16), norm2_g,
        jnp.pad(router_w[0], ((0, 0), (0, 128 - N_EXP))), router_b.reshape(N_EXP, 1))
    rt = _route(top_idx, rank, counts)
    xs = _scatter(hn, rt)
    ys = _experts(xs, rt, expert_w1[0], expert_b1[0], expert_w2[0], expert_b2[0])
    out = _combine(ys, h, gates.T, final_norm_g.reshape(1, D), rt)
    return out.reshape(1, T, D)
```

```python
import functools
import math

import jax
import jax.numpy as jnp
from jax import lax
from jax.experimental import pallas as pl
from jax.experimental.pallas import tpu as pltpu

F32 = jnp.float32
BF16 = jnp.bfloat16
HIGHEST = lax.Precision.HIGHEST

T = 8192
D = 2048
GLA_H = 4
GLA_DK = 128
GLA_DV = 256
GLA_CHUNK = 64
GLA_SUB = 16
GATE_RANK = 16
GATE_TAU = 16.0
S5_G = 64
S5_N = 16
S5_P = 64
S5_L = 16
S5_GT = 8
N_EXP = 32
TOP_K = 4
D_FF = 2048
SWIGLU_ALPHA = 1.702
SWIGLU_LIMIT = 7.0
RMS_EPS = 1e-6

MOE_RB = 128
MOE_RMAX = 1536
MOE_TF = 512
MOE_NF = D_FF // MOE_TF
MOE_NV = -(-T * TOP_K // MOE_RMAX) + N_EXP
MOE_ROWS = T * TOP_K + N_EXP * MOE_RB
VMEM_LIMIT = 56 * 1024 * 1024

NN = (((1,), (0,)), ((), ()))
NT = (((1,), (1,)), ((), ()))
TN = (((0,), (0,)), ((), ()))


def _dot(a, b, dims=NN, precision=None):
    return lax.dot_general(a, b, dims, preferred_element_type=F32, precision=precision)


def _dot3(a, b):
    a_hi, b_hi = a.astype(BF16), b.astype(BF16)
    a_lo = (a - a_hi.astype(F32)).astype(BF16)
    b_lo = (b - b_hi.astype(F32)).astype(BF16)
    return _dot(a_hi, b_hi) + (_dot(a_lo, b_hi) + _dot(a_hi, b_lo))


def _rms(x, g):
    return x * lax.rsqrt(jnp.mean(x * x, axis=-1, keepdims=True) + RMS_EPS) * g


def _sigmoid(x):
    return 1.0 / (1.0 + jnp.exp(-x))


def _pack_halves(lo_half, hi_half):
    a = lax.bitcast_convert_type(lo_half.astype(F32), jnp.int32)
    b = lax.bitcast_convert_type(hi_half.astype(F32), jnp.int32)
    return a | lax.shift_right_logical(b, 16)


def _unpack_halves(p):
    a = lax.bitcast_convert_type(p & jnp.int32(-65536), F32).astype(BF16)
    b = lax.bitcast_convert_type(lax.shift_left(p, 16), F32).astype(BF16)
    return a, b


def _const_spec(shape):
    nd = len(shape)
    return pl.BlockSpec(shape, lambda *_: (0,) * nd, pipeline_mode=pl.Buffered(1))


PROJ_QKVR = 2 * GLA_H * GLA_DK + 2 * GLA_H * GLA_DV
PROJ_TAIL = GATE_RANK + S5_G * S5_N
PROJ_TAIL_PAD = -(-PROJ_TAIL // 128) * 128
PROJ_CHUNK = 512


def _proj_kernel(x_ref, g1_ref, win_ref, w2_ref, gb_ref, qkvr_ref, lg_ref, u_ref,
                 wq_ref, wt_ref, stage_ref, tail_ref, sem, tsem):
    @pl.when(pl.program_id(0) == 0)
    def _():
        nchunk = PROJ_QKVR // PROJ_CHUNK

        def piece(c):
            cols = pl.ds(c * PROJ_CHUNK, PROJ_CHUNK)
            return pltpu.make_async_copy(win_ref.at[0, :, cols], stage_ref.at[c % 2], sem.at[c % 2])

        tail = pltpu.make_async_copy(win_ref.at[0, :, pl.ds(PROJ_QKVR, PROJ_TAIL)], tail_ref, tsem)
        tail.start()
        piece(0).start()
        for c in range(nchunk):
            piece(c).wait()
            if c + 1 < nchunk:
                piece(c + 1).start()
            wq_ref[:, c * PROJ_CHUNK:(c + 1) * PROJ_CHUNK] = stage_ref[c % 2].astype(BF16)
        tail.wait()
        wt_ref[...] = jnp.zeros_like(wt_ref)
        wt_ref[:, :PROJ_TAIL] = tail_ref[...].astype(BF16)

    hb = _rms(x_ref[...], g1_ref[...]).astype(BF16)
    p = _dot(hb, wq_ref[...])
    nq = GLA_H * GLA_DK
    qkvr_ref[:, :nq] = (p[:, :nq] * (GLA_DK ** -0.5)).astype(BF16)
    qkvr_ref[:, nq:] = p[:, nq:].astype(BF16)
    gu = _dot(hb, wt_ref[...])
    z = _dot3(gu[:, :128], w2_ref[...]) + gb_ref[...]
    lg_ref[...] = (jnp.minimum(z, 0.0) - jnp.log(1.0 + jnp.exp(-jnp.abs(z)))) * (1.0 / GATE_TAU)
    u_ref[...] = gu[:, GATE_RANK:PROJ_TAIL]


def _proj(x, g1, w_in, w2, gb):
    tm = 256
    return pl.pallas_call(
        _proj_kernel,
        grid=(T // tm,),
        in_specs=[
            pl.BlockSpec((tm, D), lambda i: (i, 0)),
            _const_spec((1, D)),
            pl.BlockSpec(memory_space=pl.ANY),
            _const_spec(w2.shape), _const_spec(gb.shape),
        ],
        out_specs=[
            pl.BlockSpec((tm, PROJ_QKVR), lambda i: (i, 0)),
            pl.BlockSpec((tm, GLA_H * GLA_DK), lambda i: (i, 0)),
            pl.BlockSpec((tm, S5_G * S5_N), lambda i: (i, 0)),
        ],
        out_shape=[
            jax.ShapeDtypeStruct((T, PROJ_QKVR), BF16),
            jax.ShapeDtypeStruct((T, GLA_H * GLA_DK), F32),
            jax.ShapeDtypeStruct((T, S5_G * S5_N), F32),
        ],
        scratch_shapes=[
            pltpu.VMEM((D, PROJ_QKVR), BF16),
            pltpu.VMEM((D, PROJ_TAIL_PAD), BF16),
            pltpu.VMEM((2, D, PROJ_CHUNK), F32),
            pltpu.VMEM((D, PROJ_TAIL), F32),
            pltpu.SemaphoreType.DMA((2,)),
            pltpu.SemaphoreType.DMA,
        ],
        compiler_params=pltpu.CompilerParams(
            dimension_semantics=("arbitrary",), vmem_limit_bytes=VMEM_LIMIT),
        name="proj",
    )(x, g1, w_in, w2, gb)


def _gla_kernel(q_ref, k_ref, v_ref, r_ref, lg_ref, gn_ref, tril_ref, esel_ref, o_ref, state_ref):
    C, S = GLA_CHUNK, GLA_SUB
    nsub = C // S

    @pl.when(pl.program_id(0) == 0)
    def _():
        state_ref[...] = jnp.zeros_like(state_ref)

    row = lax.broadcasted_iota(jnp.int32, (C, C), 0)
    col = lax.broadcasted_iota(jnp.int32, (C, C), 1)
    below = (col // S) < (row // S)
    diag = ((col // S) == (row // S)) & (col <= row)
    tril = tril_ref[...]
    esel = esel_ref[...]
    gn = gn_ref[...]

    def head(h, rows, b):
        kcols = slice(h * GLA_DK, (h + 1) * GLA_DK)
        vcols = slice(h * GLA_DV, (h + 1) * GLA_DV)
        q = q_ref[rows, kcols].astype(F32)
        k = k_ref[rows, kcols].astype(F32)
        v = v_ref[rows, vcols]
        bl = b[C - 1:C, :]
        st = state_ref[h]
        o = _dot((q * jnp.exp2(b)).astype(BF16), st.astype(BF16), NT)
        s_rows = [jnp.zeros((S, C), F32)]
        for j in range(1, nsub):
            ref = b[j * S - 1:j * S, :]
            qj = (q[j * S:(j + 1) * S] * jnp.exp2(b[j * S:(j + 1) * S] - ref)).astype(BF16)
            kj = (k * jnp.exp2(jnp.minimum(ref - b, 0.0))).astype(BF16)
            s_rows.append(_dot(qj, kj, NT))
        s_off = jnp.concatenate(s_rows, axis=0)
        q3 = q.reshape(nsub, S, GLA_DK)
        k3 = k.reshape(nsub, S, GLA_DK)
        b3 = b.reshape(nsub, S, GLA_DK)
        xs = []
        for s in range(S):
            lo = (s // 8) * 8
            dec = jnp.exp2(jnp.minimum(b3[:, lo:] - b3[:, s:s + 1, :], 0.0))
            x = q3[:, lo:] * k3[:, s:s + 1, :] * dec
            if lo:
                x = jnp.concatenate([jnp.zeros((nsub, lo, GLA_DK), F32), x], axis=1)
            xs.append(x.reshape(C, GLA_DK).astype(BF16))
        dsc = _dot(jnp.concatenate(xs, axis=1), esel)
        a = jnp.where(below, s_off, jnp.where(diag, dsc, 0.0))
        o = o + _dot(a.astype(BF16), v)
        kout = (k * jnp.exp2(bl - b)).astype(BF16)
        state_ref[h] = st * jnp.exp2(bl) + _dot(v, kout, TN)
        y = _rms(o, gn)
        r = r_ref[rows, vcols].astype(F32)
        o_ref[rows, vcols] = (y * (r * _sigmoid(r))).astype(BF16)

    def chunk(c, carry):
        rows = pl.ds(pl.multiple_of(c * C, C), C)
        g = lg_ref[rows, :]
        g_hi = g.astype(BF16)
        g_mid = (g - g_hi.astype(F32)).astype(BF16)
        g_lo = (g - g_hi.astype(F32) - g_mid.astype(F32)).astype(BF16)
        b = (_dot(tril, g_hi) + (_dot(tril, g_mid) + _dot(tril, g_lo))) * math.log2(math.e)
        for h in range(GLA_H):
            head(h, rows, b[:, h * GLA_DK:(h + 1) * GLA_DK])
        return carry

    lax.fori_loop(0, q_ref.shape[0] // C, chunk, 0, unroll=8)


def _gla(qkvr, lg, gn):
    tb = 512
    C, S = GLA_CHUNK, GLA_SUB
    nk, nv = GLA_H * GLA_DK, GLA_H * GLA_DV
    tril = jnp.tril(jnp.ones((C, C), BF16))
    esel = (jnp.arange(S * GLA_DK)[:, None] // GLA_DK == jnp.arange(C)[None, :] % S).astype(BF16)
    return pl.pallas_call(
        _gla_kernel,
        grid=(T // tb,),
        in_specs=[
            pl.BlockSpec((tb, nk), lambda i: (i, 0)),
            pl.BlockSpec((tb, nk), lambda i: (i, 1)),
            pl.BlockSpec((tb, nv), lambda i: (i, 1)),
            pl.BlockSpec((tb, nv), lambda i: (i, 2)),
            pl.BlockSpec((tb, nk), lambda i: (i, 0)),
            _const_spec((1, GLA_DV)), _const_spec((C, C)), _const_spec((S * GLA_DK, C)),
        ],
        out_specs=pl.BlockSpec((tb, nv), lambda i: (i, 0)),
        out_shape=jax.ShapeDtypeStruct((T, nv), BF16),
        scratch_shapes=[pltpu.VMEM((GLA_H, GLA_DV, GLA_DK), F32)],
        compiler_params=pltpu.CompilerParams(
            dimension_semantics=("arbitrary",), vmem_limit_bytes=VMEM_LIMIT),
        name="gla",
    )(qkvr, qkvr, qkvr, qkvr, lg, gn, tril, esel)


def _s5_tables(a_re, a_im, b_re, b_im, c_re, c_im, d_skip, log_step):
    L, G, P, N = S5_L, S5_G, S5_P, S5_N
    delta = jnp.exp(log_step)[:, None]
    ar, ai = a_re * delta, a_im * delta

    def apow(tau):
        tau = jnp.asarray(tau, F32)[None, :, None]
        mag = jnp.exp(ar[:, None, :] * tau)
        ph = ai[:, None, :] * tau
        return mag * jnp.cos(ph), mag * jnp.sin(ph)

    p_re, p_im = apow(jnp.arange(L + 1))
    ab_re, ab_im = p_re[:, 1], p_im[:, 1]
    den = a_re * a_re + a_im * a_im
    cf_re = ((ab_re - 1.0) * a_re + ab_im * a_im) / den
    cf_im = (ab_im * a_re - (ab_re - 1.0) * a_im) / den
    bb_re = cf_re[:, :, None] * b_re - cf_im[:, :, None] * b_im
    bb_im = cf_re[:, :, None] * b_im + cf_im[:, :, None] * b_re
    e_re = p_re[:, :L, :, None] * bb_re[:, None] - p_im[:, :L, :, None] * bb_im[:, None]
    e_im = p_re[:, :L, :, None] * bb_im[:, None] + p_im[:, :L, :, None] * bb_re[:, None]
    kk = (jnp.einsum('gnp,gtpm->gtnm', c_re, e_re, precision=HIGHEST)
          - jnp.einsum('gnp,gtpm->gtnm', c_im, e_im, precision=HIGHEST))
    kk = kk.at[:, 0].add(jnp.eye(N, dtype=F32)[None] * d_skip[:, :, None])
    r_re, r_im = apow(L - 1 - jnp.arange(L))
    bs_re = (r_re[:, :, None, :] * bb_re.transpose(0, 2, 1)[:, None] - r_im[:, :, None, :] * bb_im.transpose(0, 2, 1)[:, None])
    bs_im = (r_re[:, :, None, :] * bb_im.transpose(0, 2, 1)[:, None] + r_im[:, :, None, :] * bb_re.transpose(0, 2, 1)[:, None])
    ct_re = c_re.transpose(0, 2, 1)[:, :, None, :]
    ct_im = c_im.transpose(0, 2, 1)[:, :, None, :]
    q_re = p_re[:, 1:].transpose(0, 2, 1)[:, :, :, None]
    q_im = p_im[:, 1:].transpose(0, 2, 1)[:, :, :, None]
    cre = ct_re * q_re - ct_im * q_im
    cim = -(ct_re * q_im + ct_im * q_re)
    GT = S5_GT
    nt = G // GT

    def tile_groups(w):
        return w.reshape((nt, GT) + w.shape[1:])

    kc = tile_groups(kk).transpose(0, 1, 4, 2, 3).reshape(nt, GT * N, L * N)

    def in_table(w):
        return tile_groups(w).transpose(0, 2, 1, 3, 4).reshape(nt, L * GT * N, P)

    def out_table(w):
        return w.reshape(nt, GT * P, L * N)

    nstep = (T // L - 1).bit_length()
    sc_re, sc_im = apow(jnp.asarray([L * 2 ** i for i in range(nstep)]))

    def lanes(w):
        return tile_groups(w).transpose(0, 2, 1, 3).reshape(nt, w.shape[1], GT * P)

    r1 = jnp.arange(L * N)[:, None]
    c1 = jnp.arange(L * GT * N)[None, :]
    wide_tn = ((r1 // N == c1 // (GT * N)) & (r1 % N == c1 % N)).astype(BF16)
    wide_p = (jnp.arange(P)[:, None] == jnp.arange(GT * P)[None, :] % P).astype(BF16)
    return dict(
        kc=kc.astype(BF16),
        bs_re=in_table(bs_re).astype(BF16), bs_im=in_table(bs_im).astype(BF16),
        cre=out_table(cre).astype(BF16), cim=out_table(cim).astype(BF16),
        sc_re=lanes(sc_re), sc_im=lanes(sc_im), wide_tn=wide_tn, wide_p=wide_p,
    )


def _s5_kernel(u_ref, kc_ref, bsr_ref, bsi_ref, cre_ref, cim_ref, scr_ref, sci_ref, wtn_ref, wp_ref,
               y_ref, m_ref):
    L, N, P = S5_L, S5_N, S5_P
    W = S5_GT * N
    nch = u_ref.shape[0] // L

    @pl.when(pl.program_id(0) == 0)
    def _():
        m_ref[...] = jnp.zeros_like(m_ref)

    def widen(compact, wide, row_group, col_group):
        full = _dot(compact, wide)
        r = lax.broadcasted_iota(jnp.int32, full.shape, 0)
        c = lax.broadcasted_iota(jnp.int32, full.shape, 1)
        return jnp.where(row_group(r) == col_group(c), full, 0.0).astype(BF16)

    wtn, wp = wtn_ref[...], wp_ref[...]
    bd = widen(kc_ref[0], wtn, lambda r: r // N, lambda c: (c % W) // N)
    bsr = widen(bsr_ref[0], wp, lambda r: (r % W) // N, lambda c: c // P)
    bsi = widen(bsi_ref[0], wp, lambda r: (r % W) // N, lambda c: c // P)
    cre = widen(cre_ref[0], wtn, lambda r: r // P, lambda c: (c % W) // N)
    cim = widen(cim_ref[0], wtn, lambda r: r // P, lambda c: (c % W) // N)
    for s in range(L):
        m_ref[s * W:(s + 1) * W, s * W:] = bd[:, :(L - s) * W]
    ucat = jnp.concatenate([u_ref[pl.ds(s, nch, stride=L), :].astype(BF16) for s in range(L)], axis=1)
    hr = _dot(ucat, bsr)
    hi = _dot(ucat, bsi)
    row = lax.broadcasted_iota(jnp.int32, hr.shape, 0)

    def shift(x, d):
        if d % 8 == 0:
            return jnp.concatenate([jnp.zeros((d, x.shape[1]), F32), x[:nch - d]], axis=0)
        return jnp.where(row >= d, pltpu.roll(x, d, 0), 0.0)

    for i in range(scr_ref.shape[1]):
        mr, mi = scr_ref[0, i:i + 1, :], sci_ref[0, i:i + 1, :]
        pr, pi = shift(hr, 2 ** i), shift(hi, 2 ** i)
        hr, hi = hr + mr * pr - mi * pi, hi + mr * pi + mi * pr
    gr, gi = shift(hr, 1), shift(hi, 1)
    grb, gib = gr.astype(BF16), gi.astype(BF16)
    for tp in range(L // 2):
        cols = slice(2 * tp * W, (2 * tp + 2) * W)
        depth = (2 * tp + 2) * W
        y = _dot(ucat[:, :depth], m_ref[:depth, cols]) + _dot(grb, cre[:, cols]) + _dot(gib, cim[:, cols])
        for t in (2 * tp, 2 * tp + 1):
            y_ref[pl.ds(t, nch, stride=L), :] = y[:, (t - 2 * tp) * W:(t - 2 * tp + 1) * W]


def _s5(u, tb):
    W = S5_GT * S5_N
    nt = S5_G // S5_GT

    def tile_spec(a):
        return pl.BlockSpec((1,) + a.shape[1:], lambda g: (g,) + (0,) * (a.ndim - 1))

    ws = [tb[n] for n in ["kc", "bs_re", "bs_im", "cre", "cim", "sc_re", "sc_im"]]
    consts = [tb["wide_tn"], tb["wide_p"]]
    return pl.pallas_call(
        _s5_kernel,
        grid=(nt,),
        in_specs=([pl.BlockSpec((T, W), lambda g: (0, g))] + [tile_spec(w) for w in ws]
                  + [_const_spec(c.shape) for c in consts]),
        out_specs=pl.BlockSpec((T, W), lambda g: (0, g)),
        out_shape=jax.ShapeDtypeStruct((T, S5_G * S5_N), F32),
        scratch_shapes=[pltpu.VMEM((S5_L * W, S5_L * W), BF16)],
        compiler_params=pltpu.CompilerParams(
            dimension_semantics=("arbitrary",), vmem_limit_bytes=VMEM_LIMIT),
        name="s5",
    )(u, *ws, *consts)


def _post_kernel(y_ref, gla_ref, x_ref, gw_ref, gb_ref, sg_ref, wo_ref, n2_ref, rw_ref, rb_ref, before_ref,
                 h_ref, hn_ref, idx_ref, gate_ref, rank_ref, cnt_ref, run_ref):
    @pl.when(pl.program_id(0) == 0)
    def _():
        run_ref[...] = jnp.zeros_like(run_ref)

    y = y_ref[...]
    z = 0.5 * y * (1.0 + jnp.tanh(math.sqrt(2.0 / math.pi) * (y + 0.044715 * (y * y * y))))
    z = z * _sigmoid(_dot(z.astype(BF16), gw_ref[...]) + gb_ref[...])
    s5o = _rms(z, sg_ref[...]).astype(BF16)
    half = GLA_H * GLA_DV
    h = x_ref[...] + _dot(gla_ref[...], wo_ref[:half, :]) + _dot(s5o, wo_ref[half:, :])
    h_ref[...] = h
    hn = _rms(h, n2_ref[...])
    hi = hn.astype(BF16)
    hn_ref[...] = _pack_halves(hi[:, :D // 2], hi[:, D // 2:])
    lo = (hn - hi.astype(F32)).astype(BF16)
    rw = rw_ref[...]
    rw_hi = rw.astype(BF16)
    rw_lo = (rw - rw_hi.astype(F32)).astype(BF16)
    lg = _dot(hi, rw_hi) + (_dot(lo, rw_hi) + _dot(hi, rw_lo))
    lt = lg.T[:N_EXP] + rb_ref[...]
    eid = lax.broadcasted_iota(jnp.int32, lt.shape, 0).astype(F32)
    vals, idxs = [], []
    for _ in range(TOP_K):
        m = jnp.max(lt, axis=0, keepdims=True)
        sel = jnp.min(jnp.where(lt == m, eid, float(N_EXP)), axis=0, keepdims=True)
        vals.append(m)
        idxs.append(sel)
        lt = jnp.where(eid == sel, -jnp.inf, lt)
    ex = [jnp.exp(vv - vals[0]) for vv in vals]
    inv = 1.0 / (ex[0] + ex[1] + ex[2] + ex[3])
    idx_ref[...] = jnp.concatenate(idxs, axis=0).astype(jnp.int32)
    gate_ref[...] = jnp.concatenate([e * inv for e in ex], axis=0)
    run = run_ref[:, :1]
    ranks = []
    for sel in idxs:
        onehot = jnp.where(eid == sel, 1.0, 0.0)
        earlier = _dot(onehot.astype(BF16), before_ref[...])
        ranks.append(jnp.sum(onehot * (run + earlier), axis=0, keepdims=True))
        run = run + jnp.sum(onehot, axis=1, keepdims=True)
    rank_ref[...] = jnp.concatenate(ranks, axis=0).astype(jnp.int32)
    run_ref[...] = jnp.broadcast_to(run, run_ref.shape)
    cnt_ref[...] = jnp.broadcast_to(run, cnt_ref.shape)


def _post(y, gla, x, gw, gb, sg, wo, n2, rwt, rb):
    tm = 256
    width = S5_G * S5_N
    before = (jnp.arange(tm)[:, None] < jnp.arange(tm)[None, :]).astype(BF16)
    return pl.pallas_call(
        _post_kernel,
        grid=(T // tm,),
        in_specs=[
            pl.BlockSpec((tm, width), lambda i: (i, 0)),
            pl.BlockSpec((tm, GLA_H * GLA_DV), lambda i: (i, 0)),
            pl.BlockSpec((tm, D), lambda i: (i, 0)),
            _const_spec(gw.shape), _const_spec(gb.shape), _const_spec(sg.shape), _const_spec(wo.shape),
            _const_spec(n2.shape), _const_spec(rwt.shape), _const_spec(rb.shape), _const_spec(before.shape),
        ],
        out_specs=[
            pl.BlockSpec((tm, D), lambda i: (i, 0)),
            pl.BlockSpec((tm, D // 2), lambda i: (i, 0)),
            pl.BlockSpec((TOP_K, tm), lambda i: (0, i)),
            pl.BlockSpec((TOP_K, tm), lambda i: (0, i)),
            pl.BlockSpec((TOP_K, tm), lambda i: (0, i)),
            pl.BlockSpec((N_EXP, 128), lambda i: (0, 0)),
        ],
        out_shape=[
            jax.ShapeDtypeStruct((T, D), F32),
            jax.ShapeDtypeStruct((T, D // 2), jnp.int32),
            jax.ShapeDtypeStruct((TOP_K, T), jnp.int32),
            jax.ShapeDtypeStruct((TOP_K, T), F32),
            jax.ShapeDtypeStruct((TOP_K, T), jnp.int32),
            jax.ShapeDtypeStruct((N_EXP, 128), F32),
        ],
        scratch_shapes=[pltpu.VMEM((N_EXP, 128), F32)],
        compiler_params=pltpu.CompilerParams(
            dimension_semantics=("arbitrary",), vmem_limit_bytes=VMEM_LIMIT),
        name="post",
    )(y, gla, x, gw, gb, sg, wo, n2, rwt, rb, before)


def _route(top_idx, rank, counts):
    e_flat = top_idx.reshape(-1)
    rank = rank.reshape(-1)
    counts = counts[:, 0].astype(jnp.int32)
    padded = (counts + MOE_RB - 1) // MOE_RB * MOE_RB
    pad_end = jnp.cumsum(padded)
    pad_start = pad_end - padded
    dest = _dest(top_idx, rank.reshape(top_idx.shape), pad_start.astype(jnp.int32)).reshape(-1)
    nvis_e = (counts + MOE_RMAX - 1) // MOE_RMAX
    vis_end = jnp.cumsum(nvis_e)
    n_vis = vis_end[-1]
    v = jnp.arange(MOE_NV, dtype=jnp.int32)
    vc = jnp.minimum(v, n_vis - 1)
    ve = jnp.minimum(jnp.searchsorted(vis_end, vc, side='right'), N_EXP - 1).astype(jnp.int32)
    local = vc - (vis_end[ve] - nvis_e[ve])
    vcnt = jnp.where(v < n_vis, jnp.minimum(MOE_RMAX, counts[ve] - local * MOE_RMAX), 0)
    vrow = pad_start[ve] + local * MOE_RMAX
    return dict(dest=dest, fill_from=(pad_start + counts).astype(jnp.int32), fill_to=pad_end.astype(jnp.int32),
                tail_blk=(pad_end[-1:] // MOE_RB).astype(jnp.int32),
                vis_e=ve, vis_cnt=vcnt.astype(jnp.int32), vis_row=vrow.astype(jnp.int32),
                n_vis=n_vis.reshape(1).astype(jnp.int32))


def _dest_kernel(start_ref, idx_ref, rank_ref, dest_ref):
    e = idx_ref[...]
    d = rank_ref[...]
    for x in range(N_EXP):
        d = d + jnp.where(e == x, start_ref[x], 0)
    dest_ref[...] = d


def _dest(top_idx, rank, pad_start):
    full = pl.BlockSpec(top_idx.shape, lambda i, *_: (0, 0))
    return pl.pallas_call(
        _dest_kernel,
        grid_spec=pltpu.PrefetchScalarGridSpec(num_scalar_prefetch=1, grid=(1,), in_specs=[full, full], out_specs=full),
        out_shape=jax.ShapeDtypeStruct(top_idx.shape, jnp.int32),
        name="dest",
    )(pad_start, top_idx, rank)


def _row_copy(src_ref, srow, dst_ref, drow, sem):
    return pltpu.make_async_copy(src_ref.at[pl.ds(srow, 1), :], dst_ref.at[pl.ds(drow, 1), :], sem)


def _tail_fill(src_ref, dst_ref, tail_ref, sem):
    def cp(b):
        return pltpu.make_async_copy(src_ref.at[pl.ds(0, MOE_RB), :],
                                     dst_ref.at[pl.ds(pl.multiple_of(b * MOE_RB, MOE_RB), MOE_RB), :], sem)

    def start(b, c):
        cp(b).start()
        return c
    lax.fori_loop(tail_ref[0], MOE_ROWS // MOE_RB, start, 0)

    def finish(b, c):
        cp(b).wait()
        return c
    lax.fori_loop(tail_ref[0], MOE_ROWS // MOE_RB, finish, 0)


def _scatter_kernel(dest_ref, from_ref, to_ref, tail_ref, hn_ref, xs_ref, zero_ref, sem, zsem):
    i = pl.program_id(0)
    tm = hn_ref.shape[0]

    @pl.when(i == 0)
    def _():
        zero_ref[...] = jnp.zeros_like(zero_ref)
        _tail_fill(zero_ref, xs_ref, tail_ref, zsem)

        def per_expert(e, c):
            def fill(r, c2):
                _row_copy(zero_ref, 0, xs_ref, r, zsem).start()
                return c2
            lax.fori_loop(from_ref[e], to_ref[e], fill, 0)

            def drain(r, c2):
                _row_copy(zero_ref, 0, xs_ref, r, zsem).wait()
                return c2
            lax.fori_loop(from_ref[e], to_ref[e], drain, 0)
            return c
        lax.fori_loop(0, N_EXP, per_expert, 0)

    def issue(t, c):
        for k in range(TOP_K):
            _row_copy(hn_ref, t, xs_ref, dest_ref[k * T + i * tm + t], sem).start(priority=k % 2)
        return c
    lax.fori_loop(0, tm, issue, 0)

    for k in range(TOP_K):
        pltpu.make_async_copy(hn_ref, xs_ref.at[pl.ds(0, tm), :], sem).wait()


def _scatter(hn, rt):
    tm = 1024
    return pl.pallas_call(
        _scatter_kernel,
        grid_spec=pltpu.PrefetchScalarGridSpec(
            num_scalar_prefetch=4,
            grid=(T // tm,),
            in_specs=[pl.BlockSpec((tm, D // 2), lambda i, *_: (i, 0))],
            out_specs=pl.BlockSpec(memory_space=pl.ANY),
            scratch_shapes=[pltpu.VMEM((MOE_RB, D // 2), jnp.int32), pltpu.SemaphoreType.DMA,
                            pltpu.SemaphoreType.DMA],
        ),
        out_shape=jax.ShapeDtypeStruct((MOE_ROWS, D // 2), jnp.int32),
        compiler_params=pltpu.CompilerParams(
            dimension_semantics=("arbitrary",), vmem_limit_bytes=VMEM_LIMIT),
        name="scatter",
    )(rt["dest"], rt["fill_from"], rt["fill_to"], rt["tail_blk"], hn)


def _experts_kernel(ve_ref, vcnt_ref, vrow_ref, nvis_ref, tail_ref, xs_ref, w1_ref, w2_ref, b1_ref, b2_ref, ys_ref,
                    xin_ref, xb_ref, acc_ref, wg_ref, wl_ref, wd_ref, isem, osem, wsem):
    v = pl.program_id(0)
    RB = MOE_RB
    BIG, MID = 4 * RB, 2 * RB
    TF, NF = MOE_TF, MOE_NF
    nvis = nvis_ref[0]
    active = v < nvis

    def geometry(vv):
        nblk = (vcnt_ref[vv] + RB - 1) // RB
        return vrow_ref[vv], nblk // 4, (nblk // 2) % 2, nblk % 2

    row0, nbig, mid, small = geometry(v)
    mid_r0 = nbig * BIG
    small_r0 = mid_r0 + mid * MID

    def x_copy(vrow, r0, n, slot):
        return pltpu.make_async_copy(xs_ref.at[pl.ds(pl.multiple_of(vrow + r0, RB), n), :],
                                     xin_ref.at[slot, pl.ds(0, n), :], isem.at[slot])

    def y_copy(r0, n):
        return pltpu.make_async_copy(acc_ref.at[pl.ds(pl.multiple_of(r0, RB), n), :],
                                     ys_ref.at[pl.ds(pl.multiple_of(row0 + r0, RB), n), :], osem)

    def tile_copies(e, t, slot):
        cols = pl.ds(pl.multiple_of(t * TF, TF), TF)
        lin_cols = pl.ds(pl.multiple_of(D_FF + t * TF, TF), TF)
        return (pltpu.make_async_copy(w1_ref.at[e, :, cols], wg_ref.at[slot], wsem.at[slot]),
                pltpu.make_async_copy(w1_ref.at[e, :, lin_cols], wl_ref.at[slot], wsem.at[slot]),
                pltpu.make_async_copy(w2_ref.at[e, cols, :], wd_ref.at[slot], wsem.at[slot]))

    def start_unit(vv, u):
        vrow, nb, md, sm = geometry(vv)

        @pl.when(u < nb)
        def _():
            x_copy(vrow, u * BIG, BIG, u % 2).start()

        @pl.when((u == nb) & (md == 1))
        def _():
            x_copy(vrow, nb * BIG, MID, u % 2).start()

        @pl.when((u == nb + md) & (sm == 1))
        def _():
            x_copy(vrow, nb * BIG + md * MID, RB, u % 2).start()

    @pl.when(v == 0)
    def _():
        acc_ref[...] = jnp.zeros_like(acc_ref)
        _tail_fill(acc_ref, ys_ref, tail_ref, osem)
        start_unit(0, 0)
        for cp in tile_copies(ve_ref[0], 0, 0):
            cp.start()

    def step(j, carry):
        slot = j % 2
        first = j == 0
        last = j == NF - 1
        for cp in tile_copies(ve_ref[v], j, slot):
            cp.wait()

        @pl.when(j + 1 < NF)
        def _():
            for cp in tile_copies(ve_ref[v], j + 1, 1 - slot):
                cp.start()

        @pl.when(last & (v + 1 < nvis))
        def _():
            for cp in tile_copies(ve_ref[jnp.minimum(v + 1, MOE_NV - 1)], 0, 1 - slot):
                cp.start()

        b1g = b1_ref[0, :, pl.ds(pl.multiple_of(j * TF, TF), TF)]
        b1l = b1_ref[0, :, pl.ds(pl.multiple_of(D_FF + j * TF, TF), TF)]

        def arrive(u, r0, n):
            xslot = u % 2
            x_copy(row0, r0, n, xslot).wait()
            start_unit(v, u + 1)
            lo_half, hi_half = _unpack_halves(xin_ref[xslot, pl.ds(0, n), :])
            rows = pl.ds(pl.multiple_of(r0, RB), n)
            xb_ref[rows, :D // 2] = lo_half
            xb_ref[rows, D // 2:] = hi_half

        def compute(r0, n):
            rows = pl.ds(pl.multiple_of(r0, RB), n)
            x = xb_ref[rows, :]
            glu = jnp.minimum(_dot(x, wg_ref[slot].astype(BF16)) + b1g, SWIGLU_LIMIT)
            lin = jnp.clip(_dot(x, wl_ref[slot].astype(BF16)) + b1l, -SWIGLU_LIMIT, SWIGLU_LIMIT)
            act = glu * _sigmoid(SWIGLU_ALPHA * glu) * (lin + 1.0)
            part = _dot(act.astype(BF16), wd_ref[slot].astype(BF16))
            acc_ref[rows, :] = part + jnp.where(first, b2_ref[0], acc_ref[rows, :])

        def units(*us):
            @pl.when(first)
            def _():
                for u, r0, n in us:
                    arrive(u, r0, n)

            for _, r0, n in us:
                compute(r0, n)

            @pl.when(last)
            def _():
                for _, r0, n in us:
                    y_copy(r0, n).start()

        def pair(i, c):
            units((2 * i, 2 * i * BIG, BIG), (2 * i + 1, (2 * i + 1) * BIG, BIG))
            return c
        lax.fori_loop(0, nbig // 2, pair, 0)

        @pl.when(nbig % 2 == 1)
        def _():
            units((nbig - 1, (nbig - 1) * BIG, BIG))

        @pl.when(mid == 1)
        def _():
            units((nbig, mid_r0, MID))

        @pl.when(small == 1)
        def _():
            units((nbig + mid, small_r0, RB))

        @pl.when(last)
        def _():
            def finish(i, c):
                y_copy(i * BIG, BIG).wait()
                return c
            lax.fori_loop(0, nbig, finish, 0)

            @pl.when(mid == 1)
            def _():
                y_copy(mid_r0, MID).wait()

            @pl.when(small == 1)
            def _():
                y_copy(small_r0, RB).wait()

            @pl.when(v + 1 < nvis)
            def _():
                start_unit(v + 1, 0)
        return carry

    @pl.when(active)
    def _():
        lax.fori_loop(0, NF, step, 0)


def _experts(xs, rt, w1, b1, w2, b2):
    b1 = b1.reshape(N_EXP, 1, 2 * D_FF)
    b2 = b2.reshape(N_EXP, 1, D)
    any_spec = pl.BlockSpec(memory_space=pl.ANY)
    return pl.pallas_call(
        _experts_kernel,
        grid_spec=pltpu.PrefetchScalarGridSpec(
            num_scalar_prefetch=5,
            grid=(MOE_NV,),
            in_specs=[
                any_spec, any_spec, any_spec,
                pl.BlockSpec((1, 1, 2 * D_FF), lambda v, ve, *s: (ve[v], 0, 0)),
                pl.BlockSpec((1, 1, D), lambda v, ve, *s: (ve[v], 0, 0)),
            ],
            out_specs=any_spec,
            scratch_shapes=[
                pltpu.VMEM((2, 4 * MOE_RB, D // 2), jnp.int32),
                pltpu.VMEM((MOE_RMAX, D), BF16),
                pltpu.VMEM((MOE_RMAX, D), F32),
                pltpu.VMEM((2, D, MOE_TF), F32),
                pltpu.VMEM((2, D, MOE_TF), F32),
                pltpu.VMEM((2, MOE_TF, D), F32),
                pltpu.SemaphoreType.DMA((2,)),
                pltpu.SemaphoreType.DMA,
                pltpu.SemaphoreType.DMA((2,)),
            ],
        ),
        out_shape=jax.ShapeDtypeStruct((MOE_ROWS, D), F32),
        compiler_params=pltpu.CompilerParams(
            dimension_semantics=("arbitrary",), vmem_limit_bytes=VMEM_LIMIT),
        name="experts",
    )(rt["vis_e"], rt["vis_cnt"], rt["vis_row"], rt["n_vis"], rt["tail_blk"], xs, w1, w2, b1, b2)


def _combine_kernel(dest_ref, ys_ref, h_ref, gate_ref, gf_ref, o_ref, buf_ref, sem):
    i = pl.program_id(0)
    tm = h_ref.shape[0]

    def gather(tile, slot):
        def issue(t, c):
            for k in range(TOP_K):
                _row_copy(ys_ref, dest_ref[k * T + tile * tm + t], buf_ref.at[slot, k], t,
                          sem.at[slot]).start(priority=k % 2)
            return c
        lax.fori_loop(0, tm, issue, 0)

    @pl.when(i == 0)
    def _():
        gather(0, 0)

    @pl.when(i + 1 < pl.num_programs(0))
    def _():
        gather(i + 1, (i + 1) % 2)

    slot = i % 2
    for k in range(TOP_K):
        pltpu.make_async_copy(ys_ref.at[pl.ds(0, tm), :], buf_ref.at[slot, k], sem.at[slot]).wait()

    h = h_ref[...]
    for k in range(TOP_K):
        h = h + gate_ref[:, k:k + 1] * buf_ref[slot, k]
    o_ref[...] = _rms(h, gf_ref[...])


def _combine(ys, h, gates_t, gf, rt):
    tm = 256
    return pl.pallas_call(
        _combine_kernel,
        grid_spec=pltpu.PrefetchScalarGridSpec(
            num_scalar_prefetch=1,
            grid=(T // tm,),
            in_specs=[
                pl.BlockSpec(memory_space=pl.ANY),
                pl.BlockSpec((tm, D), lambda i, *_: (i, 0)),
                pl.BlockSpec((tm, TOP_K), lambda i, *_: (i, 0)),
                pl.BlockSpec((1, D), lambda i, *_: (0, 0)),
            ],
            out_specs=pl.BlockSpec((tm, D), lambda i, *_: (i, 0)),
            scratch_shapes=[pltpu.VMEM((2, TOP_K, tm, D), F32), pltpu.SemaphoreType.DMA((2,))],
        ),
        out_shape=jax.ShapeDtypeStruct((T, D), F32),
        compiler_params=pltpu.CompilerParams(
            dimension_semantics=("arbitrary",), vmem_limit_bytes=VMEM_LIMIT),
        name="combine",
    )(rt["dest"], ys, h, gates_t, gf)


def kernel(x, norm1_g, w_in, gla_gate_w2, gla_gate_b, gla_norm_g, s5_a_re, s5_a_im, s5_b_re, s5_b_im, s5_c_re, s5_c_im, s5_d, s5_log_step, s5_glu_w, s5_glu_b, s5_norm_g, w_out, norm2_g, router_w, router_b, expert_w1, expert_b1, expert_w2, expert_b2, final_norm_g):
    assert x.shape == (1, T, D) and w_in.shape[0] == 1
    xt = x.reshape(T, D)
    w2g = jnp.pad(gla_gate_w2[0], ((0, 128 - GATE_RANK), (0, 0)))
    qkvr, lg, u = _proj(xt, norm1_g, w_in, w2g, gla_gate_b)
    gla = _gla(qkvr, lg, gla_norm_g)
    tables = _s5_tables(s5_a_re[0], s5_a_im[0], s5_b_re[0], s5_b_im[0], s5_c_re[0], s5_c_im[0],
                        s5_d[0], s5_log_step[0])
    y = _s5(u, tables)
    h, hn, top_idx, gates, rank, counts = _post(
        y, gla, xt, s5_glu_w[0].astype(BF16), s5_glu_b, s5_norm_g, w_out[0].astype(BF16), norm2_g,
        jnp.pad(router_w[0], ((0, 0), (0, 128 - N_EXP))), router_b.reshape(N_EXP, 1))
    rt = _route(top_idx, rank, counts)
    xs = _scatter(hn, rt)
    ys = _experts(xs, rt, expert_w1[0], expert_b1[0], expert_w2[0], expert_b2[0])
    out = _combine(ys, h, gates.T, final_norm_g.reshape(1, D), rt)
    return out.reshape(1, T, D)
```

```python
import math

import jax
import jax.numpy as jnp
from jax import lax
from jax.experimental import pallas as pl
from jax.experimental.pallas import tpu as pltpu

F32 = jnp.float32
BF16 = jnp.bfloat16
HIGHEST = lax.Precision.HIGHEST

T = 8192
D = 2048
GLA_H = 4
GLA_DK = 128
GLA_DV = 256
GLA_CHUNK = 64
GLA_SUB = 16
GATE_RANK = 16
GATE_TAU = 16.0
S5_G = 64
S5_N = 16
S5_P = 64
S5_L = 16
S5_GT = 8
N_EXP = 32
TOP_K = 4
D_FF = 2048
SWIGLU_ALPHA = 1.702
SWIGLU_LIMIT = 7.0
RMS_EPS = 1e-6

MOE_RB = 128
MOE_RMAX = 2048
MOE_TF = 512
MOE_NF = D_FF // MOE_TF
MOE_NV = -(-T * TOP_K // MOE_RMAX) + N_EXP
MOE_ROWS = T * TOP_K + N_EXP * MOE_RB
VMEM_LIMIT = 56 * 1024 * 1024

NN = (((1,), (0,)), ((), ()))
NT = (((1,), (1,)), ((), ()))
TN = (((0,), (0,)), ((), ()))


def _dot(a, b, dims=NN, precision=None):
    return lax.dot_general(a, b, dims, preferred_element_type=F32, precision=precision)


def _dot3(a, b):
    a_hi, b_hi = a.astype(BF16), b.astype(BF16)
    a_lo = (a - a_hi.astype(F32)).astype(BF16)
    b_lo = (b - b_hi.astype(F32)).astype(BF16)
    return _dot(a_hi, b_hi) + (_dot(a_lo, b_hi) + _dot(a_hi, b_lo))


def _rms(x, g):
    return x * lax.rsqrt(jnp.mean(x * x, axis=-1, keepdims=True) + RMS_EPS) * g


def _sigmoid(x):
    return 1.0 / (1.0 + jnp.exp(-x))


def _pack_halves(lo_half, hi_half):
    a = lax.bitcast_convert_type(lo_half.astype(F32), jnp.int32)
    b = lax.bitcast_convert_type(hi_half.astype(F32), jnp.int32)
    return a | lax.shift_right_logical(b, 16)


def _unpack_halves(p):
    a = lax.bitcast_convert_type(p & jnp.int32(-65536), F32).astype(BF16)
    b = lax.bitcast_convert_type(lax.shift_left(p, 16), F32).astype(BF16)
    return a, b


def _const_spec(shape):
    nd = len(shape)
    return pl.BlockSpec(shape, lambda *_: (0,) * nd, pipeline_mode=pl.Buffered(1))


PROJ_QKVR = 2 * GLA_H * GLA_DK + 2 * GLA_H * GLA_DV
PROJ_U = S5_G * S5_N
PROJ_CHUNK = 512


def _proj_kernel(x_ref, g1_ref, wt_in_ref, w2_ref, gb_ref, qkvr_ref, lg_ref, u_ref,
                 wq_ref, wg_ref, wu_ref, stage_ref, gstage_ref, sem, gsem):
    @pl.when(pl.program_id(0) == 0)
    def _():
        pieces = ([(wq_ref, c) for c in range(PROJ_QKVR // PROJ_CHUNK)]
                  + [(wu_ref, c) for c in range(PROJ_U // PROJ_CHUNK)])

        def first_row(k):
            dst, c = pieces[k]
            return c * PROJ_CHUNK + (0 if dst is wq_ref else PROJ_QKVR + GATE_RANK)

        def piece(k):
            return pltpu.make_async_copy(wt_in_ref.at[0, pl.ds(first_row(k), PROJ_CHUNK), :], stage_ref.at[k % 2],
                                         sem.at[k % 2])

        gate = pltpu.make_async_copy(wt_in_ref.at[0, pl.ds(PROJ_QKVR, GATE_RANK), :], gstage_ref, gsem)
        gate.start()
        piece(0).start()
        for k, (dst, c) in enumerate(pieces):
            piece(k).wait()
            if k + 1 < len(pieces):
                piece(k + 1).start()
            dst[:, c * PROJ_CHUNK:(c + 1) * PROJ_CHUNK] = stage_ref[k % 2].T.astype(BF16)
        gate.wait()
        padded = jnp.concatenate([gstage_ref[...], jnp.zeros((128 - GATE_RANK, D), F32)], axis=0)
        wg_ref[...] = padded.T.astype(BF16)

    hb = _rms(x_ref[...], g1_ref[...]).astype(BF16)
    p = _dot(hb, wq_ref[...])
    nq = GLA_H * GLA_DK
    qkvr_ref[:, :nq] = (p[:, :nq] * (GLA_DK ** -0.5)).astype(BF16)
    qkvr_ref[:, nq:] = p[:, nq:].astype(BF16)
    glr = _dot(hb, wg_ref[...])
    z = _dot3(glr, w2_ref[...]) + gb_ref[...]
    lg_ref[...] = (jnp.minimum(z, 0.0) - jnp.log(1.0 + jnp.exp(-jnp.abs(z)))) * (1.0 / GATE_TAU)
    u_ref[...] = _dot(hb, wu_ref[...])


def _proj(x, g1, wt_in, w2, gb):
    tm = 256
    return pl.pallas_call(
        _proj_kernel,
        grid=(T // tm,),
        in_specs=[
            pl.BlockSpec((tm, D), lambda i: (i, 0)),
            _const_spec((1, D)),
            pl.BlockSpec(memory_space=pl.ANY),
            _const_spec(w2.shape), _const_spec(gb.shape),
        ],
        out_specs=[
            pl.BlockSpec((tm, PROJ_QKVR), lambda i: (i, 0)),
            pl.BlockSpec((tm, GLA_H * GLA_DK), lambda i: (i, 0)),
            pl.BlockSpec((tm, PROJ_U), lambda i: (i, 0)),
        ],
        out_shape=[
            jax.ShapeDtypeStruct((T, PROJ_QKVR), BF16),
            jax.ShapeDtypeStruct((T, GLA_H * GLA_DK), F32),
            jax.ShapeDtypeStruct((T, PROJ_U), F32),
        ],
        scratch_shapes=[
            pltpu.VMEM((D, PROJ_QKVR), BF16),
            pltpu.VMEM((D, 128), BF16),
            pltpu.VMEM((D, PROJ_U), BF16),
            pltpu.VMEM((2, PROJ_CHUNK, D), F32),
            pltpu.VMEM((GATE_RANK, D), F32),
            pltpu.SemaphoreType.DMA((2,)),
            pltpu.SemaphoreType.DMA,
        ],
        compiler_params=pltpu.CompilerParams(
            dimension_semantics=("arbitrary",), vmem_limit_bytes=VMEM_LIMIT),
        name="proj",
    )(x, g1, wt_in, w2, gb)


def _gla_kernel(q_ref, k_ref, v_ref, r_ref, lg_ref, gn_ref, tril_ref, esel_ref, o_ref, state_ref):
    C, S = GLA_CHUNK, GLA_SUB
    nsub = C // S

    @pl.when(pl.program_id(0) == 0)
    def _():
        state_ref[...] = jnp.zeros_like(state_ref)

    row = lax.broadcasted_iota(jnp.int32, (C, C), 0)
    col = lax.broadcasted_iota(jnp.int32, (C, C), 1)
    below = (col // S) < (row // S)
    diag = ((col // S) == (row // S)) & (col <= row)
    tril = tril_ref[...]
    esel = esel_ref[...]
    gn = gn_ref[...]

    def head(h, rows, b):
        kcols = slice(h * GLA_DK, (h + 1) * GLA_DK)
        vcols = slice(h * GLA_DV, (h + 1) * GLA_DV)
        q = q_ref[rows, kcols].astype(F32)
        k = k_ref[rows, kcols].astype(F32)
        v = v_ref[rows, vcols]
        bl = b[C - 1:C, :]
        st = state_ref[h]
        o = _dot((q * jnp.exp2(b)).astype(BF16), st.astype(BF16), NT)
        s_rows = [jnp.zeros((S, C), F32)]
        for j in range(1, nsub):
            ref = b[j * S - 1:j * S, :]
            qj = (q[j * S:(j + 1) * S] * jnp.exp2(b[j * S:(j + 1) * S] - ref)).astype(BF16)
            kj = (k * jnp.exp2(jnp.minimum(ref - b, 0.0))).astype(BF16)
            s_rows.append(_dot(qj, kj, NT))
        s_off = jnp.concatenate(s_rows, axis=0)
        q3 = q.reshape(nsub, S, GLA_DK)
        k3 = k.reshape(nsub, S, GLA_DK)
        b3 = b.reshape(nsub, S, GLA_DK)
        xs = []
        for s in range(S):
            lo = (s // 8) * 8
            dec = jnp.exp2(jnp.minimum(b3[:, lo:] - b3[:, s:s + 1, :], 0.0))
            x = q3[:, lo:] * k3[:, s:s + 1, :] * dec
            if lo:
                x = jnp.concatenate([jnp.zeros((nsub, lo, GLA_DK), F32), x], axis=1)
            xs.append(x.reshape(C, GLA_DK).astype(BF16))
        dsc = _dot(jnp.concatenate(xs, axis=1), esel)
        a = jnp.where(below, s_off, jnp.where(diag, dsc, 0.0))
        o = o + _dot(a.astype(BF16), v)
        kout = (k * jnp.exp2(bl - b)).astype(BF16)
        state_ref[h] = st * jnp.exp2(bl) + _dot(v, kout, TN)
        y = _rms(o, gn)
        r = r_ref[rows, vcols].astype(F32)
        o_ref[rows, vcols] = (y * (r * _sigmoid(r))).astype(BF16)

    def chunk(c, carry):
        rows = pl.ds(pl.multiple_of(c * C, C), C)
        g = lg_ref[rows, :]
        g_hi = g.astype(BF16)
        g_mid = (g - g_hi.astype(F32)).astype(BF16)
        g_lo = (g - g_hi.astype(F32) - g_mid.astype(F32)).astype(BF16)
        b = (_dot(tril, g_hi) + (_dot(tril, g_mid) + _dot(tril, g_lo))) * math.log2(math.e)
        for h in range(GLA_H):
            head(h, rows, b[:, h * GLA_DK:(h + 1) * GLA_DK])
        return carry

    lax.fori_loop(0, q_ref.shape[0] // C, chunk, 0, unroll=8)


def _gla(qkvr, lg, gn):
    tb = 512
    C, S = GLA_CHUNK, GLA_SUB
    nk, nv = GLA_H * GLA_DK, GLA_H * GLA_DV
    tril = jnp.tril(jnp.ones((C, C), BF16))
    esel = (jnp.arange(S * GLA_DK)[:, None] // GLA_DK == jnp.arange(C)[None, :] % S).astype(BF16)
    return pl.pallas_call(
        _gla_kernel,
        grid=(T // tb,),
        in_specs=[
            pl.BlockSpec((tb, nk), lambda i: (i, 0)),
            pl.BlockSpec((tb, nk), lambda i: (i, 1)),
            pl.BlockSpec((tb, nv), lambda i: (i, 1)),
            pl.BlockSpec((tb, nv), lambda i: (i, 2)),
            pl.BlockSpec((tb, nk), lambda i: (i, 0)),
            _const_spec((1, GLA_DV)), _const_spec((C, C)), _const_spec((S * GLA_DK, C)),
        ],
        out_specs=pl.BlockSpec((tb, nv), lambda i: (i, 0)),
        out_shape=jax.ShapeDtypeStruct((T, nv), BF16),
        scratch_shapes=[pltpu.VMEM((GLA_H, GLA_DV, GLA_DK), F32)],
        compiler_params=pltpu.CompilerParams(
            dimension_semantics=("arbitrary",), vmem_limit_bytes=VMEM_LIMIT),
        name="gla",
    )(qkvr, qkvr, qkvr, qkvr, lg, gn, tril, esel)


def _s5_tables(a_re, a_im, b_re, b_im, c_re, c_im, d_skip, log_step):
    L, G, P, N = S5_L, S5_G, S5_P, S5_N
    delta = jnp.exp(log_step)[:, None]
    ar, ai = a_re * delta, a_im * delta

    def apow(tau):
        tau = jnp.asarray(tau, F32)[None, :, None]
        mag = jnp.exp(ar[:, None, :] * tau)
        ph = ai[:, None, :] * tau
        return mag * jnp.cos(ph), mag * jnp.sin(ph)

    p_re, p_im = apow(jnp.arange(L + 1))
    ab_re, ab_im = p_re[:, 1], p_im[:, 1]
    den = a_re * a_re + a_im * a_im
    cf_re = ((ab_re - 1.0) * a_re + ab_im * a_im) / den
    cf_im = (ab_im * a_re - (ab_re - 1.0) * a_im) / den
    bb_re = cf_re[:, :, None] * b_re - cf_im[:, :, None] * b_im
    bb_im = cf_re[:, :, None] * b_im + cf_im[:, :, None] * b_re
    e_re = p_re[:, :L, :, None] * bb_re[:, None] - p_im[:, :L, :, None] * bb_im[:, None]
    e_im = p_re[:, :L, :, None] * bb_im[:, None] + p_im[:, :L, :, None] * bb_re[:, None]
    kk = (jnp.einsum('gnp,gtpm->gtnm', c_re, e_re, precision=HIGHEST)
          - jnp.einsum('gnp,gtpm->gtnm', c_im, e_im, precision=HIGHEST))
    kk = kk.at[:, 0].add(jnp.eye(N, dtype=F32)[None] * d_skip[:, :, None])
    r_re, r_im = apow(L - 1 - jnp.arange(L))
    bs_re = (r_re[:, :, None, :] * bb_re.transpose(0, 2, 1)[:, None] - r_im[:, :, None, :] * bb_im.transpose(0, 2, 1)[:, None])
    bs_im = (r_re[:, :, None, :] * bb_im.transpose(0, 2, 1)[:, None] + r_im[:, :, None, :] * bb_re.transpose(0, 2, 1)[:, None])
    ct_re = c_re.transpose(0, 2, 1)[:, :, None, :]
    ct_im = c_im.transpose(0, 2, 1)[:, :, None, :]
    q_re = p_re[:, 1:].transpose(0, 2, 1)[:, :, :, None]
    q_im = p_im[:, 1:].transpose(0, 2, 1)[:, :, :, None]
    cre = ct_re * q_re - ct_im * q_im
    cim = -(ct_re * q_im + ct_im * q_re)
    GT = S5_GT
    nt = G // GT

    def tile_groups(w):
        return w.reshape((nt, GT) + w.shape[1:])

    kc = tile_groups(kk).transpose(0, 1, 4, 2, 3).reshape(nt, GT * N, L * N)

    def in_table(w):
        return tile_groups(w).transpose(0, 2, 1, 3, 4).reshape(nt, L * GT * N, P)

    def out_table(w):
        return w.reshape(nt, GT * P, L * N)

    nstep = (T // L - 1).bit_length()
    sc_re, sc_im = apow(jnp.asarray([L * 2 ** i for i in range(nstep)]))

    def lanes(w):
        return tile_groups(w).transpose(0, 2, 1, 3).reshape(nt, w.shape[1], GT * P)

    r1 = jnp.arange(L * N)[:, None]
    c1 = jnp.arange(L * GT * N)[None, :]
    wide_tn = ((r1 // N == c1 // (GT * N)) & (r1 % N == c1 % N)).astype(BF16)
    wide_p = (jnp.arange(P)[:, None] == jnp.arange(GT * P)[None, :] % P).astype(BF16)
    return dict(
        kc=kc.astype(BF16),
        bs_re=in_table(bs_re).astype(BF16), bs_im=in_table(bs_im).astype(BF16),
        cre=out_table(cre).astype(BF16), cim=out_table(cim).astype(BF16),
        sc_re=lanes(sc_re), sc_im=lanes(sc_im), wide_tn=wide_tn, wide_p=wide_p,
    )


def _s5_kernel(u_ref, kc_ref, bsr_ref, bsi_ref, cre_ref, cim_ref, scr_ref, sci_ref, wtn_ref, wp_ref,
               y_ref, m_ref):
    L, N, P = S5_L, S5_N, S5_P
    W = S5_GT * N
    nch = u_ref.shape[0] // L

    @pl.when(pl.program_id(0) == 0)
    def _():
        m_ref[...] = jnp.zeros_like(m_ref)

    def widen(compact, wide, row_group, col_group):
        full = _dot(compact, wide)
        r = lax.broadcasted_iota(jnp.int32, full.shape, 0)
        c = lax.broadcasted_iota(jnp.int32, full.shape, 1)
        return jnp.where(row_group(r) == col_group(c), full, 0.0).astype(BF16)

    wtn, wp = wtn_ref[...], wp_ref[...]
    bd = widen(kc_ref[0], wtn, lambda r: r // N, lambda c: (c % W) // N)
    bsr = widen(bsr_ref[0], wp, lambda r: (r % W) // N, lambda c: c // P)
    bsi = widen(bsi_ref[0], wp, lambda r: (r % W) // N, lambda c: c // P)
    cre = widen(cre_ref[0], wtn, lambda r: r // P, lambda c: (c % W) // N)
    cim = widen(cim_ref[0], wtn, lambda r: r // P, lambda c: (c % W) // N)
    for s in range(L):
        m_ref[s * W:(s + 1) * W, s * W:] = bd[:, :(L - s) * W]
    ucat = jnp.concatenate([u_ref[pl.ds(s, nch, stride=L), :].astype(BF16) for s in range(L)], axis=1)
    hr = _dot(ucat, bsr)
    hi = _dot(ucat, bsi)
    row = lax.broadcasted_iota(jnp.int32, hr.shape, 0)

    def shift(x, d):
        if d % 8 == 0:
            return jnp.concatenate([jnp.zeros((d, x.shape[1]), F32), x[:nch - d]], axis=0)
        return jnp.where(row >= d, pltpu.roll(x, d, 0), 0.0)

    for i in range(scr_ref.shape[1]):
        mr, mi = scr_ref[0, i:i + 1, :], sci_ref[0, i:i + 1, :]
        pr, pi = shift(hr, 2 ** i), shift(hi, 2 ** i)
        hr, hi = hr + mr * pr - mi * pi, hi + mr * pi + mi * pr
    gr, gi = shift(hr, 1), shift(hi, 1)
    grb, gib = gr.astype(BF16), gi.astype(BF16)
    for tp in range(L // 2):
        cols = slice(2 * tp * W, (2 * tp + 2) * W)
        depth = (2 * tp + 2) * W
        y = _dot(ucat[:, :depth], m_ref[:depth, cols]) + _dot(grb, cre[:, cols]) + _dot(gib, cim[:, cols])
        for t in (2 * tp, 2 * tp + 1):
            y_ref[pl.ds(t, nch, stride=L), :] = y[:, (t - 2 * tp) * W:(t - 2 * tp + 1) * W]


def _s5(u, tb):
    W = S5_GT * S5_N
    nt = S5_G // S5_GT

    def tile_spec(a):
        return pl.BlockSpec((1,) + a.shape[1:], lambda g: (g,) + (0,) * (a.ndim - 1))

    ws = [tb[n] for n in ["kc", "bs_re", "bs_im", "cre", "cim", "sc_re", "sc_im"]]
    consts = [tb["wide_tn"], tb["wide_p"]]
    return pl.pallas_call(
        _s5_kernel,
        grid=(nt,),
        in_specs=([pl.BlockSpec((T, W), lambda g: (0, g))] + [tile_spec(w) for w in ws]
                  + [_const_spec(c.shape) for c in consts]),
        out_specs=pl.BlockSpec((T, W), lambda g: (0, g)),
        out_shape=jax.ShapeDtypeStruct((T, S5_G * S5_N), F32),
        scratch_shapes=[pltpu.VMEM((S5_L * W, S5_L * W), BF16)],
        compiler_params=pltpu.CompilerParams(
            dimension_semantics=("arbitrary",), vmem_limit_bytes=VMEM_LIMIT),
        name="s5",
    )(u, *ws, *consts)


def _post_kernel(y_ref, gla_ref, x_ref, gw_ref, gb_ref, sg_ref, wo_ref, n2_ref, rw_ref, rb_ref, before_ref,
                 h_ref, hn_ref, idx_ref, gate_ref, rank_ref, cnt_ref, run_ref):
    @pl.when(pl.program_id(0) == 0)
    def _():
        run_ref[...] = jnp.zeros_like(run_ref)

    y = y_ref[...]
    z = 0.5 * y * (1.0 + jnp.tanh(math.sqrt(2.0 / math.pi) * (y + 0.044715 * (y * y * y))))
    z = z * _sigmoid(_dot(z.astype(BF16), gw_ref[...]) + gb_ref[...])
    s5o = _rms(z, sg_ref[...]).astype(BF16)
    half = GLA_H * GLA_DV
    h = x_ref[...] + _dot(gla_ref[...], wo_ref[:half, :]) + _dot(s5o, wo_ref[half:, :])
    h_ref[...] = h
    hn = _rms(h, n2_ref[...])
    hi = hn.astype(BF16)
    hn_ref[...] = _pack_halves(hi[:, :D // 2], hi[:, D // 2:])
    lo = (hn - hi.astype(F32)).astype(BF16)
    rw = rw_ref[...]
    rw_hi = rw.astype(BF16)
    rw_lo = (rw - rw_hi.astype(F32)).astype(BF16)
    lg = _dot(hi, rw_hi) + (_dot(lo, rw_hi) + _dot(hi, rw_lo))
    lt = lg.T[:N_EXP] + rb_ref[...]
    eid = lax.broadcasted_iota(jnp.int32, lt.shape, 0).astype(F32)
    vals, idxs = [], []
    for _ in range(TOP_K):
        m = jnp.max(lt, axis=0, keepdims=True)
        sel = jnp.min(jnp.where(lt == m, eid, float(N_EXP)), axis=0, keepdims=True)
        vals.append(m)
        idxs.append(sel)
        lt = jnp.where(eid == sel, -jnp.inf, lt)
    ex = [jnp.exp(vv - vals[0]) for vv in vals]
    inv = 1.0 / (ex[0] + ex[1] + ex[2] + ex[3])
    idx_ref[...] = jnp.concatenate(idxs, axis=0).astype(jnp.int32)
    gate_ref[...] = jnp.concatenate([e * inv for e in ex], axis=0)
    run = run_ref[:, :1]
    ranks = []
    for sel in idxs:
        onehot = jnp.where(eid == sel, 1.0, 0.0)
        earlier = _dot(onehot.astype(BF16), before_ref[...])
        ranks.append(jnp.sum(onehot * (run + earlier), axis=0, keepdims=True))
        run = run + jnp.sum(onehot, axis=1, keepdims=True)
    rank_ref[...] = jnp.concatenate(ranks, axis=0).astype(jnp.int32)
    run_ref[...] = jnp.broadcast_to(run, run_ref.shape)
    cnt_ref[...] = jnp.broadcast_to(run, cnt_ref.shape)


def _post(y, gla, x, gw, gb, sg, wo, n2, rwt, rb):
    tm = 256
    width = S5_G * S5_N
    before = (jnp.arange(tm)[:, None] < jnp.arange(tm)[None, :]).astype(BF16)
    return pl.pallas_call(
        _post_kernel,
        grid=(T // tm,),
        in_specs=[
            pl.BlockSpec((tm, width), lambda i: (i, 0)),
            pl.BlockSpec((tm, GLA_H * GLA_DV), lambda i: (i, 0)),
            pl.BlockSpec((tm, D), lambda i: (i, 0)),
            _const_spec(gw.shape), _const_spec(gb.shape), _const_spec(sg.shape), _const_spec(wo.shape),
            _const_spec(n2.shape), _const_spec(rwt.shape), _const_spec(rb.shape), _const_spec(before.shape),
        ],
        out_specs=[
            pl.BlockSpec((tm, D), lambda i: (i, 0)),
            pl.BlockSpec((tm, D // 2), lambda i: (i, 0)),
            pl.BlockSpec((TOP_K, tm), lambda i: (0, i)),
            pl.BlockSpec((TOP_K, tm), lambda i: (0, i)),
            pl.BlockSpec((TOP_K, tm), lambda i: (0, i)),
            pl.BlockSpec((N_EXP, 128), lambda i: (0, 0)),
        ],
        out_shape=[
            jax.ShapeDtypeStruct((T, D), F32),
            jax.ShapeDtypeStruct((T, D // 2), jnp.int32),
            jax.ShapeDtypeStruct((TOP_K, T), jnp.int32),
            jax.ShapeDtypeStruct((TOP_K, T), F32),
            jax.ShapeDtypeStruct((TOP_K, T), jnp.int32),
            jax.ShapeDtypeStruct((N_EXP, 128), F32),
        ],
        scratch_shapes=[pltpu.VMEM((N_EXP, 128), F32)],
        compiler_params=pltpu.CompilerParams(
            dimension_semantics=("arbitrary",), vmem_limit_bytes=VMEM_LIMIT),
        name="post",
    )(y, gla, x, gw, gb, sg, wo, n2, rwt, rb, before)


def _route(top_idx, rank, counts):
    e_flat = top_idx.reshape(-1)
    rank = rank.reshape(-1)
    counts = counts[:, 0].astype(jnp.int32)
    padded = (counts + MOE_RB - 1) // MOE_RB * MOE_RB
    pad_end = jnp.cumsum(padded)
    pad_start = pad_end - padded
    dest = _dest(top_idx, rank.reshape(top_idx.shape), pad_start.astype(jnp.int32)).reshape(-1)
    nvis_e = (counts + MOE_RMAX - 1) // MOE_RMAX
    vis_end = jnp.cumsum(nvis_e)
    n_vis = vis_end[-1]
    v = jnp.arange(MOE_NV, dtype=jnp.int32)
    vc = jnp.minimum(v, n_vis - 1)
    ve = jnp.minimum(jnp.searchsorted(vis_end, vc, side='right'), N_EXP - 1).astype(jnp.int32)
    local = vc - (vis_end[ve] - nvis_e[ve])
    vcnt = jnp.where(v < n_vis, jnp.minimum(MOE_RMAX, counts[ve] - local * MOE_RMAX), 0)
    vrow = pad_start[ve] + local * MOE_RMAX
    return dict(dest=dest, fill_from=(pad_start + counts).astype(jnp.int32), fill_to=pad_end.astype(jnp.int32),
                tail_blk=(pad_end[-1:] // MOE_RB).astype(jnp.int32),
                vis_e=ve, vis_cnt=vcnt.astype(jnp.int32), vis_row=vrow.astype(jnp.int32),
                n_vis=n_vis.reshape(1).astype(jnp.int32))


def _dest_kernel(start_ref, idx_ref, rank_ref, dest_ref):
    e = idx_ref[...]
    d = rank_ref[...]
    for x in range(N_EXP):
        d = d + jnp.where(e == x, start_ref[x], 0)
    dest_ref[...] = d


def _dest(top_idx, rank, pad_start):
    full = pl.BlockSpec(top_idx.shape, lambda i, *_: (0, 0))
    return pl.pallas_call(
        _dest_kernel,
        grid_spec=pltpu.PrefetchScalarGridSpec(num_scalar_prefetch=1, grid=(1,), in_specs=[full, full], out_specs=full),
        out_shape=jax.ShapeDtypeStruct(top_idx.shape, jnp.int32),
        name="dest",
    )(pad_start, top_idx, rank)


def _row_copy(src_ref, srow, dst_ref, drow, sem):
    return pltpu.make_async_copy(src_ref.at[pl.ds(srow, 1), :], dst_ref.at[pl.ds(drow, 1), :], sem)


def _tail_fill(src_ref, dst_ref, tail_ref, sem):
    def cp(b):
        return pltpu.make_async_copy(src_ref.at[pl.ds(0, MOE_RB), :],
                                     dst_ref.at[pl.ds(pl.multiple_of(b * MOE_RB, MOE_RB), MOE_RB), :], sem)

    def start(b, c):
        cp(b).start()
        return c
    lax.fori_loop(tail_ref[0], MOE_ROWS // MOE_RB, start, 0)

    def finish(b, c):
        cp(b).wait()
        return c
    lax.fori_loop(tail_ref[0], MOE_ROWS // MOE_RB, finish, 0)


def _scatter_kernel(dest_ref, from_ref, to_ref, tail_ref, hn_ref, xs_ref, zero_ref, sem, zsem):
    i = pl.program_id(0)
    tm = hn_ref.shape[0]

    @pl.when(i == 0)
    def _():
        zero_ref[...] = jnp.zeros_like(zero_ref)
        _tail_fill(zero_ref, xs_ref, tail_ref, zsem)

        def per_expert(e, c):
            def fill(r, c2):
                _row_copy(zero_ref, 0, xs_ref, r, zsem).start()
                return c2
            lax.fori_loop(from_ref[e], to_ref[e], fill, 0)

            def drain(r, c2):
                _row_copy(zero_ref, 0, xs_ref, r, zsem).wait()
                return c2
            lax.fori_loop(from_ref[e], to_ref[e], drain, 0)
            return c
        lax.fori_loop(0, N_EXP, per_expert, 0)

    def issue(t, c):
        for k in range(TOP_K):
            _row_copy(hn_ref, t, xs_ref, dest_ref[k * T + i * tm + t], sem).start(priority=k % 2)
        return c
    lax.fori_loop(0, tm, issue, 0)

    for k in range(TOP_K):
        pltpu.make_async_copy(hn_ref, xs_ref.at[pl.ds(0, tm), :], sem).wait()


def _scatter(hn, rt):
    tm = 1024
    return pl.pallas_call(
        _scatter_kernel,
        grid_spec=pltpu.PrefetchScalarGridSpec(
            num_scalar_prefetch=4,
            grid=(T // tm,),
            in_specs=[pl.BlockSpec((tm, D // 2), lambda i, *_: (i, 0))],
            out_specs=pl.BlockSpec(memory_space=pl.ANY),
            scratch_shapes=[pltpu.VMEM((MOE_RB, D // 2), jnp.int32), pltpu.SemaphoreType.DMA,
                            pltpu.SemaphoreType.DMA],
        ),
        out_shape=jax.ShapeDtypeStruct((MOE_ROWS, D // 2), jnp.int32),
        compiler_params=pltpu.CompilerParams(
            dimension_semantics=("arbitrary",), vmem_limit_bytes=VMEM_LIMIT),
        name="scatter",
    )(rt["dest"], rt["fill_from"], rt["fill_to"], rt["tail_blk"], hn)


def _experts_kernel(ve_ref, vcnt_ref, vrow_ref, nvis_ref, tail_ref, xs_ref, w1_ref, w2_ref, b1_ref, b2_ref, ys_ref,
                    xin_ref, xb_ref, acc_ref, wg_ref, wl_ref, wd_ref, isem, osem, wsem):
    v = pl.program_id(0)
    RB = MOE_RB
    BIG, MID = 4 * RB, 2 * RB
    TF, NF = MOE_TF, MOE_NF
    nvis = nvis_ref[0]
    active = v < nvis

    def geometry(vv):
        nblk = (vcnt_ref[vv] + RB - 1) // RB
        return vrow_ref[vv], nblk // 4, (nblk // 2) % 2, nblk % 2

    row0, nbig, mid, small = geometry(v)
    mid_r0 = nbig * BIG
    small_r0 = mid_r0 + mid * MID

    def x_copy(vrow, r0, n, slot):
        return pltpu.make_async_copy(xs_ref.at[pl.ds(pl.multiple_of(vrow + r0, RB), n), :],
                                     xin_ref.at[slot, pl.ds(0, n), :], isem.at[slot])

    def y_copy(r0, n):
        return pltpu.make_async_copy(acc_ref.at[pl.ds(pl.multiple_of(r0, RB), n), :],
                                     ys_ref.at[pl.ds(pl.multiple_of(row0 + r0, RB), n), :], osem)

    def tile_copies(e, t, slot):
        cols = pl.ds(pl.multiple_of(t * TF, TF), TF)
        lin_cols = pl.ds(pl.multiple_of(D_FF + t * TF, TF), TF)
        return (pltpu.make_async_copy(w1_ref.at[e, :, cols], wg_ref.at[slot], wsem.at[slot]),
                pltpu.make_async_copy(w1_ref.at[e, :, lin_cols], wl_ref.at[slot], wsem.at[slot]),
                pltpu.make_async_copy(w2_ref.at[e, cols, :], wd_ref.at[slot], wsem.at[slot]))

    def start_unit(vv, u):
        vrow, nb, md, sm = geometry(vv)

        @pl.when(u < nb)
        def _():
            x_copy(vrow, u * BIG, BIG, u % 2).start()

        @pl.when((u == nb) & (md == 1))
        def _():
            x_copy(vrow, nb * BIG, MID, u % 2).start()

        @pl.when((u == nb + md) & (sm == 1))
        def _():
            x_copy(vrow, nb * BIG + md * MID, RB, u % 2).start()

    @pl.when(v == 0)
    def _():
        acc_ref[...] = jnp.zeros_like(acc_ref)
        _tail_fill(acc_ref, ys_ref, tail_ref, osem)
        start_unit(0, 0)
        for cp in tile_copies(ve_ref[0], 0, 0):
            cp.start()

    def step(j, carry):
        slot = j % 2
        first = j == 0
        last = j == NF - 1
        for cp in tile_copies(ve_ref[v], j, slot):
            cp.wait()

        @pl.when(j + 1 < NF)
        def _():
            for cp in tile_copies(ve_ref[v], j + 1, 1 - slot):
                cp.start()

        @pl.when(last & (v + 1 < nvis))
        def _():
            for cp in tile_copies(ve_ref[jnp.minimum(v + 1, MOE_NV - 1)], 0, 1 - slot):
                cp.start()

        b1g = b1_ref[0, :, pl.ds(pl.multiple_of(j * TF, TF), TF)]
        b1l = b1_ref[0, :, pl.ds(pl.multiple_of(D_FF + j * TF, TF), TF)]

        def block(u, r0, n):
            rows = pl.ds(pl.multiple_of(r0, RB), n)
            xslot = u % 2

            @pl.when(first)
            def _():
                x_copy(row0, r0, n, xslot).wait()
                start_unit(v, u + 1)
                lo_half, hi_half = _unpack_halves(xin_ref[xslot, pl.ds(0, n), :])
                xb_ref[rows, :D // 2] = lo_half
                xb_ref[rows, D // 2:] = hi_half

            x = xb_ref[rows, :]
            glu = jnp.minimum(_dot(x, wg_ref[slot].astype(BF16)) + b1g, SWIGLU_LIMIT)
            lin = jnp.clip(_dot(x, wl_ref[slot].astype(BF16)) + b1l, -SWIGLU_LIMIT, SWIGLU_LIMIT)
            act = glu * _sigmoid(SWIGLU_ALPHA * glu) * (lin + 1.0)
            part = _dot(act.astype(BF16), wd_ref[slot].astype(BF16))

            acc_ref[rows, :] = part + jnp.where(first, b2_ref[0], acc_ref[rows, :])

            @pl.when(last)
            def _():
                y_copy(r0, n).start()

        def big(i, c):
            block(i, i * BIG, BIG)
            return c
        lax.fori_loop(0, nbig, big, 0)

        @pl.when(mid == 1)
        def _():
            block(nbig, mid_r0, MID)

        @pl.when(small == 1)
        def _():
            block(nbig + mid, small_r0, RB)

        @pl.when(last)
        def _():
            def finish(i, c):
                y_copy(i * BIG, BIG).wait()
                return c
            lax.fori_loop(0, nbig, finish, 0)

            @pl.when(mid == 1)
            def _():
                y_copy(mid_r0, MID).wait()

            @pl.when(small == 1)
            def _():
                y_copy(small_r0, RB).wait()

            @pl.when(v + 1 < nvis)
            def _():
                start_unit(v + 1, 0)
        return carry

    @pl.when(active)
    def _():
        lax.fori_loop(0, NF, step, 0)


def _experts(xs, rt, w1, b1, w2, b2):
    b1 = b1.reshape(N_EXP, 1, 2 * D_FF)
    b2 = b2.reshape(N_EXP, 1, D)
    any_spec = pl.BlockSpec(memory_space=pl.ANY)
    return pl.pallas_call(
        _experts_kernel,
        grid_spec=pltpu.PrefetchScalarGridSpec(
            num_scalar_prefetch=5,
            grid=(MOE_NV,),
            in_specs=[
                any_spec, any_spec, any_spec,
                pl.BlockSpec((1, 1, 2 * D_FF), lambda v, ve, *s: (ve[v], 0, 0)),
                pl.BlockSpec((1, 1, D), lambda v, ve, *s: (ve[v], 0, 0)),
            ],
            out_specs=any_spec,
            scratch_shapes=[
                pltpu.VMEM((2, 4 * MOE_RB, D // 2), jnp.int32),
                pltpu.VMEM((MOE_RMAX, D), BF16),
                pltpu.VMEM((MOE_RMAX, D), F32),
                pltpu.VMEM((2, D, MOE_TF), F32),
                pltpu.VMEM((2, D, MOE_TF), F32),
                pltpu.VMEM((2, MOE_TF, D), F32),
                pltpu.SemaphoreType.DMA((2,)),
                pltpu.SemaphoreType.DMA,
                pltpu.SemaphoreType.DMA((2,)),
            ],
        ),
        out_shape=jax.ShapeDtypeStruct((MOE_ROWS, D), F32),
        compiler_params=pltpu.CompilerParams(
            dimension_semantics=("arbitrary",), vmem_limit_bytes=VMEM_LIMIT),
        name="experts",
    )(rt["vis_e"], rt["vis_cnt"], rt["vis_row"], rt["n_vis"], rt["tail_blk"], xs, w1, w2, b1, b2)


def _combine_kernel(dest_ref, ys_ref, h_ref, gate_ref, gf_ref, o_ref, buf_ref, sem):
    i = pl.program_id(0)
    tm = h_ref.shape[0]

    def gather(tile, slot):
        def issue(t, c):
            for k in range(TOP_K):
                _row_copy(ys_ref, dest_ref[k * T + tile * tm + t], buf_ref.at[slot, k], t,
                          sem.at[slot]).start(priority=k % 2)
            return c
        lax.fori_loop(0, tm, issue, 0)

    @pl.when(i == 0)
    def _():
        gather(0, 0)

    @pl.when(i + 1 < pl.num_programs(0))
    def _():
        gather(i + 1, (i + 1) % 2)

    slot = i % 2
    for k in range(TOP_K):
        pltpu.make_async_copy(ys_ref.at[pl.ds(0, tm), :], buf_ref.at[slot, k], sem.at[slot]).wait()

    h = h_ref[...]
    for k in range(TOP_K):
        h = h + gate_ref[:, k:k + 1] * buf_ref[slot, k]
    o_ref[...] = _rms(h, gf_ref[...])


def _combine(ys, h, gates_t, gf, rt):
    tm = 256
    return pl.pallas_call(
        _combine_kernel,
        grid_spec=pltpu.PrefetchScalarGridSpec(
            num_scalar_prefetch=1,
            grid=(T // tm,),
            in_specs=[
                pl.BlockSpec(memory_space=pl.ANY),
                pl.BlockSpec((tm, D), lambda i, *_: (i, 0)),
                pl.BlockSpec((tm, TOP_K), lambda i, *_: (i, 0)),
                pl.BlockSpec((1, D), lambda i, *_: (0, 0)),
            ],
            out_specs=pl.BlockSpec((tm, D), lambda i, *_: (i, 0)),
            scratch_shapes=[pltpu.VMEM((2, TOP_K, tm, D), F32), pltpu.SemaphoreType.DMA((2,))],
        ),
        out_shape=jax.ShapeDtypeStruct((T, D), F32),
        compiler_params=pltpu.CompilerParams(
            dimension_semantics=("arbitrary",), vmem_limit_bytes=VMEM_LIMIT),
        name="combine",
    )(rt["dest"], ys, h, gates_t, gf)


def kernel(x, norm1_g, w_in, gla_gate_w2, gla_gate_b, gla_norm_g, s5_a_re, s5_a_im, s5_b_re, s5_b_im, s5_c_re, s5_c_im, s5_d, s5_log_step, s5_glu_w, s5_glu_b, s5_norm_g, w_out, norm2_g, router_w, router_b, expert_w1, expert_b1, expert_w2, expert_b2, final_norm_g):
    assert x.shape == (1, T, D) and w_in.shape[0] == 1
    xt = x.reshape(T, D)
    w2g = jnp.pad(gla_gate_w2[0], ((0, 128 - GATE_RANK), (0, 0)))
    qkvr, lg, u = _proj(xt, norm1_g, jnp.swapaxes(w_in, 1, 2), w2g, gla_gate_b)
    gla = _gla(qkvr, lg, gla_norm_g)
    tables = _s5_tables(s5_a_re[0], s5_a_im[0], s5_b_re[0], s5_b_im[0], s5_c_re[0], s5_c_im[0],
                        s5_d[0], s5_log_step[0])
    y = _s5(u, tables)
    h, hn, top_idx, gates, rank, counts = _post(
        y, gla, xt, s5_glu_w[0].astype(BF16), s5_glu_b, s5_norm_g, w_out[0].astype(BF16), norm2_g,
        jnp.pad(router_w[0], ((0, 0), (0, 128 - N_EXP))), router_b.reshape(N_EXP, 1))
    rt = _route(top_idx, rank, counts)
    xs = _scatter(hn, rt)
    ys = _experts(xs, rt, expert_w1[0], expert_b1[0], expert_w2[0], expert_b2[0])
    out = _combine(ys, h, gates.T, final_norm_g.reshape(1, D), rt)
    return out.reshape(1, T, D)
```

```python
import math

import jax
import jax.numpy as jnp
from jax import lax
from jax.experimental import pallas as pl
from jax.experimental.pallas import tpu as pltpu

F32 = jnp.float32
BF16 = jnp.bfloat16
HIGHEST = lax.Precision.HIGHEST

T = 8192
D = 2048
GLA_H = 4
GLA_DK = 128
GLA_DV = 256
GLA_CHUNK = 64
GLA_SUB = 16
GATE_RANK = 16
GATE_TAU = 16.0
S5_G = 64
S5_N = 16
S5_P = 64
S5_L = 16
S5_GT = 8
N_EXP = 32
TOP_K = 4
D_FF = 2048
SWIGLU_ALPHA = 1.702
SWIGLU_LIMIT = 7.0
RMS_EPS = 1e-6

MOE_RB = 128
MOE_RMAX = 2176
MOE_TF = 512
MOE_NF = D_FF // MOE_TF
MOE_NV = -(-T * TOP_K // MOE_RMAX) + N_EXP
MOE_ROWS = T * TOP_K + N_EXP * MOE_RB
VMEM_LIMIT = 56 * 1024 * 1024

NN = (((1,), (0,)), ((), ()))
NT = (((1,), (1,)), ((), ()))
TN = (((0,), (0,)), ((), ()))


def _dot(a, b, dims=NN, precision=None):
    return lax.dot_general(a, b, dims, preferred_element_type=F32, precision=precision)


def _dot3(a, b):
    a_hi, b_hi = a.astype(BF16), b.astype(BF16)
    a_lo = (a - a_hi.astype(F32)).astype(BF16)
    b_lo = (b - b_hi.astype(F32)).astype(BF16)
    return _dot(a_hi, b_hi) + (_dot(a_lo, b_hi) + _dot(a_hi, b_lo))


def _rms(x, g):
    return x * lax.rsqrt(jnp.mean(x * x, axis=-1, keepdims=True) + RMS_EPS) * g


def _sigmoid(x):
    return 1.0 / (1.0 + jnp.exp(-x))


def _pack_halves(lo_half, hi_half):
    a = lax.bitcast_convert_type(lo_half.astype(F32), jnp.int32)
    b = lax.bitcast_convert_type(hi_half.astype(F32), jnp.int32)
    return a | lax.shift_right_logical(b, 16)


def _unpack_halves(p):
    a = lax.bitcast_convert_type(p & jnp.int32(-65536), F32).astype(BF16)
    b = lax.bitcast_convert_type(lax.shift_left(p, 16), F32).astype(BF16)
    return a, b


def _const_spec(shape):
    nd = len(shape)
    return pl.BlockSpec(shape, lambda *_: (0,) * nd, pipeline_mode=pl.Buffered(1))


PROJ_QKVR = 2 * GLA_H * GLA_DK + 2 * GLA_H * GLA_DV
PROJ_U = S5_G * S5_N
PROJ_CHUNK = 512


def _proj_kernel(x_ref, g1_ref, wt_in_ref, w2_ref, gb_ref, qkvr_ref, lg_ref, u_ref,
                 wq_ref, wg_ref, wu_ref, stage_ref, gstage_ref, sem, gsem):
    @pl.when(pl.program_id(0) == 0)
    def _():
        pieces = ([(wq_ref, c) for c in range(PROJ_QKVR // PROJ_CHUNK)]
                  + [(wu_ref, c) for c in range(PROJ_U // PROJ_CHUNK)])

        def first_row(k):
            dst, c = pieces[k]
            return c * PROJ_CHUNK + (0 if dst is wq_ref else PROJ_QKVR + GATE_RANK)

        def piece(k):
            return pltpu.make_async_copy(wt_in_ref.at[0, pl.ds(first_row(k), PROJ_CHUNK), :], stage_ref.at[k % 2],
                                         sem.at[k % 2])

        gate = pltpu.make_async_copy(wt_in_ref.at[0, pl.ds(PROJ_QKVR, GATE_RANK), :], gstage_ref, gsem)
        gate.start()
        piece(0).start()
        for k, (dst, c) in enumerate(pieces):
            piece(k).wait()
            if k + 1 < len(pieces):
                piece(k + 1).start()
            dst[:, c * PROJ_CHUNK:(c + 1) * PROJ_CHUNK] = stage_ref[k % 2].T.astype(BF16)
        gate.wait()
        padded = jnp.concatenate([gstage_ref[...], jnp.zeros((128 - GATE_RANK, D), F32)], axis=0)
        wg_ref[...] = padded.T.astype(BF16)

    hb = _rms(x_ref[...], g1_ref[...]).astype(BF16)
    p = _dot(hb, wq_ref[...])
    nq = GLA_H * GLA_DK
    qkvr_ref[:, :nq] = (p[:, :nq] * (GLA_DK ** -0.5)).astype(BF16)
    qkvr_ref[:, nq:] = p[:, nq:].astype(BF16)
    glr = _dot(hb, wg_ref[...])
    z = _dot3(glr, w2_ref[...]) + gb_ref[...]
    lg_ref[...] = (jnp.minimum(z, 0.0) - jnp.log(1.0 + jnp.exp(-jnp.abs(z)))) * (1.0 / GATE_TAU)
    u_ref[...] = _dot(hb, wu_ref[...])


def _proj(x, g1, wt_in, w2, gb):
    tm = 256
    return pl.pallas_call(
        _proj_kernel,
        grid=(T // tm,),
        in_specs=[
            pl.BlockSpec((tm, D), lambda i: (i, 0)),
            _const_spec((1, D)),
            pl.BlockSpec(memory_space=pl.ANY),
            _const_spec(w2.shape), _const_spec(gb.shape),
        ],
        out_specs=[
            pl.BlockSpec((tm, PROJ_QKVR), lambda i: (i, 0)),
            pl.BlockSpec((tm, GLA_H * GLA_DK), lambda i: (i, 0)),
            pl.BlockSpec((tm, PROJ_U), lambda i: (i, 0)),
        ],
        out_shape=[
            jax.ShapeDtypeStruct((T, PROJ_QKVR), BF16),
            jax.ShapeDtypeStruct((T, GLA_H * GLA_DK), F32),
            jax.ShapeDtypeStruct((T, PROJ_U), F32),
        ],
        scratch_shapes=[
            pltpu.VMEM((D, PROJ_QKVR), BF16),
            pltpu.VMEM((D, 128), BF16),
            pltpu.VMEM((D, PROJ_U), BF16),
            pltpu.VMEM((2, PROJ_CHUNK, D), F32),
            pltpu.VMEM((GATE_RANK, D), F32),
            pltpu.SemaphoreType.DMA((2,)),
            pltpu.SemaphoreType.DMA,
        ],
        compiler_params=pltpu.CompilerParams(
            dimension_semantics=("arbitrary",), vmem_limit_bytes=VMEM_LIMIT),
        name="proj",
    )(x, g1, wt_in, w2, gb)


def _gla_kernel(q_ref, k_ref, v_ref, r_ref, lg_ref, gn_ref, tril_ref, esel_ref, o_ref, state_ref):
    C, S = GLA_CHUNK, GLA_SUB
    nsub = C // S

    @pl.when(pl.program_id(0) == 0)
    def _():
        state_ref[...] = jnp.zeros_like(state_ref)

    row = lax.broadcasted_iota(jnp.int32, (C, C), 0)
    col = lax.broadcasted_iota(jnp.int32, (C, C), 1)
    below = (col // S) < (row // S)
    diag = ((col // S) == (row // S)) & (col <= row)
    tril = tril_ref[...]
    esel = esel_ref[...]
    gn = gn_ref[...]

    def head(h, rows, b):
        kcols = slice(h * GLA_DK, (h + 1) * GLA_DK)
        vcols = slice(h * GLA_DV, (h + 1) * GLA_DV)
        q = q_ref[rows, kcols].astype(F32)
        k = k_ref[rows, kcols].astype(F32)
        v = v_ref[rows, vcols]
        bl = b[C - 1:C, :]
        st = state_ref[h]
        o = _dot((q * jnp.exp2(b)).astype(BF16), st.astype(BF16), NT)
        s_rows = [jnp.zeros((S, C), F32)]
        for j in range(1, nsub):
            ref = b[j * S - 1:j * S, :]
            qj = (q[j * S:(j + 1) * S] * jnp.exp2(b[j * S:(j + 1) * S] - ref)).astype(BF16)
            kj = (k * jnp.exp2(jnp.minimum(ref - b, 0.0))).astype(BF16)
            s_rows.append(_dot(qj, kj, NT))
        s_off = jnp.concatenate(s_rows, axis=0)
        q3 = q.reshape(nsub, S, GLA_DK)
        k3 = k.reshape(nsub, S, GLA_DK)
        b3 = b.reshape(nsub, S, GLA_DK)
        xs = []
        for s in range(S):
            lo = (s // 8) * 8
            dec = jnp.exp2(jnp.minimum(b3[:, lo:] - b3[:, s:s + 1, :], 0.0))
            x = q3[:, lo:] * k3[:, s:s + 1, :] * dec
            if lo:
                x = jnp.concatenate([jnp.zeros((nsub, lo, GLA_DK), F32), x], axis=1)
            xs.append(x.reshape(C, GLA_DK).astype(BF16))
        dsc = _dot(jnp.concatenate(xs, axis=1), esel)
        a = jnp.where(below, s_off, jnp.where(diag, dsc, 0.0))
        o = o + _dot(a.astype(BF16), v)
        kout = (k * jnp.exp2(bl - b)).astype(BF16)
        state_ref[h] = st * jnp.exp2(bl) + _dot(v, kout, TN)
        y = _rms(o, gn)
        r = r_ref[rows, vcols].astype(F32)
        o_ref[rows, vcols] = (y * (r * _sigmoid(r))).astype(BF16)

    def chunk(c, carry):
        rows = pl.ds(pl.multiple_of(c * C, C), C)
        g = lg_ref[rows, :]
        g_hi = g.astype(BF16)
        g_mid = (g - g_hi.astype(F32)).astype(BF16)
        g_lo = (g - g_hi.astype(F32) - g_mid.astype(F32)).astype(BF16)
        b = (_dot(tril, g_hi) + (_dot(tril, g_mid) + _dot(tril, g_lo))) * math.log2(math.e)
        for h in range(GLA_H):
            head(h, rows, b[:, h * GLA_DK:(h + 1) * GLA_DK])
        return carry

    lax.fori_loop(0, q_ref.shape[0] // C, chunk, 0, unroll=8)


def _gla(qkvr, lg, gn):
    tb = 512
    C, S = GLA_CHUNK, GLA_SUB
    nk, nv = GLA_H * GLA_DK, GLA_H * GLA_DV
    tril = jnp.tril(jnp.ones((C, C), BF16))
    esel = (jnp.arange(S * GLA_DK)[:, None] // GLA_DK == jnp.arange(C)[None, :] % S).astype(BF16)
    return pl.pallas_call(
        _gla_kernel,
        grid=(T // tb,),
        in_specs=[
            pl.BlockSpec((tb, nk), lambda i: (i, 0)),
            pl.BlockSpec((tb, nk), lambda i: (i, 1)),
            pl.BlockSpec((tb, nv), lambda i: (i, 1)),
            pl.BlockSpec((tb, nv), lambda i: (i, 2)),
            pl.BlockSpec((tb, nk), lambda i: (i, 0)),
            _const_spec((1, GLA_DV)), _const_spec((C, C)), _const_spec((S * GLA_DK, C)),
        ],
        out_specs=pl.BlockSpec((tb, nv), lambda i: (i, 0)),
        out_shape=jax.ShapeDtypeStruct((T, nv), BF16),
        scratch_shapes=[pltpu.VMEM((GLA_H, GLA_DV, GLA_DK), F32)],
        compiler_params=pltpu.CompilerParams(
            dimension_semantics=("arbitrary",), vmem_limit_bytes=VMEM_LIMIT),
        name="gla",
    )(qkvr, qkvr, qkvr, qkvr, lg, gn, tril, esel)


def _s5_tables(a_re, a_im, b_re, b_im, c_re, c_im, d_skip, log_step):
    L, G, P, N = S5_L, S5_G, S5_P, S5_N
    delta = jnp.exp(log_step)[:, None]
    ar, ai = a_re * delta, a_im * delta

    def apow(tau):
        tau = jnp.asarray(tau, F32)[None, :, None]
        mag = jnp.exp(ar[:, None, :] * tau)
        ph = ai[:, None, :] * tau
        return mag * jnp.cos(ph), mag * jnp.sin(ph)

    p_re, p_im = apow(jnp.arange(L + 1))
    ab_re, ab_im = p_re[:, 1], p_im[:, 1]
    den = a_re * a_re + a_im * a_im
    cf_re = ((ab_re - 1.0) * a_re + ab_im * a_im) / den
    cf_im = (ab_im * a_re - (ab_re - 1.0) * a_im) / den
    bb_re = cf_re[:, :, None] * b_re - cf_im[:, :, None] * b_im
    bb_im = cf_re[:, :, None] * b_im + cf_im[:, :, None] * b_re
    e_re = p_re[:, :L, :, None] * bb_re[:, None] - p_im[:, :L, :, None] * bb_im[:, None]
    e_im = p_re[:, :L, :, None] * bb_im[:, None] + p_im[:, :L, :, None] * bb_re[:, None]
    kk = (jnp.einsum('gnp,gtpm->gtnm', c_re, e_re, precision=HIGHEST)
          - jnp.einsum('gnp,gtpm->gtnm', c_im, e_im, precision=HIGHEST))
    kk = kk.at[:, 0].add(jnp.eye(N, dtype=F32)[None] * d_skip[:, :, None])
    r_re, r_im = apow(L - 1 - jnp.arange(L))
    bs_re = (r_re[:, :, None, :] * bb_re.transpose(0, 2, 1)[:, None] - r_im[:, :, None, :] * bb_im.transpose(0, 2, 1)[:, None])
    bs_im = (r_re[:, :, None, :] * bb_im.transpose(0, 2, 1)[:, None] + r_im[:, :, None, :] * bb_re.transpose(0, 2, 1)[:, None])
    ct_re = c_re.transpose(0, 2, 1)[:, :, None, :]
    ct_im = c_im.transpose(0, 2, 1)[:, :, None, :]
    q_re = p_re[:, 1:].transpose(0, 2, 1)[:, :, :, None]
    q_im = p_im[:, 1:].transpose(0, 2, 1)[:, :, :, None]
    cre = ct_re * q_re - ct_im * q_im
    cim = -(ct_re * q_im + ct_im * q_re)
    GT = S5_GT
    nt = G // GT

    def tile_groups(w):
        return w.reshape((nt, GT) + w.shape[1:])

    kc = tile_groups(kk).transpose(0, 1, 4, 2, 3).reshape(nt, GT * N, L * N)

    def in_table(w):
        return tile_groups(w).transpose(0, 2, 1, 3, 4).reshape(nt, L * GT * N, P)

    def out_table(w):
        return w.reshape(nt, GT * P, L * N)

    nstep = (T // L - 1).bit_length()
    sc_re, sc_im = apow(jnp.asarray([L * 2 ** i for i in range(nstep)]))

    def lanes(w):
        return tile_groups(w).transpose(0, 2, 1, 3).reshape(nt, w.shape[1], GT * P)

    r1 = jnp.arange(L * N)[:, None]
    c1 = jnp.arange(L * GT * N)[None, :]
    wide_tn = ((r1 // N == c1 // (GT * N)) & (r1 % N == c1 % N)).astype(BF16)
    wide_p = (jnp.arange(P)[:, None] == jnp.arange(GT * P)[None, :] % P).astype(BF16)
    return dict(
        kc=kc.astype(BF16),
        bs_re=in_table(bs_re).astype(BF16), bs_im=in_table(bs_im).astype(BF16),
        cre=out_table(cre).astype(BF16), cim=out_table(cim).astype(BF16),
        sc_re=lanes(sc_re), sc_im=lanes(sc_im), wide_tn=wide_tn, wide_p=wide_p,
    )


def _s5_kernel(u_ref, kc_ref, bsr_ref, bsi_ref, cre_ref, cim_ref, scr_ref, sci_ref, wtn_ref, wp_ref,
               y_ref, m_ref):
    L, N, P = S5_L, S5_N, S5_P
    W = S5_GT * N
    nch = u_ref.shape[0] // L

    @pl.when(pl.program_id(0) == 0)
    def _():
        m_ref[...] = jnp.zeros_like(m_ref)

    def widen(compact, wide, row_group, col_group):
        full = _dot(compact, wide)
        r = lax.broadcasted_iota(jnp.int32, full.shape, 0)
        c = lax.broadcasted_iota(jnp.int32, full.shape, 1)
        return jnp.where(row_group(r) == col_group(c), full, 0.0).astype(BF16)

    wtn, wp = wtn_ref[...], wp_ref[...]
    bd = widen(kc_ref[0], wtn, lambda r: r // N, lambda c: (c % W) // N)
    bsr = widen(bsr_ref[0], wp, lambda r: (r % W) // N, lambda c: c // P)
    bsi = widen(bsi_ref[0], wp, lambda r: (r % W) // N, lambda c: c // P)
    cre = widen(cre_ref[0], wtn, lambda r: r // P, lambda c: (c % W) // N)
    cim = widen(cim_ref[0], wtn, lambda r: r // P, lambda c: (c % W) // N)
    for s in range(L):
        m_ref[s * W:(s + 1) * W, s * W:] = bd[:, :(L - s) * W]
    ucat = jnp.concatenate([u_ref[pl.ds(s, nch, stride=L), :].astype(BF16) for s in range(L)], axis=1)
    hr = _dot(ucat, bsr)
    hi = _dot(ucat, bsi)
    row = lax.broadcasted_iota(jnp.int32, hr.shape, 0)

    def shift(x, d):
        if d % 8 == 0:
            return jnp.concatenate([jnp.zeros((d, x.shape[1]), F32), x[:nch - d]], axis=0)
        return jnp.where(row >= d, pltpu.roll(x, d, 0), 0.0)

    for i in range(scr_ref.shape[1]):
        mr, mi = scr_ref[0, i:i + 1, :], sci_ref[0, i:i + 1, :]
        pr, pi = shift(hr, 2 ** i), shift(hi, 2 ** i)
        hr, hi = hr + mr * pr - mi * pi, hi + mr * pi + mi * pr
    gr, gi = shift(hr, 1), shift(hi, 1)
    grb, gib = gr.astype(BF16), gi.astype(BF16)
    for tp in range(L // 2):
        cols = slice(2 * tp * W, (2 * tp + 2) * W)
        depth = (2 * tp + 2) * W
        y = _dot(ucat[:, :depth], m_ref[:depth, cols]) + _dot(grb, cre[:, cols]) + _dot(gib, cim[:, cols])
        for t in (2 * tp, 2 * tp + 1):
            y_ref[pl.ds(t, nch, stride=L), :] = y[:, (t - 2 * tp) * W:(t - 2 * tp + 1) * W]


def _s5(u, tb):
    W = S5_GT * S5_N
    nt = S5_G // S5_GT

    def tile_spec(a):
        return pl.BlockSpec((1,) + a.shape[1:], lambda g: (g,) + (0,) * (a.ndim - 1))

    ws = [tb[n] for n in ["kc", "bs_re", "bs_im", "cre", "cim", "sc_re", "sc_im"]]
    consts = [tb["wide_tn"], tb["wide_p"]]
    return pl.pallas_call(
        _s5_kernel,
        grid=(nt,),
        in_specs=([pl.BlockSpec((T, W), lambda g: (0, g))] + [tile_spec(w) for w in ws]
                  + [_const_spec(c.shape) for c in consts]),
        out_specs=pl.BlockSpec((T, W), lambda g: (0, g)),
        out_shape=jax.ShapeDtypeStruct((T, S5_G * S5_N), F32),
        scratch_shapes=[pltpu.VMEM((S5_L * W, S5_L * W), BF16)],
        compiler_params=pltpu.CompilerParams(
            dimension_semantics=("arbitrary",), vmem_limit_bytes=VMEM_LIMIT),
        name="s5",
    )(u, *ws, *consts)


def _post_kernel(y_ref, gla_ref, x_ref, gw_ref, gb_ref, sg_ref, wo_ref, n2_ref, rw_ref, rb_ref, before_ref,
                 h_ref, hn_ref, idx_ref, gate_ref, rank_ref, cnt_ref, run_ref):
    @pl.when(pl.program_id(0) == 0)
    def _():
        run_ref[...] = jnp.zeros_like(run_ref)

    y = y_ref[...]
    z = 0.5 * y * (1.0 + jnp.tanh(math.sqrt(2.0 / math.pi) * (y + 0.044715 * (y * y * y))))
    z = z * _sigmoid(_dot(z.astype(BF16), gw_ref[...]) + gb_ref[...])
    s5o = _rms(z, sg_ref[...]).astype(BF16)
    half = GLA_H * GLA_DV
    h = x_ref[...] + _dot(gla_ref[...], wo_ref[:half, :]) + _dot(s5o, wo_ref[half:, :])
    h_ref[...] = h
    hn = _rms(h, n2_ref[...])
    hi = hn.astype(BF16)
    hn_ref[...] = _pack_halves(hi[:, :D // 2], hi[:, D // 2:])
    lo = (hn - hi.astype(F32)).astype(BF16)
    rw = rw_ref[...]
    rw_hi = rw.astype(BF16)
    rw_lo = (rw - rw_hi.astype(F32)).astype(BF16)
    lg = _dot(hi, rw_hi) + (_dot(lo, rw_hi) + _dot(hi, rw_lo))
    lt = lg.T[:N_EXP] + rb_ref[...]
    eid = lax.broadcasted_iota(jnp.int32, lt.shape, 0).astype(F32)
    vals, idxs = [], []
    for _ in range(TOP_K):
        m = jnp.max(lt, axis=0, keepdims=True)
        sel = jnp.min(jnp.where(lt == m, eid, float(N_EXP)), axis=0, keepdims=True)
        vals.append(m)
        idxs.append(sel)
        lt = jnp.where(eid == sel, -jnp.inf, lt)
    ex = [jnp.exp(vv - vals[0]) for vv in vals]
    inv = 1.0 / (ex[0] + ex[1] + ex[2] + ex[3])
    idx_ref[...] = jnp.concatenate(idxs, axis=0).astype(jnp.int32)
    gate_ref[...] = jnp.concatenate([e * inv for e in ex], axis=0)
    run = run_ref[:, :1]
    ranks = []
    for sel in idxs:
        onehot = jnp.where(eid == sel, 1.0, 0.0)
        earlier = _dot(onehot.astype(BF16), before_ref[...])
        ranks.append(jnp.sum(onehot * (run + earlier), axis=0, keepdims=True))
        run = run + jnp.sum(onehot, axis=1, keepdims=True)
    rank_ref[...] = jnp.concatenate(ranks, axis=0).astype(jnp.int32)
    run_ref[...] = jnp.broadcast_to(run, run_ref.shape)
    cnt_ref[...] = jnp.broadcast_to(run, cnt_ref.shape)


def _post(y, gla, x, gw, gb, sg, wo, n2, rwt, rb):
    tm = 256
    width = S5_G * S5_N
    before = (jnp.arange(tm)[:, None] < jnp.arange(tm)[None, :]).astype(BF16)
    return pl.pallas_call(
        _post_kernel,
        grid=(T // tm,),
        in_specs=[
            pl.BlockSpec((tm, width), lambda i: (i, 0)),
            pl.BlockSpec((tm, GLA_H * GLA_DV), lambda i: (i, 0)),
            pl.BlockSpec((tm, D), lambda i: (i, 0)),
            _const_spec(gw.shape), _const_spec(gb.shape), _const_spec(sg.shape), _const_spec(wo.shape),
            _const_spec(n2.shape), _const_spec(rwt.shape), _const_spec(rb.shape), _const_spec(before.shape),
        ],
        out_specs=[
            pl.BlockSpec((tm, D), lambda i: (i, 0)),
            pl.BlockSpec((tm, D // 2), lambda i: (i, 0)),
            pl.BlockSpec((TOP_K, tm), lambda i: (0, i)),
            pl.BlockSpec((TOP_K, tm), lambda i: (0, i)),
            pl.BlockSpec((TOP_K, tm), lambda i: (0, i)),
            pl.BlockSpec((N_EXP, 128), lambda i: (0, 0)),
        ],
        out_shape=[
            jax.ShapeDtypeStruct((T, D), F32),
            jax.ShapeDtypeStruct((T, D // 2), jnp.int32),
            jax.ShapeDtypeStruct((TOP_K, T), jnp.int32),
            jax.ShapeDtypeStruct((TOP_K, T), F32),
            jax.ShapeDtypeStruct((TOP_K, T), jnp.int32),
            jax.ShapeDtypeStruct((N_EXP, 128), F32),
        ],
        scratch_shapes=[pltpu.VMEM((N_EXP, 128), F32)],
        compiler_params=pltpu.CompilerParams(
            dimension_semantics=("arbitrary",), vmem_limit_bytes=VMEM_LIMIT),
        name="post",
    )(y, gla, x, gw, gb, sg, wo, n2, rwt, rb, before)


def _route(top_idx, rank, counts):
    e_flat = top_idx.reshape(-1)
    rank = rank.reshape(-1)
    counts = counts[:, 0].astype(jnp.int32)
    padded = (counts + MOE_RB - 1) // MOE_RB * MOE_RB
    pad_end = jnp.cumsum(padded)
    pad_start = pad_end - padded
    dest = _dest(top_idx, rank.reshape(top_idx.shape), pad_start.astype(jnp.int32)).reshape(-1)
    nvis_e = (counts + MOE_RMAX - 1) // MOE_RMAX
    vis_end = jnp.cumsum(nvis_e)
    n_vis = vis_end[-1]
    v = jnp.arange(MOE_NV, dtype=jnp.int32)
    vc = jnp.minimum(v, n_vis - 1)
    ve = jnp.minimum(jnp.searchsorted(vis_end, vc, side='right'), N_EXP - 1).astype(jnp.int32)
    local = vc - (vis_end[ve] - nvis_e[ve])
    vcnt = jnp.where(v < n_vis, jnp.minimum(MOE_RMAX, counts[ve] - local * MOE_RMAX), 0)
    vrow = pad_start[ve] + local * MOE_RMAX
    return dict(dest=dest, fill_from=(pad_start + counts).astype(jnp.int32), fill_to=pad_end.astype(jnp.int32),
                tail_blk=(pad_end[-1:] // MOE_RB).astype(jnp.int32),
                vis_e=ve, vis_cnt=vcnt.astype(jnp.int32), vis_row=vrow.astype(jnp.int32),
                n_vis=n_vis.reshape(1).astype(jnp.int32))


def _dest_kernel(start_ref, idx_ref, rank_ref, dest_ref):
    e = idx_ref[...]
    d = rank_ref[...]
    for x in range(N_EXP):
        d = d + jnp.where(e == x, start_ref[x], 0)
    dest_ref[...] = d


def _dest(top_idx, rank, pad_start):
    full = pl.BlockSpec(top_idx.shape, lambda i, *_: (0, 0))
    return pl.pallas_call(
        _dest_kernel,
        grid_spec=pltpu.PrefetchScalarGridSpec(num_scalar_prefetch=1, grid=(1,), in_specs=[full, full], out_specs=full),
        out_shape=jax.ShapeDtypeStruct(top_idx.shape, jnp.int32),
        name="dest",
    )(pad_start, top_idx, rank)


def _row_copy(src_ref, srow, dst_ref, drow, sem):
    return pltpu.make_async_copy(src_ref.at[pl.ds(srow, 1), :], dst_ref.at[pl.ds(drow, 1), :], sem)


def _tail_fill(src_ref, dst_ref, tail_ref, sem):
    def cp(b):
        return pltpu.make_async_copy(src_ref.at[pl.ds(0, MOE_RB), :],
                                     dst_ref.at[pl.ds(pl.multiple_of(b * MOE_RB, MOE_RB), MOE_RB), :], sem)

    def start(b, c):
        cp(b).start()
        return c
    lax.fori_loop(tail_ref[0], MOE_ROWS // MOE_RB, start, 0)

    def finish(b, c):
        cp(b).wait()
        return c
    lax.fori_loop(tail_ref[0], MOE_ROWS // MOE_RB, finish, 0)


def _scatter_kernel(dest_ref, from_ref, to_ref, tail_ref, hn_ref, xs_ref, zero_ref, sem, zsem):
    i = pl.program_id(0)
    tm = hn_ref.shape[0]

    @pl.when(i == 0)
    def _():
        zero_ref[...] = jnp.zeros_like(zero_ref)
        _tail_fill(zero_ref, xs_ref, tail_ref, zsem)

        def per_expert(e, c):
            def fill(r, c2):
                _row_copy(zero_ref, 0, xs_ref, r, zsem).start()
                return c2
            lax.fori_loop(from_ref[e], to_ref[e], fill, 0)

            def drain(r, c2):
                _row_copy(zero_ref, 0, xs_ref, r, zsem).wait()
                return c2
            lax.fori_loop(from_ref[e], to_ref[e], drain, 0)
            return c
        lax.fori_loop(0, N_EXP, per_expert, 0)

    def issue(t, c):
        for k in range(TOP_K):
            _row_copy(hn_ref, t, xs_ref, dest_ref[k * T + i * tm + t], sem).start(priority=k % 2)
        return c
    lax.fori_loop(0, tm, issue, 0)

    for k in range(TOP_K):
        pltpu.make_async_copy(hn_ref, xs_ref.at[pl.ds(0, tm), :], sem).wait()


def _scatter(hn, rt):
    tm = 1024
    return pl.pallas_call(
        _scatter_kernel,
        grid_spec=pltpu.PrefetchScalarGridSpec(
            num_scalar_prefetch=4,
            grid=(T // tm,),
            in_specs=[pl.BlockSpec((tm, D // 2), lambda i, *_: (i, 0))],
            out_specs=pl.BlockSpec(memory_space=pl.ANY),
            scratch_shapes=[pltpu.VMEM((MOE_RB, D // 2), jnp.int32), pltpu.SemaphoreType.DMA,
                            pltpu.SemaphoreType.DMA],
        ),
        out_shape=jax.ShapeDtypeStruct((MOE_ROWS, D // 2), jnp.int32),
        compiler_params=pltpu.CompilerParams(
            dimension_semantics=("arbitrary",), vmem_limit_bytes=VMEM_LIMIT),
        name="scatter",
    )(rt["dest"], rt["fill_from"], rt["fill_to"], rt["tail_blk"], hn)


def _experts_kernel(ve_ref, vcnt_ref, vrow_ref, nvis_ref, tail_ref, xs_ref, w1_ref, w2_ref, b1_ref, b2_ref, ys_ref,
                    xin_ref, xb_ref, acc_ref, wg_ref, wl_ref, wd_ref, isem, osem, wsem):
    v = pl.program_id(0)
    RB = MOE_RB
    BIG, MID = 4 * RB, 2 * RB
    TF, NF = MOE_TF, MOE_NF
    nvis = nvis_ref[0]
    active = v < nvis

    def geometry(vv):
        nblk = (vcnt_ref[vv] + RB - 1) // RB
        return vrow_ref[vv], nblk // 4, (nblk // 2) % 2, nblk % 2

    row0, nbig, mid, small = geometry(v)
    mid_r0 = nbig * BIG
    small_r0 = mid_r0 + mid * MID

    def x_copies(vrow, r0, n):
        return [pltpu.make_async_copy(xs_ref.at[pl.ds(pl.multiple_of(vrow + r0 + h, RB), min(MID, n - h)), :],
                                      xin_ref.at[h // MID, pl.ds(0, min(MID, n - h)), :], isem.at[h // MID])
                for h in range(0, n, MID)]

    def y_copy(r0, n):
        return pltpu.make_async_copy(acc_ref.at[pl.ds(pl.multiple_of(r0, RB), n), :],
                                     ys_ref.at[pl.ds(pl.multiple_of(row0 + r0, RB), n), :], osem)

    def tile_copies(e, t, slot):
        cols = pl.ds(pl.multiple_of(t * TF, TF), TF)
        lin_cols = pl.ds(pl.multiple_of(D_FF + t * TF, TF), TF)
        return (pltpu.make_async_copy(w1_ref.at[e, :, cols], wg_ref.at[slot], wsem.at[slot]),
                pltpu.make_async_copy(w1_ref.at[e, :, lin_cols], wl_ref.at[slot], wsem.at[slot]),
                pltpu.make_async_copy(w2_ref.at[e, cols, :], wd_ref.at[slot], wsem.at[slot]))

    def start_unit(vv, u):
        vrow, nb, md, sm = geometry(vv)

        @pl.when(u < nb)
        def _():
            for cp in x_copies(vrow, u * BIG, BIG):
                cp.start()

        @pl.when((u == nb) & (md == 1))
        def _():
            for cp in x_copies(vrow, nb * BIG, MID):
                cp.start()

        @pl.when((u == nb + md) & (sm == 1))
        def _():
            for cp in x_copies(vrow, nb * BIG + md * MID, RB):
                cp.start()

    @pl.when(v == 0)
    def _():
        acc_ref[...] = jnp.zeros_like(acc_ref)
        _tail_fill(acc_ref, ys_ref, tail_ref, osem)
        start_unit(0, 0)
        for cp in tile_copies(ve_ref[0], 0, 0):
            cp.start()

    def step(j, carry):
        slot = j % 2
        first = j == 0
        last = j == NF - 1
        for cp in tile_copies(ve_ref[v], j, slot):
            cp.wait()

        @pl.when(j + 1 < NF)
        def _():
            for cp in tile_copies(ve_ref[v], j + 1, 1 - slot):
                cp.start()

        @pl.when(last & (v + 1 < nvis))
        def _():
            for cp in tile_copies(ve_ref[jnp.minimum(v + 1, MOE_NV - 1)], 0, 1 - slot):
                cp.start()

        b1g = b1_ref[0, :, pl.ds(pl.multiple_of(j * TF, TF), TF)]
        b1l = b1_ref[0, :, pl.ds(pl.multiple_of(D_FF + j * TF, TF), TF)]

        def block(u, r0, n):
            rows = pl.ds(pl.multiple_of(r0, RB), n)

            @pl.when(first)
            def _():
                for h, cp in zip(range(0, n, MID), x_copies(row0, r0, n)):
                    cp.wait()
                    m = min(MID, n - h)
                    lo_half, hi_half = _unpack_halves(xin_ref[h // MID, pl.ds(0, m), :])
                    piece = pl.ds(pl.multiple_of(r0 + h, RB), m)
                    xb_ref[piece, :D // 2] = lo_half
                    xb_ref[piece, D // 2:] = hi_half
                start_unit(v, u + 1)

            x = xb_ref[rows, :]
            glu = jnp.minimum(_dot(x, wg_ref[slot].astype(BF16)) + b1g, SWIGLU_LIMIT)
            lin = jnp.clip(_dot(x, wl_ref[slot].astype(BF16)) + b1l, -SWIGLU_LIMIT, SWIGLU_LIMIT)
            act = glu * _sigmoid(SWIGLU_ALPHA * glu) * (lin + 1.0)
            part = _dot(act.astype(BF16), wd_ref[slot].astype(BF16))

            acc_ref[rows, :] = part + jnp.where(first, b2_ref[0], acc_ref[rows, :])

            @pl.when(last)
            def _():
                y_copy(r0, n).start()

        def big(i, c):
            block(i, i * BIG, BIG)
            return c
        lax.fori_loop(0, nbig, big, 0)

        @pl.when(mid == 1)
        def _():
            block(nbig, mid_r0, MID)

        @pl.when(small == 1)
        def _():
            block(nbig + mid, small_r0, RB)

        @pl.when(last)
        def _():
            def finish(i, c):
                y_copy(i * BIG, BIG).wait()
                return c
            lax.fori_loop(0, nbig, finish, 0)

            @pl.when(mid == 1)
            def _():
                y_copy(mid_r0, MID).wait()

            @pl.when(small == 1)
            def _():
                y_copy(small_r0, RB).wait()

            @pl.when(v + 1 < nvis)
            def _():
                start_unit(v + 1, 0)
        return carry

    @pl.when(active)
    def _():
        lax.fori_loop(0, NF, step, 0)


def _experts(xs, rt, w1, b1, w2, b2):
    b1 = b1.reshape(N_EXP, 1, 2 * D_FF)
    b2 = b2.reshape(N_EXP, 1, D)
    any_spec = pl.BlockSpec(memory_space=pl.ANY)
    return pl.pallas_call(
        _experts_kernel,
        grid_spec=pltpu.PrefetchScalarGridSpec(
            num_scalar_prefetch=5,
            grid=(MOE_NV,),
            in_specs=[
                any_spec, any_spec, any_spec,
                pl.BlockSpec((1, 1, 2 * D_FF), lambda v, ve, *s: (ve[v], 0, 0)),
                pl.BlockSpec((1, 1, D), lambda v, ve, *s: (ve[v], 0, 0)),
            ],
            out_specs=any_spec,
            scratch_shapes=[
                pltpu.VMEM((2, 2 * MOE_RB, D // 2), jnp.int32),
                pltpu.VMEM((MOE_RMAX, D), BF16),
                pltpu.VMEM((MOE_RMAX, D), F32),
                pltpu.VMEM((2, D, MOE_TF), F32),
                pltpu.VMEM((2, D, MOE_TF), F32),
                pltpu.VMEM((2, MOE_TF, D), F32),
                pltpu.SemaphoreType.DMA((2,)),
                pltpu.SemaphoreType.DMA,
                pltpu.SemaphoreType.DMA((2,)),
            ],
        ),
        out_shape=jax.ShapeDtypeStruct((MOE_ROWS, D), F32),
        compiler_params=pltpu.CompilerParams(
            dimension_semantics=("arbitrary",), vmem_limit_bytes=VMEM_LIMIT),
        name="experts",
    )(rt["vis_e"], rt["vis_cnt"], rt["vis_row"], rt["n_vis"], rt["tail_blk"], xs, w1, w2, b1, b2)


def _combine_kernel(dest_ref, ys_ref, h_ref, gate_ref, gf_ref, o_ref, buf_ref, sem):
    i = pl.program_id(0)
    tm = h_ref.shape[0]

    def gather(tile, slot):
        def issue(t, c):
            for k in range(TOP_K):
                _row_copy(ys_ref, dest_ref[k * T + tile * tm + t], buf_ref.at[slot, k], t,
                          sem.at[slot]).start(priority=k % 2)
            return c
        lax.fori_loop(0, tm, issue, 0)

    @pl.when(i == 0)
    def _():
        gather(0, 0)

    @pl.when(i + 1 < pl.num_programs(0))
    def _():
        gather(i + 1, (i + 1) % 2)

    slot = i % 2
    for k in range(TOP_K):
        pltpu.make_async_copy(ys_ref.at[pl.ds(0, tm), :], buf_ref.at[slot, k], sem.at[slot]).wait()

    h = h_ref[...]
    for k in range(TOP_K):
        h = h + gate_ref[:, k:k + 1] * buf_ref[slot, k]
    o_ref[...] = _rms(h, gf_ref[...])


def _combine(ys, h, gates_t, gf, rt):
    tm = 256
    return pl.pallas_call(
        _combine_kernel,
        grid_spec=pltpu.PrefetchScalarGridSpec(
            num_scalar_prefetch=1,
            grid=(T // tm,),
            in_specs=[
                pl.BlockSpec(memory_space=pl.ANY),
                pl.BlockSpec((tm, D), lambda i, *_: (i, 0)),
                pl.BlockSpec((tm, TOP_K), lambda i, *_: (i, 0)),
                pl.BlockSpec((1, D), lambda i, *_: (0, 0)),
            ],
            out_specs=pl.BlockSpec((tm, D), lambda i, *_: (i, 0)),
            scratch_shapes=[pltpu.VMEM((2, TOP_K, tm, D), F32), pltpu.SemaphoreType.DMA((2,))],
        ),
        out_shape=jax.ShapeDtypeStruct((T, D), F32),
        compiler_params=pltpu.CompilerParams(
            dimension_semantics=("arbitrary",), vmem_limit_bytes=VMEM_LIMIT),
        name="combine",
    )(rt["dest"], ys, h, gates_t, gf)


def kernel(x, norm1_g, w_in, gla_gate_w2, gla_gate_b, gla_norm_g, s5_a_re, s5_a_im, s5_b_re, s5_b_im, s5_c_re, s5_c_im, s5_d, s5_log_step, s5_glu_w, s5_glu_b, s5_norm_g, w_out, norm2_g, router_w, router_b, expert_w1, expert_b1, expert_w2, expert_b2, final_norm_g):
    assert x.shape == (1, T, D) and w_in.shape[0] == 1
    xt = x.reshape(T, D)
    w2g = jnp.pad(gla_gate_w2[0], ((0, 128 - GATE_RANK), (0, 0)))
    qkvr, lg, u = _proj(xt, norm1_g, jnp.swapaxes(w_in, 1, 2), w2g, gla_gate_b)
    gla = _gla(qkvr, lg, gla_norm_g)
    tables = _s5_tables(s5_a_re[0], s5_a_im[0], s5_b_re[0], s5_b_im[0], s5_c_re[0], s5_c_im[0],
                        s5_d[0], s5_log_step[0])
    y = _s5(u, tables)
    h, hn, top_idx, gates, rank, counts = _post(
        y, gla, xt, s5_glu_w[0].astype(BF16), s5_glu_b, s5_norm_g, w_out[0].astype(BF16), norm2_g,
        jnp.pad(router_w[0], ((0, 0), (0, 128 - N_EXP))), router_b.reshape(N_EXP, 1))
    rt = _route(top_idx, rank, counts)
    xs = _scatter(hn, rt)
    ys = _experts(xs, rt, expert_w1[0], expert_b1[0], expert_w2[0], expert_b2[0])
    out = _combine(ys, h, gates.T, final_norm_g.reshape(1, D), rt)
    return out.reshape(1, T, D)
```

```python
import math

import jax
import jax.numpy as jnp
from jax import lax
from jax.experimental import pallas as pl
from jax.experimental.pallas import tpu as pltpu

F32 = jnp.float32
BF16 = jnp.bfloat16

T = 8192
D = 2048
GLA_H = 4
GLA_DK = 128
GLA_DV = 256
GLA_CHUNK = 64
GLA_SUB = 16
GATE_RANK = 16
GATE_TAU = 16.0
S5_G = 64
S5_N = 16
S5_P = 64
S5_L = 16
S5_GT = 8
N_EXP = 32
TOP_K = 4
D_FF = 2048
SWIGLU_ALPHA = 1.702
SWIGLU_LIMIT = 7.0
RMS_EPS = 1e-6

MOE_RB = 128
MOE_RMAX = 2176
MOE_TF = 512
MOE_NF = D_FF // MOE_TF
MOE_NV = -(-T * TOP_K // MOE_RMAX) + N_EXP
MOE_ROWS = T * TOP_K + N_EXP * MOE_RB
VMEM_LIMIT = 56 * 1024 * 1024

NN = (((1,), (0,)), ((), ()))
NT = (((1,), (1,)), ((), ()))
TN = (((0,), (0,)), ((), ()))


def _dot(a, b, dims=NN, precision=None):
    return lax.dot_general(a, b, dims, preferred_element_type=F32, precision=precision)


def _dot3(a, b):
    a_hi, b_hi = a.astype(BF16), b.astype(BF16)
    a_lo = (a - a_hi.astype(F32)).astype(BF16)
    b_lo = (b - b_hi.astype(F32)).astype(BF16)
    return _dot(a_hi, b_hi) + (_dot(a_lo, b_hi) + _dot(a_hi, b_lo))


def _rms(x, g):
    return x * lax.rsqrt(jnp.mean(x * x, axis=-1, keepdims=True) + RMS_EPS) * g


def _sigmoid(x):
    return 1.0 / (1.0 + jnp.exp(-x))


def _pack_halves(lo_half, hi_half):
    a = lax.bitcast_convert_type(lo_half.astype(F32), jnp.int32)
    b = lax.bitcast_convert_type(hi_half.astype(F32), jnp.int32)
    return a | lax.shift_right_logical(b, 16)


def _unpack_halves(p):
    a = lax.bitcast_convert_type(p & jnp.int32(-65536), F32).astype(BF16)
    b = lax.bitcast_convert_type(lax.shift_left(p, 16), F32).astype(BF16)
    return a, b


def _const_spec(shape):
    nd = len(shape)
    return pl.BlockSpec(shape, lambda *_: (0,) * nd, pipeline_mode=pl.Buffered(1))


PROJ_QKVR = 2 * GLA_H * GLA_DK + 2 * GLA_H * GLA_DV
PROJ_U = S5_G * S5_N
PROJ_CHUNK = 512


def _proj_kernel(x_ref, g1_ref, wt_in_ref, w2_ref, gb_ref, qkvr_ref, lg_ref, u_ref,
                 wq_ref, wg_ref, wu_ref, stage_ref, gstage_ref, sem, gsem):
    @pl.when(pl.program_id(0) == 0)
    def _():
        pieces = ([(wq_ref, c) for c in range(PROJ_QKVR // PROJ_CHUNK)]
                  + [(wu_ref, c) for c in range(PROJ_U // PROJ_CHUNK)])

        def first_row(k):
            dst, c = pieces[k]
            return c * PROJ_CHUNK + (0 if dst is wq_ref else PROJ_QKVR + GATE_RANK)

        def piece(k):
            return pltpu.make_async_copy(wt_in_ref.at[0, pl.ds(first_row(k), PROJ_CHUNK), :], stage_ref.at[k % 2],
                                         sem.at[k % 2])

        gate = pltpu.make_async_copy(wt_in_ref.at[0, pl.ds(PROJ_QKVR, GATE_RANK), :], gstage_ref, gsem)
        gate.start()
        piece(0).start()
        for k, (dst, c) in enumerate(pieces):
            piece(k).wait()
            if k + 1 < len(pieces):
                piece(k + 1).start()
            dst[:, c * PROJ_CHUNK:(c + 1) * PROJ_CHUNK] = stage_ref[k % 2].T.astype(BF16)
        gate.wait()
        padded = jnp.concatenate([gstage_ref[...], jnp.zeros((128 - GATE_RANK, D), F32)], axis=0)
        wg_ref[...] = padded.T.astype(BF16)

    hb = _rms(x_ref[...], g1_ref[...]).astype(BF16)
    p = _dot(hb, wq_ref[...])
    nq = GLA_H * GLA_DK
    qkvr_ref[:, :nq] = (p[:, :nq] * (GLA_DK ** -0.5)).astype(BF16)
    qkvr_ref[:, nq:] = p[:, nq:].astype(BF16)
    glr = _dot(hb, wg_ref[...])
    z = _dot3(glr, w2_ref[...]) + gb_ref[...]
    lg_ref[...] = (jnp.minimum(z, 0.0) - jnp.log(1.0 + jnp.exp(-jnp.abs(z)))) * (1.0 / GATE_TAU)
    u_ref[...] = _dot(hb, wu_ref[...])


def _proj(x, g1, wt_in, w2, gb):
    tm = 256
    return pl.pallas_call(
        _proj_kernel,
        grid=(T // tm,),
        in_specs=[
            pl.BlockSpec((tm, D), lambda i: (i, 0)),
            _const_spec((1, D)),
            pl.BlockSpec(memory_space=pl.ANY),
            _const_spec(w2.shape), _const_spec(gb.shape),
        ],
        out_specs=[
            pl.BlockSpec((tm, PROJ_QKVR), lambda i: (i, 0)),
            pl.BlockSpec((tm, GLA_H * GLA_DK), lambda i: (i, 0)),
            pl.BlockSpec((tm, PROJ_U), lambda i: (i, 0)),
        ],
        out_shape=[
            jax.ShapeDtypeStruct((T, PROJ_QKVR), BF16),
            jax.ShapeDtypeStruct((T, GLA_H * GLA_DK), F32),
            jax.ShapeDtypeStruct((T, PROJ_U), F32),
        ],
        scratch_shapes=[
            pltpu.VMEM((D, PROJ_QKVR), BF16),
            pltpu.VMEM((D, 128), BF16),
            pltpu.VMEM((D, PROJ_U), BF16),
            pltpu.VMEM((2, PROJ_CHUNK, D), F32),
            pltpu.VMEM((GATE_RANK, D), F32),
            pltpu.SemaphoreType.DMA((2,)),
            pltpu.SemaphoreType.DMA,
        ],
        compiler_params=pltpu.CompilerParams(
            dimension_semantics=("arbitrary",), vmem_limit_bytes=VMEM_LIMIT),
        name="proj",
    )(x, g1, wt_in, w2, gb)


def _gla_kernel(q_ref, k_ref, v_ref, r_ref, lg_ref, gn_ref, tril_ref, esel_ref, o_ref, state_ref):
    C, S = GLA_CHUNK, GLA_SUB
    nsub = C // S

    @pl.when(pl.program_id(0) == 0)
    def _():
        state_ref[...] = jnp.zeros_like(state_ref)

    row = lax.broadcasted_iota(jnp.int32, (C, C), 0)
    col = lax.broadcasted_iota(jnp.int32, (C, C), 1)
    below = (col // S) < (row // S)
    diag = ((col // S) == (row // S)) & (col <= row)
    tril = tril_ref[...]
    esel = esel_ref[...]
    gn = gn_ref[...]

    def head(h, rows, b):
        kcols = slice(h * GLA_DK, (h + 1) * GLA_DK)
        vcols = slice(h * GLA_DV, (h + 1) * GLA_DV)
        q = q_ref[rows, kcols].astype(F32)
        k = k_ref[rows, kcols].astype(F32)
        v = v_ref[rows, vcols]
        bl = b[C - 1:C, :]
        st = state_ref[h]
        o = _dot((q * jnp.exp2(b)).astype(BF16), st.astype(BF16), NT)
        s_rows = [jnp.zeros((S, C), F32)]
        for j in range(1, nsub):
            ref = b[j * S - 1:j * S, :]
            qj = (q[j * S:(j + 1) * S] * jnp.exp2(b[j * S:(j + 1) * S] - ref)).astype(BF16)
            kj = (k * jnp.exp2(jnp.minimum(ref - b, 0.0))).astype(BF16)
            s_rows.append(_dot(qj, kj, NT))
        s_off = jnp.concatenate(s_rows, axis=0)
        q3 = q.reshape(nsub, S, GLA_DK)
        k3 = k.reshape(nsub, S, GLA_DK)
        b3 = b.reshape(nsub, S, GLA_DK)
        xs = []
        for s in range(S):
            lo = (s // 8) * 8
            dec = jnp.exp2(jnp.minimum(b3[:, lo:] - b3[:, s:s + 1, :], 0.0))
            x = q3[:, lo:] * k3[:, s:s + 1, :] * dec
            if lo:
                x = jnp.concatenate([jnp.zeros((nsub, lo, GLA_DK), F32), x], axis=1)
            xs.append(x.reshape(C, GLA_DK).astype(BF16))
        dsc = _dot(jnp.concatenate(xs, axis=1), esel)
        a = jnp.where(below, s_off, jnp.where(diag, dsc, 0.0))
        o = o + _dot(a.astype(BF16), v)
        kout = (k * jnp.exp2(bl - b)).astype(BF16)
        state_ref[h] = st * jnp.exp2(bl) + _dot(v, kout, TN)
        y = _rms(o, gn)
        r = r_ref[rows, vcols].astype(F32)
        o_ref[rows, vcols] = (y * (r * _sigmoid(r))).astype(BF16)

    def chunk(c, carry):
        rows = pl.ds(pl.multiple_of(c * C, C), C)
        g = lg_ref[rows, :]
        g_hi = g.astype(BF16)
        g_mid = (g - g_hi.astype(F32)).astype(BF16)
        g_lo = (g - g_hi.astype(F32) - g_mid.astype(F32)).astype(BF16)
        b = (_dot(tril, g_hi) + (_dot(tril, g_mid) + _dot(tril, g_lo))) * math.log2(math.e)
        for h in range(GLA_H):
            head(h, rows, b[:, h * GLA_DK:(h + 1) * GLA_DK])
        return carry

    lax.fori_loop(0, q_ref.shape[0] // C, chunk, 0, unroll=8)


def _gla(qkvr, lg, gn):
    tb = 512
    C, S = GLA_CHUNK, GLA_SUB
    nk, nv = GLA_H * GLA_DK, GLA_H * GLA_DV
    tril = jnp.tril(jnp.ones((C, C), BF16))
    esel = (jnp.arange(S * GLA_DK)[:, None] // GLA_DK == jnp.arange(C)[None, :] % S).astype(BF16)
    return pl.pallas_call(
        _gla_kernel,
        grid=(T // tb,),
        in_specs=[
            pl.BlockSpec((tb, nk), lambda i: (i, 0)),
            pl.BlockSpec((tb, nk), lambda i: (i, 1)),
            pl.BlockSpec((tb, nv), lambda i: (i, 1)),
            pl.BlockSpec((tb, nv), lambda i: (i, 2)),
            pl.BlockSpec((tb, nk), lambda i: (i, 0)),
            _const_spec((1, GLA_DV)), _const_spec((C, C)), _const_spec((S * GLA_DK, C)),
        ],
        out_specs=pl.BlockSpec((tb, nv), lambda i: (i, 0)),
        out_shape=jax.ShapeDtypeStruct((T, nv), BF16),
        scratch_shapes=[pltpu.VMEM((GLA_H, GLA_DV, GLA_DK), F32)],
        compiler_params=pltpu.CompilerParams(
            dimension_semantics=("arbitrary",), vmem_limit_bytes=VMEM_LIMIT),
        name="gla",
    )(qkvr, qkvr, qkvr, qkvr, lg, gn, tril, esel)


def _s5_tables(a_re, a_im, b_re, b_im, c_re, c_im, d_skip, log_step):
    L, G, P, N = S5_L, S5_G, S5_P, S5_N
    delta = jnp.exp(log_step)[:, None]
    ar, ai = a_re * delta, a_im * delta

    def apow(tau):
        tau = jnp.asarray(tau, F32)[None, :, None]
        mag = jnp.exp(ar[:, None, :] * tau)
        ph = ai[:, None, :] * tau
        return mag * jnp.cos(ph), mag * jnp.sin(ph)

    p_re, p_im = apow(jnp.arange(L + 1))
    ab_re, ab_im = p_re[:, 1], p_im[:, 1]
    den = a_re * a_re + a_im * a_im
    cf_re = ((ab_re - 1.0) * a_re + ab_im * a_im) / den
    cf_im = (ab_im * a_re - (ab_re - 1.0) * a_im) / den
    bb_re = cf_re[:, :, None] * b_re - cf_im[:, :, None] * b_im
    bb_im = cf_re[:, :, None] * b_im + cf_im[:, :, None] * b_re
    e_re = p_re[:, :L, :, None] * bb_re[:, None] - p_im[:, :L, :, None] * bb_im[:, None]
    e_im = p_re[:, :L, :, None] * bb_im[:, None] + p_im[:, :L, :, None] * bb_re[:, None]
    kk = jnp.einsum('gnp,gtpm->gtnm', jnp.concatenate([c_re, -c_im], axis=2),
                    jnp.concatenate([e_re, e_im], axis=2), precision=lax.Precision.HIGH)
    kk = kk.at[:, 0].add(jnp.eye(N, dtype=F32)[None] * d_skip[:, :, None])
    r_re, r_im = apow(L - 1 - jnp.arange(L))
    bs_re = (r_re[:, :, None, :] * bb_re.transpose(0, 2, 1)[:, None] - r_im[:, :, None, :] * bb_im.transpose(0, 2, 1)[:, None])
    bs_im = (r_re[:, :, None, :] * bb_im.transpose(0, 2, 1)[:, None] + r_im[:, :, None, :] * bb_re.transpose(0, 2, 1)[:, None])
    ct_re = c_re.transpose(0, 2, 1)[:, :, None, :]
    ct_im = c_im.transpose(0, 2, 1)[:, :, None, :]
    q_re = p_re[:, 1:].transpose(0, 2, 1)[:, :, :, None]
    q_im = p_im[:, 1:].transpose(0, 2, 1)[:, :, :, None]
    cre = ct_re * q_re - ct_im * q_im
    cim = -(ct_re * q_im + ct_im * q_re)
    GT = S5_GT
    nt = G // GT

    def tile_groups(w):
        return w.reshape((nt, GT) + w.shape[1:])

    kc = tile_groups(kk).transpose(0, 1, 4, 2, 3).reshape(nt, GT * N, L * N)

    def in_table(w):
        return tile_groups(w).transpose(0, 2, 1, 3, 4).reshape(nt, L * GT * N, P)

    def out_table(w):
        return w.reshape(nt, GT * P, L * N)

    nstep = (T // L - 1).bit_length()
    sc_re, sc_im = apow(jnp.asarray([L * 2 ** i for i in range(nstep)]))

    def lanes(w):
        return tile_groups(w).transpose(0, 2, 1, 3).reshape(nt, w.shape[1], GT * P)

    r1 = jnp.arange(L * N)[:, None]
    c1 = jnp.arange(L * GT * N)[None, :]
    wide_tn = ((r1 // N == c1 // (GT * N)) & (r1 % N == c1 % N)).astype(BF16)
    wide_p = (jnp.arange(P)[:, None] == jnp.arange(GT * P)[None, :] % P).astype(BF16)
    return dict(
        kc=kc.astype(BF16),
        bs_re=in_table(bs_re).astype(BF16), bs_im=in_table(bs_im).astype(BF16),
        cre=out_table(cre).astype(BF16), cim=out_table(cim).astype(BF16),
        sc_re=lanes(sc_re), sc_im=lanes(sc_im), wide_tn=wide_tn, wide_p=wide_p,
    )


def _s5_kernel(u_ref, kc_ref, bsr_ref, bsi_ref, cre_ref, cim_ref, scr_ref, sci_ref, wtn_ref, wp_ref,
               y_ref, m_ref):
    L, N, P = S5_L, S5_N, S5_P
    W = S5_GT * N
    nch = u_ref.shape[0] // L

    @pl.when(pl.program_id(0) == 0)
    def _():
        m_ref[...] = jnp.zeros_like(m_ref)

    def widen(compact, wide, row_group, col_group):
        full = _dot(compact, wide)
        r = lax.broadcasted_iota(jnp.int32, full.shape, 0)
        c = lax.broadcasted_iota(jnp.int32, full.shape, 1)
        return jnp.where(row_group(r) == col_group(c), full, 0.0).astype(BF16)

    wtn, wp = wtn_ref[...], wp_ref[...]
    bd = widen(kc_ref[0], wtn, lambda r: r // N, lambda c: (c % W) // N)
    bsr = widen(bsr_ref[0], wp, lambda r: (r % W) // N, lambda c: c // P)
    bsi = widen(bsi_ref[0], wp, lambda r: (r % W) // N, lambda c: c // P)
    cre = widen(cre_ref[0], wtn, lambda r: r // P, lambda c: (c % W) // N)
    cim = widen(cim_ref[0], wtn, lambda r: r // P, lambda c: (c % W) // N)
    for s in range(L):
        m_ref[s * W:(s + 1) * W, s * W:] = bd[:, :(L - s) * W]
    ucat = jnp.concatenate([u_ref[pl.ds(s, nch, stride=L), :].astype(BF16) for s in range(L)], axis=1)
    hr = _dot(ucat, bsr)
    hi = _dot(ucat, bsi)
    row = lax.broadcasted_iota(jnp.int32, hr.shape, 0)

    def shift(x, d):
        if d % 8 == 0:
            return jnp.concatenate([jnp.zeros((d, x.shape[1]), F32), x[:nch - d]], axis=0)
        return jnp.where(row >= d, pltpu.roll(x, d, 0), 0.0)

    for i in range(scr_ref.shape[1]):
        mr, mi = scr_ref[0, i:i + 1, :], sci_ref[0, i:i + 1, :]
        pr, pi = shift(hr, 2 ** i), shift(hi, 2 ** i)
        hr, hi = hr + mr * pr - mi * pi, hi + mr * pi + mi * pr
    gr, gi = shift(hr, 1), shift(hi, 1)
    grb, gib = gr.astype(BF16), gi.astype(BF16)
    for tp in range(L // 2):
        cols = slice(2 * tp * W, (2 * tp + 2) * W)
        depth = (2 * tp + 2) * W
        y = _dot(ucat[:, :depth], m_ref[:depth, cols]) + _dot(grb, cre[:, cols]) + _dot(gib, cim[:, cols])
        for t in (2 * tp, 2 * tp + 1):
            y_ref[pl.ds(t, nch, stride=L), :] = y[:, (t - 2 * tp) * W:(t - 2 * tp + 1) * W]


def _s5(u, tb):
    W = S5_GT * S5_N
    nt = S5_G // S5_GT

    def tile_spec(a):
        return pl.BlockSpec((1,) + a.shape[1:], lambda g: (g,) + (0,) * (a.ndim - 1))

    ws = [tb[n] for n in ["kc", "bs_re", "bs_im", "cre", "cim", "sc_re", "sc_im"]]
    consts = [tb["wide_tn"], tb["wide_p"]]
    return pl.pallas_call(
        _s5_kernel,
        grid=(nt,),
        in_specs=([pl.BlockSpec((T, W), lambda g: (0, g))] + [tile_spec(w) for w in ws]
                  + [_const_spec(c.shape) for c in consts]),
        out_specs=pl.BlockSpec((T, W), lambda g: (0, g)),
        out_shape=jax.ShapeDtypeStruct((T, S5_G * S5_N), F32),
        scratch_shapes=[pltpu.VMEM((S5_L * W, S5_L * W), BF16)],
        compiler_params=pltpu.CompilerParams(
            dimension_semantics=("arbitrary",), vmem_limit_bytes=VMEM_LIMIT),
        name="s5",
    )(u, *ws, *consts)


def _post_kernel(y_ref, gla_ref, x_ref, gw_ref, gb_ref, sg_ref, wo_ref, n2_ref, rw_ref, rb_ref, before_ref,
                 h_ref, hn_ref, idx_ref, gate_ref, rank_ref, cnt_ref, run_ref):
    @pl.when(pl.program_id(0) == 0)
    def _():
        run_ref[...] = jnp.zeros_like(run_ref)

    y = y_ref[...]
    z = 0.5 * y * (1.0 + jnp.tanh(math.sqrt(2.0 / math.pi) * (y + 0.044715 * (y * y * y))))
    z = z * _sigmoid(_dot(z.astype(BF16), gw_ref[...]) + gb_ref[...])
    s5o = _rms(z, sg_ref[...]).astype(BF16)
    half = GLA_H * GLA_DV
    h = x_ref[...] + _dot(gla_ref[...], wo_ref[:half, :]) + _dot(s5o, wo_ref[half:, :])
    h_ref[...] = h
    hn = _rms(h, n2_ref[...])
    hi = hn.astype(BF16)
    hn_ref[...] = _pack_halves(hi[:, :D // 2], hi[:, D // 2:])
    lo = (hn - hi.astype(F32)).astype(BF16)
    rw = rw_ref[...]
    rw_hi = rw.astype(BF16)
    rw_lo = (rw - rw_hi.astype(F32)).astype(BF16)
    lg = _dot(hi, rw_hi) + (_dot(lo, rw_hi) + _dot(hi, rw_lo))
    lt = lg.T[:N_EXP] + rb_ref[...]
    eid = lax.broadcasted_iota(jnp.int32, lt.shape, 0).astype(F32)
    vals, idxs = [], []
    for _ in range(TOP_K):
        m = jnp.max(lt, axis=0, keepdims=True)
        sel = jnp.min(jnp.where(lt == m, eid, float(N_EXP)), axis=0, keepdims=True)
        vals.append(m)
        idxs.append(sel)
        lt = jnp.where(eid == sel, -jnp.inf, lt)
    ex = [jnp.exp(vv - vals[0]) for vv in vals]
    inv = 1.0 / (ex[0] + ex[1] + ex[2] + ex[3])
    idx_ref[...] = jnp.concatenate(idxs, axis=0).astype(jnp.int32)
    gate_ref[...] = jnp.concatenate([e * inv for e in ex], axis=0)
    run = run_ref[:, :1]
    ranks = []
    for sel in idxs:
        onehot = jnp.where(eid == sel, 1.0, 0.0)
        earlier = _dot(onehot.astype(BF16), before_ref[...])
        ranks.append(jnp.sum(onehot * (run + earlier), axis=0, keepdims=True))
        run = run + jnp.sum(onehot, axis=1, keepdims=True)
    rank_ref[...] = jnp.concatenate(ranks, axis=0).astype(jnp.int32)
    run_ref[...] = jnp.broadcast_to(run, run_ref.shape)
    cnt_ref[...] = jnp.broadcast_to(run, cnt_ref.shape)


def _post(y, gla, x, gw, gb, sg, wo, n2, rwt, rb):
    tm = 256
    width = S5_G * S5_N
    before = (jnp.arange(tm)[:, None] < jnp.arange(tm)[None, :]).astype(BF16)
    return pl.pallas_call(
        _post_kernel,
        grid=(T // tm,),
        in_specs=[
            pl.BlockSpec((tm, width), lambda i: (i, 0)),
            pl.BlockSpec((tm, GLA_H * GLA_DV), lambda i: (i, 0)),
            pl.BlockSpec((tm, D), lambda i: (i, 0)),
            _const_spec(gw.shape), _const_spec(gb.shape), _const_spec(sg.shape), _const_spec(wo.shape),
            _const_spec(n2.shape), _const_spec(rwt.shape), _const_spec(rb.shape), _const_spec(before.shape),
        ],
        out_specs=[
            pl.BlockSpec((tm, D), lambda i: (i, 0)),
            pl.BlockSpec((tm, D // 2), lambda i: (i, 0)),
            pl.BlockSpec((TOP_K, tm), lambda i: (0, i)),
            pl.BlockSpec((TOP_K, tm), lambda i: (0, i)),
            pl.BlockSpec((TOP_K, tm), lambda i: (0, i)),
            pl.BlockSpec((N_EXP, 128), lambda i: (0, 0)),
        ],
        out_shape=[
            jax.ShapeDtypeStruct((T, D), F32),
            jax.ShapeDtypeStruct((T, D // 2), jnp.int32),
            jax.ShapeDtypeStruct((TOP_K, T), jnp.int32),
            jax.ShapeDtypeStruct((TOP_K, T), F32),
            jax.ShapeDtypeStruct((TOP_K, T), jnp.int32),
            jax.ShapeDtypeStruct((N_EXP, 128), F32),
        ],
        scratch_shapes=[pltpu.VMEM((N_EXP, 128), F32)],
        compiler_params=pltpu.CompilerParams(
            dimension_semantics=("arbitrary",), vmem_limit_bytes=VMEM_LIMIT),
        name="post",
    )(y, gla, x, gw, gb, sg, wo, n2, rwt, rb, before)


def _route(top_idx, rank, counts):
    e_flat = top_idx.reshape(-1)
    rank = rank.reshape(-1)
    counts = counts[:, 0].astype(jnp.int32)
    padded = (counts + MOE_RB - 1) // MOE_RB * MOE_RB
    pad_end = jnp.cumsum(padded)
    pad_start = pad_end - padded
    dest = _dest(top_idx, rank.reshape(top_idx.shape), pad_start.astype(jnp.int32)).reshape(-1)
    nvis_e = (counts + MOE_RMAX - 1) // MOE_RMAX
    vis_end = jnp.cumsum(nvis_e)
    n_vis = vis_end[-1]
    v = jnp.arange(MOE_NV, dtype=jnp.int32)
    vc = jnp.minimum(v, n_vis - 1)
    ve = jnp.minimum(jnp.searchsorted(vis_end, vc, side='right'), N_EXP - 1).astype(jnp.int32)
    local = vc - (vis_end[ve] - nvis_e[ve])
    vcnt = jnp.where(v < n_vis, jnp.minimum(MOE_RMAX, counts[ve] - local * MOE_RMAX), 0)
    vrow = pad_start[ve] + local * MOE_RMAX
    return dict(dest=dest, fill_from=(pad_start + counts).astype(jnp.int32), fill_to=pad_end.astype(jnp.int32),
                tail_blk=(pad_end[-1:] // MOE_RB).astype(jnp.int32),
                vis_e=ve, vis_cnt=vcnt.astype(jnp.int32), vis_row=vrow.astype(jnp.int32),
                n_vis=n_vis.reshape(1).astype(jnp.int32))


def _dest_kernel(start_ref, idx_ref, rank_ref, dest_ref):
    e = idx_ref[...]
    d = rank_ref[...]
    for x in range(N_EXP):
        d = d + jnp.where(e == x, start_ref[x], 0)
    dest_ref[...] = d


def _dest(top_idx, rank, pad_start):
    full = pl.BlockSpec(top_idx.shape, lambda i, *_: (0, 0))
    return pl.pallas_call(
        _dest_kernel,
        grid_spec=pltpu.PrefetchScalarGridSpec(num_scalar_prefetch=1, grid=(1,), in_specs=[full, full], out_specs=full),
        out_shape=jax.ShapeDtypeStruct(top_idx.shape, jnp.int32),
        name="dest",
    )(pad_start, top_idx, rank)


def _row_copy(src_ref, srow, dst_ref, drow, sem):
    return pltpu.make_async_copy(src_ref.at[pl.ds(srow, 1), :], dst_ref.at[pl.ds(drow, 1), :], sem)


def _tail_fill(src_ref, dst_ref, tail_ref, sem):
    def cp(b):
        return pltpu.make_async_copy(src_ref.at[pl.ds(0, MOE_RB), :],
                                     dst_ref.at[pl.ds(pl.multiple_of(b * MOE_RB, MOE_RB), MOE_RB), :], sem)

    def start(b, c):
        cp(b).start()
        return c
    lax.fori_loop(tail_ref[0], MOE_ROWS // MOE_RB, start, 0)

    def finish(b, c):
        cp(b).wait()
        return c
    lax.fori_loop(tail_ref[0], MOE_ROWS // MOE_RB, finish, 0)


def _scatter_kernel(dest_ref, from_ref, to_ref, tail_ref, hn_ref, xs_ref, zero_ref, sem, zsem):
    i = pl.program_id(0)
    tm = hn_ref.shape[0]

    @pl.when(i == 0)
    def _():
        zero_ref[...] = jnp.zeros_like(zero_ref)
        _tail_fill(zero_ref, xs_ref, tail_ref, zsem)

        def per_expert(e, c):
            def fill(r, c2):
                _row_copy(zero_ref, 0, xs_ref, r, zsem).start()
                return c2
            lax.fori_loop(from_ref[e], to_ref[e], fill, 0)

            def drain(r, c2):
                _row_copy(zero_ref, 0, xs_ref, r, zsem).wait()
                return c2
            lax.fori_loop(from_ref[e], to_ref[e], drain, 0)
            return c
        lax.fori_loop(0, N_EXP, per_expert, 0)

    def issue(t, c):
        for k in range(TOP_K):
            _row_copy(hn_ref, t, xs_ref, dest_ref[k * T + i * tm + t], sem).start(priority=k % 2)
        return c
    lax.fori_loop(0, tm, issue, 0)

    for k in range(TOP_K):
        pltpu.make_async_copy(hn_ref, xs_ref.at[pl.ds(0, tm), :], sem).wait()


def _scatter(hn, rt):
    tm = 1024
    return pl.pallas_call(
        _scatter_kernel,
        grid_spec=pltpu.PrefetchScalarGridSpec(
            num_scalar_prefetch=4,
            grid=(T // tm,),
            in_specs=[pl.BlockSpec((tm, D // 2), lambda i, *_: (i, 0))],
            out_specs=pl.BlockSpec(memory_space=pl.ANY),
            scratch_shapes=[pltpu.VMEM((MOE_RB, D // 2), jnp.int32), pltpu.SemaphoreType.DMA,
                            pltpu.SemaphoreType.DMA],
        ),
        out_shape=jax.ShapeDtypeStruct((MOE_ROWS, D // 2), jnp.int32),
        compiler_params=pltpu.CompilerParams(
            dimension_semantics=("arbitrary",), vmem_limit_bytes=VMEM_LIMIT),
        name="scatter",
    )(rt["dest"], rt["fill_from"], rt["fill_to"], rt["tail_blk"], hn)


def _experts_kernel(ve_ref, vcnt_ref, vrow_ref, nvis_ref, tail_ref, xs_ref, w1_ref, w2_ref, b1_ref, b2_ref, ys_ref,
                    xin_ref, xb_ref, acc_ref, wg_ref, wl_ref, wd_ref, isem, osem, wsem):
    v = pl.program_id(0)
    RB = MOE_RB
    BIG, MID = 4 * RB, 2 * RB
    TF, NF = MOE_TF, MOE_NF
    nvis = nvis_ref[0]
    active = v < nvis

    def geometry(vv):
        nblk = (vcnt_ref[vv] + RB - 1) // RB
        return vrow_ref[vv], nblk // 4, (nblk // 2) % 2, nblk % 2

    row0, nbig, mid, small = geometry(v)
    mid_r0 = nbig * BIG
    small_r0 = mid_r0 + mid * MID

    def x_copies(vrow, r0, n):
        return [pltpu.make_async_copy(xs_ref.at[pl.ds(pl.multiple_of(vrow + r0 + h, RB), min(MID, n - h)), :],
                                      xin_ref.at[h // MID, pl.ds(0, min(MID, n - h)), :], isem.at[h // MID])
                for h in range(0, n, MID)]

    def y_copy(r0, n):
        return pltpu.make_async_copy(acc_ref.at[pl.ds(pl.multiple_of(r0, RB), n), :],
                                     ys_ref.at[pl.ds(pl.multiple_of(row0 + r0, RB), n), :], osem)

    def tile_copies(e, t, slot):
        cols = pl.ds(pl.multiple_of(t * TF, TF), TF)
        lin_cols = pl.ds(pl.multiple_of(D_FF + t * TF, TF), TF)
        return (pltpu.make_async_copy(w1_ref.at[e, :, cols], wg_ref.at[slot], wsem.at[slot]),
                pltpu.make_async_copy(w1_ref.at[e, :, lin_cols], wl_ref.at[slot], wsem.at[slot]),
                pltpu.make_async_copy(w2_ref.at[e, cols, :], wd_ref.at[slot], wsem.at[slot]))

    def start_unit(vv, u):
        vrow, nb, md, sm = geometry(vv)

        @pl.when(u < nb)
        def _():
            for cp in x_copies(vrow, u * BIG, BIG):
                cp.start()

        @pl.when((u == nb) & (md == 1))
        def _():
            for cp in x_copies(vrow, nb * BIG, MID):
                cp.start()

        @pl.when((u == nb + md) & (sm == 1))
        def _():
            for cp in x_copies(vrow, nb * BIG + md * MID, RB):
                cp.start()

    @pl.when(v == 0)
    def _():
        acc_ref[...] = jnp.zeros_like(acc_ref)
        _tail_fill(acc_ref, ys_ref, tail_ref, osem)
        start_unit(0, 0)
        for cp in tile_copies(ve_ref[0], 0, 0):
            cp.start()

    def step(j, carry):
        slot = j % 2
        first = j == 0
        last = j == NF - 1
        for cp in tile_copies(ve_ref[v], j, slot):
            cp.wait()

        @pl.when(j + 1 < NF)
        def _():
            for cp in tile_copies(ve_ref[v], j + 1, 1 - slot):
                cp.start()

        @pl.when(last & (v + 1 < nvis))
        def _():
            for cp in tile_copies(ve_ref[jnp.minimum(v + 1, MOE_NV - 1)], 0, 1 - slot):
                cp.start()

        b1g = b1_ref[0, :, pl.ds(pl.multiple_of(j * TF, TF), TF)]
        b1l = b1_ref[0, :, pl.ds(pl.multiple_of(D_FF + j * TF, TF), TF)]

        def block(u, r0, n):
            rows = pl.ds(pl.multiple_of(r0, RB), n)

            @pl.when(first)
            def _():
                for h, cp in zip(range(0, n, MID), x_copies(row0, r0, n)):
                    cp.wait()
                    m = min(MID, n - h)
                    lo_half, hi_half = _unpack_halves(xin_ref[h // MID, pl.ds(0, m), :])
                    piece = pl.ds(pl.multiple_of(r0 + h, RB), m)
                    xb_ref[piece, :D // 2] = lo_half
                    xb_ref[piece, D // 2:] = hi_half
                start_unit(v, u + 1)

            x = xb_ref[rows, :]
            glu = jnp.minimum(_dot(x, wg_ref[slot].astype(BF16)) + b1g, SWIGLU_LIMIT)
            lin = jnp.clip(_dot(x, wl_ref[slot].astype(BF16)) + b1l, -SWIGLU_LIMIT, SWIGLU_LIMIT)
            act = glu * _sigmoid(SWIGLU_ALPHA * glu) * (lin + 1.0)
            part = _dot(act.astype(BF16), wd_ref[slot].astype(BF16))

            acc_ref[rows, :] = part + jnp.where(first, b2_ref[0], acc_ref[rows, :])

            @pl.when(last)
            def _():
                y_copy(r0, n).start()

        def big(i, c):
            block(i, i * BIG, BIG)
            return c
        lax.fori_loop(0, nbig, big, 0)

        @pl.when(mid == 1)
        def _():
            block(nbig, mid_r0, MID)

        @pl.when(small == 1)
        def _():
            block(nbig + mid, small_r0, RB)

        @pl.when(last)
        def _():
            def finish(i, c):
                y_copy(i * BIG, BIG).wait()
                return c
            lax.fori_loop(0, nbig, finish, 0)

            @pl.when(mid == 1)
            def _():
                y_copy(mid_r0, MID).wait()

            @pl.when(small == 1)
            def _():
                y_copy(small_r0, RB).wait()

            @pl.when(v + 1 < nvis)
            def _():
                start_unit(v + 1, 0)
        return carry

    @pl.when(active)
    def _():
        lax.fori_loop(0, NF, step, 0)


def _experts(xs, rt, w1, b1, w2, b2):
    b1 = b1.reshape(N_EXP, 1, 2 * D_FF)
    b2 = b2.reshape(N_EXP, 1, D)
    any_spec = pl.BlockSpec(memory_space=pl.ANY)
    return pl.pallas_call(
        _experts_kernel,
        grid_spec=pltpu.PrefetchScalarGridSpec(
            num_scalar_prefetch=5,
            grid=(MOE_NV,),
            in_specs=[
                any_spec, any_spec, any_spec,
                pl.BlockSpec((1, 1, 2 * D_FF), lambda v, ve, *s: (ve[v], 0, 0)),
                pl.BlockSpec((1, 1, D), lambda v, ve, *s: (ve[v], 0, 0)),
            ],
            out_specs=any_spec,
            scratch_shapes=[
                pltpu.VMEM((2, 2 * MOE_RB, D // 2), jnp.int32),
                pltpu.VMEM((MOE_RMAX, D), BF16),
                pltpu.VMEM((MOE_RMAX, D), F32),
                pltpu.VMEM((2, D, MOE_TF), F32),
                pltpu.VMEM((2, D, MOE_TF), F32),
                pltpu.VMEM((2, MOE_TF, D), F32),
                pltpu.SemaphoreType.DMA((2,)),
                pltpu.SemaphoreType.DMA,
                pltpu.SemaphoreType.DMA((2,)),
            ],
        ),
        out_shape=jax.ShapeDtypeStruct((MOE_ROWS, D), F32),
        compiler_params=pltpu.CompilerParams(
            dimension_semantics=("arbitrary",), vmem_limit_bytes=VMEM_LIMIT),
        name="experts",
    )(rt["vis_e"], rt["vis_cnt"], rt["vis_row"], rt["n_vis"], rt["tail_blk"], xs, w1, w2, b1, b2)


def _combine_kernel(dest_ref, ys_ref, h_ref, gate_ref, gf_ref, o_ref, buf_ref, sem):
    i = pl.program_id(0)
    tm = h_ref.shape[0]

    def gather(tile, slot):
        def issue(t, c):
            for k in range(TOP_K):
                _row_copy(ys_ref, dest_ref[k * T + tile * tm + t], buf_ref.at[slot, k], t,
                          sem.at[slot]).start(priority=k % 2)
            return c
        lax.fori_loop(0, tm, issue, 0)

    @pl.when(i == 0)
    def _():
        gather(0, 0)

    @pl.when(i + 1 < pl.num_programs(0))
    def _():
        gather(i + 1, (i + 1) % 2)

    slot = i % 2
    for k in range(TOP_K):
        pltpu.make_async_copy(ys_ref.at[pl.ds(0, tm), :], buf_ref.at[slot, k], sem.at[slot]).wait()

    h = h_ref[...]
    for k in range(TOP_K):
        h = h + gate_ref[:, k:k + 1] * buf_ref[slot, k]
    o_ref[...] = _rms(h, gf_ref[...])


def _combine(ys, h, gates_t, gf, rt):
    tm = 256
    return pl.pallas_call(
        _combine_kernel,
        grid_spec=pltpu.PrefetchScalarGridSpec(
            num_scalar_prefetch=1,
            grid=(T // tm,),
            in_specs=[
                pl.BlockSpec(memory_space=pl.ANY),
                pl.BlockSpec((tm, D), lambda i, *_: (i, 0)),
                pl.BlockSpec((tm, TOP_K), lambda i, *_: (i, 0)),
                pl.BlockSpec((1, D), lambda i, *_: (0, 0)),
            ],
            out_specs=pl.BlockSpec((tm, D), lambda i, *_: (i, 0)),
            scratch_shapes=[pltpu.VMEM((2, TOP_K, tm, D), F32), pltpu.SemaphoreType.DMA((2,))],
        ),
        out_shape=jax.ShapeDtypeStruct((T, D), F32),
        compiler_params=pltpu.CompilerParams(
            dimension_semantics=("arbitrary",), vmem_limit_bytes=VMEM_LIMIT),
        name="combine",
    )(rt["dest"], ys, h, gates_t, gf)


def kernel(x, norm1_g, w_in, gla_gate_w2, gla_gate_b, gla_norm_g, s5_a_re, s5_a_im, s5_b_re, s5_b_im, s5_c_re, s5_c_im, s5_d, s5_log_step, s5_glu_w, s5_glu_b, s5_norm_g, w_out, norm2_g, router_w, router_b, expert_w1, expert_b1, expert_w2, expert_b2, final_norm_g):
    assert x.shape == (1, T, D) and w_in.shape[0] == 1
    xt = x.reshape(T, D)
    w2g = jnp.pad(gla_gate_w2[0], ((0, 128 - GATE_RANK), (0, 0)))
    qkvr, lg, u = _proj(xt, norm1_g, jnp.swapaxes(w_in, 1, 2), w2g, gla_gate_b)
    gla = _gla(qkvr, lg, gla_norm_g)
    tables = _s5_tables(s5_a_re[0], s5_a_im[0], s5_b_re[0], s5_b_im[0], s5_c_re[0], s5_c_im[0],
                        s5_d[0], s5_log_step[0])
    y = _s5(u, tables)
    h, hn, top_idx, gates, rank, counts = _post(
        y, gla, xt, s5_glu_w[0].astype(BF16), s5_glu_b, s5_norm_g, w_out[0].astype(BF16), norm2_g,
        jnp.pad(router_w[0], ((0, 0), (0, 128 - N_EXP))), router_b.reshape(N_EXP, 1))
    rt = _route(top_idx, rank, counts)
    xs = _scatter(hn, rt)
    ys = _experts(xs, rt, expert_w1[0], expert_b1[0], expert_w2[0], expert_b2[0])
    out = _combine(ys, h, gates.T, final_norm_g.reshape(1, D), rt)
    return out.reshape(1, T, D)
```

```python
import math

import jax
import jax.numpy as jnp
from jax import lax
from jax.experimental import pallas as pl
from jax.experimental.pallas import tpu as pltpu

F32 = jnp.float32
BF16 = jnp.bfloat16
HIGHEST = lax.Precision.HIGHEST

T = 8192
D = 2048
GLA_H = 4
GLA_DK = 128
GLA_DV = 256
GLA_CHUNK = 64
GLA_SUB = 16
GATE_RANK = 16
GATE_TAU = 16.0
S5_G = 64
S5_N = 16
S5_P = 64
S5_L = 16
S5_GT = 8
N_EXP = 32
TOP_K = 4
D_FF = 2048
SWIGLU_ALPHA = 1.702
SWIGLU_LIMIT = 7.0
RMS_EPS = 1e-6

MOE_RB = 128
MOE_RMAX = 2176
MOE_TF = 512
MOE_NF = D_FF // MOE_TF
MOE_NV = -(-T * TOP_K // MOE_RMAX) + N_EXP
MOE_ROWS = T * TOP_K + N_EXP * MOE_RB
VMEM_LIMIT = 56 * 1024 * 1024

NN = (((1,), (0,)), ((), ()))
NT = (((1,), (1,)), ((), ()))
TN = (((0,), (0,)), ((), ()))


def _dot(a, b, dims=NN, precision=None):
    return lax.dot_general(a, b, dims, preferred_element_type=F32, precision=precision)


def _dot3(a, b):
    a_hi, b_hi = a.astype(BF16), b.astype(BF16)
    a_lo = (a - a_hi.astype(F32)).astype(BF16)
    b_lo = (b - b_hi.astype(F32)).astype(BF16)
    return _dot(a_hi, b_hi) + (_dot(a_lo, b_hi) + _dot(a_hi, b_lo))


def _rms(x, g):
    return x * lax.rsqrt(jnp.mean(x * x, axis=-1, keepdims=True) + RMS_EPS) * g


def _sigmoid(x):
    return 1.0 / (1.0 + jnp.exp(-x))


def _pack_halves(lo_half, hi_half):
    a = lax.bitcast_convert_type(lo_half.astype(F32), jnp.int32)
    b = lax.bitcast_convert_type(hi_half.astype(F32), jnp.int32)
    return a | lax.shift_right_logical(b, 16)


def _unpack_halves(p):
    a = lax.bitcast_convert_type(p & jnp.int32(-65536), F32).astype(BF16)
    b = lax.bitcast_convert_type(lax.shift_left(p, 16), F32).astype(BF16)
    return a, b


def _const_spec(shape):
    nd = len(shape)
    return pl.BlockSpec(shape, lambda *_: (0,) * nd, pipeline_mode=pl.Buffered(1))


PROJ_QKVR = 2 * GLA_H * GLA_DK + 2 * GLA_H * GLA_DV
PROJ_U = S5_G * S5_N
PROJ_CHUNK = 512


def _proj_kernel(x_ref, g1_ref, wt_in_ref, w2_ref, gb_ref, qkvr_ref, lg_ref, u_ref,
                 wq_ref, wg_ref, wu_ref, stage_ref, gstage_ref, sem, gsem):
    @pl.when(pl.program_id(0) == 0)
    def _():
        pieces = ([(wq_ref, c) for c in range(PROJ_QKVR // PROJ_CHUNK)]
                  + [(wu_ref, c) for c in range(PROJ_U // PROJ_CHUNK)])

        def first_row(k):
            dst, c = pieces[k]
            return c * PROJ_CHUNK + (0 if dst is wq_ref else PROJ_QKVR + GATE_RANK)

        def piece(k):
            return pltpu.make_async_copy(wt_in_ref.at[0, pl.ds(first_row(k), PROJ_CHUNK), :], stage_ref.at[k % 2],
                                         sem.at[k % 2])

        gate = pltpu.make_async_copy(wt_in_ref.at[0, pl.ds(PROJ_QKVR, GATE_RANK), :], gstage_ref, gsem)
        gate.start()
        piece(0).start()
        for k, (dst, c) in enumerate(pieces):
            piece(k).wait()
            if k + 1 < len(pieces):
                piece(k + 1).start()
            dst[:, c * PROJ_CHUNK:(c + 1) * PROJ_CHUNK] = stage_ref[k % 2].T.astype(BF16)
        gate.wait()
        padded = jnp.concatenate([gstage_ref[...], jnp.zeros((128 - GATE_RANK, D), F32)], axis=0)
        wg_ref[...] = padded.T.astype(BF16)

    hb = _rms(x_ref[...], g1_ref[...]).astype(BF16)
    p = _dot(hb, wq_ref[...])
    nq = GLA_H * GLA_DK
    qkvr_ref[:, :nq] = (p[:, :nq] * (GLA_DK ** -0.5)).astype(BF16)
    qkvr_ref[:, nq:] = p[:, nq:].astype(BF16)
    glr = _dot(hb, wg_ref[...])
    z = _dot3(glr, w2_ref[...]) + gb_ref[...]
    lg_ref[...] = (jnp.minimum(z, 0.0) - jnp.log(1.0 + jnp.exp(-jnp.abs(z)))) * (1.0 / GATE_TAU)
    u_ref[...] = _dot(hb, wu_ref[...])


def _proj(x, g1, wt_in, w2, gb):
    tm = 256
    return pl.pallas_call(
        _proj_kernel,
        grid=(T // tm,),
        in_specs=[
            pl.BlockSpec((tm, D), lambda i: (i, 0)),
            _const_spec((1, D)),
            pl.BlockSpec(memory_space=pl.ANY),
            _const_spec(w2.shape), _const_spec(gb.shape),
        ],
        out_specs=[
            pl.BlockSpec((tm, PROJ_QKVR), lambda i: (i, 0)),
            pl.BlockSpec((tm, GLA_H * GLA_DK), lambda i: (i, 0)),
            pl.BlockSpec((tm, PROJ_U), lambda i: (i, 0)),
        ],
        out_shape=[
            jax.ShapeDtypeStruct((T, PROJ_QKVR), BF16),
            jax.ShapeDtypeStruct((T, GLA_H * GLA_DK), F32),
            jax.ShapeDtypeStruct((T, PROJ_U), F32),
        ],
        scratch_shapes=[
            pltpu.VMEM((D, PROJ_QKVR), BF16),
            pltpu.VMEM((D, 128), BF16),
            pltpu.VMEM((D, PROJ_U), BF16),
            pltpu.VMEM((2, PROJ_CHUNK, D), F32),
            pltpu.VMEM((GATE_RANK, D), F32),
            pltpu.SemaphoreType.DMA((2,)),
            pltpu.SemaphoreType.DMA,
        ],
        compiler_params=pltpu.CompilerParams(
            dimension_semantics=("arbitrary",), vmem_limit_bytes=VMEM_LIMIT),
        name="proj",
    )(x, g1, wt_in, w2, gb)


def _gla_kernel(q_ref, k_ref, v_ref, r_ref, lg_ref, gn_ref, tril_ref, esel_ref, o_ref, state_ref):
    C, S = GLA_CHUNK, GLA_SUB
    nsub = C // S

    @pl.when(pl.program_id(0) == 0)
    def _():
        state_ref[...] = jnp.zeros_like(state_ref)

    row = lax.broadcasted_iota(jnp.int32, (C, C), 0)
    col = lax.broadcasted_iota(jnp.int32, (C, C), 1)
    below = (col // S) < (row // S)
    diag = ((col // S) == (row // S)) & (col <= row)
    tril = tril_ref[...]
    esel = esel_ref[...]
    gn = gn_ref[...]

    def head(h, rows, b):
        kcols = slice(h * GLA_DK, (h + 1) * GLA_DK)
        vcols = slice(h * GLA_DV, (h + 1) * GLA_DV)
        q = q_ref[rows, kcols].astype(F32)
        k = k_ref[rows, kcols].astype(F32)
        v = v_ref[rows, vcols]
        bl = b[C - 1:C, :]
        st = state_ref[h]
        o = _dot((q * jnp.exp2(b)).astype(BF16), st.astype(BF16), NT)
        s_rows = [jnp.zeros((S, C), F32)]
        for j in range(1, nsub):
            ref = b[j * S - 1:j * S, :]
            qj = (q[j * S:(j + 1) * S] * jnp.exp2(b[j * S:(j + 1) * S] - ref)).astype(BF16)
            kj = (k * jnp.exp2(jnp.minimum(ref - b, 0.0))).astype(BF16)
            s_rows.append(_dot(qj, kj, NT))
        s_off = jnp.concatenate(s_rows, axis=0)
        q3 = q.reshape(nsub, S, GLA_DK)
        k3 = k.reshape(nsub, S, GLA_DK)
        b3 = b.reshape(nsub, S, GLA_DK)
        xs = []
        for s in range(S):
            lo = (s // 8) * 8
            dec = jnp.exp2(jnp.minimum(b3[:, lo:] - b3[:, s:s + 1, :], 0.0))
            x = q3[:, lo:] * k3[:, s:s + 1, :] * dec
            if lo:
                x = jnp.concatenate([jnp.zeros((nsub, lo, GLA_DK), F32), x], axis=1)
            xs.append(x.reshape(C, GLA_DK).astype(BF16))
        dsc = _dot(jnp.concatenate(xs, axis=1), esel)
        a = jnp.where(below, s_off, jnp.where(diag, dsc, 0.0))
        o = o + _dot(a.astype(BF16), v)
        kout = (k * jnp.exp2(bl - b)).astype(BF16)
        state_ref[h] = st * jnp.exp2(bl) + _dot(v, kout, TN)
        y = _rms(o, gn)
        r = r_ref[rows, vcols].astype(F32)
        o_ref[rows, vcols] = (y * (r * _sigmoid(r))).astype(BF16)

    def chunk(c, carry):
        rows = pl.ds(pl.multiple_of(c * C, C), C)
        g = lg_ref[rows, :]
        g_hi = g.astype(BF16)
        g_mid = (g - g_hi.astype(F32)).astype(BF16)
        g_lo = (g - g_hi.astype(F32) - g_mid.astype(F32)).astype(BF16)
        b = (_dot(tril, g_hi) + (_dot(tril, g_mid) + _dot(tril, g_lo))) * math.log2(math.e)
        for h in range(GLA_H):
            head(h, rows, b[:, h * GLA_DK:(h + 1) * GLA_DK])
        return carry

    lax.fori_loop(0, q_ref.shape[0] // C, chunk, 0, unroll=8)


def _gla(qkvr, lg, gn):
    tb = 512
    C, S = GLA_CHUNK, GLA_SUB
    nk, nv = GLA_H * GLA_DK, GLA_H * GLA_DV
    tril = jnp.tril(jnp.ones((C, C), BF16))
    esel = (jnp.arange(S * GLA_DK)[:, None] // GLA_DK == jnp.arange(C)[None, :] % S).astype(BF16)
    return pl.pallas_call(
        _gla_kernel,
        grid=(T // tb,),
        in_specs=[
            pl.BlockSpec((tb, nk), lambda i: (i, 0)),
            pl.BlockSpec((tb, nk), lambda i: (i, 1)),
            pl.BlockSpec((tb, nv), lambda i: (i, 1)),
            pl.BlockSpec((tb, nv), lambda i: (i, 2)),
            pl.BlockSpec((tb, nk), lambda i: (i, 0)),
            _const_spec((1, GLA_DV)), _const_spec((C, C)), _const_spec((S * GLA_DK, C)),
        ],
        out_specs=pl.BlockSpec((tb, nv), lambda i: (i, 0)),
        out_shape=jax.ShapeDtypeStruct((T, nv), BF16),
        scratch_shapes=[pltpu.VMEM((GLA_H, GLA_DV, GLA_DK), F32)],
        compiler_params=pltpu.CompilerParams(
            dimension_semantics=("arbitrary",), vmem_limit_bytes=VMEM_LIMIT),
        name="gla",
    )(qkvr, qkvr, qkvr, qkvr, lg, gn, tril, esel)


def _s5_tables(a_re, a_im, b_re, b_im, c_re, c_im, d_skip, log_step):
    L, G, P, N = S5_L, S5_G, S5_P, S5_N
    delta = jnp.exp(log_step)[:, None]
    ar, ai = a_re * delta, a_im * delta

    def apow(tau):
        tau = jnp.asarray(tau, F32)[None, :, None]
        mag = jnp.exp(ar[:, None, :] * tau)
        ph = ai[:, None, :] * tau
        return mag * jnp.cos(ph), mag * jnp.sin(ph)

    p_re, p_im = apow(jnp.arange(L + 1))
    ab_re, ab_im = p_re[:, 1], p_im[:, 1]
    den = a_re * a_re + a_im * a_im
    cf_re = ((ab_re - 1.0) * a_re + ab_im * a_im) / den
    cf_im = (ab_im * a_re - (ab_re - 1.0) * a_im) / den
    bb_re = cf_re[:, :, None] * b_re - cf_im[:, :, None] * b_im
    bb_im = cf_re[:, :, None] * b_im + cf_im[:, :, None] * b_re
    e_re = p_re[:, :L, :, None] * bb_re[:, None] - p_im[:, :L, :, None] * bb_im[:, None]
    e_im = p_re[:, :L, :, None] * bb_im[:, None] + p_im[:, :L, :, None] * bb_re[:, None]
    kk = (jnp.einsum('gnp,gtpm->gtnm', c_re, e_re, precision=HIGHEST)
          - jnp.einsum('gnp,gtpm->gtnm', c_im, e_im, precision=HIGHEST))
    kk = kk.at[:, 0].add(jnp.eye(N, dtype=F32)[None] * d_skip[:, :, None])
    r_re, r_im = apow(L - 1 - jnp.arange(L))
    bs_re = (r_re[:, :, None, :] * bb_re.transpose(0, 2, 1)[:, None] - r_im[:, :, None, :] * bb_im.transpose(0, 2, 1)[:, None])
    bs_im = (r_re[:, :, None, :] * bb_im.transpose(0, 2, 1)[:, None] + r_im[:, :, None, :] * bb_re.transpose(0, 2, 1)[:, None])
    ct_re = c_re.transpose(0, 2, 1)[:, :, None, :]
    ct_im = c_im.transpose(0, 2, 1)[:, :, None, :]
    q_re = p_re[:, 1:].transpose(0, 2, 1)[:, :, :, None]
    q_im = p_im[:, 1:].transpose(0, 2, 1)[:, :, :, None]
    cre = ct_re * q_re - ct_im * q_im
    cim = -(ct_re * q_im + ct_im * q_re)
    GT = S5_GT
    nt = G // GT

    def tile_groups(w):
        return w.reshape((nt, GT) + w.shape[1:])

    kc = tile_groups(kk).transpose(0, 1, 4, 2, 3).reshape(nt, GT * N, L * N)

    def in_table(w):
        return tile_groups(w).transpose(0, 2, 1, 3, 4).reshape(nt, L * GT * N, P)

    def out_table(w):
        return w.reshape(nt, GT * P, L * N)

    nstep = (T // L - 1).bit_length()
    sc_re, sc_im = apow(jnp.asarray([L * 2 ** i for i in range(nstep)]))

    def lanes(w):
        return tile_groups(w).transpose(0, 2, 1, 3).reshape(nt, w.shape[1], GT * P)

    r1 = jnp.arange(L * N)[:, None]
    c1 = jnp.arange(L * GT * N)[None, :]
    wide_tn = ((r1 // N == c1 // (GT * N)) & (r1 % N == c1 % N)).astype(BF16)
    wide_p = (jnp.arange(P)[:, None] == jnp.arange(GT * P)[None, :] % P).astype(BF16)
    return dict(
        kc=kc.astype(BF16),
        bs_re=in_table(bs_re).astype(BF16), bs_im=in_table(bs_im).astype(BF16),
        cre=out_table(cre).astype(BF16), cim=out_table(cim).astype(BF16),
        sc_re=lanes(sc_re), sc_im=lanes(sc_im), wide_tn=wide_tn, wide_p=wide_p,
    )


def _s5_kernel(u_ref, kc_ref, bsr_ref, bsi_ref, cre_ref, cim_ref, scr_ref, sci_ref, wtn_ref, wp_ref,
               y_ref, m_ref):
    L, N, P = S5_L, S5_N, S5_P
    W = S5_GT * N
    nch = u_ref.shape[0] // L

    @pl.when(pl.program_id(0) == 0)
    def _():
        m_ref[...] = jnp.zeros_like(m_ref)

    def widen(compact, wide, row_group, col_group):
        full = _dot(compact, wide)
        r = lax.broadcasted_iota(jnp.int32, full.shape, 0)
        c = lax.broadcasted_iota(jnp.int32, full.shape, 1)
        return jnp.where(row_group(r) == col_group(c), full, 0.0).astype(BF16)

    wtn, wp = wtn_ref[...], wp_ref[...]
    bd = widen(kc_ref[0], wtn, lambda r: r // N, lambda c: (c % W) // N)
    bsr = widen(bsr_ref[0], wp, lambda r: (r % W) // N, lambda c: c // P)
    bsi = widen(bsi_ref[0], wp, lambda r: (r % W) // N, lambda c: c // P)
    cre = widen(cre_ref[0], wtn, lambda r: r // P, lambda c: (c % W) // N)
    cim = widen(cim_ref[0], wtn, lambda r: r // P, lambda c: (c % W) // N)
    for s in range(L):
        m_ref[s * W:(s + 1) * W, s * W:] = bd[:, :(L - s) * W]
    ucat = jnp.concatenate([u_ref[pl.ds(s, nch, stride=L), :].astype(BF16) for s in range(L)], axis=1)
    hr = _dot(ucat, bsr)
    hi = _dot(ucat, bsi)
    row = lax.broadcasted_iota(jnp.int32, hr.shape, 0)

    def shift(x, d):
        if d % 8 == 0:
            return jnp.concatenate([jnp.zeros((d, x.shape[1]), F32), x[:nch - d]], axis=0)
        return jnp.where(row >= d, pltpu.roll(x, d, 0), 0.0)

    for i in range(scr_ref.shape[1]):
        mr, mi = scr_ref[0, i:i + 1, :], sci_ref[0, i:i + 1, :]
        pr, pi = shift(hr, 2 ** i), shift(hi, 2 ** i)
        hr, hi = hr + mr * pr - mi * pi, hi + mr * pi + mi * pr
    gr, gi = shift(hr, 1), shift(hi, 1)
    grb, gib = gr.astype(BF16), gi.astype(BF16)
    for tp in range(L // 2):
        cols = slice(2 * tp * W, (2 * tp + 2) * W)
        depth = (2 * tp + 2) * W
        y = _dot(ucat[:, :depth], m_ref[:depth, cols]) + _dot(grb, cre[:, cols]) + _dot(gib, cim[:, cols])
        for t in (2 * tp, 2 * tp + 1):
            y_ref[pl.ds(t, nch, stride=L), :] = y[:, (t - 2 * tp) * W:(t - 2 * tp + 1) * W]


def _s5(u, tb):
    W = S5_GT * S5_N
    nt = S5_G // S5_GT

    def tile_spec(a):
        return pl.BlockSpec((1,) + a.shape[1:], lambda g: (g,) + (0,) * (a.ndim - 1))

    ws = [tb[n] for n in ["kc", "bs_re", "bs_im", "cre", "cim", "sc_re", "sc_im"]]
    consts = [tb["wide_tn"], tb["wide_p"]]
    return pl.pallas_call(
        _s5_kernel,
        grid=(nt,),
        in_specs=([pl.BlockSpec((T, W), lambda g: (0, g))] + [tile_spec(w) for w in ws]
                  + [_const_spec(c.shape) for c in consts]),
        out_specs=pl.BlockSpec((T, W), lambda g: (0, g)),
        out_shape=jax.ShapeDtypeStruct((T, S5_G * S5_N), F32),
        scratch_shapes=[pltpu.VMEM((S5_L * W, S5_L * W), BF16)],
        compiler_params=pltpu.CompilerParams(
            dimension_semantics=("arbitrary",), vmem_limit_bytes=VMEM_LIMIT),
        name="s5",
    )(u, *ws, *consts)


def _post_kernel(y_ref, gla_ref, x_ref, gw_ref, gb_ref, sg_ref, wo_ref, n2_ref, rw_ref, rb_ref, before_ref,
                 h_ref, hn_ref, idx_ref, gate_ref, rank_ref, cnt_ref, run_ref):
    @pl.when(pl.program_id(0) == 0)
    def _():
        run_ref[...] = jnp.zeros_like(run_ref)

    y = y_ref[...]
    z = 0.5 * y * (1.0 + jnp.tanh(math.sqrt(2.0 / math.pi) * (y + 0.044715 * (y * y * y))))
    z = z * _sigmoid(_dot(z.astype(BF16), gw_ref[...]) + gb_ref[...])
    s5o = _rms(z, sg_ref[...]).astype(BF16)
    half = GLA_H * GLA_DV
    h = x_ref[...] + _dot(gla_ref[...], wo_ref[:half, :]) + _dot(s5o, wo_ref[half:, :])
    h_ref[...] = h
    hn = _rms(h, n2_ref[...])
    hi = hn.astype(BF16)
    hn_ref[...] = _pack_halves(hi[:, :D // 2], hi[:, D // 2:])
    lo = (hn - hi.astype(F32)).astype(BF16)
    rw = rw_ref[...]
    rw_hi = rw.astype(BF16)
    rw_lo = (rw - rw_hi.astype(F32)).astype(BF16)
    lg = _dot(hi, rw_hi) + (_dot(lo, rw_hi) + _dot(hi, rw_lo))
    lt = lg.T[:N_EXP] + rb_ref[...]
    eid = lax.broadcasted_iota(jnp.int32, lt.shape, 0).astype(F32)
    vals, idxs = [], []
    for _ in range(TOP_K):
        m = jnp.max(lt, axis=0, keepdims=True)
        sel = jnp.min(jnp.where(lt == m, eid, float(N_EXP)), axis=0, keepdims=True)
        vals.append(m)
        idxs.append(sel)
        lt = jnp.where(eid == sel, -jnp.inf, lt)
    ex = [jnp.exp(vv - vals[0]) for vv in vals]
    inv = 1.0 / (ex[0] + ex[1] + ex[2] + ex[3])
    idx_ref[...] = jnp.concatenate(idxs, axis=0).astype(jnp.int32)
    gate_ref[...] = jnp.concatenate([e * inv for e in ex], axis=0)
    run = run_ref[:, :1]
    ranks = []
    for sel in idxs:
        onehot = jnp.where(eid == sel, 1.0, 0.0)
        earlier = _dot(onehot.astype(BF16), before_ref[...])
        ranks.append(jnp.sum(onehot * (run + earlier), axis=0, keepdims=True))
        run = run + jnp.sum(onehot, axis=1, keepdims=True)
    rank_ref[...] = jnp.concatenate(ranks, axis=0).astype(jnp.int32)
    run_ref[...] = jnp.broadcast_to(run, run_ref.shape)
    cnt_ref[...] = jnp.broadcast_to(run, cnt_ref.shape)


def _post(y, gla, x, gw, gb, sg, wo, n2, rwt, rb):
    tm = 256
    width = S5_G * S5_N
    before = (jnp.arange(tm)[:, None] < jnp.arange(tm)[None, :]).astype(BF16)
    return pl.pallas_call(
        _post_kernel,
        grid=(T // tm,),
        in_specs=[
            pl.BlockSpec((tm, width), lambda i: (i, 0)),
            pl.BlockSpec((tm, GLA_H * GLA_DV), lambda i: (i, 0)),
            pl.BlockSpec((tm, D), lambda i: (i, 0)),
            _const_spec(gw.shape), _const_spec(gb.shape), _const_spec(sg.shape), _const_spec(wo.shape),
            _const_spec(n2.shape), _const_spec(rwt.shape), _const_spec(rb.shape), _const_spec(before.shape),
        ],
        out_specs=[
            pl.BlockSpec((tm, D), lambda i: (i, 0)),
            pl.BlockSpec((tm, D // 2), lambda i: (i, 0)),
            pl.BlockSpec((TOP_K, tm), lambda i: (0, i)),
            pl.BlockSpec((TOP_K, tm), lambda i: (0, i)),
            pl.BlockSpec((TOP_K, tm), lambda i: (0, i)),
            pl.BlockSpec((N_EXP, 128), lambda i: (0, 0)),
        ],
        out_shape=[
            jax.ShapeDtypeStruct((T, D), F32),
            jax.ShapeDtypeStruct((T, D // 2), jnp.int32),
            jax.ShapeDtypeStruct((TOP_K, T), jnp.int32),
            jax.ShapeDtypeStruct((TOP_K, T), F32),
            jax.ShapeDtypeStruct((TOP_K, T), jnp.int32),
            jax.ShapeDtypeStruct((N_EXP, 128), F32),
        ],
        scratch_shapes=[pltpu.VMEM((N_EXP, 128), F32)],
        compiler_params=pltpu.CompilerParams(
            dimension_semantics=("arbitrary",), vmem_limit_bytes=VMEM_LIMIT),
        name="post",
    )(y, gla, x, gw, gb, sg, wo, n2, rwt, rb, before)


def _route(top_idx, rank, counts):
    e_flat = top_idx.reshape(-1)
    rank = rank.reshape(-1)
    counts = counts[:, 0].astype(jnp.int32)
    padded = (counts + MOE_RB - 1) // MOE_RB * MOE_RB
    pad_end = jnp.cumsum(padded)
    pad_start = pad_end - padded
    dest = _dest(top_idx, rank.reshape(top_idx.shape), pad_start.astype(jnp.int32)).reshape(-1)
    nvis_e = (counts + MOE_RMAX - 1) // MOE_RMAX
    vis_end = jnp.cumsum(nvis_e)
    n_vis = vis_end[-1]
    v = jnp.arange(MOE_NV, dtype=jnp.int32)
    vc = jnp.minimum(v, n_vis - 1)
    ve = jnp.minimum(jnp.searchsorted(vis_end, vc, side='right'), N_EXP - 1).astype(jnp.int32)
    local = vc - (vis_end[ve] - nvis_e[ve])
    vcnt = jnp.where(v < n_vis, jnp.minimum(MOE_RMAX, counts[ve] - local * MOE_RMAX), 0)
    vrow = pad_start[ve] + local * MOE_RMAX
    return dict(dest=dest, fill_from=(pad_start + counts).astype(jnp.int32), fill_to=pad_end.astype(jnp.int32),
                tail_blk=(pad_end[-1:] // MOE_RB).astype(jnp.int32),
                vis_e=ve, vis_cnt=vcnt.astype(jnp.int32), vis_row=vrow.astype(jnp.int32),
                n_vis=n_vis.reshape(1).astype(jnp.int32))


def _dest_kernel(start_ref, idx_ref, rank_ref, dest_ref):
    e = idx_ref[...]
    d = rank_ref[...]
    for x in range(N_EXP):
        d = d + jnp.where(e == x, start_ref[x], 0)
    dest_ref[...] = d


def _dest(top_idx, rank, pad_start):
    full = pl.BlockSpec(top_idx.shape, lambda i, *_: (0, 0))
    return pl.pallas_call(
        _dest_kernel,
        grid_spec=pltpu.PrefetchScalarGridSpec(num_scalar_prefetch=1, grid=(1,), in_specs=[full, full], out_specs=full),
        out_shape=jax.ShapeDtypeStruct(top_idx.shape, jnp.int32),
        name="dest",
    )(pad_start, top_idx, rank)


def _row_copy(src_ref, srow, dst_ref, drow, sem):
    return pltpu.make_async_copy(src_ref.at[pl.ds(srow, 1), :], dst_ref.at[pl.ds(drow, 1), :], sem)


def _tail_fill(src_ref, dst_ref, tail_ref, sem):
    def cp(b):
        return pltpu.make_async_copy(src_ref.at[pl.ds(0, MOE_RB), :],
                                     dst_ref.at[pl.ds(pl.multiple_of(b * MOE_RB, MOE_RB), MOE_RB), :], sem)

    def start(b, c):
        cp(b).start()
        return c
    lax.fori_loop(tail_ref[0], MOE_ROWS // MOE_RB, start, 0)

    def finish(b, c):
        cp(b).wait()
        return c
    lax.fori_loop(tail_ref[0], MOE_ROWS // MOE_RB, finish, 0)


def _scatter_kernel(dest_ref, from_ref, to_ref, tail_ref, hn_ref, xs_ref, zero_ref, sem, zsem):
    i = pl.program_id(0)
    tm = hn_ref.shape[0]

    @pl.when(i == 0)
    def _():
        zero_ref[...] = jnp.zeros_like(zero_ref)
        _tail_fill(zero_ref, xs_ref, tail_ref, zsem)

        def per_expert(e, c):
            def fill(r, c2):
                _row_copy(zero_ref, 0, xs_ref, r, zsem).start()
                return c2
            lax.fori_loop(from_ref[e], to_ref[e], fill, 0)

            def drain(r, c2):
                _row_copy(zero_ref, 0, xs_ref, r, zsem).wait()
                return c2
            lax.fori_loop(from_ref[e], to_ref[e], drain, 0)
            return c
        lax.fori_loop(0, N_EXP, per_expert, 0)

    def issue(t, c):
        for k in range(TOP_K):
            _row_copy(hn_ref, t, xs_ref, dest_ref[k * T + i * tm + t], sem).start(priority=k % 2)
        return c
    lax.fori_loop(0, tm, issue, 0)

    for k in range(TOP_K):
        pltpu.make_async_copy(hn_ref, xs_ref.at[pl.ds(0, tm), :], sem).wait()


def _scatter(hn, rt):
    tm = 1024
    return pl.pallas_call(
        _scatter_kernel,
        grid_spec=pltpu.PrefetchScalarGridSpec(
            num_scalar_prefetch=4,
            grid=(T // tm,),
            in_specs=[pl.BlockSpec((tm, D // 2), lambda i, *_: (i, 0))],
            out_specs=pl.BlockSpec(memory_space=pl.ANY),
            scratch_shapes=[pltpu.VMEM((MOE_RB, D // 2), jnp.int32), pltpu.SemaphoreType.DMA,
                            pltpu.SemaphoreType.DMA],
        ),
        out_shape=jax.ShapeDtypeStruct((MOE_ROWS, D // 2), jnp.int32),
        compiler_params=pltpu.CompilerParams(
            dimension_semantics=("arbitrary",), vmem_limit_bytes=VMEM_LIMIT),
        name="scatter",
    )(rt["dest"], rt["fill_from"], rt["fill_to"], rt["tail_blk"], hn)


def _experts_kernel(ve_ref, vcnt_ref, vrow_ref, nvis_ref, tail_ref, xs_ref, w1_ref, w2_ref, b1_ref, b2_ref, ys_ref,
                    xin_ref, xb_ref, acc_ref, wg_ref, wl_ref, wd_ref, isem, osem, wsem):
    v = pl.program_id(0)
    RB = MOE_RB
    BIG, MID = 4 * RB, 2 * RB
    TF, NF = MOE_TF, MOE_NF
    nvis = nvis_ref[0]
    active = v < nvis

    def geometry(vv):
        nblk = (vcnt_ref[vv] + RB - 1) // RB
        return vrow_ref[vv], nblk // 4, (nblk // 2) % 2, nblk % 2

    row0, nbig, mid, small = geometry(v)
    mid_r0 = nbig * BIG
    small_r0 = mid_r0 + mid * MID

    def x_copies(vrow, r0, n):
        return [pltpu.make_async_copy(xs_ref.at[pl.ds(pl.multiple_of(vrow + r0 + h, RB), min(MID, n - h)), :],
                                      xin_ref.at[h // MID, pl.ds(0, min(MID, n - h)), :], isem.at[h // MID])
                for h in range(0, n, MID)]

    def y_copy(r0, n):
        return pltpu.make_async_copy(acc_ref.at[pl.ds(pl.multiple_of(r0, RB), n), :],
                                     ys_ref.at[pl.ds(pl.multiple_of(row0 + r0, RB), n), :], osem)

    def tile_copies(e, t, slot):
        cols = pl.ds(pl.multiple_of(t * TF, TF), TF)
        lin_cols = pl.ds(pl.multiple_of(D_FF + t * TF, TF), TF)
        return (pltpu.make_async_copy(w1_ref.at[e, :, cols], wg_ref.at[slot], wsem.at[slot]),
                pltpu.make_async_copy(w1_ref.at[e, :, lin_cols], wl_ref.at[slot], wsem.at[slot]),
                pltpu.make_async_copy(w2_ref.at[e, cols, :], wd_ref.at[slot], wsem.at[slot]))

    def start_unit(vv, u):
        vrow, nb, md, sm = geometry(vv)

        @pl.when(u < nb)
        def _():
            for cp in x_copies(vrow, u * BIG, BIG):
                cp.start()

        @pl.when((u == nb) & (md == 1))
        def _():
            for cp in x_copies(vrow, nb * BIG, MID):
                cp.start()

        @pl.when((u == nb + md) & (sm == 1))
        def _():
            for cp in x_copies(vrow, nb * BIG + md * MID, RB):
                cp.start()

    @pl.when(v == 0)
    def _():
        acc_ref[...] = jnp.zeros_like(acc_ref)
        _tail_fill(acc_ref, ys_ref, tail_ref, osem)
        start_unit(0, 0)
        for cp in tile_copies(ve_ref[0], 0, 0):
            cp.start()

    def step(j, carry):
        slot = j % 2
        first = j == 0
        last = j == NF - 1
        for cp in tile_copies(ve_ref[v], j, slot):
            cp.wait()

        @pl.when(j + 1 < NF)
        def _():
            for cp in tile_copies(ve_ref[v], j + 1, 1 - slot):
                cp.start()

        @pl.when(last & (v + 1 < nvis))
        def _():
            for cp in tile_copies(ve_ref[jnp.minimum(v + 1, MOE_NV - 1)], 0, 1 - slot):
                cp.start()

        b1g = b1_ref[0, :, pl.ds(pl.multiple_of(j * TF, TF), TF)]
        b1l = b1_ref[0, :, pl.ds(pl.multiple_of(D_FF + j * TF, TF), TF)]

        def block(u, r0, n):
            rows = pl.ds(pl.multiple_of(r0, RB), n)

            @pl.when(first)
            def _():
                for h, cp in zip(range(0, n, MID), x_copies(row0, r0, n)):
                    cp.wait()
                    m = min(MID, n - h)
                    lo_half, hi_half = _unpack_halves(xin_ref[h // MID, pl.ds(0, m), :])
                    piece = pl.ds(pl.multiple_of(r0 + h, RB), m)
                    xb_ref[piece, :D // 2] = lo_half
                    xb_ref[piece, D // 2:] = hi_half
                start_unit(v, u + 1)

            x = xb_ref[rows, :]
            glu = jnp.minimum(_dot(x, wg_ref[slot].astype(BF16)) + b1g, SWIGLU_LIMIT)
            lin = jnp.clip(_dot(x, wl_ref[slot].astype(BF16)) + b1l, -SWIGLU_LIMIT, SWIGLU_LIMIT)
            act = glu * _sigmoid(SWIGLU_ALPHA * glu) * (lin + 1.0)
            part = _dot(act.astype(BF16), wd_ref[slot].astype(BF16))

            acc_ref[rows, :] = part + jnp.where(first, b2_ref[0], acc_ref[rows, :])

            @pl.when(last)
            def _():
                y_copy(r0, n).start()

        def big(i, c):
            block(i, i * BIG, BIG)
            return c
        lax.fori_loop(0, nbig, big, 0)

        @pl.when(mid == 1)
        def _():
            block(nbig, mid_r0, MID)

        @pl.when(small == 1)
        def _():
            block(nbig + mid, small_r0, RB)

        @pl.when(last)
        def _():
            def finish(i, c):
                y_copy(i * BIG, BIG).wait()
                return c
            lax.fori_loop(0, nbig, finish, 0)

            @pl.when(mid == 1)
            def _():
                y_copy(mid_r0, MID).wait()

            @pl.when(small == 1)
            def _():
                y_copy(small_r0, RB).wait()

            @pl.when(v + 1 < nvis)
            def _():
                start_unit(v + 1, 0)
        return carry

    @pl.when(active)
    def _():
        lax.fori_loop(0, NF, step, 0)


def _experts(xs, rt, w1, b1, w2, b2):
    b1 = b1.reshape(N_EXP, 1, 2 * D_FF)
    b2 = b2.reshape(N_EXP, 1, D)
    any_spec = pl.BlockSpec(memory_space=pl.ANY)
    return pl.pallas_call(
        _experts_kernel,
        grid_spec=pltpu.PrefetchScalarGridSpec(
            num_scalar_prefetch=5,
            grid=(MOE_NV,),
            in_specs=[
                any_spec, any_spec, any_spec,
                pl.BlockSpec((1, 1, 2 * D_FF), lambda v, ve, *s: (ve[v], 0, 0)),
                pl.BlockSpec((1, 1, D), lambda v, ve, *s: (ve[v], 0, 0)),
            ],
            out_specs=any_spec,
            scratch_shapes=[
                pltpu.VMEM((2, 2 * MOE_RB, D // 2), jnp.int32),
                pltpu.VMEM((MOE_RMAX, D), BF16),
                pltpu.VMEM((MOE_RMAX, D), F32),
                pltpu.VMEM((2, D, MOE_TF), F32),
                pltpu.VMEM((2, D, MOE_TF), F32),
                pltpu.VMEM((2, MOE_TF, D), F32),
                pltpu.SemaphoreType.DMA((2,)),
                pltpu.SemaphoreType.DMA,
                pltpu.SemaphoreType.DMA((2,)),
            ],
        ),
        out_shape=jax.ShapeDtypeStruct((MOE_ROWS, D), F32),
        compiler_params=pltpu.CompilerParams(
            dimension_semantics=("arbitrary",), vmem_limit_bytes=VMEM_LIMIT),
        name="experts",
    )(rt["vis_e"], rt["vis_cnt"], rt["vis_row"], rt["n_vis"], rt["tail_blk"], xs, w1, w2, b1, b2)


def _combine_kernel(dest_ref, ys_ref, h_ref, gate_ref, gf_ref, o_ref, buf_ref, sem):
    i = pl.program_id(0)
    tm = h_ref.shape[0]

    def gather(tile, slot):
        def issue(t, c):
            for k in range(TOP_K):
                _row_copy(ys_ref, dest_ref[k * T + tile * tm + t], buf_ref.at[slot, k], t,
                          sem.at[slot]).start(priority=k % 2)
            return c
        lax.fori_loop(0, tm, issue, 0)

    @pl.when(i == 0)
    def _():
        gather(0, 0)

    @pl.when(i + 1 < pl.num_programs(0))
    def _():
        gather(i + 1, (i + 1) % 2)

    slot = i % 2
    for k in range(TOP_K):
        pltpu.make_async_copy(ys_ref.at[pl.ds(0, tm), :], buf_ref.at[slot, k], sem.at[slot]).wait()

    h = h_ref[...]
    for k in range(TOP_K):
        h = h + gate_ref[:, k:k + 1] * buf_ref[slot, k]
    o_ref[...] = _rms(h, gf_ref[...])


def _combine(ys, h, gates_t, gf, rt):
    tm = 512
    return pl.pallas_call(
        _combine_kernel,
        grid_spec=pltpu.PrefetchScalarGridSpec(
            num_scalar_prefetch=1,
            grid=(T // tm,),
            in_specs=[
                pl.BlockSpec(memory_space=pl.ANY),
                pl.BlockSpec((tm, D), lambda i, *_: (i, 0)),
                pl.BlockSpec((tm, TOP_K), lambda i, *_: (i, 0)),
                pl.BlockSpec((1, D), lambda i, *_: (0, 0)),
            ],
            out_specs=pl.BlockSpec((tm, D), lambda i, *_: (i, 0)),
            scratch_shapes=[pltpu.VMEM((2, TOP_K, tm, D), F32), pltpu.SemaphoreType.DMA((2,))],
        ),
        out_shape=jax.ShapeDtypeStruct((T, D), F32),
        compiler_params=pltpu.CompilerParams(
            dimension_semantics=("arbitrary",), vmem_limit_bytes=VMEM_LIMIT),
        name="combine",
    )(rt["dest"], ys, h, gates_t, gf)


def kernel(x, norm1_g, w_in, gla_gate_w2, gla_gate_b, gla_norm_g, s5_a_re, s5_a_im, s5_b_re, s5_b_im, s5_c_re, s5_c_im, s5_d, s5_log_step, s5_glu_w, s5_glu_b, s5_norm_g, w_out, norm2_g, router_w, router_b, expert_w1, expert_b1, expert_w2, expert_b2, final_norm_g):
    assert x.shape == (1, T, D) and w_in.shape[0] == 1
    xt = x.reshape(T, D)
    w2g = jnp.pad(gla_gate_w2[0], ((0, 128 - GATE_RANK), (0, 0)))
    qkvr, lg, u = _proj(xt, norm1_g, jnp.swapaxes(w_in, 1, 2), w2g, gla_gate_b)
    gla = _gla(qkvr, lg, gla_norm_g)
    tables = _s5_tables(s5_a_re[0], s5_a_im[0], s5_b_re[0], s5_b_im[0], s5_c_re[0], s5_c_im[0],
                        s5_d[0], s5_log_step[0])
    y = _s5(u, tables)
    h, hn, top_idx, gates, rank, counts = _post(
        y, gla, xt, s5_glu_w[0].astype(BF16), s5_glu_b, s5_norm_g, w_out[0].astype(BF16), norm2_g,
        jnp.pad(router_w[0], ((0, 0), (0, 128 - N_EXP))), router_b.reshape(N_EXP, 1))
    rt = _route(top_idx, rank, counts)
    xs = _scatter(hn, rt)
    ys = _experts(xs, rt, expert_w1[0], expert_b1[0], expert_w2[0], expert_b2[0])
    out = _combine(ys, h, gates.T, final_norm_g.reshape(1, D), rt)
    return out.reshape(1, T, D)
```

```python
import math

import jax
import jax.numpy as jnp
from jax import lax
from jax.experimental import pallas as pl
from jax.experimental.pallas import tpu as pltpu

F32 = jnp.float32
BF16 = jnp.bfloat16
HIGHEST = lax.Precision.HIGHEST

T = 8192
D = 2048
GLA_H = 4
GLA_DK = 128
GLA_DV = 256
GLA_CHUNK = 64
GLA_SUB = 16
GATE_RANK = 16
GATE_TAU = 16.0
S5_G = 64
S5_N = 16
S5_P = 64
S5_L = 16
S5_GT = 8
N_EXP = 32
TOP_K = 4
D_FF = 2048
SWIGLU_ALPHA = 1.702
SWIGLU_LIMIT = 7.0
RMS_EPS = 1e-6

MOE_RB = 128
MOE_RMAX = 2176
MOE_TF = 512
MOE_NF = D_FF // MOE_TF
MOE_NV = -(-T * TOP_K // MOE_RMAX) + N_EXP
MOE_ROWS = T * TOP_K + N_EXP * MOE_RB
VMEM_LIMIT = 56 * 1024 * 1024

NN = (((1,), (0,)), ((), ()))
NT = (((1,), (1,)), ((), ()))
TN = (((0,), (0,)), ((), ()))


def _dot(a, b, dims=NN, precision=None):
    return lax.dot_general(a, b, dims, preferred_element_type=F32, precision=precision)


def _dot3(a, b):
    a_hi, b_hi = a.astype(BF16), b.astype(BF16)
    a_lo = (a - a_hi.astype(F32)).astype(BF16)
    b_lo = (b - b_hi.astype(F32)).astype(BF16)
    return _dot(a_hi, b_hi) + (_dot(a_lo, b_hi) + _dot(a_hi, b_lo))


def _rms(x, g):
    return x * lax.rsqrt(jnp.mean(x * x, axis=-1, keepdims=True) + RMS_EPS) * g


def _sigmoid(x):
    return 1.0 / (1.0 + jnp.exp(-x))


def _pack_halves(lo_half, hi_half):
    a = lax.bitcast_convert_type(lo_half.astype(F32), jnp.int32)
    b = lax.bitcast_convert_type(hi_half.astype(F32), jnp.int32)
    return a | lax.shift_right_logical(b, 16)


def _unpack_halves(p):
    a = lax.bitcast_convert_type(p & jnp.int32(-65536), F32).astype(BF16)
    b = lax.bitcast_convert_type(lax.shift_left(p, 16), F32).astype(BF16)
    return a, b


def _const_spec(shape):
    nd = len(shape)
    return pl.BlockSpec(shape, lambda *_: (0,) * nd, pipeline_mode=pl.Buffered(1))


PROJ_QKVR = 2 * GLA_H * GLA_DK + 2 * GLA_H * GLA_DV
PROJ_U = S5_G * S5_N
PROJ_CHUNK = 512


def _proj_kernel(x_ref, g1_ref, wt_in_ref, w2_ref, gb_ref, qkvr_ref, lg_ref, u_ref,
                 wq_ref, wg_ref, wu_ref, stage_ref, gstage_ref, sem, gsem):
    @pl.when(pl.program_id(0) == 0)
    def _():
        pieces = ([(wq_ref, c) for c in range(PROJ_QKVR // PROJ_CHUNK)]
                  + [(wu_ref, c) for c in range(PROJ_U // PROJ_CHUNK)])

        def first_row(k):
            dst, c = pieces[k]
            return c * PROJ_CHUNK + (0 if dst is wq_ref else PROJ_QKVR + GATE_RANK)

        def piece(k):
            return pltpu.make_async_copy(wt_in_ref.at[0, pl.ds(first_row(k), PROJ_CHUNK), :], stage_ref.at[k % 2],
                                         sem.at[k % 2])

        gate = pltpu.make_async_copy(wt_in_ref.at[0, pl.ds(PROJ_QKVR, GATE_RANK), :], gstage_ref, gsem)
        gate.start()
        piece(0).start()
        for k, (dst, c) in enumerate(pieces):
            piece(k).wait()
            if k + 1 < len(pieces):
                piece(k + 1).start()
            dst[:, c * PROJ_CHUNK:(c + 1) * PROJ_CHUNK] = stage_ref[k % 2].T.astype(BF16)
        gate.wait()
        padded = jnp.concatenate([gstage_ref[...], jnp.zeros((128 - GATE_RANK, D), F32)], axis=0)
        wg_ref[...] = padded.T.astype(BF16)

    hb = _rms(x_ref[...], g1_ref[...]).astype(BF16)
    p = _dot(hb, wq_ref[...])
    nq = GLA_H * GLA_DK
    qkvr_ref[:, :nq] = (p[:, :nq] * (GLA_DK ** -0.5)).astype(BF16)
    qkvr_ref[:, nq:] = p[:, nq:].astype(BF16)
    glr = _dot(hb, wg_ref[...])
    z = _dot3(glr, w2_ref[...]) + gb_ref[...]
    lg_ref[...] = (jnp.minimum(z, 0.0) - jnp.log(1.0 + jnp.exp(-jnp.abs(z)))) * (1.0 / GATE_TAU)
    u_ref[...] = _dot(hb, wu_ref[...])


def _proj(x, g1, wt_in, w2, gb):
    tm = 256
    return pl.pallas_call(
        _proj_kernel,
        grid=(T // tm,),
        in_specs=[
            pl.BlockSpec((tm, D), lambda i: (i, 0)),
            _const_spec((1, D)),
            pl.BlockSpec(memory_space=pl.ANY),
            _const_spec(w2.shape), _const_spec(gb.shape),
        ],
        out_specs=[
            pl.BlockSpec((tm, PROJ_QKVR), lambda i: (i, 0)),
            pl.BlockSpec((tm, GLA_H * GLA_DK), lambda i: (i, 0)),
            pl.BlockSpec((tm, PROJ_U), lambda i: (i, 0)),
        ],
        out_shape=[
            jax.ShapeDtypeStruct((T, PROJ_QKVR), BF16),
            jax.ShapeDtypeStruct((T, GLA_H * GLA_DK), F32),
            jax.ShapeDtypeStruct((T, PROJ_U), F32),
        ],
        scratch_shapes=[
            pltpu.VMEM((D, PROJ_QKVR), BF16),
            pltpu.VMEM((D, 128), BF16),
            pltpu.VMEM((D, PROJ_U), BF16),
            pltpu.VMEM((2, PROJ_CHUNK, D), F32),
            pltpu.VMEM((GATE_RANK, D), F32),
            pltpu.SemaphoreType.DMA((2,)),
            pltpu.SemaphoreType.DMA,
        ],
        compiler_params=pltpu.CompilerParams(
            dimension_semantics=("arbitrary",), vmem_limit_bytes=VMEM_LIMIT),
        name="proj",
    )(x, g1, wt_in, w2, gb)


def _gla_kernel(q_ref, k_ref, v_ref, r_ref, lg_ref, gn_ref, tril_ref, esel_ref, o_ref, state_ref):
    C, S = GLA_CHUNK, GLA_SUB
    nsub = C // S

    @pl.when(pl.program_id(0) == 0)
    def _():
        state_ref[...] = jnp.zeros_like(state_ref)

    row = lax.broadcasted_iota(jnp.int32, (C, C), 0)
    col = lax.broadcasted_iota(jnp.int32, (C, C), 1)
    below = (col // S) < (row // S)
    diag = ((col // S) == (row // S)) & (col <= row)
    tril = tril_ref[...]
    esel = esel_ref[...]
    gn = gn_ref[...]

    def head(h, rows, b):
        kcols = slice(h * GLA_DK, (h + 1) * GLA_DK)
        vcols = slice(h * GLA_DV, (h + 1) * GLA_DV)
        q = q_ref[rows, kcols].astype(F32)
        k = k_ref[rows, kcols].astype(F32)
        v = v_ref[rows, vcols]
        bl = b[C - 1:C, :]
        st = state_ref[h]
        o = _dot((q * jnp.exp2(b)).astype(BF16), st.astype(BF16), NT)
        s_rows = [jnp.zeros((S, C), F32)]
        for j in range(1, nsub):
            ref = b[j * S - 1:j * S, :]
            qj = (q[j * S:(j + 1) * S] * jnp.exp2(b[j * S:(j + 1) * S] - ref)).astype(BF16)
            kj = (k * jnp.exp2(jnp.minimum(ref - b, 0.0))).astype(BF16)
            s_rows.append(_dot(qj, kj, NT))
        s_off = jnp.concatenate(s_rows, axis=0)
        q3 = q.reshape(nsub, S, GLA_DK)
        k3 = k.reshape(nsub, S, GLA_DK)
        b3 = b.reshape(nsub, S, GLA_DK)
        xs = []
        for s in range(S):
            lo = (s // 8) * 8
            dec = jnp.exp2(jnp.minimum(b3[:, lo:] - b3[:, s:s + 1, :], 0.0))
            x = q3[:, lo:] * k3[:, s:s + 1, :] * dec
            if lo:
                x = jnp.concatenate([jnp.zeros((nsub, lo, GLA_DK), F32), x], axis=1)
            xs.append(x.reshape(C, GLA_DK).astype(BF16))
        dsc = _dot(jnp.concatenate(xs, axis=1), esel)
        a = jnp.where(below, s_off, jnp.where(diag, dsc, 0.0))
        o = o + _dot(a.astype(BF16), v)
        kout = (k * jnp.exp2(bl - b)).astype(BF16)
        state_ref[h] = st * jnp.exp2(bl) + _dot(v, kout, TN)
        y = _rms(o, gn)
        r = r_ref[rows, vcols].astype(F32)
        o_ref[rows, vcols] = (y * (r * _sigmoid(r))).astype(BF16)

    def chunk(c, carry):
        rows = pl.ds(pl.multiple_of(c * C, C), C)
        g = lg_ref[rows, :]
        g_hi = g.astype(BF16)
        g_mid = (g - g_hi.astype(F32)).astype(BF16)
        g_lo = (g - g_hi.astype(F32) - g_mid.astype(F32)).astype(BF16)
        b = (_dot(tril, g_hi) + (_dot(tril, g_mid) + _dot(tril, g_lo))) * math.log2(math.e)
        for h in range(GLA_H):
            head(h, rows, b[:, h * GLA_DK:(h + 1) * GLA_DK])
        return carry

    lax.fori_loop(0, q_ref.shape[0] // C, chunk, 0, unroll=8)


def _gla(qkvr, lg, gn):
    tb = 512
    C, S = GLA_CHUNK, GLA_SUB
    nk, nv = GLA_H * GLA_DK, GLA_H * GLA_DV
    tril = jnp.tril(jnp.ones((C, C), BF16))
    esel = (jnp.arange(S * GLA_DK)[:, None] // GLA_DK == jnp.arange(C)[None, :] % S).astype(BF16)
    return pl.pallas_call(
        _gla_kernel,
        grid=(T // tb,),
        in_specs=[
            pl.BlockSpec((tb, nk), lambda i: (i, 0)),
            pl.BlockSpec((tb, nk), lambda i: (i, 1)),
            pl.BlockSpec((tb, nv), lambda i: (i, 1)),
            pl.BlockSpec((tb, nv), lambda i: (i, 2)),
            pl.BlockSpec((tb, nk), lambda i: (i, 0)),
            _const_spec((1, GLA_DV)), _const_spec((C, C)), _const_spec((S * GLA_DK, C)),
        ],
        out_specs=pl.BlockSpec((tb, nv), lambda i: (i, 0)),
        out_shape=jax.ShapeDtypeStruct((T, nv), BF16),
        scratch_shapes=[pltpu.VMEM((GLA_H, GLA_DV, GLA_DK), F32)],
        compiler_params=pltpu.CompilerParams(
            dimension_semantics=("arbitrary",), vmem_limit_bytes=VMEM_LIMIT),
        name="gla",
    )(qkvr, qkvr, qkvr, qkvr, lg, gn, tril, esel)


def _s5_tables(a_re, a_im, b_re, b_im, c_re, c_im, d_skip, log_step):
    L, G, P, N = S5_L, S5_G, S5_P, S5_N
    delta = jnp.exp(log_step)[:, None]
    ar, ai = a_re * delta, a_im * delta

    def apow(tau):
        tau = jnp.asarray(tau, F32)[None, :, None]
        mag = jnp.exp(ar[:, None, :] * tau)
        ph = ai[:, None, :] * tau
        return mag * jnp.cos(ph), mag * jnp.sin(ph)

    p_re, p_im = apow(jnp.arange(L + 1))
    ab_re, ab_im = p_re[:, 1], p_im[:, 1]
    den = a_re * a_re + a_im * a_im
    cf_re = ((ab_re - 1.0) * a_re + ab_im * a_im) / den
    cf_im = (ab_im * a_re - (ab_re - 1.0) * a_im) / den
    bb_re = cf_re[:, :, None] * b_re - cf_im[:, :, None] * b_im
    bb_im = cf_re[:, :, None] * b_im + cf_im[:, :, None] * b_re
    e_re = p_re[:, :L, :, None] * bb_re[:, None] - p_im[:, :L, :, None] * bb_im[:, None]
    e_im = p_re[:, :L, :, None] * bb_im[:, None] + p_im[:, :L, :, None] * bb_re[:, None]
    kk = (jnp.einsum('gnp,gtpm->gtnm', c_re, e_re, precision=HIGHEST)
          - jnp.einsum('gnp,gtpm->gtnm', c_im, e_im, precision=HIGHEST))
    kk = kk.at[:, 0].add(jnp.eye(N, dtype=F32)[None] * d_skip[:, :, None])
    r_re, r_im = apow(L - 1 - jnp.arange(L))
    bs_re = (r_re[:, :, None, :] * bb_re.transpose(0, 2, 1)[:, None] - r_im[:, :, None, :] * bb_im.transpose(0, 2, 1)[:, None])
    bs_im = (r_re[:, :, None, :] * bb_im.transpose(0, 2, 1)[:, None] + r_im[:, :, None, :] * bb_re.transpose(0, 2, 1)[:, None])
    ct_re = c_re.transpose(0, 2, 1)[:, :, None, :]
    ct_im = c_im.transpose(0, 2, 1)[:, :, None, :]
    q_re = p_re[:, 1:].transpose(0, 2, 1)[:, :, :, None]
    q_im = p_im[:, 1:].transpose(0, 2, 1)[:, :, :, None]
    cre = ct_re * q_re - ct_im * q_im
    cim = -(ct_re * q_im + ct_im * q_re)
    GT = S5_GT
    nt = G // GT

    def tile_groups(w):
        return w.reshape((nt, GT) + w.shape[1:])

    kc = tile_groups(kk).transpose(0, 1, 4, 2, 3).reshape(nt, GT * N, L * N)

    def in_table(w):
        return tile_groups(w).transpose(0, 2, 1, 3, 4).reshape(nt, L * GT * N, P)

    def out_table(w):
        return w.reshape(nt, GT * P, L * N)

    nstep = (T // L - 1).bit_length()
    sc_re, sc_im = apow(jnp.asarray([L * 2 ** i for i in range(nstep)]))

    def lanes(w):
        return tile_groups(w).transpose(0, 2, 1, 3).reshape(nt, w.shape[1], GT * P)

    r1 = jnp.arange(L * N)[:, None]
    c1 = jnp.arange(L * GT * N)[None, :]
    wide_tn = ((r1 // N == c1 // (GT * N)) & (r1 % N == c1 % N)).astype(BF16)
    wide_p = (jnp.arange(P)[:, None] == jnp.arange(GT * P)[None, :] % P).astype(BF16)
    return dict(
        kc=kc.astype(BF16),
        bs_re=in_table(bs_re).astype(BF16), bs_im=in_table(bs_im).astype(BF16),
        cre=out_table(cre).astype(BF16), cim=out_table(cim).astype(BF16),
        sc_re=lanes(sc_re), sc_im=lanes(sc_im), wide_tn=wide_tn, wide_p=wide_p,
    )


def _s5_kernel(u_ref, kc_ref, bsr_ref, bsi_ref, cre_ref, cim_ref, scr_ref, sci_ref, wtn_ref, wp_ref,
               y_ref, m_ref):
    L, N, P = S5_L, S5_N, S5_P
    W = S5_GT * N
    nch = u_ref.shape[0] // L

    @pl.when(pl.program_id(0) == 0)
    def _():
        m_ref[...] = jnp.zeros_like(m_ref)

    def widen(compact, wide, row_group, col_group):
        full = _dot(compact, wide)
        r = lax.broadcasted_iota(jnp.int32, full.shape, 0)
        c = lax.broadcasted_iota(jnp.int32, full.shape, 1)
        return jnp.where(row_group(r) == col_group(c), full, 0.0).astype(BF16)

    wtn, wp = wtn_ref[...], wp_ref[...]
    bd = widen(kc_ref[0], wtn, lambda r: r // N, lambda c: (c % W) // N)
    bsr = widen(bsr_ref[0], wp, lambda r: (r % W) // N, lambda c: c // P)
    bsi = widen(bsi_ref[0], wp, lambda r: (r % W) // N, lambda c: c // P)
    cre = widen(cre_ref[0], wtn, lambda r: r // P, lambda c: (c % W) // N)
    cim = widen(cim_ref[0], wtn, lambda r: r // P, lambda c: (c % W) // N)
    for s in range(L):
        m_ref[s * W:(s + 1) * W, s * W:] = bd[:, :(L - s) * W]
    ucat = jnp.concatenate([u_ref[pl.ds(s, nch, stride=L), :].astype(BF16) for s in range(L)], axis=1)
    hr = _dot(ucat, bsr)
    hi = _dot(ucat, bsi)
    row = lax.broadcasted_iota(jnp.int32, hr.shape, 0)

    def shift(x, d):
        if d % 8 == 0:
            return jnp.concatenate([jnp.zeros((d, x.shape[1]), F32), x[:nch - d]], axis=0)
        return jnp.where(row >= d, pltpu.roll(x, d, 0), 0.0)

    for i in range(scr_ref.shape[1]):
        mr, mi = scr_ref[0, i:i + 1, :], sci_ref[0, i:i + 1, :]
        pr, pi = shift(hr, 2 ** i), shift(hi, 2 ** i)
        hr, hi = hr + mr * pr - mi * pi, hi + mr * pi + mi * pr
    gr, gi = shift(hr, 1), shift(hi, 1)
    grb, gib = gr.astype(BF16), gi.astype(BF16)
    for tp in range(L // 2):
        cols = slice(2 * tp * W, (2 * tp + 2) * W)
        depth = (2 * tp + 2) * W
        y = _dot(ucat[:, :depth], m_ref[:depth, cols]) + _dot(grb, cre[:, cols]) + _dot(gib, cim[:, cols])
        for t in (2 * tp, 2 * tp + 1):
            y_ref[pl.ds(t, nch, stride=L), :] = y[:, (t - 2 * tp) * W:(t - 2 * tp + 1) * W]


def _s5(u, tb):
    W = S5_GT * S5_N
    nt = S5_G // S5_GT

    def tile_spec(a):
        return pl.BlockSpec((1,) + a.shape[1:], lambda g: (g,) + (0,) * (a.ndim - 1))

    ws = [tb[n] for n in ["kc", "bs_re", "bs_im", "cre", "cim", "sc_re", "sc_im"]]
    consts = [tb["wide_tn"], tb["wide_p"]]
    return pl.pallas_call(
        _s5_kernel,
        grid=(nt,),
        in_specs=([pl.BlockSpec((T, W), lambda g: (0, g))] + [tile_spec(w) for w in ws]
                  + [_const_spec(c.shape) for c in consts]),
        out_specs=pl.BlockSpec((T, W), lambda g: (0, g)),
        out_shape=jax.ShapeDtypeStruct((T, S5_G * S5_N), F32),
        scratch_shapes=[pltpu.VMEM((S5_L * W, S5_L * W), BF16)],
        compiler_params=pltpu.CompilerParams(
            dimension_semantics=("arbitrary",), vmem_limit_bytes=VMEM_LIMIT),
        name="s5",
    )(u, *ws, *consts)


def _post_kernel(y_ref, gla_ref, x_ref, gw_ref, gb_ref, sg_ref, wo_ref, n2_ref, rw_ref, rb_ref, before_ref,
                 h_ref, hn_ref, idx_ref, gate_ref, rank_ref, cnt_ref, run_ref):
    @pl.when(pl.program_id(0) == 0)
    def _():
        run_ref[...] = jnp.zeros_like(run_ref)

    y = y_ref[...]
    z = 0.5 * y * (1.0 + jnp.tanh(math.sqrt(2.0 / math.pi) * (y + 0.044715 * (y * y * y))))
    z = z * _sigmoid(_dot(z.astype(BF16), gw_ref[...]) + gb_ref[...])
    s5o = _rms(z, sg_ref[...]).astype(BF16)
    half = GLA_H * GLA_DV
    gla = gla_ref[...]
    ssq = jnp.zeros((y.shape[0], 1), F32)
    for c in range(2):
        cols = slice(c * (D // 2), (c + 1) * (D // 2))
        hc = x_ref[:, cols] + _dot(gla, wo_ref[:half, cols]) + _dot(s5o, wo_ref[half:, cols])
        h_ref[:, cols] = hc
        ssq = ssq + jnp.sum(hc * hc, axis=-1, keepdims=True)
    h = h_ref[...]
    hn = h * lax.rsqrt(ssq * (1.0 / D) + RMS_EPS) * n2_ref[...]
    hi = hn.astype(BF16)
    hn_ref[...] = _pack_halves(hi[:, :D // 2], hi[:, D // 2:])
    lo = (hn - hi.astype(F32)).astype(BF16)
    rw = rw_ref[...]
    rw_hi = rw.astype(BF16)
    rw_lo = (rw - rw_hi.astype(F32)).astype(BF16)
    lg = _dot(hi, rw_hi) + (_dot(lo, rw_hi) + _dot(hi, rw_lo))
    lt = lg.T[:N_EXP] + rb_ref[...]
    eid = lax.broadcasted_iota(jnp.int32, lt.shape, 0).astype(F32)
    vals, idxs = [], []
    for _ in range(TOP_K):
        m = jnp.max(lt, axis=0, keepdims=True)
        sel = jnp.min(jnp.where(lt == m, eid, float(N_EXP)), axis=0, keepdims=True)
        vals.append(m)
        idxs.append(sel)
        lt = jnp.where(eid == sel, -jnp.inf, lt)
    ex = [jnp.exp(vv - vals[0]) for vv in vals]
    inv = 1.0 / (ex[0] + ex[1] + ex[2] + ex[3])
    idx_ref[...] = jnp.concatenate(idxs, axis=0).astype(jnp.int32)
    gate_ref[...] = jnp.concatenate([e * inv for e in ex], axis=0)
    run = run_ref[:, :1]
    ranks = []
    for sel in idxs:
        onehot = jnp.where(eid == sel, 1.0, 0.0)
        earlier = _dot(onehot.astype(BF16), before_ref[...])
        ranks.append(jnp.sum(onehot * (run + earlier), axis=0, keepdims=True))
        run = run + jnp.sum(onehot, axis=1, keepdims=True)
    rank_ref[...] = jnp.concatenate(ranks, axis=0).astype(jnp.int32)
    run_ref[...] = jnp.broadcast_to(run, run_ref.shape)
    cnt_ref[...] = jnp.broadcast_to(run, cnt_ref.shape)


def _post(y, gla, x, gw, gb, sg, wo, n2, rwt, rb):
    tm = 256
    width = S5_G * S5_N
    before = (jnp.arange(tm)[:, None] < jnp.arange(tm)[None, :]).astype(BF16)
    return pl.pallas_call(
        _post_kernel,
        grid=(T // tm,),
        in_specs=[
            pl.BlockSpec((tm, width), lambda i: (i, 0)),
            pl.BlockSpec((tm, GLA_H * GLA_DV), lambda i: (i, 0)),
            pl.BlockSpec((tm, D), lambda i: (i, 0)),
            _const_spec(gw.shape), _const_spec(gb.shape), _const_spec(sg.shape), _const_spec(wo.shape),
            _const_spec(n2.shape), _const_spec(rwt.shape), _const_spec(rb.shape), _const_spec(before.shape),
        ],
        out_specs=[
            pl.BlockSpec((tm, D), lambda i: (i, 0)),
            pl.BlockSpec((tm, D // 2), lambda i: (i, 0)),
            pl.BlockSpec((TOP_K, tm), lambda i: (0, i)),
            pl.BlockSpec((TOP_K, tm), lambda i: (0, i)),
            pl.BlockSpec((TOP_K, tm), lambda i: (0, i)),
            pl.BlockSpec((N_EXP, 128), lambda i: (0, 0)),
        ],
        out_shape=[
            jax.ShapeDtypeStruct((T, D), F32),
            jax.ShapeDtypeStruct((T, D // 2), jnp.int32),
            jax.ShapeDtypeStruct((TOP_K, T), jnp.int32),
            jax.ShapeDtypeStruct((TOP_K, T), F32),
            jax.ShapeDtypeStruct((TOP_K, T), jnp.int32),
            jax.ShapeDtypeStruct((N_EXP, 128), F32),
        ],
        scratch_shapes=[pltpu.VMEM((N_EXP, 128), F32)],
        compiler_params=pltpu.CompilerParams(
            dimension_semantics=("arbitrary",), vmem_limit_bytes=VMEM_LIMIT),
        name="post",
    )(y, gla, x, gw, gb, sg, wo, n2, rwt, rb, before)


def _route(top_idx, rank, counts):
    e_flat = top_idx.reshape(-1)
    rank = rank.reshape(-1)
    counts = counts[:, 0].astype(jnp.int32)
    padded = (counts + MOE_RB - 1) // MOE_RB * MOE_RB
    pad_end = jnp.cumsum(padded)
    pad_start = pad_end - padded
    dest = _dest(top_idx, rank.reshape(top_idx.shape), pad_start.astype(jnp.int32)).reshape(-1)
    nvis_e = (counts + MOE_RMAX - 1) // MOE_RMAX
    vis_end = jnp.cumsum(nvis_e)
    n_vis = vis_end[-1]
    v = jnp.arange(MOE_NV, dtype=jnp.int32)
    vc = jnp.minimum(v, n_vis - 1)
    ve = jnp.minimum(jnp.searchsorted(vis_end, vc, side='right'), N_EXP - 1).astype(jnp.int32)
    local = vc - (vis_end[ve] - nvis_e[ve])
    vcnt = jnp.where(v < n_vis, jnp.minimum(MOE_RMAX, counts[ve] - local * MOE_RMAX), 0)
    vrow = pad_start[ve] + local * MOE_RMAX
    return dict(dest=dest, fill_from=(pad_start + counts).astype(jnp.int32), fill_to=pad_end.astype(jnp.int32),
                tail_blk=(pad_end[-1:] // MOE_RB).astype(jnp.int32),
                vis_e=ve, vis_cnt=vcnt.astype(jnp.int32), vis_row=vrow.astype(jnp.int32),
                n_vis=n_vis.reshape(1).astype(jnp.int32))


def _dest_kernel(start_ref, idx_ref, rank_ref, dest_ref):
    e = idx_ref[...]
    d = rank_ref[...]
    for x in range(N_EXP):
        d = d + jnp.where(e == x, start_ref[x], 0)
    dest_ref[...] = d


def _dest(top_idx, rank, pad_start):
    full = pl.BlockSpec(top_idx.shape, lambda i, *_: (0, 0))
    return pl.pallas_call(
        _dest_kernel,
        grid_spec=pltpu.PrefetchScalarGridSpec(num_scalar_prefetch=1, grid=(1,), in_specs=[full, full], out_specs=full),
        out_shape=jax.ShapeDtypeStruct(top_idx.shape, jnp.int32),
        name="dest",
    )(pad_start, top_idx, rank)


def _row_copy(src_ref, srow, dst_ref, drow, sem):
    return pltpu.make_async_copy(src_ref.at[pl.ds(srow, 1), :], dst_ref.at[pl.ds(drow, 1), :], sem)


def _tail_fill(src_ref, dst_ref, tail_ref, sem):
    def cp(b):
        return pltpu.make_async_copy(src_ref.at[pl.ds(0, MOE_RB), :],
                                     dst_ref.at[pl.ds(pl.multiple_of(b * MOE_RB, MOE_RB), MOE_RB), :], sem)

    def start(b, c):
        cp(b).start()
        return c
    lax.fori_loop(tail_ref[0], MOE_ROWS // MOE_RB, start, 0)

    def finish(b, c):
        cp(b).wait()
        return c
    lax.fori_loop(tail_ref[0], MOE_ROWS // MOE_RB, finish, 0)


def _scatter_kernel(dest_ref, from_ref, to_ref, tail_ref, hn_ref, xs_ref, zero_ref, sem, zsem):
    i = pl.program_id(0)
    tm = hn_ref.shape[0]

    @pl.when(i == 0)
    def _():
        zero_ref[...] = jnp.zeros_like(zero_ref)
        _tail_fill(zero_ref, xs_ref, tail_ref, zsem)

        def per_expert(e, c):
            def fill(r, c2):
                _row_copy(zero_ref, 0, xs_ref, r, zsem).start()
                return c2
            lax.fori_loop(from_ref[e], to_ref[e], fill, 0)

            def drain(r, c2):
                _row_copy(zero_ref, 0, xs_ref, r, zsem).wait()
                return c2
            lax.fori_loop(from_ref[e], to_ref[e], drain, 0)
            return c
        lax.fori_loop(0, N_EXP, per_expert, 0)

    def issue(t, c):
        for k in range(TOP_K):
            _row_copy(hn_ref, t, xs_ref, dest_ref[k * T + i * tm + t], sem).start(priority=k % 2)
        return c
    lax.fori_loop(0, tm, issue, 0)

    for k in range(TOP_K):
        pltpu.make_async_copy(hn_ref, xs_ref.at[pl.ds(0, tm), :], sem).wait()


def _scatter(hn, rt):
    tm = 1024
    return pl.pallas_call(
        _scatter_kernel,
        grid_spec=pltpu.PrefetchScalarGridSpec(
            num_scalar_prefetch=4,
            grid=(T // tm,),
            in_specs=[pl.BlockSpec((tm, D // 2), lambda i, *_: (i, 0))],
            out_specs=pl.BlockSpec(memory_space=pl.ANY),
            scratch_shapes=[pltpu.VMEM((MOE_RB, D // 2), jnp.int32), pltpu.SemaphoreType.DMA,
                            pltpu.SemaphoreType.DMA],
        ),
        out_shape=jax.ShapeDtypeStruct((MOE_ROWS, D // 2), jnp.int32),
        compiler_params=pltpu.CompilerParams(
            dimension_semantics=("arbitrary",), vmem_limit_bytes=VMEM_LIMIT),
        name="scatter",
    )(rt["dest"], rt["fill_from"], rt["fill_to"], rt["tail_blk"], hn)


def _experts_kernel(ve_ref, vcnt_ref, vrow_ref, nvis_ref, tail_ref, xs_ref, w1_ref, w2_ref, b1_ref, b2_ref, ys_ref,
                    xin_ref, xb_ref, acc_ref, wg_ref, wl_ref, wd_ref, isem, osem, wsem):
    v = pl.program_id(0)
    RB = MOE_RB
    BIG, MID = 4 * RB, 2 * RB
    TF, NF = MOE_TF, MOE_NF
    nvis = nvis_ref[0]
    active = v < nvis

    def geometry(vv):
        nblk = (vcnt_ref[vv] + RB - 1) // RB
        return vrow_ref[vv], nblk // 4, (nblk // 2) % 2, nblk % 2

    row0, nbig, mid, small = geometry(v)
    mid_r0 = nbig * BIG
    small_r0 = mid_r0 + mid * MID

    def x_copies(vrow, r0, n):
        return [pltpu.make_async_copy(xs_ref.at[pl.ds(pl.multiple_of(vrow + r0 + h, RB), min(MID, n - h)), :],
                                      xin_ref.at[h // MID, pl.ds(0, min(MID, n - h)), :], isem.at[h // MID])
                for h in range(0, n, MID)]

    def y_copy(r0, n):
        return pltpu.make_async_copy(acc_ref.at[pl.ds(pl.multiple_of(r0, RB), n), :],
                                     ys_ref.at[pl.ds(pl.multiple_of(row0 + r0, RB), n), :], osem)

    def tile_copies(e, t, slot):
        cols = pl.ds(pl.multiple_of(t * TF, TF), TF)
        lin_cols = pl.ds(pl.multiple_of(D_FF + t * TF, TF), TF)
        return (pltpu.make_async_copy(w1_ref.at[e, :, cols], wg_ref.at[slot], wsem.at[slot]),
                pltpu.make_async_copy(w1_ref.at[e, :, lin_cols], wl_ref.at[slot], wsem.at[slot]),
                pltpu.make_async_copy(w2_ref.at[e, cols, :], wd_ref.at[slot], wsem.at[slot]))

    def start_unit(vv, u):
        vrow, nb, md, sm = geometry(vv)

        @pl.when(u < nb)
        def _():
            for cp in x_copies(vrow, u * BIG, BIG):
                cp.start()

        @pl.when((u == nb) & (md == 1))
        def _():
            for cp in x_copies(vrow, nb * BIG, MID):
                cp.start()

        @pl.when((u == nb + md) & (sm == 1))
        def _():
            for cp in x_copies(vrow, nb * BIG + md * MID, RB):
                cp.start()

    @pl.when(v == 0)
    def _():
        acc_ref[...] = jnp.zeros_like(acc_ref)
        _tail_fill(acc_ref, ys_ref, tail_ref, osem)
        start_unit(0, 0)
        for cp in tile_copies(ve_ref[0], 0, 0):
            cp.start()

    def step(j, carry):
        slot = j % 2
        first = j == 0
        last = j == NF - 1
        for cp in tile_copies(ve_ref[v], j, slot):
            cp.wait()

        @pl.when(j + 1 < NF)
        def _():
            for cp in tile_copies(ve_ref[v], j + 1, 1 - slot):
                cp.start()

        @pl.when(last & (v + 1 < nvis))
        def _():
            for cp in tile_copies(ve_ref[jnp.minimum(v + 1, MOE_NV - 1)], 0, 1 - slot):
                cp.start()

        b1g = b1_ref[0, :, pl.ds(pl.multiple_of(j * TF, TF), TF)]
        b1l = b1_ref[0, :, pl.ds(pl.multiple_of(D_FF + j * TF, TF), TF)]

        def block(u, r0, n):
            rows = pl.ds(pl.multiple_of(r0, RB), n)

            @pl.when(first)
            def _():
                for h, cp in zip(range(0, n, MID), x_copies(row0, r0, n)):
                    cp.wait()
                    m = min(MID, n - h)
                    lo_half, hi_half = _unpack_halves(xin_ref[h // MID, pl.ds(0, m), :])
                    piece = pl.ds(pl.multiple_of(r0 + h, RB), m)
                    xb_ref[piece, :D // 2] = lo_half
                    xb_ref[piece, D // 2:] = hi_half
                start_unit(v, u + 1)

            x = xb_ref[rows, :]
            glu = jnp.minimum(_dot(x, wg_ref[slot].astype(BF16)) + b1g, SWIGLU_LIMIT)
            lin = jnp.clip(_dot(x, wl_ref[slot].astype(BF16)) + b1l, -SWIGLU_LIMIT, SWIGLU_LIMIT)
            act = glu * _sigmoid(SWIGLU_ALPHA * glu) * (lin + 1.0)
            part = _dot(act.astype(BF16), wd_ref[slot].astype(BF16))

            acc_ref[rows, :] = part + jnp.where(first, b2_ref[0], acc_ref[rows, :])

            @pl.when(last)
            def _():
                y_copy(r0, n).start()

        def big(i, c):
            block(i, i * BIG, BIG)
            return c
        lax.fori_loop(0, nbig, big, 0)

        @pl.when(mid == 1)
        def _():
            block(nbig, mid_r0, MID)

        @pl.when(small == 1)
        def _():
            block(nbig + mid, small_r0, RB)

        @pl.when(last)
        def _():
            def finish(i, c):
                y_copy(i * BIG, BIG).wait()
                return c
            lax.fori_loop(0, nbig, finish, 0)

            @pl.when(mid == 1)
            def _():
                y_copy(mid_r0, MID).wait()

            @pl.when(small == 1)
            def _():
                y_copy(small_r0, RB).wait()

            @pl.when(v + 1 < nvis)
            def _():
                start_unit(v + 1, 0)
        return carry

    @pl.when(active)
    def _():
        lax.fori_loop(0, NF, step, 0)


def _experts(xs, rt, w1, b1, w2, b2):
    b1 = b1.reshape(N_EXP, 1, 2 * D_FF)
    b2 = b2.reshape(N_EXP, 1, D)
    any_spec = pl.BlockSpec(memory_space=pl.ANY)
    return pl.pallas_call(
        _experts_kernel,
        grid_spec=pltpu.PrefetchScalarGridSpec(
            num_scalar_prefetch=5,
            grid=(MOE_NV,),
            in_specs=[
                any_spec, any_spec, any_spec,
                pl.BlockSpec((1, 1, 2 * D_FF), lambda v, ve, *s: (ve[v], 0, 0)),
                pl.BlockSpec((1, 1, D), lambda v, ve, *s: (ve[v], 0, 0)),
            ],
            out_specs=any_spec,
            scratch_shapes=[
                pltpu.VMEM((2, 2 * MOE_RB, D // 2), jnp.int32),
                pltpu.VMEM((MOE_RMAX, D), BF16),
                pltpu.VMEM((MOE_RMAX, D), F32),
                pltpu.VMEM((2, D, MOE_TF), F32),
                pltpu.VMEM((2, D, MOE_TF), F32),
                pltpu.VMEM((2, MOE_TF, D), F32),
                pltpu.SemaphoreType.DMA((2,)),
                pltpu.SemaphoreType.DMA,
                pltpu.SemaphoreType.DMA((2,)),
            ],
        ),
        out_shape=jax.ShapeDtypeStruct((MOE_ROWS, D), F32),
        compiler_params=pltpu.CompilerParams(
            dimension_semantics=("arbitrary",), vmem_limit_bytes=VMEM_LIMIT),
        name="experts",
    )(rt["vis_e"], rt["vis_cnt"], rt["vis_row"], rt["n_vis"], rt["tail_blk"], xs, w1, w2, b1, b2)


def _combine_kernel(dest_ref, ys_ref, h_ref, gate_ref, gf_ref, o_ref, buf_ref, sem):
    i = pl.program_id(0)
    tm = h_ref.shape[0]

    def gather(tile, slot):
        def issue(t, c):
            for k in range(TOP_K):
                _row_copy(ys_ref, dest_ref[k * T + tile * tm + t], buf_ref.at[slot, k], t,
                          sem.at[slot]).start(priority=k % 2)
            return c
        lax.fori_loop(0, tm, issue, 0)

    @pl.when(i == 0)
    def _():
        gather(0, 0)

    @pl.when(i + 1 < pl.num_programs(0))
    def _():
        gather(i + 1, (i + 1) % 2)

    slot = i % 2
    for k in range(TOP_K):
        pltpu.make_async_copy(ys_ref.at[pl.ds(0, tm), :], buf_ref.at[slot, k], sem.at[slot]).wait()

    h = h_ref[...]
    for k in range(TOP_K):
        h = h + gate_ref[:, k:k + 1] * buf_ref[slot, k]
    o_ref[...] = _rms(h, gf_ref[...])


def _combine(ys, h, gates_t, gf, rt):
    tm = 256
    return pl.pallas_call(
        _combine_kernel,
        grid_spec=pltpu.PrefetchScalarGridSpec(
            num_scalar_prefetch=1,
            grid=(T // tm,),
            in_specs=[
                pl.BlockSpec(memory_space=pl.ANY),
                pl.BlockSpec((tm, D), lambda i, *_: (i, 0)),
                pl.BlockSpec((tm, TOP_K), lambda i, *_: (i, 0)),
                pl.BlockSpec((1, D), lambda i, *_: (0, 0)),
            ],
            out_specs=pl.BlockSpec((tm, D), lambda i, *_: (i, 0)),
            scratch_shapes=[pltpu.VMEM((2, TOP_K, tm, D), F32), pltpu.SemaphoreType.DMA((2,))],
        ),
        out_shape=jax.ShapeDtypeStruct((T, D), F32),
        compiler_params=pltpu.CompilerParams(
            dimension_semantics=("arbitrary",), vmem_limit_bytes=VMEM_LIMIT),
        name="combine",
    )(rt["dest"], ys, h, gates_t, gf)


def kernel(x, norm1_g, w_in, gla_gate_w2, gla_gate_b, gla_norm_g, s5_a_re, s5_a_im, s5_b_re, s5_b_im, s5_c_re, s5_c_im, s5_d, s5_log_step, s5_glu_w, s5_glu_b, s5_norm_g, w_out, norm2_g, router_w, router_b, expert_w1, expert_b1, expert_w2, expert_b2, final_norm_g):
    assert x.shape == (1, T, D) and w_in.shape[0] == 1
    xt = x.reshape(T, D)
    w2g = jnp.pad(gla_gate_w2[0], ((0, 128 - GATE_RANK), (0, 0)))
    qkvr, lg, u = _proj(xt, norm1_g, jnp.swapaxes(w_in, 1, 2), w2g, gla_gate_b)
    gla = _gla(qkvr, lg, gla_norm_g)
    tables = _s5_tables(s5_a_re[0], s5_a_im[0], s5_b_re[0], s5_b_im[0], s5_c_re[0], s5_c_im[0],
                        s5_d[0], s5_log_step[0])
    y = _s5(u, tables)
    h, hn, top_idx, gates, rank, counts = _post(
        y, gla, xt, s5_glu_w[0].astype(BF16), s5_glu_b, s5_norm_g, w_out[0].astype(BF16), norm2_g,
        jnp.pad(router_w[0], ((0, 0), (0, 128 - N_EXP))), router_b.reshape(N_EXP, 1))
    rt = _route(top_idx, rank, counts)
    xs = _scatter(hn, rt)
    ys = _experts(xs, rt, expert_w1[0], expert_b1[0], expert_w2[0], expert_b2[0])
    out = _combine(ys, h, gates.T, final_norm_g.reshape(1, D), rt)
    return out.reshape(1, T, D)
```
